```python
import jax
import jax.numpy as jnp
from jax import lax
import numpy as np

D_MODEL = 1024
BATCH = 2
SEQ = 8192
DEPTH = 1
DEC_BATCH = 16
DEC_SEQ = 32
PAST_LEN = 4096

CHUNK = 64
D_RWKV = 1024
RWKV_HEAD_DIM = 64
RWKV_HEADS = D_RWKV // RWKV_HEAD_DIM
W_LORA = 64
A_LORA = 64
G_LORA = 128
RWKV_PROJ = 3 * D_RWKV + W_LORA + A_LORA + G_LORA
RWKV_GN_EPS = 64e-5
D_MLSTM = 1024
MLSTM_HEADS = 4
MLSTM_HEAD_DIM = D_MLSTM // MLSTM_HEADS
CONV_W = 4
MLSTM_PROJ = 4 * D_MLSTM + 2 * MLSTM_HEADS
N_MEM = 256
MEM_HEADS = 4
MEM_HEAD_DIM = 256
D_MEM = MEM_HEADS * MEM_HEAD_DIM
N_BRANCH = 3
OFF_MLSTM = RWKV_PROJ
OFF_MEMQ = OFF_MLSTM + MLSTM_PROJ
OFF_GATE = OFF_MEMQ + D_MEM
IN_PROJ = OFF_GATE + N_BRANCH * D_MODEL
N_EXPERTS = 32
TOP_K = 4
EXPERT_HIDDEN = 128
SHARED_HIDDEN = 128
ROUTED_SCALE = 2.5
DN_ALPHA = (2 * DEPTH) ** 0.25
DN_BETA = (8 * DEPTH) ** -0.25
LN_EPS = 1e-5

kernel_name = 'hybrid_rwkv7_mlstm_memxattn_moe_stream_step'


def layer_norm(x, w, b):
    xf = x.astype(jnp.float32)
    mu = jnp.mean(xf, axis=-1, keepdims=True)
    var = jnp.mean(jnp.square(xf - mu), axis=-1, keepdims=True)
    return ((xf - mu) * lax.rsqrt(var + LN_EPS) * w + b).astype(x.dtype)


def head_norm(y, w, b, eps):
    n_heads, head_dim = y.shape[-2], y.shape[-1]
    mu = jnp.mean(y, axis=-1, keepdims=True)
    var = jnp.mean(jnp.square(y - mu), axis=-1, keepdims=True)
    return (y - mu) * lax.rsqrt(var + eps) * w.reshape(n_heads, head_dim) + b.reshape(n_heads, head_dim)


def causal_prefix(u, prev):
    full = jnp.concatenate([prev.astype(u.dtype), u], axis=1)
    return full, full[:, u.shape[1]:]


def rwkv7_recurrence(s0, r, decay, k, v, a_vec, b_vec):
    def step(s, inp):
        r_t, w_t, k_t, v_t, a_t, b_t = inp
        sa = jnp.einsum('bhvk,bhk->bhv', s, a_t)
        s = s * w_t[:, :, None, :] + sa[..., None] * b_t[:, :, None, :] + v_t[..., None] * k_t[:, :, None, :]
        return s, jnp.einsum('bhvk,bhk->bhv', s, r_t)
    xs = tuple(jnp.moveaxis(t, 1, 0) for t in (r, decay, k, v, a_vec, b_vec))
    s_last, ys = lax.scan(step, s0, xs)
    return jnp.moveaxis(ys, 0, 1), s_last


def mlstm_block_step(carry, inp):
    c, n, m = carry
    q, k, v, ig, lf = inp
    blk = q.shape[2]
    b = jnp.cumsum(lf, axis=-1)
    causal = jnp.tril(jnp.ones((blk, blk), dtype=bool))
    log_w = jnp.where(causal, b[..., :, None] - b[..., None, :] + ig[..., None, :], -jnp.inf)
    inter = b + m[..., None]
    m_t = jnp.maximum(inter, jnp.max(log_w, axis=-1))
    w_intra = jnp.exp(log_w - m_t[..., None])
    w_inter = jnp.exp(inter - m_t)
    s = jnp.einsum('bhtk,bhsk->bhts', q, k) * w_intra
    num = w_inter[..., None] * jnp.einsum('bhvk,bhtk->bhtv', c, q) + jnp.einsum('bhts,bhsv->bhtv', s, v)
    den = w_inter * jnp.einsum('bhk,bhtk->bht', n, q) + jnp.sum(s, axis=-1)
    h = num / jnp.maximum(jnp.abs(den), jnp.exp(-m_t))[..., None]
    g = b[..., -1:] - b + ig
    m_new = jnp.maximum(b[..., -1] + m, jnp.max(g, axis=-1))
    carry_decay = jnp.exp(b[..., -1] + m - m_new)
    wk = jnp.exp(g - m_new[..., None])
    c_new = carry_decay[..., None, None] * c + jnp.einsum('bhs,bhsv,bhsk->bhvk', wk, v, k)
    n_new = carry_decay[..., None] * n + jnp.einsum('bhs,bhsk->bhk', wk, k)
    return (c_new, n_new, m_new), h


def mlstm_seq(c0, n0, m0, q, k, v, ig, lf):
    bsz, n_heads, t_len, head_dim = q.shape
    blk = min(CHUNK, t_len)
    n_blk = t_len // blk
    def to_blocks(t):
        t = t.reshape(t.shape[:2] + (n_blk, blk) + t.shape[3:])
        return jnp.moveaxis(t, 2, 0)
    (c_t, n_t, m_t), h = lax.scan(mlstm_block_step, (c0, n0, m0),
                                  (to_blocks(q), to_blocks(k), to_blocks(v), to_blocks(ig), to_blocks(lf)))
    h = jnp.moveaxis(h, 0, 2).reshape(bsz, n_heads, t_len, head_dim)
    return h, (c_t, n_t, m_t)


def mem_attention(q, mem_k, mem_v):
    s = jnp.einsum('bthd,bmhd->bhtm', q, mem_k.astype(q.dtype)).astype(jnp.float32) * MEM_HEAD_DIM ** -0.5
    prob = jax.nn.softmax(s, axis=-1).astype(q.dtype)
    return jnp.einsum('bhtm,bmhd->bthd', prob, mem_v.astype(q.dtype))


def moe_ffn(x, p):
    bsz, t_len, d = x.shape
    xt = x.reshape(bsz * t_len, d)
    scores = jax.nn.sigmoid((xt @ p['w_router']).astype(jnp.float32))
    _, idx = lax.top_k(scores + p['router_bias'].astype(jnp.float32), TOP_K)
    sel = jnp.take_along_axis(scores, idx, axis=-1)
    sel = sel / jnp.sum(sel, axis=-1, keepdims=True) * ROUTED_SCALE
    gates = jnp.einsum('nk,nke->ne', sel, jax.nn.one_hot(idx, N_EXPERTS, dtype=jnp.float32)).astype(x.dtype)
    h = jax.nn.silu(jnp.einsum('nd,edf->nef', xt, p['w_e1'])) * jnp.einsum('nd,edf->nef', xt, p['w_e3'])
    routed = jnp.einsum('nef,efd->nd', h * gates[..., None], p['w_e2'])
    shared = (jax.nn.silu(xt @ p['w_s1']) * (xt @ p['w_s3'])) @ p['w_s2']
    return (routed + shared).reshape(bsz, t_len, d)


def trunk_layer(x, mem_k, mem_v, rw_prev, wkv0, conv_prev, c0, n0, m0, p):
    f32 = jnp.float32
    bsz, t_len, _ = x.shape
    proj = x @ p['w_in'] + p['b_in']
    rw = proj[..., :OFF_MLSTM]
    ml = proj[..., OFF_MLSTM:OFF_MEMQ]
    mq = proj[..., OFF_MEMQ:OFF_GATE]
    gate_a, gate_b, gate_c = jnp.split(jax.nn.sigmoid(proj[..., OFF_GATE:]), N_BRANCH, axis=-1)

    rw_full, new_rw_prev = causal_prefix(rw, rw_prev)
    xr = rw + (rw_full[:, :t_len] - rw) * p['rwkv_mu']
    r, k, v, wd, ad, gd = jnp.split(xr, [D_RWKV, 2 * D_RWKV, 3 * D_RWKV, 3 * D_RWKV + W_LORA,
                                         3 * D_RWKV + W_LORA + A_LORA], axis=-1)
    w_log = -jax.nn.softplus(-(p['rwkv_w0'] + jnp.tanh(wd) @ p['rwkv_w2'])) - 0.5
    decay = jnp.exp(-jnp.exp(w_log.astype(f32)))
    a = jax.nn.sigmoid(p['rwkv_a0'] + ad @ p['rwkv_a2'])
    g = jax.nn.sigmoid(gd) @ p['rwkv_g2']
    def heads_a(t):
        return t.astype(f32).reshape(bsz, t_len, RWKV_HEADS, RWKV_HEAD_DIM)
    kk = heads_a(k * p['rwkv_k_k'])
    kk = kk * lax.rsqrt(jnp.maximum(jnp.sum(kk * kk, axis=-1, keepdims=True), 1e-24))
    k = k * (1 + (a - 1) * p['rwkv_k_a'])
    rh, kh, vh, ah = heads_a(r), heads_a(k), heads_a(v), heads_a(a)
    ya, wkv_t = rwkv7_recurrence(wkv0.astype(f32), rh, heads_a(decay), kh, vh, -kk, kk * ah)
    ya = head_norm(ya, p['rwkv_ln_w'], p['rwkv_ln_b'], RWKV_GN_EPS)
    ya = ya + jnp.sum(rh * kh * p['rwkv_r_k'], axis=-1, keepdims=True) * vh
    y_a = (ya.reshape(bsz, t_len, D_RWKV).astype(x.dtype) * g) @ p['w_pa']

    qk_pre, mv, mo, mi, mf = jnp.split(ml, [2 * D_MLSTM, 3 * D_MLSTM, 4 * D_MLSTM,
                                            4 * D_MLSTM + MLSTM_HEADS], axis=-1)
    qk_full, new_conv = causal_prefix(qk_pre, conv_prev)
    qk = p['mlstm_conv_b'] + sum(qk_full[:, j:j + t_len] * p['mlstm_conv_w'][j] for j in range(CONV_W))
    qk = jax.nn.silu(qk)
    def heads_b(t):
        return jnp.moveaxis(t.astype(f32).reshape(bsz, t_len, MLSTM_HEADS, MLSTM_HEAD_DIM), 1, 2)
    qh = heads_b(qk[..., :D_MLSTM])
    kh_m = heads_b(qk[..., D_MLSTM:]) * MLSTM_HEAD_DIM ** -0.5
    vh_m = heads_b(mv)
    ig = jnp.moveaxis((mi + p['mlstm_b_i']).astype(f32), 1, 2)
    lf = jax.nn.log_sigmoid(jnp.moveaxis((mf + p['mlstm_b_f']).astype(f32), 1, 2))
    hb, (c_t, n_t, m_t) = mlstm_seq(c0.astype(f32), n0.astype(f32), m0.astype(f32), qh, kh_m, vh_m, ig, lf)
    hb = jax.nn.sigmoid(mo.astype(f32)).reshape(bsz, t_len, MLSTM_HEADS, MLSTM_HEAD_DIM) * jnp.moveaxis(hb, 1, 2)
    hb = head_norm(hb, p['mlstm_ln_w'], p['mlstm_ln_b'], LN_EPS)
    y_b = hb.reshape(bsz, t_len, D_MLSTM).astype(x.dtype) @ p['w_pb']

    y_c = mem_attention(mq.reshape(bsz, t_len, MEM_HEADS, MEM_HEAD_DIM), mem_k, mem_v)
    y_c = y_c.reshape(bsz, t_len, D_MEM) @ p['w_pc']

    mixed = (gate_a * y_a + gate_b * y_b + gate_c * y_c) @ p['w_out']
    x = layer_norm(DN_ALPHA * x + mixed, p['ln1_w'], p['ln1_b'])
    x = layer_norm(DN_ALPHA * x + moe_ffn(x, p), p['ln2_w'], p['ln2_b'])
    dt = x.dtype
    return x, (new_rw_prev.astype(dt), wkv_t.astype(dt), new_conv.astype(dt),
               c_t.astype(dt), n_t.astype(dt), m_t.astype(dt))


def setup_inputs(seed: int = 0) -> dict:
    key = jax.random.key(seed)
    keys = iter(jax.random.split(key, 64))
    def nrm(shape, scale):
        return jax.random.normal(next(keys), shape, jnp.float32) * scale
    def unif(shape, lo, hi):
        return jax.random.uniform(next(keys), shape, jnp.float32, lo, hi)
    L = DEPTH
    hd = MLSTM_HEAD_DIM
    return {
        'x_prompt': nrm((BATCH, SEQ, D_MODEL), 1.0),
        'x_sample': nrm((DEC_BATCH, DEC_SEQ, D_MODEL), 1.0),
        'state_rwkv_shift': nrm((L, DEC_BATCH, 1, RWKV_PROJ), 1.0),
        'state_rwkv_wkv': nrm((L, DEC_BATCH, RWKV_HEADS, RWKV_HEAD_DIM, RWKV_HEAD_DIM), 0.5),
        'state_mlstm_conv': nrm((L, DEC_BATCH, CONV_W - 1, 2 * D_MLSTM), 1.0),
        'state_mlstm_c': nrm((L, DEC_BATCH, MLSTM_HEADS, hd, hd), 1.0),
        'state_mlstm_n': jnp.abs(nrm((L, DEC_BATCH, MLSTM_HEADS, hd), 1.0)),
        'state_mlstm_m': nrm((L, DEC_BATCH, MLSTM_HEADS), 0.5),
        'cache_mem_k': nrm((L, DEC_BATCH, N_MEM, MEM_HEADS, MEM_HEAD_DIM), 1.0),
        'cache_mem_v': nrm((L, DEC_BATCH, N_MEM, MEM_HEADS, MEM_HEAD_DIM), 1.0),
        'mem_prompt': nrm((BATCH, N_MEM, D_MODEL), 1.0),
        'w_in': nrm((L, D_MODEL, IN_PROJ), D_MODEL ** -0.5),
        'b_in': nrm((L, IN_PROJ), 0.01),
        'rwkv_mu': unif((L, RWKV_PROJ), 0.0, 1.0),
        'rwkv_w0': unif((L, D_RWKV), -6.0, -1.0),
        'rwkv_w2': nrm((L, W_LORA, D_RWKV), 0.1 * W_LORA ** -0.5),
        'rwkv_a0': nrm((L, D_RWKV), 0.1),
        'rwkv_a2': nrm((L, A_LORA, D_RWKV), 0.1 * A_LORA ** -0.5),
        'rwkv_g2': nrm((L, G_LORA, D_RWKV), G_LORA ** -0.5),
        'rwkv_k_k': 0.85 + nrm((L, D_RWKV), 0.05),
        'rwkv_k_a': 1.0 + nrm((L, D_RWKV), 0.05),
        'rwkv_r_k': nrm((L, RWKV_HEADS, RWKV_HEAD_DIM), 0.1),
        'rwkv_ln_w': 1.0 + nrm((L, D_RWKV), 0.02),
        'rwkv_ln_b': nrm((L, D_RWKV), 0.02),
        'mlstm_conv_w': nrm((L, CONV_W, 2 * D_MLSTM), CONV_W ** -0.5),
        'mlstm_conv_b': nrm((L, 2 * D_MLSTM), 0.01),
        'mlstm_b_i': nrm((L, MLSTM_HEADS), 0.1),
        'mlstm_b_f': jnp.linspace(3.0, 6.0, MLSTM_HEADS, dtype=jnp.float32) + nrm((L, MLSTM_HEADS), 0.01),
        'mlstm_ln_w': 1.0 + nrm((L, D_MLSTM), 0.02),
        'mlstm_ln_b': nrm((L, D_MLSTM), 0.02),
        'w_mem_kv': nrm((L, D_MODEL, 2 * D_MEM), D_MODEL ** -0.5),
        'w_pa': nrm((L, D_RWKV, D_MODEL), D_RWKV ** -0.5),
        'w_pb': nrm((L, D_MLSTM, D_MODEL), D_MLSTM ** -0.5),
        'w_pc': nrm((L, D_MEM, D_MODEL), D_MEM ** -0.5),
        'w_out': nrm((L, D_MODEL, D_MODEL), D_MODEL ** -0.5 * DN_BETA),
        'ln1_w': 1.0 + nrm((L, D_MODEL), 0.02),
        'ln1_b': nrm((L, D_MODEL), 0.02),
        'w_router': nrm((L, D_MODEL, N_EXPERTS), D_MODEL ** -0.5),
        'router_bias': nrm((L, N_EXPERTS), 0.01),
        'w_e1': nrm((L, N_EXPERTS, D_MODEL, EXPERT_HIDDEN), D_MODEL ** -0.5),
        'w_e3': nrm((L, N_EXPERTS, D_MODEL, EXPERT_HIDDEN), D_MODEL ** -0.5),
        'w_e2': nrm((L, N_EXPERTS, EXPERT_HIDDEN, D_MODEL), EXPERT_HIDDEN ** -0.5 * DN_BETA),
        'w_s1': nrm((L, D_MODEL, SHARED_HIDDEN), D_MODEL ** -0.5),
        'w_s3': nrm((L, D_MODEL, SHARED_HIDDEN), D_MODEL ** -0.5),
        'w_s2': nrm((L, SHARED_HIDDEN, D_MODEL), SHARED_HIDDEN ** -0.5 * DN_BETA),
        'ln2_w': 1.0 + nrm((L, D_MODEL), 0.02),
        'ln2_b': nrm((L, D_MODEL), 0.02),
    }


def reference(x_prompt, x_sample, state_rwkv_shift, state_rwkv_wkv, state_mlstm_conv, state_mlstm_c,
              state_mlstm_n, state_mlstm_m, cache_mem_k, cache_mem_v, mem_prompt,
              w_in, b_in, rwkv_mu, rwkv_w0, rwkv_w2, rwkv_a0, rwkv_a2, rwkv_g2, rwkv_k_k, rwkv_k_a,
              rwkv_r_k, rwkv_ln_w, rwkv_ln_b, mlstm_conv_w, mlstm_conv_b, mlstm_b_i, mlstm_b_f,
              mlstm_ln_w, mlstm_ln_b, w_mem_kv, w_pa, w_pb, w_pc, w_out, ln1_w, ln1_b,
              w_router, router_bias, w_e1, w_e3, w_e2, w_s1, w_s3, w_s2, ln2_w, ln2_b):
    weights = dict(w_in=w_in, b_in=b_in, rwkv_mu=rwkv_mu, rwkv_w0=rwkv_w0, rwkv_w2=rwkv_w2,
                   rwkv_a0=rwkv_a0, rwkv_a2=rwkv_a2, rwkv_g2=rwkv_g2, rwkv_k_k=rwkv_k_k,
                   rwkv_k_a=rwkv_k_a, rwkv_r_k=rwkv_r_k, rwkv_ln_w=rwkv_ln_w, rwkv_ln_b=rwkv_ln_b,
                   mlstm_conv_w=mlstm_conv_w, mlstm_conv_b=mlstm_conv_b, mlstm_b_i=mlstm_b_i,
                   mlstm_b_f=mlstm_b_f, mlstm_ln_w=mlstm_ln_w, mlstm_ln_b=mlstm_ln_b,
                   w_mem_kv=w_mem_kv, w_pa=w_pa, w_pb=w_pb, w_pc=w_pc, w_out=w_out,
                   ln1_w=ln1_w, ln1_b=ln1_b, w_router=w_router, router_bias=router_bias,
                   w_e1=w_e1, w_e3=w_e3, w_e2=w_e2, w_s1=w_s1, w_s3=w_s3, w_s2=w_s2,
                   ln2_w=ln2_w, ln2_b=ln2_b)
    bp = x_prompt.shape[0]
    dt = x_prompt.dtype
    f32 = jnp.float32
    y_p, y_s = x_prompt, x_sample
    st_prompt, st_sample, mem_ks, mem_vs = [], [], [], []
    for layer in range(DEPTH):
        p = {name: w[layer] for name, w in weights.items()}
        kv = mem_prompt @ p['w_mem_kv']
        mk = kv[..., :D_MEM].reshape(bp, N_MEM, MEM_HEADS, MEM_HEAD_DIM)
        mv = kv[..., D_MEM:].reshape(bp, N_MEM, MEM_HEADS, MEM_HEAD_DIM)
        y_p, st_p = trunk_layer(
            y_p, mk, mv,
            jnp.zeros((bp, 1, RWKV_PROJ), dt),
            jnp.zeros((bp, RWKV_HEADS, RWKV_HEAD_DIM, RWKV_HEAD_DIM), f32),
            jnp.zeros((bp, CONV_W - 1, 2 * D_MLSTM), dt),
            jnp.zeros((bp, MLSTM_HEADS, MLSTM_HEAD_DIM, MLSTM_HEAD_DIM), f32),
            jnp.zeros((bp, MLSTM_HEADS, MLSTM_HEAD_DIM), f32),
            jnp.zeros((bp, MLSTM_HEADS), f32),
            p)
        y_s, st_s = trunk_layer(
            y_s, cache_mem_k[layer], cache_mem_v[layer],
            state_rwkv_shift[layer], state_rwkv_wkv[layer], state_mlstm_conv[layer],
            state_mlstm_c[layer], state_mlstm_n[layer], state_mlstm_m[layer],
            p)
        st_prompt.append(st_p)
        st_sample.append(st_s)
        mem_ks.append(mk)
        mem_vs.append(mv)
    shift_p = jnp.stack([s[0] for s in st_prompt])
    wkv_p = jnp.stack([s[1] for s in st_prompt])
    conv_p = jnp.stack([s[2] for s in st_prompt])
    c_p = jnp.stack([s[3] for s in st_prompt])
    n_p = jnp.stack([s[4] for s in st_prompt])
    m_p = jnp.stack([s[5] for s in st_prompt])
    mem_k_p = jnp.stack(mem_ks)
    mem_v_p = jnp.stack(mem_vs)
    shift_s = jnp.stack([s[0] for s in st_sample])
    wkv_s = jnp.stack([s[1] for s in st_sample])
    conv_s = jnp.stack([s[2] for s in st_sample])
    c_s = jnp.stack([s[3] for s in st_sample])
    n_s = jnp.stack([s[4] for s in st_sample])
    m_s = jnp.stack([s[5] for s in st_sample])
    return (y_p, y_s, shift_p, wkv_p, conv_p, c_p, n_p, m_p, mem_k_p, mem_v_p,
            shift_s, wkv_s, conv_s, c_s, n_s, m_s)
```

```python
import functools

import jax
import jax.numpy as jnp
from jax import lax
from jax.experimental import pallas as pl
from jax.experimental.pallas import tpu as pltpu

F32 = jnp.float32
BF16 = jnp.bfloat16
HIGHEST = lax.Precision.HIGHEST

D_MODEL = 1024
DEPTH = 1
D_RWKV = 1024
RWKV_HEAD_DIM = 64
RWKV_HEADS = 16
W_LORA = 64
A_LORA = 64
G_LORA = 128
LORA_ALL = W_LORA + A_LORA + G_LORA
RWKV_PROJ = 3 * D_RWKV + LORA_ALL
RWKV_GN_EPS = 64e-5
D_MLSTM = 1024
MLSTM_HEADS = 4
MLSTM_HEAD_DIM = 256
CONV_W = 4
MLSTM_CHUNK = 64
N_MEM = 256
MEM_HEADS = 4
MEM_HEAD_DIM = 256
D_MEM = 1024
N_EXPERTS = 32
TOP_K = 4
EXPERT_HIDDEN = 128
ROUTED_SCALE = 2.5
DN_ALPHA = (2 * DEPTH) ** 0.25
LN_EPS = 1e-5

LANES = 128
RWKV_PAIRS = D_RWKV // LANES
ML_COLS = 4 * D_MLSTM + LANES
TAIL_COLS = D_MEM + 3 * D_MODEL
VMEM_LIMIT = 56 * 1024 * 1024


def _dot(a, b, precision=None):
    return jnp.dot(a, b, preferred_element_type=F32, precision=precision)


def _dot_nt(a, b, precision=None):
    return lax.dot_general(a, b, (((1,), (1,)), ((), ())), preferred_element_type=F32, precision=precision)


def _dot_tn(a, b, precision=None):
    return lax.dot_general(a, b, (((0,), (0,)), ((), ())), preferred_element_type=F32, precision=precision)


def _bdot(a, b):
    return _dot(a.astype(BF16), b.astype(BF16))


def _bdot_nt(a, b):
    return _dot_nt(a.astype(BF16), b.astype(BF16))


def _sigmoid(x):
    return 1.0 / (1.0 + jnp.exp(-x))


def _softplus(x):
    return jnp.maximum(x, 0.0) + jnp.log(1.0 + jnp.exp(-jnp.abs(x)))


def _silu(x):
    return x * _sigmoid(x)


def _layer_norm(x, w, b):
    mu = jnp.mean(x, axis=-1, keepdims=True)
    d = x - mu
    var = jnp.mean(d * d, axis=-1, keepdims=True)
    return d * lax.rsqrt(var + LN_EPS) * w + b


def _params(semantics):
    return pltpu.CompilerParams(dimension_semantics=semantics, vmem_limit_bytes=VMEM_LIMIT)


def _const_spec(shape):
    zeros = (0,) * len(shape)
    return pl.BlockSpec(shape, lambda *_: zeros, pipeline_mode=pl.Buffered(1))


def _mm_kernel(x_ref, w_ref, b_ref, o_ref):
    o_ref[...] = _dot(x_ref[...], w_ref[...]) + b_ref[...]


def _matmul_bias(x, w, b, tm, tn):
    n, k = x.shape
    nc = w.shape[1]
    return pl.pallas_call(
        _mm_kernel,
        grid=(nc // tn, n // tm),
        in_specs=[
            pl.BlockSpec((tm, k), lambda j, i: (i, 0)),
            pl.BlockSpec((k, tn), lambda j, i: (0, j)),
            pl.BlockSpec((1, tn), lambda j, i: (0, j)),
        ],
        out_specs=pl.BlockSpec((tm, tn), lambda j, i: (i, j)),
        out_shape=jax.ShapeDtypeStruct((n, nc), F32),
        compiler_params=_params(("arbitrary", "arbitrary")),
        name="matmul_bias",
    )(x, w, b)


def _rwkv_kernel(p_ref, prev_ref, h0_ref, mu_ref, w0_ref, a0_ref, kk_ref, ka_ref, rk_ref, lnw_ref, lnb_ref,
                 wl_ref, y_ref, hout_ref, h_scr, xbuf, *, chunk, n_chunks):
    c_idx = pl.program_id(1)
    L = chunk

    @pl.when(c_idx == 0)
    def _():
        h_scr[...] = h0_ref[0]
        xbuf[7:8, :] = prev_ref[0]

    rw = p_ref[...]
    xbuf[8:8 + L, :] = rw
    prev_rows = xbuf[7:7 + L, :]
    xbuf[7:8, :] = rw[L - 1:L, :]

    xr = rw + (prev_rows - rw) * mu_ref[...]
    r_all = xr[:, 0:D_RWKV]
    k_all = xr[:, D_RWKV:2 * D_RWKV]
    v_all = xr[:, 2 * D_RWKV:3 * D_RWKV]
    slab = xr[:, 3 * D_RWKV:]
    lane_l = lax.broadcasted_iota(jnp.int32, slab.shape, 1)
    act = jnp.where(lane_l < W_LORA, jnp.tanh(slab), jnp.where(lane_l < W_LORA + A_LORA, slab, _sigmoid(slab)))
    lora = _bdot(act, wl_ref[...])
    w_log = -_softplus(-(w0_ref[...] + lora[:, 0:D_RWKV])) - 0.5
    lw_all = -jnp.exp(w_log)
    a_sig = _sigmoid(a0_ref[...] + lora[:, D_RWKV:2 * D_RWKV])
    g_all = lora[:, 2 * D_RWKV:]
    kk_all = k_all * kk_ref[...]
    k2_all = k_all * (1.0 + (a_sig - 1.0) * ka_ref[...])

    ti = lax.broadcasted_iota(jnp.int32, (L, L), 0)
    tj = lax.broadcasted_iota(jnp.int32, (L, L), 1)
    ltri = (tj <= ti).astype(F32)
    cum_all = _dot(ltri, lw_all, HIGHEST)

    ri = lax.broadcasted_iota(jnp.int32, (2 * L, 2 * L), 0)
    ci = lax.broadcasted_iota(jnp.int32, (2 * L, 2 * L), 1)
    same = (ri >= L) == (ci >= L)
    strict = same & (ci < ri)
    incl = same & (ci <= ri)
    eye = (ri == ci).astype(F32)
    gi = lax.broadcasted_iota(jnp.int32, (LANES, LANES), 0)
    gj = lax.broadcasted_iota(jnp.int32, (LANES, LANES), 1)
    head_ones = ((gi >= RWKV_HEAD_DIM) == (gj >= RWKV_HEAD_DIM)).astype(F32)
    first_head = lax.broadcasted_iota(jnp.int32, (L, LANES), 1) < RWKV_HEAD_DIM

    def stack(x):
        return jnp.concatenate([jnp.where(first_head, x, 0.0), jnp.where(first_head, 0.0, x)], axis=0)

    def seg_sum(x):
        return _dot(x, head_ones, HIGHEST)

    n_sq = L.bit_length() - 1
    for p in range(RWKV_PAIRS):
        sl = slice(p * LANES, (p + 1) * LANES)
        r, k2, v, kk0, asg, lw, cum = (t[:, sl] for t in (r_all, k2_all, v_all, kk_all, a_sig, lw_all, cum_all))
        kk = kk0 * lax.rsqrt(jnp.maximum(seg_sum(kk0 * kk0), 1e-24))
        a_vec = -kk
        b_vec = kk * asg
        c_last = cum[L - 1:L, :]
        e_pos = jnp.exp(cum)
        e_neg = jnp.exp(-cum)
        e_prev = jnp.exp(cum - lw)
        e_end = jnp.exp(c_last - cum)
        a_s = stack(a_vec * e_prev)
        r_s = stack(r * e_pos)
        b_s = stack(b_vec * e_neg)
        k_s = stack(k2 * e_neg)
        bh_s = stack(b_vec * e_end)
        kh_s = stack(k2 * e_end)
        v_s = stack(v)
        h_bd = h_scr[p]
        m_ab = jnp.where(strict, _dot_nt(a_s, b_s, HIGHEST), 0.0)
        m_ak = jnp.where(strict, _dot_nt(a_s, k_s, HIGHEST), 0.0)
        n_rb = jnp.where(incl, _dot_nt(r_s, b_s, HIGHEST), 0.0)
        n_rk = jnp.where(incl, _dot_nt(r_s, k_s, HIGHEST), 0.0)
        w0 = _dot_nt(a_s, h_bd, HIGHEST) + _dot(m_ak, v_s, HIGHEST)
        inv = eye + m_ab
        pw = _dot(m_ab, m_ab, HIGHEST)
        for i in range(n_sq - 1):
            inv = inv + _dot(inv, pw, HIGHEST)
            if i + 1 < n_sq - 1:
                pw = _dot(pw, pw, HIGHEST)
        c_s = _dot(inv, w0, HIGHEST)
        y_s = _dot_nt(r_s, h_bd, HIGHEST) + _dot(n_rb, c_s, HIGHEST) + _dot(n_rk, v_s, HIGHEST)
        y = y_s[0:L] + y_s[L:2 * L]
        h_scr[p] = h_bd * jnp.exp(c_last) + _dot_tn(c_s, bh_s, HIGHEST) + _dot_tn(v_s, kh_s, HIGHEST)

        mean = seg_sum(y) * (1.0 / RWKV_HEAD_DIM)
        d = y - mean
        var = seg_sum(d * d) * (1.0 / RWKV_HEAD_DIM)
        yn = d * lax.rsqrt(var + RWKV_GN_EPS) * lnw_ref[:, sl] + lnb_ref[:, sl]
        bonus = seg_sum(r * k2 * rk_ref[:, sl]) * v
        y_ref[:, sl] = (yn + bonus) * g_all[:, sl]

    @pl.when(c_idx == n_chunks - 1)
    def _():
        hout_ref[0] = h_scr[...]


def _rwkv(p_rw, shift_prev, h0_bd, w, batch, t_len, chunk):
    n_chunks = t_len // chunk
    row = lambda b, c: (b * n_chunks + c, 0)
    vec = lambda n: pl.BlockSpec((1, n), lambda b, c: (0, 0))
    kern = functools.partial(_rwkv_kernel, chunk=chunk, n_chunks=n_chunks)
    return pl.pallas_call(
        kern,
        grid=(batch, n_chunks),
        in_specs=[
            pl.BlockSpec((chunk, RWKV_PROJ), row),
            pl.BlockSpec((1, 1, RWKV_PROJ), lambda b, c: (b, 0, 0)),
            pl.BlockSpec((1, RWKV_PAIRS, LANES, LANES), lambda b, c: (b, 0, 0, 0)),
            vec(RWKV_PROJ), vec(D_RWKV), vec(D_RWKV), vec(D_RWKV), vec(D_RWKV), vec(D_RWKV), vec(D_RWKV),
            vec(D_RWKV),
            pl.BlockSpec((LORA_ALL, 3 * D_RWKV), lambda b, c: (0, 0)),
        ],
        out_specs=[
            pl.BlockSpec((chunk, D_RWKV), row),
            pl.BlockSpec((1, RWKV_PAIRS, LANES, LANES), lambda b, c: (b, 0, 0, 0)),
        ],
        out_shape=[
            jax.ShapeDtypeStruct((batch * t_len, D_RWKV), F32),
            jax.ShapeDtypeStruct((batch, RWKV_PAIRS, LANES, LANES), F32),
        ],
        scratch_shapes=[
            pltpu.VMEM((RWKV_PAIRS, LANES, LANES), F32),
            pltpu.VMEM((chunk + 8, RWKV_PROJ), F32),
        ],
        compiler_params=_params(("arbitrary", "arbitrary")),
        name="rwkv7_chunked",
    )(p_rw, shift_prev, h0_bd, w["mu"], w["w0"], w["a0"], w["k_k"], w["k_a"], w["r_k"], w["rw_ln_w"],
      w["rw_ln_b"], w["w_lora"])


def _mlstm_kernel(p_ref, conv_prev_ref, c0_ref, n0_ref, m0_ref, cw_ref, cb_ref, bif_ref, lnw_ref, lnb_ref,
                  h_ref, cout_ref, nout_ref, mout_ref, c_scr, n_scr, m_scr, xbuf, *, chunk, n_chunks):
    c_idx = pl.program_id(1)
    L = chunk
    d2 = 2 * D_MLSTM

    @pl.when(c_idx == 0)
    def _():
        c_scr[...] = c0_ref[0]
        n_scr[...] = n0_ref[0]
        m_scr[...] = m0_ref[0]
        xbuf[5:8, :] = conv_prev_ref[0]

    x = p_ref[:, 0:d2]
    xbuf[8:8 + L, :] = x
    conv = cb_ref[...] + cw_ref[3:4, :] * x
    for j in range(CONV_W - 1):
        conv = conv + cw_ref[j:j + 1, :] * xbuf[5 + j:5 + j + L, :]
    xbuf[5:8, :] = x[L - 3:L, :]
    qk = _silu(conv)
    q_all = qk[:, 0:D_MLSTM]
    k_all = qk[:, D_MLSTM:d2] * (MLSTM_HEAD_DIM ** -0.5)
    v_all = p_ref[:, d2:d2 + D_MLSTM]
    o_all = _sigmoid(p_ref[:, d2 + D_MLSTM:d2 + 2 * D_MLSTM])
    gates = p_ref[:, 4 * D_MLSTM:] + bif_ref[...]
    lane_g = lax.broadcasted_iota(jnp.int32, gates.shape, 1)
    is_f = (lane_g >= MLSTM_HEADS) & (lane_g < 2 * MLSTM_HEADS)
    glog = jnp.where(is_f, -_softplus(-gates), gates)

    ti = lax.broadcasted_iota(jnp.int32, (L, L), 0)
    tj = lax.broadcasted_iota(jnp.int32, (L, L), 1)
    causal = tj <= ti
    ltri = causal.astype(F32)
    utri = (ti <= tj).astype(F32)
    b_cols = _dot(ltri, glog, HIGHEST)
    sel = (lax.broadcasted_iota(jnp.int32, (8, LANES), 0) == lax.broadcasted_iota(jnp.int32, (8, LANES), 1))
    g_rows = _dot_nt(sel.astype(F32), glog, HIGHEST)
    b_rows = _dot(g_rows, utri, HIGHEST)

    for h in range(MLSTM_HEADS):
        sl = slice(h * MLSTM_HEAD_DIM, (h + 1) * MLSTM_HEAD_DIM)
        q, k, v = q_all[:, sl], k_all[:, sl], v_all[:, sl]
        ig_row = g_rows[h:h + 1, :]
        b_row = b_rows[MLSTM_HEADS + h:MLSTM_HEADS + h + 1, :]
        ig_col = glog[:, h:h + 1]
        b_col = b_cols[:, MLSTM_HEADS + h:MLSTM_HEADS + h + 1]
        m_prev = m_scr[:, h:h + 1]
        c_st = c_scr[h]
        n_st = n_scr[h:h + 1, :]

        log_w = jnp.where(causal, b_col - b_row + ig_row, -jnp.inf)
        inter = b_col + m_prev
        m_t = jnp.maximum(inter, jnp.max(log_w, axis=-1, keepdims=True))
        w_intra = jnp.exp(log_w - m_t)
        w_inter = jnp.exp(inter - m_t)
        s = _bdot_nt(q, k) * w_intra
        num = w_inter * _bdot_nt(q, c_st) + _bdot(s, v)
        den = w_inter * jnp.sum(q * n_st, axis=-1, keepdims=True) + jnp.sum(s, axis=-1, keepdims=True)
        hh = num / jnp.maximum(jnp.abs(den), jnp.exp(-m_t))

        b_last = b_col[L - 1:L, :]
        g_col = b_last - b_col + ig_col
        m_new = jnp.maximum(b_last + m_prev, jnp.max(g_col, axis=0, keepdims=True))
        carry = jnp.exp(b_last + m_prev - m_new)
        wk = jnp.exp(g_col - m_new)
        c_scr[h] = carry * c_st + _dot_tn((wk * v).astype(BF16), k.astype(BF16))
        n_scr[h:h + 1, :] = carry * n_st + jnp.sum(wk * k, axis=0, keepdims=True)
        m_scr[:, h:h + 1] = m_new

        hb = o_all[:, sl] * hh
        h_ref[:, sl] = _layer_norm(hb, lnw_ref[:, sl], lnb_ref[:, sl])

    @pl.when(c_idx == n_chunks - 1)
    def _():
        cout_ref[0] = c_scr[...]
        nout_ref[0] = n_scr[...]
        mout_ref[0] = m_scr[...]


def _mlstm(p_ml, conv_prev, c0, n0, m0, w, batch, t_len, chunk):
    n_chunks = t_len // chunk
    row = lambda b, c: (b * n_chunks + c, 0)
    vec = lambda n: pl.BlockSpec((1, n), lambda b, c: (0, 0))
    hd = MLSTM_HEAD_DIM
    kern = functools.partial(_mlstm_kernel, chunk=chunk, n_chunks=n_chunks)
    state_specs = [
        pl.BlockSpec((1, MLSTM_HEADS, hd, hd), lambda b, c: (b, 0, 0, 0)),
        pl.BlockSpec((1, MLSTM_HEADS, hd), lambda b, c: (b, 0, 0)),
        pl.BlockSpec((1, 1, MLSTM_HEADS), lambda b, c: (b, 0, 0)),
    ]
    return pl.pallas_call(
        kern,
        grid=(batch, n_chunks),
        in_specs=[
            pl.BlockSpec((chunk, ML_COLS), row),
            pl.BlockSpec((1, CONV_W - 1, 2 * D_MLSTM), lambda b, c: (b, 0, 0)),
            *state_specs,
            pl.BlockSpec((CONV_W, 2 * D_MLSTM), lambda b, c: (0, 0)),
            vec(2 * D_MLSTM), vec(LANES), vec(D_MLSTM), vec(D_MLSTM),
        ],
        out_specs=[pl.BlockSpec((chunk, D_MLSTM), row), *state_specs],
        out_shape=[
            jax.ShapeDtypeStruct((batch * t_len, D_MLSTM), F32),
            jax.ShapeDtypeStruct((batch, MLSTM_HEADS, hd, hd), F32),
            jax.ShapeDtypeStruct((batch, MLSTM_HEADS, hd), F32),
            jax.ShapeDtypeStruct((batch, 1, MLSTM_HEADS), F32),
        ],
        scratch_shapes=[
            pltpu.VMEM((MLSTM_HEADS, hd, hd), F32),
            pltpu.VMEM((MLSTM_HEADS, hd), F32),
            pltpu.VMEM((1, MLSTM_HEADS), F32),
            pltpu.VMEM((chunk + 8, 2 * D_MLSTM), F32),
        ],
        compiler_params=_params(("arbitrary", "arbitrary")),
        name="mlstm_chunkwise",
    )(p_ml, conv_prev, c0, n0, m0, w["conv_w"], w["conv_b"], w["b_if"], w["ml_ln_w"], w["ml_ln_b"])


def _tail_kernel(x_ref, ya_ref, hb_ref, pt_ref, mk_ref, mv_ref, wpa_ref, wpb_ref, wpc_ref, wout_ref,
                 lnw_ref, lnb_ref, o_ref):
    mk = mk_ref[0].astype(BF16)
    mv = mv_ref[0].astype(BF16)
    heads = []
    for h in range(MEM_HEADS):
        sl = slice(h * MEM_HEAD_DIM, (h + 1) * MEM_HEAD_DIM)
        s = _dot_nt(pt_ref[:, sl].astype(BF16), mk[:, sl]) * (MEM_HEAD_DIM ** -0.5)
        s = s - jnp.max(s, axis=-1, keepdims=True)
        e = jnp.exp(s)
        prob = e / jnp.sum(e, axis=-1, keepdims=True)
        heads.append(_dot(prob.astype(BF16), mv[:, sl]))
    attn = jnp.concatenate(heads, axis=-1)
    y_a = _bdot(ya_ref[...], wpa_ref[...])
    y_b = _bdot(hb_ref[...], wpb_ref[...])
    y_c = _bdot(attn, wpc_ref[...])
    d = D_MODEL
    mixed = (_sigmoid(pt_ref[:, D_MEM:D_MEM + d]) * y_a + _sigmoid(pt_ref[:, D_MEM + d:D_MEM + 2 * d]) * y_b
             + _sigmoid(pt_ref[:, D_MEM + 2 * d:D_MEM + 3 * d]) * y_c)
    u = _bdot(mixed, wout_ref[...])
    o_ref[...] = _layer_norm(DN_ALPHA * x_ref[...] + u, lnw_ref[...], lnb_ref[...])


def _tail(x, ya, hb, p_tail, mem_k, mem_v, w, tm, tiles_per_batch):
    n = x.shape[0]
    row = lambda i: (i, 0)
    mem = lambda i: (i // tiles_per_batch, 0, 0)
    return pl.pallas_call(
        _tail_kernel,
        grid=(n // tm,),
        in_specs=[
            pl.BlockSpec((tm, D_MODEL), row), pl.BlockSpec((tm, D_RWKV), row), pl.BlockSpec((tm, D_MLSTM), row),
            pl.BlockSpec((tm, TAIL_COLS), row),
            pl.BlockSpec((1, N_MEM, D_MEM), mem), pl.BlockSpec((1, N_MEM, D_MEM), mem),
            _const_spec((D_RWKV, D_MODEL)), _const_spec((D_MLSTM, D_MODEL)), _const_spec((D_MEM, D_MODEL)),
            _const_spec((D_MODEL, D_MODEL)), _const_spec((1, D_MODEL)), _const_spec((1, D_MODEL)),
        ],
        out_specs=pl.BlockSpec((tm, D_MODEL), row),
        out_shape=jax.ShapeDtypeStruct((n, D_MODEL), F32),
        compiler_params=_params(("arbitrary",)),
        name="attn_merge_ln",
    )(x, ya, hb, p_tail, mem_k, mem_v, w["w_pa"], w["w_pb"], w["w_pc"], w["w_out"], w["ln1_w"], w["ln1_b"])


def _moe_kernel(x_ref, wr_ref, rb_ref, w1_ref, w3_ref, w2_ref, ws1_ref, ws3_ref, ws2_ref, lnw_ref, lnb_ref, o_ref):
    x = x_ref[...]
    xb = x.astype(BF16)
    tm = x.shape[0]
    scores = _sigmoid(_dot(xb, wr_ref[...]))
    lane = lax.broadcasted_iota(jnp.int32, scores.shape, 1)
    work = jnp.where(lane < N_EXPERTS, scores + rb_ref[...], -jnp.inf)
    chosen = jnp.zeros(scores.shape, dtype=jnp.bool_)
    for _ in range(TOP_K):
        best = jnp.max(work, axis=-1, keepdims=True)
        first = jnp.min(jnp.where(work == best, lane, LANES), axis=-1, keepdims=True)
        pick = lane == first
        chosen = chosen | pick
        work = jnp.where(pick, -jnp.inf, work)
    sel = jnp.where(chosen, scores, 0.0)
    gates = sel / jnp.sum(sel, axis=-1, keepdims=True) * ROUTED_SCALE

    acc = _dot((_silu(_dot(xb, ws1_ref[...])) * _dot(xb, ws3_ref[...])).astype(BF16), ws2_ref[...])
    group = 8
    width = group * EXPERT_HIDDEN
    for s in range(N_EXPERTS // group):
        cols = slice(s * width, (s + 1) * width)
        hidden = _silu(_dot(xb, w1_ref[:, cols])) * _dot(xb, w3_ref[:, cols])
        gated = jnp.concatenate(
            [hidden[:, e * EXPERT_HIDDEN:(e + 1) * EXPERT_HIDDEN] * gates[:, s * group + e:s * group + e + 1]
             for e in range(group)], axis=-1)
        acc = acc + _dot(gated.astype(BF16), w2_ref[cols, :])
    del tm
    o_ref[...] = _layer_norm(DN_ALPHA * x + acc, lnw_ref[...], lnb_ref[...])


def _moe(x, w, tm):
    n = x.shape[0]
    row = lambda i: (i, 0)
    hid = N_EXPERTS * EXPERT_HIDDEN
    return pl.pallas_call(
        _moe_kernel,
        grid=(n // tm,),
        in_specs=[
            pl.BlockSpec((tm, D_MODEL), row),
            _const_spec((D_MODEL, LANES)), _const_spec((1, LANES)),
            _const_spec((D_MODEL, hid)), _const_spec((D_MODEL, hid)), _const_spec((hid, D_MODEL)),
            _const_spec((D_MODEL, EXPERT_HIDDEN)), _const_spec((D_MODEL, EXPERT_HIDDEN)),
            _const_spec((EXPERT_HIDDEN, D_MODEL)),
            _const_spec((1, D_MODEL)), _const_spec((1, D_MODEL)),
        ],
        out_specs=pl.BlockSpec((tm, D_MODEL), row),
        out_shape=jax.ShapeDtypeStruct((n, D_MODEL), F32),
        compiler_params=_params(("arbitrary",)),
        name="moe_ln",
    )(x, w["w_router"], w["router_bias"], w["w_e1"], w["w_e3"], w["w_e2"], w["w_s1"], w["w_s3"], w["w_s2"],
      w["ln2_w"], w["ln2_b"])


def _prep_weights(w_in, b_in, rwkv_mu, rwkv_w0, rwkv_w2, rwkv_a0, rwkv_a2, rwkv_g2, rwkv_k_k, rwkv_k_a,
                  rwkv_r_k, rwkv_ln_w, rwkv_ln_b, mlstm_conv_w, mlstm_conv_b, mlstm_b_i, mlstm_b_f,
                  mlstm_ln_w, mlstm_ln_b, w_mem_kv, w_pa, w_pb, w_pc, w_out, ln1_w, ln1_b,
                  w_router, router_bias, w_e1, w_e3, w_e2, w_s1, w_s3, w_s2, ln2_w, ln2_b):
    off_ml = RWKV_PROJ
    off_if = off_ml + 4 * D_MLSTM
    off_mq = off_if + 2 * MLSTM_HEADS
    pad_if = LANES - 2 * MLSTM_HEADS
    row = lambda t: t.reshape(1, -1)
    hid = N_EXPERTS * EXPERT_HIDDEN
    w_lora = jnp.zeros((LORA_ALL, 3 * D_RWKV), F32)
    w_lora = w_lora.at[0:W_LORA, 0:D_RWKV].set(rwkv_w2)
    w_lora = w_lora.at[W_LORA:W_LORA + A_LORA, D_RWKV:2 * D_RWKV].set(rwkv_a2)
    w_lora = w_lora.at[W_LORA + A_LORA:, 2 * D_RWKV:].set(rwkv_g2)
    return dict(
        w_rw=w_in[:, :off_ml].astype(BF16), b_rw=row(b_in[:off_ml]),
        w_ml=jnp.pad(w_in[:, off_ml:off_mq], ((0, 0), (0, pad_if))).astype(BF16),
        b_ml=row(jnp.pad(b_in[off_ml:off_mq], (0, pad_if))),
        w_tail=w_in[:, off_mq:].astype(BF16), b_tail=row(b_in[off_mq:]),
        w_mem_k=w_mem_kv[:, :D_MEM].astype(BF16), w_mem_v=w_mem_kv[:, D_MEM:].astype(BF16),
        mu=row(rwkv_mu), w0=row(rwkv_w0), a0=row(rwkv_a0), k_k=row(rwkv_k_k), k_a=row(rwkv_k_a),
        r_k=row(rwkv_r_k), rw_ln_w=row(rwkv_ln_w), rw_ln_b=row(rwkv_ln_b), w_lora=w_lora.astype(BF16),
        conv_w=mlstm_conv_w, conv_b=row(mlstm_conv_b),
        b_if=row(jnp.pad(jnp.concatenate([mlstm_b_i, mlstm_b_f]), (0, pad_if))),
        ml_ln_w=row(mlstm_ln_w), ml_ln_b=row(mlstm_ln_b),
        w_pa=w_pa.astype(BF16), w_pb=w_pb.astype(BF16), w_pc=w_pc.astype(BF16), w_out=w_out.astype(BF16),
        ln1_w=row(ln1_w), ln1_b=row(ln1_b),
        w_router=jnp.pad(w_router, ((0, 0), (0, LANES - N_EXPERTS))).astype(BF16),
        router_bias=row(jnp.pad(router_bias, (0, LANES - N_EXPERTS))),
        w_e1=jnp.transpose(w_e1, (1, 0, 2)).reshape(D_MODEL, hid).astype(BF16),
        w_e3=jnp.transpose(w_e3, (1, 0, 2)).reshape(D_MODEL, hid).astype(BF16),
        w_e2=w_e2.reshape(hid, D_MODEL).astype(BF16),
        w_s1=w_s1.astype(BF16), w_s3=w_s3.astype(BF16), w_s2=w_s2.astype(BF16),
        ln2_w=row(ln2_w), ln2_b=row(ln2_b),
    )


def _pair_state(wkv):
    bsz = wkv.shape[0]
    s = wkv.reshape(bsz, RWKV_PAIRS, 2, RWKV_HEAD_DIM, RWKV_HEAD_DIM)
    z = jnp.zeros_like(s[:, :, 0])
    top = jnp.concatenate([s[:, :, 0], z], axis=-1)
    bot = jnp.concatenate([z, s[:, :, 1]], axis=-1)
    return jnp.concatenate([top, bot], axis=-2)


def _unpair_state(h_bd):
    bsz = h_bd.shape[0]
    hd = RWKV_HEAD_DIM
    heads = jnp.stack([h_bd[:, :, :hd, :hd], h_bd[:, :, hd:, hd:]], axis=2)
    return heads.reshape(bsz, RWKV_HEADS, hd, hd)


def _trunk(x3, mem_k, mem_v, shift_prev, wkv0, conv_prev, c0, n0, m0, w, chunk_rw, chunk_ml, tm_proj, tm_tail,
           tm_moe):
    bsz, t_len, _ = x3.shape
    n = bsz * t_len
    x = x3.reshape(n, D_MODEL)
    xb = x.astype(BF16)
    p_rw = _matmul_bias(xb, w["w_rw"], w["b_rw"], tm_proj, RWKV_PROJ // 2)
    p_ml = _matmul_bias(xb, w["w_ml"], w["b_ml"], tm_proj, ML_COLS // 3)
    p_tail = _matmul_bias(xb, w["w_tail"], w["b_tail"], tm_proj, TAIL_COLS // 2)

    ya, h_bd = _rwkv(p_rw, shift_prev, _pair_state(wkv0), w, bsz, t_len, chunk_rw)
    hb, c_t, n_t, m_t = _mlstm(p_ml, conv_prev, c0, n0, m0.reshape(bsz, 1, MLSTM_HEADS), w, bsz, t_len, chunk_ml)
    x1 = _tail(x, ya, hb, p_tail, mem_k, mem_v, w, tm_tail, t_len // tm_tail)
    y = _moe(x1, w, tm_moe)

    p_rw3 = p_rw.reshape(bsz, t_len, RWKV_PROJ)
    p_ml3 = p_ml.reshape(bsz, t_len, ML_COLS)
    new_shift = p_rw3[:, t_len - 1:, :]
    new_conv = p_ml3[:, t_len - (CONV_W - 1):, :2 * D_MLSTM]
    states = (new_shift, _unpair_state(h_bd), new_conv, c_t, n_t, m_t.reshape(bsz, MLSTM_HEADS))
    return y.reshape(bsz, t_len, D_MODEL), states


def kernel(x_prompt, x_sample, state_rwkv_shift, state_rwkv_wkv, state_mlstm_conv, state_mlstm_c, state_mlstm_n, state_mlstm_m, cache_mem_k, cache_mem_v, mem_prompt, w_in, b_in, rwkv_mu, rwkv_w0, rwkv_w2, rwkv_a0, rwkv_a2, rwkv_g2, rwkv_k_k, rwkv_k_a, rwkv_r_k, rwkv_ln_w, rwkv_ln_b, mlstm_conv_w, mlstm_conv_b, mlstm_b_i, mlstm_b_f, mlstm_ln_w, mlstm_ln_b, w_mem_kv, w_pa, w_pb, w_pc, w_out, ln1_w, ln1_b, w_router, router_bias, w_e1, w_e3, w_e2, w_s1, w_s3, w_s2, ln2_w, ln2_b):
    weights = (w_in, b_in, rwkv_mu, rwkv_w0, rwkv_w2, rwkv_a0, rwkv_a2, rwkv_g2, rwkv_k_k, rwkv_k_a, rwkv_r_k,
               rwkv_ln_w, rwkv_ln_b, mlstm_conv_w, mlstm_conv_b, mlstm_b_i, mlstm_b_f, mlstm_ln_w, mlstm_ln_b,
               w_mem_kv, w_pa, w_pb, w_pc, w_out, ln1_w, ln1_b, w_router, router_bias, w_e1, w_e3, w_e2,
               w_s1, w_s3, w_s2, ln2_w, ln2_b)
    w = _prep_weights(*(t[0] for t in weights))
    bp, t_p, _ = x_prompt.shape
    bs, t_s, _ = x_sample.shape
    hd = MLSTM_HEAD_DIM

    mem_flat = mem_prompt.reshape(bp * N_MEM, D_MODEL).astype(BF16)
    zero_bias = jnp.zeros((1, D_MEM), F32)
    mk = _matmul_bias(mem_flat, w["w_mem_k"], zero_bias, bp * N_MEM, D_MEM).reshape(bp, N_MEM, D_MEM)
    mv = _matmul_bias(mem_flat, w["w_mem_v"], zero_bias, bp * N_MEM, D_MEM).reshape(bp, N_MEM, D_MEM)

    y_p, st_p = _trunk(
        x_prompt, mk, mv,
        jnp.zeros((bp, 1, RWKV_PROJ), F32), jnp.zeros((bp, RWKV_HEADS, RWKV_HEAD_DIM, RWKV_HEAD_DIM), F32),
        jnp.zeros((bp, CONV_W - 1, 2 * D_MLSTM), F32), jnp.zeros((bp, MLSTM_HEADS, hd, hd), F32),
        jnp.zeros((bp, MLSTM_HEADS, hd), F32), jnp.zeros((bp, MLSTM_HEADS), F32),
        w, chunk_rw=64, chunk_ml=min(MLSTM_CHUNK, t_p), tm_proj=1024, tm_tail=256, tm_moe=512)
    y_s, st_s = _trunk(
        x_sample, cache_mem_k[0].reshape(bs, N_MEM, D_MEM), cache_mem_v[0].reshape(bs, N_MEM, D_MEM),
        state_rwkv_shift[0], state_rwkv_wkv[0], state_mlstm_conv[0], state_mlstm_c[0], state_mlstm_n[0],
        state_mlstm_m[0],
        w, chunk_rw=t_s, chunk_ml=min(MLSTM_CHUNK, t_s), tm_proj=bs * t_s, tm_tail=t_s, tm_moe=bs * t_s)

    lead = lambda t: t[None]
    mem_shape = (1, bp, N_MEM, MEM_HEADS, MEM_HEAD_DIM)
    return (y_p, y_s, *(lead(t) for t in st_p), mk.reshape(mem_shape), mv.reshape(mem_shape),
            *(lead(t) for t in st_s))
```

```python
import functools

import jax
import jax.numpy as jnp
from jax import lax
from jax.experimental import pallas as pl
from jax.experimental.pallas import tpu as pltpu

F32 = jnp.float32
BF16 = jnp.bfloat16
HIGHEST = lax.Precision.HIGHEST

D_MODEL = 1024
DEPTH = 1
D_RWKV = 1024
RWKV_HEAD_DIM = 64
RWKV_HEADS = 16
W_LORA = 64
A_LORA = 64
G_LORA = 128
LORA_ALL = W_LORA + A_LORA + G_LORA
RWKV_PROJ = 3 * D_RWKV + LORA_ALL
RWKV_GN_EPS = 64e-5
D_MLSTM = 1024
MLSTM_HEADS = 4
MLSTM_HEAD_DIM = 256
CONV_W = 4
MLSTM_CHUNK = 64
N_MEM = 256
MEM_HEADS = 4
MEM_HEAD_DIM = 256
D_MEM = 1024
N_EXPERTS = 32
TOP_K = 4
EXPERT_HIDDEN = 128
ROUTED_SCALE = 2.5
DN_ALPHA = (2 * DEPTH) ** 0.25
LN_EPS = 1e-5

LANES = 128
RWKV_PAIRS = D_RWKV // LANES
ML_COLS = 4 * D_MLSTM + LANES
TAIL_COLS = D_MEM + 3 * D_MODEL
VMEM_LIMIT = 56 * 1024 * 1024


def _dot(a, b, precision=None):
    return jnp.dot(a, b, preferred_element_type=F32, precision=precision)


def _dot_nt(a, b, precision=None):
    return lax.dot_general(a, b, (((1,), (1,)), ((), ())), preferred_element_type=F32, precision=precision)


def _dot_tn(a, b, precision=None):
    return lax.dot_general(a, b, (((0,), (0,)), ((), ())), preferred_element_type=F32, precision=precision)


_DOTS = {"nn": _dot, "nt": _dot_nt, "tn": _dot_tn}


def _split(a):
    hi = a.astype(BF16)
    return hi, (a - hi.astype(F32)).astype(BF16)


def _mm(a, b, passes, kind="nn"):
    dot = _DOTS[kind]
    if passes == 6:
        return dot(a, b, HIGHEST)
    if passes == 1:
        return dot(a.astype(BF16), b.astype(BF16))
    a_hi, a_lo = _split(a)
    b_hi, b_lo = _split(b)
    return dot(a_hi, b_hi) + (dot(a_lo, b_hi) + dot(a_hi, b_lo))


P_SCORE = 1
P_STATE_IN = 1
P_SOLVE = 1
P_OUT = 1
P_STATE_OUT = 1


def _bdot(a, b):
    return _dot(a.astype(BF16), b.astype(BF16))


def _bdot_nt(a, b):
    return _dot_nt(a.astype(BF16), b.astype(BF16))


def _sigmoid(x):
    return 1.0 / (1.0 + jnp.exp(-x))


def _softplus(x):
    return jnp.maximum(x, 0.0) + jnp.log(1.0 + jnp.exp(-jnp.abs(x)))


def _silu(x):
    return x * _sigmoid(x)


def _layer_norm(x, w, b):
    mu = jnp.mean(x, axis=-1, keepdims=True)
    d = x - mu
    var = jnp.mean(d * d, axis=-1, keepdims=True)
    return d * lax.rsqrt(var + LN_EPS) * w + b


def _params(semantics):
    return pltpu.CompilerParams(dimension_semantics=semantics, vmem_limit_bytes=VMEM_LIMIT)


def _const_spec(shape):
    zeros = (0,) * len(shape)
    return pl.BlockSpec(shape, lambda *_: zeros, pipeline_mode=pl.Buffered(1))


def _mm_kernel(x_ref, w_ref, b_ref, o_ref):
    o_ref[...] = _dot(x_ref[...], w_ref[...]) + b_ref[...]


def _matmul_bias(x, w, b, tm, tn):
    n, k = x.shape
    nc = w.shape[1]
    return pl.pallas_call(
        _mm_kernel,
        grid=(nc // tn, n // tm),
        in_specs=[
            pl.BlockSpec((tm, k), lambda j, i: (i, 0)),
            pl.BlockSpec((k, tn), lambda j, i: (0, j)),
            pl.BlockSpec((1, tn), lambda j, i: (0, j)),
        ],
        out_specs=pl.BlockSpec((tm, tn), lambda j, i: (i, j)),
        out_shape=jax.ShapeDtypeStruct((n, nc), F32),
        compiler_params=_params(("arbitrary", "arbitrary")),
        name="matmul_bias",
    )(x, w, b)


def _rwkv_kernel(p_ref, prev_ref, h0_ref, mu_ref, w0_ref, a0_ref, kk_ref, ka_ref, rk_ref, lnw_ref, lnb_ref,
                 wl_ref, y_ref, hout_ref, h_scr, xbuf, *, chunk, n_chunks):
    c_idx = pl.program_id(1)
    L = chunk

    @pl.when(c_idx == 0)
    def _():
        h_scr[...] = h0_ref[0]
        xbuf[7:8, :] = prev_ref[0]

    rw = p_ref[...]
    xbuf[8:8 + L, :] = rw
    prev_rows = xbuf[7:7 + L, :]
    xbuf[7:8, :] = rw[L - 1:L, :]

    xr = rw + (prev_rows - rw) * mu_ref[...]
    r_all = xr[:, 0:D_RWKV]
    k_all = xr[:, D_RWKV:2 * D_RWKV]
    v_all = xr[:, 2 * D_RWKV:3 * D_RWKV]
    slab = xr[:, 3 * D_RWKV:]
    lane_l = lax.broadcasted_iota(jnp.int32, slab.shape, 1)
    act = jnp.where(lane_l < W_LORA, jnp.tanh(slab), jnp.where(lane_l < W_LORA + A_LORA, slab, _sigmoid(slab)))
    lora = _bdot(act, wl_ref[...])
    w_log = -_softplus(-(w0_ref[...] + lora[:, 0:D_RWKV])) - 0.5
    lw_all = -jnp.exp(w_log)
    a_sig = _sigmoid(a0_ref[...] + lora[:, D_RWKV:2 * D_RWKV])
    g_all = lora[:, 2 * D_RWKV:]
    kk_all = k_all * kk_ref[...]
    k2_all = k_all * (1.0 + (a_sig - 1.0) * ka_ref[...])

    ti = lax.broadcasted_iota(jnp.int32, (L, L), 0)
    tj = lax.broadcasted_iota(jnp.int32, (L, L), 1)
    ltri = (tj <= ti).astype(F32)
    cum_all = _dot(ltri, lw_all, HIGHEST)

    ri = lax.broadcasted_iota(jnp.int32, (2 * L, 2 * L), 0)
    ci = lax.broadcasted_iota(jnp.int32, (2 * L, 2 * L), 1)
    same = (ri >= L) == (ci >= L)
    strict = same & (ci < ri)
    incl = same & (ci <= ri)
    eye = (ri == ci).astype(F32)
    gi = lax.broadcasted_iota(jnp.int32, (LANES, LANES), 0)
    gj = lax.broadcasted_iota(jnp.int32, (LANES, LANES), 1)
    head_ones = ((gi >= RWKV_HEAD_DIM) == (gj >= RWKV_HEAD_DIM)).astype(BF16)
    first_head = lax.broadcasted_iota(jnp.int32, (L, LANES), 1) < RWKV_HEAD_DIM

    def stack(x):
        return jnp.concatenate([jnp.where(first_head, x, 0.0), jnp.where(first_head, 0.0, x)], axis=0)

    def seg_sum(x):
        hi, lo = _split(x)
        return _dot(hi, head_ones) + _dot(lo, head_ones)

    n_sq = L.bit_length() - 1
    half = 2 * L
    pairs = range(RWKV_PAIRS)
    lanes = [slice(p * LANES, (p + 1) * LANES) for p in pairs]
    kk0 = [kk_all[:, s] for s in lanes]
    kk_ss = [seg_sum(t * t) for t in kk0]
    ar_s, bk_s, bkh_s, v_s, gamma = [], [], [], [], []
    for p in pairs:
        s = lanes[p]
        r, k2, lw, cum = r_all[:, s], k2_all[:, s], lw_all[:, s], cum_all[:, s]
        kk = kk0[p] * lax.rsqrt(jnp.maximum(kk_ss[p], 1e-24))
        a_vec = -kk
        b_vec = kk * a_sig[:, s]
        c_last = cum[L - 1:L, :]
        e_neg = jnp.exp(-cum)
        e_end = jnp.exp(c_last - cum)
        ar_s.append(jnp.concatenate([stack(a_vec * jnp.exp(cum - lw)), stack(r * jnp.exp(cum))], axis=0))
        bk_s.append(jnp.concatenate([stack(b_vec * e_neg), stack(k2 * e_neg)], axis=0))
        bkh_s.append(jnp.concatenate([stack(b_vec * e_end), stack(k2 * e_end)], axis=0))
        v_s.append(stack(v_all[:, s]))
        gamma.append(jnp.exp(c_last))
    h_bd = [h_scr[p] for p in pairs]
    sc = [_mm(ar_s[p], bk_s[p], P_SCORE, "nt") for p in pairs]
    arh = [_mm(ar_s[p], h_bd[p], P_STATE_IN, "nt") for p in pairs]
    m_ab = [jnp.where(strict, sc[p][0:half, 0:half], 0.0) for p in pairs]
    m_ak = [jnp.where(strict, sc[p][0:half, half:], 0.0) for p in pairs]
    n_rbk = [jnp.concatenate([jnp.where(incl, sc[p][half:, 0:half], 0.0),
                              jnp.where(incl, sc[p][half:, half:], 0.0)], axis=1) for p in pairs]
    w0 = [arh[p][0:half] + _mm(m_ak[p], v_s[p], P_SOLVE) for p in pairs]
    inv = [eye + m_ab[p] for p in pairs]
    pw = [_mm(m_ab[p], m_ab[p], P_SOLVE) for p in pairs]
    for i in range(n_sq - 1):
        inv = [inv[p] + _mm(inv[p], pw[p], P_SOLVE) for p in pairs]
        if i + 1 < n_sq - 1:
            pw = [_mm(pw[p], pw[p], P_SOLVE) for p in pairs]
    cv_s = [jnp.concatenate([_mm(inv[p], w0[p], P_SOLVE), v_s[p]], axis=0) for p in pairs]
    y_s = [arh[p][half:] + _mm(n_rbk[p], cv_s[p], P_OUT) for p in pairs]
    for p in pairs:
        h_scr[p] = h_bd[p] * gamma[p] + _mm(cv_s[p], bkh_s[p], P_STATE_OUT, "tn")
    y = [y_s[p][0:L] + y_s[p][L:half] for p in pairs]
    mean = [seg_sum(y[p]) * (1.0 / RWKV_HEAD_DIM) for p in pairs]
    dev = [y[p] - mean[p] for p in pairs]
    var = [seg_sum(dev[p] * dev[p]) * (1.0 / RWKV_HEAD_DIM) for p in pairs]
    rk_sum = [seg_sum(r_all[:, s] * k2_all[:, s] * rk_ref[:, s]) for s in lanes]
    for p in pairs:
        s = lanes[p]
        yn = dev[p] * lax.rsqrt(var[p] + RWKV_GN_EPS) * lnw_ref[:, s] + lnb_ref[:, s]
        y_ref[:, s] = (yn + rk_sum[p] * v_all[:, s]) * g_all[:, s]

    @pl.when(c_idx == n_chunks - 1)
    def _():
        hout_ref[0] = h_scr[...]


def _rwkv(p_rw, shift_prev, h0_bd, w, batch, t_len, chunk):
    n_chunks = t_len // chunk
    row = lambda b, c: (b * n_chunks + c, 0)
    vec = lambda n: pl.BlockSpec((1, n), lambda b, c: (0, 0))
    kern = functools.partial(_rwkv_kernel, chunk=chunk, n_chunks=n_chunks)
    return pl.pallas_call(
        kern,
        grid=(batch, n_chunks),
        in_specs=[
            pl.BlockSpec((chunk, RWKV_PROJ), row),
            pl.BlockSpec((1, 1, RWKV_PROJ), lambda b, c: (b, 0, 0)),
            pl.BlockSpec((1, RWKV_PAIRS, LANES, LANES), lambda b, c: (b, 0, 0, 0)),
            vec(RWKV_PROJ), vec(D_RWKV), vec(D_RWKV), vec(D_RWKV), vec(D_RWKV), vec(D_RWKV), vec(D_RWKV),
            vec(D_RWKV),
            pl.BlockSpec((LORA_ALL, 3 * D_RWKV), lambda b, c: (0, 0)),
        ],
        out_specs=[
            pl.BlockSpec((chunk, D_RWKV), row),
            pl.BlockSpec((1, RWKV_PAIRS, LANES, LANES), lambda b, c: (b, 0, 0, 0)),
        ],
        out_shape=[
            jax.ShapeDtypeStruct((batch * t_len, D_RWKV), F32),
            jax.ShapeDtypeStruct((batch, RWKV_PAIRS, LANES, LANES), F32),
        ],
        scratch_shapes=[
            pltpu.VMEM((RWKV_PAIRS, LANES, LANES), F32),
            pltpu.VMEM((chunk + 8, RWKV_PROJ), F32),
        ],
        compiler_params=_params(("arbitrary", "arbitrary")),
        name="rwkv7_chunked",
    )(p_rw, shift_prev, h0_bd, w["mu"], w["w0"], w["a0"], w["k_k"], w["k_a"], w["r_k"], w["rw_ln_w"],
      w["rw_ln_b"], w["w_lora"])


def _mlstm_kernel(p_ref, conv_prev_ref, c0_ref, n0_ref, m0_ref, cw_ref, cb_ref, bif_ref, lnw_ref, lnb_ref,
                  h_ref, cout_ref, nout_ref, mout_ref, c_scr, n_scr, m_scr, xbuf, *, chunk, n_chunks):
    c_idx = pl.program_id(1)
    L = chunk
    d2 = 2 * D_MLSTM

    @pl.when(c_idx == 0)
    def _():
        c_scr[...] = c0_ref[0]
        n_scr[...] = n0_ref[0]
        m_scr[...] = m0_ref[0]
        xbuf[5:8, :] = conv_prev_ref[0]

    x = p_ref[:, 0:d2]
    xbuf[8:8 + L, :] = x
    conv = cb_ref[...] + cw_ref[3:4, :] * x
    for j in range(CONV_W - 1):
        conv = conv + cw_ref[j:j + 1, :] * xbuf[5 + j:5 + j + L, :]
    xbuf[5:8, :] = x[L - 3:L, :]
    qk = _silu(conv)
    q_all = qk[:, 0:D_MLSTM]
    k_all = qk[:, D_MLSTM:d2] * (MLSTM_HEAD_DIM ** -0.5)
    v_all = p_ref[:, d2:d2 + D_MLSTM]
    o_all = _sigmoid(p_ref[:, d2 + D_MLSTM:d2 + 2 * D_MLSTM])
    gates = p_ref[:, 4 * D_MLSTM:] + bif_ref[...]
    lane_g = lax.broadcasted_iota(jnp.int32, gates.shape, 1)
    is_f = (lane_g >= MLSTM_HEADS) & (lane_g < 2 * MLSTM_HEADS)
    glog = jnp.where(is_f, -_softplus(-gates), gates)

    ti = lax.broadcasted_iota(jnp.int32, (L, L), 0)
    tj = lax.broadcasted_iota(jnp.int32, (L, L), 1)
    causal = tj <= ti
    ltri = causal.astype(F32)
    utri = (ti <= tj).astype(F32)
    b_cols = _dot(ltri, glog, HIGHEST)
    sel = (lax.broadcasted_iota(jnp.int32, (8, LANES), 0) == lax.broadcasted_iota(jnp.int32, (8, LANES), 1))
    g_rows = _dot_nt(sel.astype(F32), glog, HIGHEST)
    b_rows = _dot(g_rows, utri, HIGHEST)

    for h in range(MLSTM_HEADS):
        sl = slice(h * MLSTM_HEAD_DIM, (h + 1) * MLSTM_HEAD_DIM)
        q, k, v = q_all[:, sl], k_all[:, sl], v_all[:, sl]
        ig_row = g_rows[h:h + 1, :]
        b_row = b_rows[MLSTM_HEADS + h:MLSTM_HEADS + h + 1, :]
        ig_col = glog[:, h:h + 1]
        b_col = b_cols[:, MLSTM_HEADS + h:MLSTM_HEADS + h + 1]
        m_prev = m_scr[:, h:h + 1]
        c_st = c_scr[h]
        n_st = n_scr[h:h + 1, :]

        log_w = jnp.where(causal, b_col - b_row + ig_row, -jnp.inf)
        inter = b_col + m_prev
        m_t = jnp.maximum(inter, jnp.max(log_w, axis=-1, keepdims=True))
        w_intra = jnp.exp(log_w - m_t)
        w_inter = jnp.exp(inter - m_t)
        s = _bdot_nt(q, k) * w_intra
        num = w_inter * _bdot_nt(q, c_st) + _bdot(s, v)
        den = w_inter * jnp.sum(q * n_st, axis=-1, keepdims=True) + jnp.sum(s, axis=-1, keepdims=True)
        hh = num / jnp.maximum(jnp.abs(den), jnp.exp(-m_t))

        b_last = b_col[L - 1:L, :]
        g_col = b_last - b_col + ig_col
        m_new = jnp.maximum(b_last + m_prev, jnp.max(g_col, axis=0, keepdims=True))
        carry = jnp.exp(b_last + m_prev - m_new)
        wk = jnp.exp(g_col - m_new)
        c_scr[h] = carry * c_st + _dot_tn((wk * v).astype(BF16), k.astype(BF16))
        n_scr[h:h + 1, :] = carry * n_st + jnp.sum(wk * k, axis=0, keepdims=True)
        m_scr[:, h:h + 1] = m_new

        hb = o_all[:, sl] * hh
        h_ref[:, sl] = _layer_norm(hb, lnw_ref[:, sl], lnb_ref[:, sl])

    @pl.when(c_idx == n_chunks - 1)
    def _():
        cout_ref[0] = c_scr[...]
        nout_ref[0] = n_scr[...]
        mout_ref[0] = m_scr[...]


def _mlstm(p_ml, conv_prev, c0, n0, m0, w, batch, t_len, chunk):
    n_chunks = t_len // chunk
    row = lambda b, c: (b * n_chunks + c, 0)
    vec = lambda n: pl.BlockSpec((1, n), lambda b, c: (0, 0))
    hd = MLSTM_HEAD_DIM
    kern = functools.partial(_mlstm_kernel, chunk=chunk, n_chunks=n_chunks)
    state_specs = [
        pl.BlockSpec((1, MLSTM_HEADS, hd, hd), lambda b, c: (b, 0, 0, 0)),
        pl.BlockSpec((1, MLSTM_HEADS, hd), lambda b, c: (b, 0, 0)),
        pl.BlockSpec((1, 1, MLSTM_HEADS), lambda b, c: (b, 0, 0)),
    ]
    return pl.pallas_call(
        kern,
        grid=(batch, n_chunks),
        in_specs=[
            pl.BlockSpec((chunk, ML_COLS), row),
            pl.BlockSpec((1, CONV_W - 1, 2 * D_MLSTM), lambda b, c: (b, 0, 0)),
            *state_specs,
            pl.BlockSpec((CONV_W, 2 * D_MLSTM), lambda b, c: (0, 0)),
            vec(2 * D_MLSTM), vec(LANES), vec(D_MLSTM), vec(D_MLSTM),
        ],
        out_specs=[pl.BlockSpec((chunk, D_MLSTM), row), *state_specs],
        out_shape=[
            jax.ShapeDtypeStruct((batch * t_len, D_MLSTM), F32),
            jax.ShapeDtypeStruct((batch, MLSTM_HEADS, hd, hd), F32),
            jax.ShapeDtypeStruct((batch, MLSTM_HEADS, hd), F32),
            jax.ShapeDtypeStruct((batch, 1, MLSTM_HEADS), F32),
        ],
        scratch_shapes=[
            pltpu.VMEM((MLSTM_HEADS, hd, hd), F32),
            pltpu.VMEM((MLSTM_HEADS, hd), F32),
            pltpu.VMEM((1, MLSTM_HEADS), F32),
            pltpu.VMEM((chunk + 8, 2 * D_MLSTM), F32),
        ],
        compiler_params=_params(("arbitrary", "arbitrary")),
        name="mlstm_chunkwise",
    )(p_ml, conv_prev, c0, n0, m0, w["conv_w"], w["conv_b"], w["b_if"], w["ml_ln_w"], w["ml_ln_b"])


def _tail_kernel(x_ref, ya_ref, hb_ref, pt_ref, mk_ref, mv_ref, wpa_ref, wpb_ref, wpc_ref, wout_ref,
                 lnw_ref, lnb_ref, o_ref):
    mk = mk_ref[0].astype(BF16)
    mv = mv_ref[0].astype(BF16)
    heads = []
    for h in range(MEM_HEADS):
        sl = slice(h * MEM_HEAD_DIM, (h + 1) * MEM_HEAD_DIM)
        s = _dot_nt(pt_ref[:, sl].astype(BF16), mk[:, sl]) * (MEM_HEAD_DIM ** -0.5)
        s = s - jnp.max(s, axis=-1, keepdims=True)
        e = jnp.exp(s)
        prob = e / jnp.sum(e, axis=-1, keepdims=True)
        heads.append(_dot(prob.astype(BF16), mv[:, sl]))
    attn = jnp.concatenate(heads, axis=-1)
    y_a = _bdot(ya_ref[...], wpa_ref[...])
    y_b = _bdot(hb_ref[...], wpb_ref[...])
    y_c = _bdot(attn, wpc_ref[...])
    d = D_MODEL
    mixed = (_sigmoid(pt_ref[:, D_MEM:D_MEM + d]) * y_a + _sigmoid(pt_ref[:, D_MEM + d:D_MEM + 2 * d]) * y_b
             + _sigmoid(pt_ref[:, D_MEM + 2 * d:D_MEM + 3 * d]) * y_c)
    u = _bdot(mixed, wout_ref[...])
    o_ref[...] = _layer_norm(DN_ALPHA * x_ref[...] + u, lnw_ref[...], lnb_ref[...])


def _tail(x, ya, hb, p_tail, mem_k, mem_v, w, tm, tiles_per_batch):
    n = x.shape[0]
    row = lambda i: (i, 0)
    mem = lambda i: (i // tiles_per_batch, 0, 0)
    return pl.pallas_call(
        _tail_kernel,
        grid=(n // tm,),
        in_specs=[
            pl.BlockSpec((tm, D_MODEL), row), pl.BlockSpec((tm, D_RWKV), row), pl.BlockSpec((tm, D_MLSTM), row),
            pl.BlockSpec((tm, TAIL_COLS), row),
            pl.BlockSpec((1, N_MEM, D_MEM), mem), pl.BlockSpec((1, N_MEM, D_MEM), mem),
            _const_spec((D_RWKV, D_MODEL)), _const_spec((D_MLSTM, D_MODEL)), _const_spec((D_MEM, D_MODEL)),
            _const_spec((D_MODEL, D_MODEL)), _const_spec((1, D_MODEL)), _const_spec((1, D_MODEL)),
        ],
        out_specs=pl.BlockSpec((tm, D_MODEL), row),
        out_shape=jax.ShapeDtypeStruct((n, D_MODEL), F32),
        compiler_params=_params(("arbitrary",)),
        name="attn_merge_ln",
    )(x, ya, hb, p_tail, mem_k, mem_v, w["w_pa"], w["w_pb"], w["w_pc"], w["w_out"], w["ln1_w"], w["ln1_b"])


def _moe_kernel(x_ref, wr_ref, rb_ref, w1_ref, w3_ref, w2_ref, ws1_ref, ws3_ref, ws2_ref, lnw_ref, lnb_ref, o_ref):
    x = x_ref[...]
    xb = x.astype(BF16)
    tm = x.shape[0]
    scores = _sigmoid(_dot(xb, wr_ref[...]))
    lane = lax.broadcasted_iota(jnp.int32, scores.shape, 1)
    work = jnp.where(lane < N_EXPERTS, scores + rb_ref[...], -jnp.inf)
    chosen = jnp.zeros(scores.shape, dtype=jnp.bool_)
    for _ in range(TOP_K):
        best = jnp.max(work, axis=-1, keepdims=True)
        first = jnp.min(jnp.where(work == best, lane, LANES), axis=-1, keepdims=True)
        pick = lane == first
        chosen = chosen | pick
        work = jnp.where(pick, -jnp.inf, work)
    sel = jnp.where(chosen, scores, 0.0)
    gates = sel / jnp.sum(sel, axis=-1, keepdims=True) * ROUTED_SCALE

    acc = _dot((_silu(_dot(xb, ws1_ref[...])) * _dot(xb, ws3_ref[...])).astype(BF16), ws2_ref[...])
    group = 8
    width = group * EXPERT_HIDDEN
    for s in range(N_EXPERTS // group):
        cols = slice(s * width, (s + 1) * width)
        hidden = _silu(_dot(xb, w1_ref[:, cols])) * _dot(xb, w3_ref[:, cols])
        gated = jnp.concatenate(
            [hidden[:, e * EXPERT_HIDDEN:(e + 1) * EXPERT_HIDDEN] * gates[:, s * group + e:s * group + e + 1]
             for e in range(group)], axis=-1)
        acc = acc + _dot(gated.astype(BF16), w2_ref[cols, :])
    del tm
    o_ref[...] = _layer_norm(DN_ALPHA * x + acc, lnw_ref[...], lnb_ref[...])


def _moe(x, w, tm):
    n = x.shape[0]
    row = lambda i: (i, 0)
    hid = N_EXPERTS * EXPERT_HIDDEN
    return pl.pallas_call(
        _moe_kernel,
        grid=(n // tm,),
        in_specs=[
            pl.BlockSpec((tm, D_MODEL), row),
            _const_spec((D_MODEL, LANES)), _const_spec((1, LANES)),
            _const_spec((D_MODEL, hid)), _const_spec((D_MODEL, hid)), _const_spec((hid, D_MODEL)),
            _const_spec((D_MODEL, EXPERT_HIDDEN)), _const_spec((D_MODEL, EXPERT_HIDDEN)),
            _const_spec((EXPERT_HIDDEN, D_MODEL)),
            _const_spec((1, D_MODEL)), _const_spec((1, D_MODEL)),
        ],
        out_specs=pl.BlockSpec((tm, D_MODEL), row),
        out_shape=jax.ShapeDtypeStruct((n, D_MODEL), F32),
        compiler_params=_params(("arbitrary",)),
        name="moe_ln",
    )(x, w["w_router"], w["router_bias"], w["w_e1"], w["w_e3"], w["w_e2"], w["w_s1"], w["w_s3"], w["w_s2"],
      w["ln2_w"], w["ln2_b"])


def _prep_weights(w_in, b_in, rwkv_mu, rwkv_w0, rwkv_w2, rwkv_a0, rwkv_a2, rwkv_g2, rwkv_k_k, rwkv_k_a,
                  rwkv_r_k, rwkv_ln_w, rwkv_ln_b, mlstm_conv_w, mlstm_conv_b, mlstm_b_i, mlstm_b_f,
                  mlstm_ln_w, mlstm_ln_b, w_mem_kv, w_pa, w_pb, w_pc, w_out, ln1_w, ln1_b,
                  w_router, router_bias, w_e1, w_e3, w_e2, w_s1, w_s3, w_s2, ln2_w, ln2_b):
    off_ml = RWKV_PROJ
    off_if = off_ml + 4 * D_MLSTM
    off_mq = off_if + 2 * MLSTM_HEADS
    pad_if = LANES - 2 * MLSTM_HEADS
    row = lambda t: t.reshape(1, -1)
    hid = N_EXPERTS * EXPERT_HIDDEN
    w_lora = jnp.zeros((LORA_ALL, 3 * D_RWKV), F32)
    w_lora = w_lora.at[0:W_LORA, 0:D_RWKV].set(rwkv_w2)
    w_lora = w_lora.at[W_LORA:W_LORA + A_LORA, D_RWKV:2 * D_RWKV].set(rwkv_a2)
    w_lora = w_lora.at[W_LORA + A_LORA:, 2 * D_RWKV:].set(rwkv_g2)
    return dict(
        w_rw=w_in[:, :off_ml].astype(BF16), b_rw=row(b_in[:off_ml]),
        w_ml=jnp.pad(w_in[:, off_ml:off_mq], ((0, 0), (0, pad_if))).astype(BF16),
        b_ml=row(jnp.pad(b_in[off_ml:off_mq], (0, pad_if))),
        w_tail=w_in[:, off_mq:].astype(BF16), b_tail=row(b_in[off_mq:]),
        w_mem_k=w_mem_kv[:, :D_MEM].astype(BF16), w_mem_v=w_mem_kv[:, D_MEM:].astype(BF16),
        mu=row(rwkv_mu), w0=row(rwkv_w0), a0=row(rwkv_a0), k_k=row(rwkv_k_k), k_a=row(rwkv_k_a),
        r_k=row(rwkv_r_k), rw_ln_w=row(rwkv_ln_w), rw_ln_b=row(rwkv_ln_b), w_lora=w_lora.astype(BF16),
        conv_w=mlstm_conv_w, conv_b=row(mlstm_conv_b),
        b_if=row(jnp.pad(jnp.concatenate([mlstm_b_i, mlstm_b_f]), (0, pad_if))),
        ml_ln_w=row(mlstm_ln_w), ml_ln_b=row(mlstm_ln_b),
        w_pa=w_pa.astype(BF16), w_pb=w_pb.astype(BF16), w_pc=w_pc.astype(BF16), w_out=w_out.astype(BF16),
        ln1_w=row(ln1_w), ln1_b=row(ln1_b),
        w_router=jnp.pad(w_router, ((0, 0), (0, LANES - N_EXPERTS))).astype(BF16),
        router_bias=row(jnp.pad(router_bias, (0, LANES - N_EXPERTS))),
        w_e1=jnp.transpose(w_e1, (1, 0, 2)).reshape(D_MODEL, hid).astype(BF16),
        w_e3=jnp.transpose(w_e3, (1, 0, 2)).reshape(D_MODEL, hid).astype(BF16),
        w_e2=w_e2.reshape(hid, D_MODEL).astype(BF16),
        w_s1=w_s1.astype(BF16), w_s3=w_s3.astype(BF16), w_s2=w_s2.astype(BF16),
        ln2_w=row(ln2_w), ln2_b=row(ln2_b),
    )


def _pair_state(wkv):
    bsz = wkv.shape[0]
    s = wkv.reshape(bsz, RWKV_PAIRS, 2, RWKV_HEAD_DIM, RWKV_HEAD_DIM)
    z = jnp.zeros_like(s[:, :, 0])
    top = jnp.concatenate([s[:, :, 0], z], axis=-1)
    bot = jnp.concatenate([z, s[:, :, 1]], axis=-1)
    return jnp.concatenate([top, bot], axis=-2)


def _unpair_state(h_bd):
    bsz = h_bd.shape[0]
    hd = RWKV_HEAD_DIM
    heads = jnp.stack([h_bd[:, :, :hd, :hd], h_bd[:, :, hd:, hd:]], axis=2)
    return heads.reshape(bsz, RWKV_HEADS, hd, hd)


def _trunk(x3, mem_k, mem_v, shift_prev, wkv0, conv_prev, c0, n0, m0, w, chunk_rw, chunk_ml, tm_proj, tm_tail,
           tm_moe):
    bsz, t_len, _ = x3.shape
    n = bsz * t_len
    x = x3.reshape(n, D_MODEL)
    xb = x.astype(BF16)
    p_rw = _matmul_bias(xb, w["w_rw"], w["b_rw"], tm_proj, RWKV_PROJ // 2)
    p_ml = _matmul_bias(xb, w["w_ml"], w["b_ml"], tm_proj, ML_COLS // 3)
    p_tail = _matmul_bias(xb, w["w_tail"], w["b_tail"], tm_proj, TAIL_COLS // 2)

    ya, h_bd = _rwkv(p_rw, shift_prev, _pair_state(wkv0), w, bsz, t_len, chunk_rw)
    hb, c_t, n_t, m_t = _mlstm(p_ml, conv_prev, c0, n0, m0.reshape(bsz, 1, MLSTM_HEADS), w, bsz, t_len, chunk_ml)
    x1 = _tail(x, ya, hb, p_tail, mem_k, mem_v, w, tm_tail, t_len // tm_tail)
    y = _moe(x1, w, tm_moe)

    p_rw3 = p_rw.reshape(bsz, t_len, RWKV_PROJ)
    p_ml3 = p_ml.reshape(bsz, t_len, ML_COLS)
    new_shift = p_rw3[:, t_len - 1:, :]
    new_conv = p_ml3[:, t_len - (CONV_W - 1):, :2 * D_MLSTM]
    states = (new_shift, _unpair_state(h_bd), new_conv, c_t, n_t, m_t.reshape(bsz, MLSTM_HEADS))
    return y.reshape(bsz, t_len, D_MODEL), states


def kernel(x_prompt, x_sample, state_rwkv_shift, state_rwkv_wkv, state_mlstm_conv, state_mlstm_c, state_mlstm_n, state_mlstm_m, cache_mem_k, cache_mem_v, mem_prompt, w_in, b_in, rwkv_mu, rwkv_w0, rwkv_w2, rwkv_a0, rwkv_a2, rwkv_g2, rwkv_k_k, rwkv_k_a, rwkv_r_k, rwkv_ln_w, rwkv_ln_b, mlstm_conv_w, mlstm_conv_b, mlstm_b_i, mlstm_b_f, mlstm_ln_w, mlstm_ln_b, w_mem_kv, w_pa, w_pb, w_pc, w_out, ln1_w, ln1_b, w_router, router_bias, w_e1, w_e3, w_e2, w_s1, w_s3, w_s2, ln2_w, ln2_b):
    weights = (w_in, b_in, rwkv_mu, rwkv_w0, rwkv_w2, rwkv_a0, rwkv_a2, rwkv_g2, rwkv_k_k, rwkv_k_a, rwkv_r_k,
               rwkv_ln_w, rwkv_ln_b, mlstm_conv_w, mlstm_conv_b, mlstm_b_i, mlstm_b_f, mlstm_ln_w, mlstm_ln_b,
               w_mem_kv, w_pa, w_pb, w_pc, w_out, ln1_w, ln1_b, w_router, router_bias, w_e1, w_e3, w_e2,
               w_s1, w_s3, w_s2, ln2_w, ln2_b)
    w = _prep_weights(*(t[0] for t in weights))
    bp, t_p, _ = x_prompt.shape
    bs, t_s, _ = x_sample.shape
    hd = MLSTM_HEAD_DIM

    mem_flat = mem_prompt.reshape(bp * N_MEM, D_MODEL).astype(BF16)
    zero_bias = jnp.zeros((1, D_MEM), F32)
    mk = _matmul_bias(mem_flat, w["w_mem_k"], zero_bias, bp * N_MEM, D_MEM).reshape(bp, N_MEM, D_MEM)
    mv = _matmul_bias(mem_flat, w["w_mem_v"], zero_bias, bp * N_MEM, D_MEM).reshape(bp, N_MEM, D_MEM)

    y_p, st_p = _trunk(
        x_prompt, mk, mv,
        jnp.zeros((bp, 1, RWKV_PROJ), F32), jnp.zeros((bp, RWKV_HEADS, RWKV_HEAD_DIM, RWKV_HEAD_DIM), F32),
        jnp.zeros((bp, CONV_W - 1, 2 * D_MLSTM), F32), jnp.zeros((bp, MLSTM_HEADS, hd, hd), F32),
        jnp.zeros((bp, MLSTM_HEADS, hd), F32), jnp.zeros((bp, MLSTM_HEADS), F32),
        w, chunk_rw=64, chunk_ml=min(MLSTM_CHUNK, t_p), tm_proj=1024, tm_tail=256, tm_moe=512)
    y_s, st_s = _trunk(
        x_sample, cache_mem_k[0].reshape(bs, N_MEM, D_MEM), cache_mem_v[0].reshape(bs, N_MEM, D_MEM),
        state_rwkv_shift[0], state_rwkv_wkv[0], state_mlstm_conv[0], state_mlstm_c[0], state_mlstm_n[0],
        state_mlstm_m[0],
        w, chunk_rw=t_s, chunk_ml=min(MLSTM_CHUNK, t_s), tm_proj=bs * t_s, tm_tail=t_s, tm_moe=bs * t_s)

    lead = lambda t: t[None]
    mem_shape = (1, bp, N_MEM, MEM_HEADS, MEM_HEAD_DIM)
    return (y_p, y_s, *(lead(t) for t in st_p), mk.reshape(mem_shape), mv.reshape(mem_shape),
            *(lead(t) for t in st_s))
```

```python
import functools
import itertools

import jax
import jax.numpy as jnp
from jax import lax
from jax.experimental import pallas as pl
from jax.experimental.pallas import tpu as pltpu

F32 = jnp.float32
BF16 = jnp.bfloat16

D_MODEL = 1024
DEPTH = 1
D_RWKV = 1024
RWKV_HEAD_DIM = 64
RWKV_HEADS = 16
W_LORA = 64
A_LORA = 64
G_LORA = 128
LORA_ALL = W_LORA + A_LORA + G_LORA
RWKV_PROJ = 3 * D_RWKV + LORA_ALL
RWKV_GN_EPS = 64e-5
D_MLSTM = 1024
MLSTM_HEADS = 4
MLSTM_HEAD_DIM = 256
CONV_W = 4
MLSTM_CHUNK = 64
N_MEM = 256
MEM_HEADS = 4
MEM_HEAD_DIM = 256
D_MEM = 1024
N_EXPERTS = 32
TOP_K = 4
EXPERT_HIDDEN = 128
ROUTED_SCALE = 2.5
DN_ALPHA = (2 * DEPTH) ** 0.25
LN_EPS = 1e-5

LANES = 128
RWKV_PAIRS = D_RWKV // LANES
ML_COLS = 4 * D_MLSTM + LANES
TAIL_COLS = D_MEM + 3 * D_MODEL
VMEM_LIMIT = 56 * 1024 * 1024
BATCH_GROUP = 2


def _dot(a, b):
    return jnp.dot(a, b, preferred_element_type=F32)


def _dot_nt(a, b):
    return lax.dot_general(a, b, (((1,), (1,)), ((), ())), preferred_element_type=F32)


def _dot_tn(a, b):
    return lax.dot_general(a, b, (((0,), (0,)), ((), ())), preferred_element_type=F32)


def _split(a):
    hi = a.astype(BF16)
    return hi, (a - hi.astype(F32)).astype(BF16)


def _split3(a):
    hi = a.astype(BF16)
    rest = a - hi.astype(F32)
    mid = rest.astype(BF16)
    return hi, mid, (rest - mid.astype(F32)).astype(BF16)


def _bdot(a, b):
    return _dot(a.astype(BF16), b.astype(BF16))


def _sigmoid(x):
    return 1.0 / (1.0 + jnp.exp(-x))


def _softplus(x):
    return jnp.maximum(x, 0.0) + jnp.log(1.0 + jnp.exp(-jnp.abs(x)))


def _silu(x):
    return x * _sigmoid(x)


def _layer_norm(x, w, b):
    mu = jnp.mean(x, axis=-1, keepdims=True)
    d = x - mu
    var = jnp.mean(d * d, axis=-1, keepdims=True)
    return d * lax.rsqrt(var + LN_EPS) * w + b


def _params(semantics):
    return pltpu.CompilerParams(dimension_semantics=semantics, vmem_limit_bytes=VMEM_LIMIT)


def _const_spec(shape):
    zeros = (0,) * len(shape)
    return pl.BlockSpec(shape, lambda *_: zeros, pipeline_mode=pl.Buffered(1))


def _mm_kernel(x_ref, w_ref, b_ref, o_ref):
    o_ref[...] = _dot(x_ref[...], w_ref[...]) + b_ref[...]


def _matmul_bias(x, w, b, tm, tn):
    n, k = x.shape
    nc = w.shape[1]
    return pl.pallas_call(
        _mm_kernel,
        grid=(nc // tn, n // tm),
        in_specs=[
            pl.BlockSpec((tm, k), lambda j, i: (i, 0)),
            pl.BlockSpec((k, tn), lambda j, i: (0, j)),
            pl.BlockSpec((1, tn), lambda j, i: (0, j)),
        ],
        out_specs=pl.BlockSpec((tm, tn), lambda j, i: (i, j)),
        out_shape=jax.ShapeDtypeStruct((n, nc), F32),
        compiler_params=_params(("arbitrary", "arbitrary")),
        name="matmul_bias",
    )(x, w, b)


def _rwkv_kernel(p_ref, prev_ref, h0_ref, mu_ref, w0_ref, a0_ref, kk_ref, ka_ref, rk_ref, lnw_ref, lnb_ref,
                 wl_ref, y_ref, hout_ref, h_scr, xbuf, ar_scr, bk_scr, bkh_scr, v_scr, bonus_scr, g_scr, gamma_scr,
                 *, chunk, n_chunks, n_items):
    step = pl.program_id(0)
    L = chunk
    half = 2 * L
    prep_chunk = jnp.minimum(step, n_items - 1) % n_chunks
    solve_chunk = jnp.maximum(step - 1, 0) % n_chunks

    @pl.when(step == 0)
    def _():
        for ref in (ar_scr, bk_scr, bkh_scr, v_scr, bonus_scr, g_scr):
            ref[...] = jnp.zeros(ref.shape, ref.dtype)
        gamma_scr[...] = jnp.ones(gamma_scr.shape, F32)

    @pl.when(solve_chunk == 0)
    def _():
        h_scr[...] = h0_ref[...]

    @pl.when(prep_chunk == 0)
    def _():
        for bi in range(BATCH_GROUP):
            xbuf[bi, 0:1, :] = prev_ref[bi]

    ti = lax.broadcasted_iota(jnp.int32, (L, L), 0)
    tj = lax.broadcasted_iota(jnp.int32, (L, L), 1)
    ltri = (tj <= ti).astype(BF16)
    row_m = lax.broadcasted_iota(jnp.int32, (L, half), 0)
    col_m = lax.broadcasted_iota(jnp.int32, (L, half), 1)
    col_in = jnp.where(col_m >= L, col_m - L, col_m)
    strict = col_in < row_m
    incl = col_in <= row_m
    eye = (col_in == row_m).astype(F32)
    gi = lax.broadcasted_iota(jnp.int32, (LANES, LANES), 0)
    gj = lax.broadcasted_iota(jnp.int32, (LANES, LANES), 1)
    same_head = (gi >= RWKV_HEAD_DIM) == (gj >= RWKV_HEAD_DIM)
    head_ones = same_head.astype(BF16)
    first_nat = lax.broadcasted_iota(jnp.int32, (L, LANES), 1) < RWKV_HEAD_DIM
    first_sbs = col_m < L
    lane_l = lax.broadcasted_iota(jnp.int32, (L, LORA_ALL), 1)

    def diag(x, first):
        zero = jnp.zeros_like(x)
        return jnp.concatenate([jnp.where(first, x, zero), jnp.where(first, zero, x)], axis=0)

    def seg_sum(x):
        return _dot(x.astype(BF16), head_ones)

    units = [(bi, p) for bi in range(BATCH_GROUP) for p in range(RWKV_PAIRS)]
    idx = range(len(units))
    lane_of = lambda p: slice(p * LANES, (p + 1) * LANES)

    ar = [ar_scr[bi, :, lane_of(p)] for bi, p in units]
    bk = [bk_scr[bi, :, lane_of(p)] for bi, p in units]
    bkh = [bkh_scr[bi, :, lane_of(p)] for bi, p in units]
    v_b = [v_scr[bi, :, lane_of(p)] for bi, p in units]
    bonus = [bonus_scr[bi, :, lane_of(p)] for bi, p in units]
    gate = [g_scr[bi, :, lane_of(p)] for bi, p in units]
    gamma = [gamma_scr[bi, :, lane_of(p)] for bi, p in units]

    first_row = lax.broadcasted_iota(jnp.int32, (L, RWKV_PROJ), 0) == 0

    def prepare_row(bi):
        cur = p_ref[bi]
        prev = jnp.where(first_row, xbuf[bi, 0:1, :], pltpu.roll(cur, 1, 0))
        xbuf[bi, 0:1, :] = cur[L - 1:L, :]
        xr = cur + (prev - cur) * mu_ref[...]
        r_all = xr[:, 0:D_RWKV]
        k_all = xr[:, D_RWKV:2 * D_RWKV]
        v_all = xr[:, 2 * D_RWKV:3 * D_RWKV]
        slab = xr[:, 3 * D_RWKV:]
        act = jnp.where(lane_l < W_LORA, jnp.tanh(slab),
                        jnp.where(lane_l < W_LORA + A_LORA, slab, _sigmoid(slab)))
        yield
        lora = _bdot(act, wl_ref[...])
        yield
        w_log = -_softplus(-(w0_ref[...] + lora[:, 0:D_RWKV])) - 0.5
        lw = -jnp.exp(w_log)
        a_sig = _sigmoid(a0_ref[...] + lora[:, D_RWKV:2 * D_RWKV])
        lw_hi, lw_lo = _split(lw)
        yield
        kk0 = k_all * kk_ref[...]
        k2 = k_all * (1.0 + (a_sig - 1.0) * ka_ref[...])
        sums = jnp.concatenate([kk0 * kk0, r_all * k2 * rk_ref[...]], axis=0).astype(BF16)
        yield
        cum = _dot(ltri, lw_hi) + _dot(ltri, lw_lo)
        sums = jnp.concatenate([_dot(sums[:, lane_of(p)], head_ones) for p in range(RWKV_PAIRS)], axis=1)
        yield
        kk = kk0 * lax.rsqrt(jnp.maximum(sums[0:L], 1e-24))
        b_vec = kk * a_sig
        c_last = cum[L - 1:L, :]
        e_neg = jnp.exp(-cum)
        e_end = jnp.exp(c_last - cum)
        yield
        ar_scr[bi] = jnp.concatenate([-kk * jnp.exp(cum - lw), r_all * jnp.exp(cum)], axis=0).astype(BF16)
        bk_scr[bi] = jnp.concatenate([b_vec * e_neg, k2 * e_neg], axis=0).astype(BF16)
        yield
        bkh_scr[bi] = jnp.concatenate([b_vec * e_end, k2 * e_end], axis=0).astype(BF16)
        v_scr[bi] = v_all.astype(BF16)
        bonus_scr[bi] = sums[L:] * v_all
        g_scr[bi] = lora[:, 2 * D_RWKV:]
        gamma_scr[bi] = jnp.exp(c_last)

    segments = itertools.chain(*(prepare_row(bi) for bi in range(BATCH_GROUP)))

    def prepare_some(count):
        for _ in range(count):
            next(segments, None)

    n_sq = L.bit_length() - 1
    per_stage = 2
    bk_d = [jnp.concatenate([diag(bk[u][0:L], first_nat), diag(bk[u][L:], first_nat)], axis=0) for u in idx]
    v_d = [diag(v_b[u], first_nat) for u in idx]
    h_bd = [h_scr[bi, p] for bi, p in units]
    sc = [_dot_nt(ar[u], bk_d[u]) for u in idx]
    prepare_some(per_stage)
    arh = [_dot_nt(ar[u], h_bd[u].astype(BF16)) for u in idx]
    m_ab = [jnp.where(strict, sc[u][0:L, 0:half], 0.0) for u in idx]
    m_ak = [jnp.where(strict, sc[u][0:L, half:], 0.0).astype(BF16) for u in idx]
    n_rbk = [jnp.concatenate([jnp.where(incl, sc[u][L:, 0:half], 0.0),
                              jnp.where(incl, sc[u][L:, half:], 0.0)], axis=1).astype(BF16) for u in idx]
    prepare_some(per_stage)
    w0 = [arh[u][0:L] + _dot(m_ak[u], v_d[u]) for u in idx]
    inv = [eye + m_ab[u] for u in idx]
    m_b = [m_ab[u].astype(BF16) for u in idx]
    pw = [_dot(m_b[u], diag(m_b[u], first_sbs)) for u in idx]
    prepare_some(per_stage)
    for i in range(n_sq - 1):
        pw_d = [diag(pw[u].astype(BF16), first_sbs) for u in idx]
        if i + 1 < n_sq - 1:
            z = [_dot(jnp.concatenate([inv[u], pw[u]], axis=0).astype(BF16), pw_d[u]) for u in idx]
            inv = [inv[u] + z[u][0:L] for u in idx]
            pw = [z[u][L:] for u in idx]
        else:
            inv = [inv[u] + _dot(inv[u].astype(BF16), pw_d[u]) for u in idx]
        prepare_some(per_stage)
    c_b = [_dot(inv[u].astype(BF16), diag(w0[u].astype(BF16), first_nat)).astype(BF16) for u in idx]
    cv = [jnp.concatenate([c_b[u], v_b[u]], axis=0) for u in idx]
    y = [arh[u][L:] + _dot(n_rbk[u], jnp.concatenate([diag(c_b[u], first_nat), v_d[u]], axis=0)) for u in idx]
    for u, (bi, p) in enumerate(units):
        h_scr[bi, p] = h_bd[u] * gamma[u] + jnp.where(same_head, _dot_tn(cv[u], bkh[u]), 0.0)
    mean = [seg_sum(y[u]) * (1.0 / RWKV_HEAD_DIM) for u in idx]
    dev = [y[u] - mean[u] for u in idx]
    var = [seg_sum(dev[u] * dev[u]) * (1.0 / RWKV_HEAD_DIM) for u in idx]
    for u, (bi, p) in enumerate(units):
        s = lane_of(p)
        yn = dev[u] * lax.rsqrt(var[u] + RWKV_GN_EPS) * lnw_ref[:, s] + lnb_ref[:, s]
        y_ref[bi, :, s] = (yn + bonus[u]) * gate[u]
    prepare_some(8 * BATCH_GROUP)

    @pl.when((solve_chunk == n_chunks - 1) & (step > 0))
    def _():
        hout_ref[...] = h_scr[...]


def _rwkv(p_rw, shift_prev, h0_bd, w, chunk):
    batch, t_len, _ = p_rw.shape
    n_chunks = t_len // chunk
    bg = BATCH_GROUP
    n_items = (batch // bg) * n_chunks
    prep = lambda i: jnp.minimum(i, n_items - 1)
    solve = lambda i: jnp.maximum(i - 1, 0)
    vec = lambda n: pl.BlockSpec((1, n), lambda i: (0, 0))
    state = pl.BlockSpec((bg, RWKV_PAIRS, LANES, LANES), lambda i: (solve(i) // n_chunks, 0, 0, 0))
    kern = functools.partial(_rwkv_kernel, chunk=chunk, n_chunks=n_chunks, n_items=n_items)
    wide = lambda rows, dtype: pltpu.VMEM((bg, rows, D_RWKV), dtype)
    return pl.pallas_call(
        kern,
        grid=(n_items + 1,),
        in_specs=[
            pl.BlockSpec((bg, chunk, RWKV_PROJ), lambda i: (prep(i) // n_chunks, prep(i) % n_chunks, 0)),
            pl.BlockSpec((bg, 1, RWKV_PROJ), lambda i: (prep(i) // n_chunks, 0, 0)),
            state,
            vec(RWKV_PROJ), vec(D_RWKV), vec(D_RWKV), vec(D_RWKV), vec(D_RWKV), vec(D_RWKV), vec(D_RWKV),
            vec(D_RWKV),
            pl.BlockSpec((LORA_ALL, 3 * D_RWKV), lambda i: (0, 0)),
        ],
        out_specs=[
            pl.BlockSpec((bg, chunk, D_RWKV), lambda i: (solve(i) // n_chunks, solve(i) % n_chunks, 0)),
            state,
        ],
        out_shape=[
            jax.ShapeDtypeStruct((batch, t_len, D_RWKV), F32),
            jax.ShapeDtypeStruct((batch, RWKV_PAIRS, LANES, LANES), F32),
        ],
        scratch_shapes=[
            pltpu.VMEM((bg, RWKV_PAIRS, LANES, LANES), F32),
            pltpu.VMEM((bg, 8, RWKV_PROJ), F32),
            wide(2 * chunk, BF16), wide(2 * chunk, BF16), wide(2 * chunk, BF16), wide(chunk, BF16),
            wide(chunk, F32), wide(chunk, F32), wide(1, F32),
        ],
        compiler_params=_params(("arbitrary",)),
        name="rwkv7_chunked",
    )(p_rw, shift_prev, h0_bd, w["mu"], w["w0"], w["a0"], w["k_k"], w["k_a"], w["r_k"], w["rw_ln_w"],
      w["rw_ln_b"], w["w_lora"])


def _mlstm_kernel(p_ref, conv_prev_ref, c0_ref, n0_ref, m0_ref, cw_ref, cb_ref, bif_ref, lnw_ref, lnb_ref,
                  h_ref, cout_ref, nout_ref, mout_ref, c_scr, n_scr, m_scr, xbuf, *, chunk, n_chunks):
    c_idx = pl.program_id(1)
    L = chunk
    d2 = 2 * D_MLSTM
    nh = MLSTM_HEADS

    @pl.when(c_idx == 0)
    def _():
        c_scr[...] = c0_ref[...]
        n_scr[...] = n0_ref[...]
        m_scr[...] = m0_ref[...]
        xbuf[:, 5:8, :] = conv_prev_ref[...]

    ti = lax.broadcasted_iota(jnp.int32, (L, L), 0)
    tj = lax.broadcasted_iota(jnp.int32, (L, L), 1)
    causal = tj <= ti
    ltri = causal.astype(BF16)
    utri = (ti <= tj).astype(BF16)
    sel = (lax.broadcasted_iota(jnp.int32, (8, LANES), 0)
           == lax.broadcasted_iota(jnp.int32, (8, LANES), 1)).astype(BF16)
    lane_g = lax.broadcasted_iota(jnp.int32, (L, LANES), 1)
    is_f = (lane_g >= nh) & (lane_g < 2 * nh)

    rows = []
    for bi in range(BATCH_GROUP):
        x = p_ref[bi, :, 0:d2]
        xbuf[bi, 8:8 + L, :] = x
        conv = cb_ref[...] + cw_ref[3:4, :] * x
        for j in range(CONV_W - 1):
            conv = conv + cw_ref[j:j + 1, :] * xbuf[bi, 5 + j:5 + j + L, :]
        xbuf[bi, 5:8, :] = x[L - 3:L, :]
        qk = _silu(conv)
        gates = p_ref[bi, :, 4 * D_MLSTM:] + bif_ref[...]
        glog = jnp.where(is_f, -_softplus(-gates), gates)
        parts = _split3(glog)
        b_cols = sum(_dot(ltri, t) for t in parts)
        g_rows = sum(_dot_nt(sel, t) for t in parts)
        b_rows = sum(_dot(t, utri) for t in _split3(g_rows))
        rows.append(dict(q=qk[:, 0:D_MLSTM], k=qk[:, D_MLSTM:d2] * (MLSTM_HEAD_DIM ** -0.5),
                         v=p_ref[bi, :, d2:d2 + D_MLSTM], o=_sigmoid(p_ref[bi, :, d2 + D_MLSTM:d2 + 2 * D_MLSTM]),
                         glog=glog, b_cols=b_cols, g_rows=g_rows, b_rows=b_rows))

    units = [(bi, h) for bi in range(BATCH_GROUP) for h in range(nh)]
    idx = range(len(units))
    head = lambda h: slice(h * MLSTM_HEAD_DIM, (h + 1) * MLSTM_HEAD_DIM)
    q = [rows[bi]["q"][:, head(h)] for bi, h in units]
    k = [rows[bi]["k"][:, head(h)] for bi, h in units]
    v = [rows[bi]["v"][:, head(h)] for bi, h in units]
    q_b = [t.astype(BF16) for t in q]
    k_b = [t.astype(BF16) for t in k]
    c_st = [c_scr[bi, h] for bi, h in units]
    n_st = [n_scr[bi, h:h + 1, :] for bi, h in units]
    m_prev = [m_scr[bi, :, h:h + 1] for bi, h in units]
    qk_t = [_dot_nt(q_b[u], k_b[u]) for u in idx]
    qc = [_dot_nt(q_b[u], c_st[u].astype(BF16)) for u in idx]
    s, w_inter, m_t, wk, carry, m_next = [], [], [], [], [], []
    for u, (bi, h) in enumerate(units):
        t = rows[bi]
        ig_row = t["g_rows"][h:h + 1, :]
        b_row = t["b_rows"][nh + h:nh + h + 1, :]
        ig_col = t["glog"][:, h:h + 1]
        b_col = t["b_cols"][:, nh + h:nh + h + 1]
        log_w = jnp.where(causal, b_col - b_row + ig_row, -jnp.inf)
        inter = b_col + m_prev[u]
        m_now = jnp.maximum(inter, jnp.max(log_w, axis=-1, keepdims=True))
        s.append(qk_t[u] * jnp.exp(log_w - m_now))
        w_inter.append(jnp.exp(inter - m_now))
        m_t.append(m_now)
        b_last = b_col[L - 1:L, :]
        g_col = b_last - b_col + ig_col
        m_new = jnp.maximum(b_last + m_prev[u], jnp.max(g_col, axis=0, keepdims=True))
        carry.append(jnp.exp(b_last + m_prev[u] - m_new))
        wk.append(jnp.exp(g_col - m_new))
        m_next.append(m_new)
    sv = [_dot(s[u].astype(BF16), v[u].astype(BF16)) for u in idx]
    upd = [_dot_tn((wk[u] * v[u]).astype(BF16), k_b[u]) for u in idx]
    qn = [jnp.sum(q[u] * n_st[u], axis=-1, keepdims=True) for u in idx]
    s_sum = [jnp.sum(s[u], axis=-1, keepdims=True) for u in idx]
    den = [jnp.maximum(jnp.abs(w_inter[u] * qn[u] + s_sum[u]), jnp.exp(-m_t[u])) for u in idx]
    hb = [rows[bi]["o"][:, head(h)] * ((w_inter[u] * qc[u] + sv[u]) / den[u]) for u, (bi, h) in enumerate(units)]
    mu = [jnp.mean(hb[u], axis=-1, keepdims=True) for u in idx]
    dev = [hb[u] - mu[u] for u in idx]
    var = [jnp.mean(dev[u] * dev[u], axis=-1, keepdims=True) for u in idx]
    n_new = [carry[u] * n_st[u] + jnp.sum(wk[u] * k[u], axis=0, keepdims=True) for u in idx]
    for u, (bi, h) in enumerate(units):
        sl = head(h)
        h_ref[bi, :, sl] = dev[u] * lax.rsqrt(var[u] + LN_EPS) * lnw_ref[:, sl] + lnb_ref[:, sl]
        c_scr[bi, h] = carry[u] * c_st[u] + upd[u]
        n_scr[bi, h:h + 1, :] = n_new[u]
        m_scr[bi, :, h:h + 1] = m_next[u]

    @pl.when(c_idx == n_chunks - 1)
    def _():
        cout_ref[...] = c_scr[...]
        nout_ref[...] = n_scr[...]
        mout_ref[...] = m_scr[...]


def _mlstm(p_ml, conv_prev, c0, n0, m0, w, chunk):
    batch, t_len, _ = p_ml.shape
    n_chunks = t_len // chunk
    bg = BATCH_GROUP
    seq = lambda b, c: (b, c, 0)
    vec = lambda n: pl.BlockSpec((1, n), lambda b, c: (0, 0))
    hd = MLSTM_HEAD_DIM
    kern = functools.partial(_mlstm_kernel, chunk=chunk, n_chunks=n_chunks)
    state_specs = [
        pl.BlockSpec((bg, MLSTM_HEADS, hd, hd), lambda b, c: (b, 0, 0, 0)),
        pl.BlockSpec((bg, MLSTM_HEADS, hd), lambda b, c: (b, 0, 0)),
        pl.BlockSpec((bg, 1, MLSTM_HEADS), lambda b, c: (b, 0, 0)),
    ]
    return pl.pallas_call(
        kern,
        grid=(batch // bg, n_chunks),
        in_specs=[
            pl.BlockSpec((bg, chunk, ML_COLS), seq),
            pl.BlockSpec((bg, CONV_W - 1, 2 * D_MLSTM), lambda b, c: (b, 0, 0)),
            *state_specs,
            pl.BlockSpec((CONV_W, 2 * D_MLSTM), lambda b, c: (0, 0)),
            vec(2 * D_MLSTM), vec(LANES), vec(D_MLSTM), vec(D_MLSTM),
        ],
        out_specs=[pl.BlockSpec((bg, chunk, D_MLSTM), seq), *state_specs],
        out_shape=[
            jax.ShapeDtypeStruct((batch, t_len, D_MLSTM), F32),
            jax.ShapeDtypeStruct((batch, MLSTM_HEADS, hd, hd), F32),
            jax.ShapeDtypeStruct((batch, MLSTM_HEADS, hd), F32),
            jax.ShapeDtypeStruct((batch, 1, MLSTM_HEADS), F32),
        ],
        scratch_shapes=[
            pltpu.VMEM((bg, MLSTM_HEADS, hd, hd), F32),
            pltpu.VMEM((bg, MLSTM_HEADS, hd), F32),
            pltpu.VMEM((bg, 1, MLSTM_HEADS), F32),
            pltpu.VMEM((bg, chunk + 8, 2 * D_MLSTM), F32),
        ],
        compiler_params=_params(("arbitrary", "arbitrary")),
        name="mlstm_chunkwise",
    )(p_ml, conv_prev, c0, n0, m0, w["conv_w"], w["conv_b"], w["b_if"], w["ml_ln_w"], w["ml_ln_b"])


def _tail_kernel(x_ref, ya_ref, hb_ref, pt_ref, mk_ref, mv_ref, wpa_ref, wpb_ref, wpc_ref, wout_ref,
                 lnw_ref, lnb_ref, o_ref):
    mk = mk_ref[0].astype(BF16)
    mv = mv_ref[0].astype(BF16)
    heads = []
    for h in range(MEM_HEADS):
        sl = slice(h * MEM_HEAD_DIM, (h + 1) * MEM_HEAD_DIM)
        s = _dot_nt(pt_ref[:, sl].astype(BF16), mk[:, sl]) * (MEM_HEAD_DIM ** -0.5)
        s = s - jnp.max(s, axis=-1, keepdims=True)
        e = jnp.exp(s)
        prob = e / jnp.sum(e, axis=-1, keepdims=True)
        heads.append(_dot(prob.astype(BF16), mv[:, sl]))
    attn = jnp.concatenate(heads, axis=-1)
    y_a = _bdot(ya_ref[...], wpa_ref[...])
    y_b = _bdot(hb_ref[...], wpb_ref[...])
    y_c = _bdot(attn, wpc_ref[...])
    d = D_MODEL
    mixed = (_sigmoid(pt_ref[:, D_MEM:D_MEM + d]) * y_a + _sigmoid(pt_ref[:, D_MEM + d:D_MEM + 2 * d]) * y_b
             + _sigmoid(pt_ref[:, D_MEM + 2 * d:D_MEM + 3 * d]) * y_c)
    u = _bdot(mixed, wout_ref[...])
    o_ref[...] = _layer_norm(DN_ALPHA * x_ref[...] + u, lnw_ref[...], lnb_ref[...])


def _tail(x, ya, hb, p_tail, mem_k, mem_v, w, tm, tiles_per_batch):
    n = x.shape[0]
    row = lambda i: (i, 0)
    mem = lambda i: (i // tiles_per_batch, 0, 0)
    return pl.pallas_call(
        _tail_kernel,
        grid=(n // tm,),
        in_specs=[
            pl.BlockSpec((tm, D_MODEL), row), pl.BlockSpec((tm, D_RWKV), row), pl.BlockSpec((tm, D_MLSTM), row),
            pl.BlockSpec((tm, TAIL_COLS), row),
            pl.BlockSpec((1, N_MEM, D_MEM), mem), pl.BlockSpec((1, N_MEM, D_MEM), mem),
            _const_spec((D_RWKV, D_MODEL)), _const_spec((D_MLSTM, D_MODEL)), _const_spec((D_MEM, D_MODEL)),
            _const_spec((D_MODEL, D_MODEL)), _const_spec((1, D_MODEL)), _const_spec((1, D_MODEL)),
        ],
        out_specs=pl.BlockSpec((tm, D_MODEL), row),
        out_shape=jax.ShapeDtypeStruct((n, D_MODEL), F32),
        compiler_params=_params(("arbitrary",)),
        name="attn_merge_ln",
    )(x, ya, hb, p_tail, mem_k, mem_v, w["w_pa"], w["w_pb"], w["w_pc"], w["w_out"], w["ln1_w"], w["ln1_b"])


def _moe_kernel(x_ref, wr_ref, rb_ref, w1_ref, w3_ref, w2_ref, ws1_ref, ws3_ref, ws2_ref, lnw_ref, lnb_ref, o_ref):
    x = x_ref[...]
    xb = x.astype(BF16)
    scores = _sigmoid(_dot(xb, wr_ref[...]))
    lane = lax.broadcasted_iota(jnp.int32, scores.shape, 1)
    work = jnp.where(lane < N_EXPERTS, scores + rb_ref[...], -jnp.inf)
    chosen = jnp.zeros(scores.shape, dtype=jnp.bool_)
    for _ in range(TOP_K):
        best = jnp.max(work, axis=-1, keepdims=True)
        first = jnp.min(jnp.where(work == best, lane, LANES), axis=-1, keepdims=True)
        pick = lane == first
        chosen = chosen | pick
        work = jnp.where(pick, -jnp.inf, work)
    sel = jnp.where(chosen, scores, 0.0)
    gates = sel / jnp.sum(sel, axis=-1, keepdims=True) * ROUTED_SCALE

    acc = _dot((_silu(_dot(xb, ws1_ref[...])) * _dot(xb, ws3_ref[...])).astype(BF16), ws2_ref[...])
    group = 8
    width = group * EXPERT_HIDDEN
    for s in range(N_EXPERTS // group):
        cols = slice(s * width, (s + 1) * width)
        hidden = _silu(_dot(xb, w1_ref[:, cols])) * _dot(xb, w3_ref[:, cols])
        gated = jnp.concatenate(
            [hidden[:, e * EXPERT_HIDDEN:(e + 1) * EXPERT_HIDDEN] * gates[:, s * group + e:s * group + e + 1]
             for e in range(group)], axis=-1)
        acc = acc + _dot(gated.astype(BF16), w2_ref[cols, :])
    o_ref[...] = _layer_norm(DN_ALPHA * x + acc, lnw_ref[...], lnb_ref[...])


def _moe(x, w, tm):
    n = x.shape[0]
    row = lambda i: (i, 0)
    hid = N_EXPERTS * EXPERT_HIDDEN
    return pl.pallas_call(
        _moe_kernel,
        grid=(n // tm,),
        in_specs=[
            pl.BlockSpec((tm, D_MODEL), row),
            _const_spec((D_MODEL, LANES)), _const_spec((1, LANES)),
            _const_spec((D_MODEL, hid)), _const_spec((D_MODEL, hid)), _const_spec((hid, D_MODEL)),
            _const_spec((D_MODEL, EXPERT_HIDDEN)), _const_spec((D_MODEL, EXPERT_HIDDEN)),
            _const_spec((EXPERT_HIDDEN, D_MODEL)),
            _const_spec((1, D_MODEL)), _const_spec((1, D_MODEL)),
        ],
        out_specs=pl.BlockSpec((tm, D_MODEL), row),
        out_shape=jax.ShapeDtypeStruct((n, D_MODEL), F32),
        compiler_params=_params(("arbitrary",)),
        name="moe_ln",
    )(x, w["w_router"], w["router_bias"], w["w_e1"], w["w_e3"], w["w_e2"], w["w_s1"], w["w_s3"], w["w_s2"],
      w["ln2_w"], w["ln2_b"])


def _prep_weights(w_in, b_in, rwkv_mu, rwkv_w0, rwkv_w2, rwkv_a0, rwkv_a2, rwkv_g2, rwkv_k_k, rwkv_k_a,
                  rwkv_r_k, rwkv_ln_w, rwkv_ln_b, mlstm_conv_w, mlstm_conv_b, mlstm_b_i, mlstm_b_f,
                  mlstm_ln_w, mlstm_ln_b, w_mem_kv, w_pa, w_pb, w_pc, w_out, ln1_w, ln1_b,
                  w_router, router_bias, w_e1, w_e3, w_e2, w_s1, w_s3, w_s2, ln2_w, ln2_b):
    off_ml = RWKV_PROJ
    off_if = off_ml + 4 * D_MLSTM
    off_mq = off_if + 2 * MLSTM_HEADS
    pad_if = LANES - 2 * MLSTM_HEADS
    row = lambda t: t.reshape(1, -1)
    hid = N_EXPERTS * EXPERT_HIDDEN
    w_lora = jnp.zeros((LORA_ALL, 3 * D_RWKV), F32)
    w_lora = w_lora.at[0:W_LORA, 0:D_RWKV].set(rwkv_w2)
    w_lora = w_lora.at[W_LORA:W_LORA + A_LORA, D_RWKV:2 * D_RWKV].set(rwkv_a2)
    w_lora = w_lora.at[W_LORA + A_LORA:, 2 * D_RWKV:].set(rwkv_g2)
    return dict(
        w_rw=w_in[:, :off_ml].astype(BF16), b_rw=row(b_in[:off_ml]),
        w_ml=jnp.pad(w_in[:, off_ml:off_mq], ((0, 0), (0, pad_if))).astype(BF16),
        b_ml=row(jnp.pad(b_in[off_ml:off_mq], (0, pad_if))),
        w_tail=w_in[:, off_mq:].astype(BF16), b_tail=row(b_in[off_mq:]),
        w_mem_k=w_mem_kv[:, :D_MEM].astype(BF16), w_mem_v=w_mem_kv[:, D_MEM:].astype(BF16),
        mu=row(rwkv_mu), w0=row(rwkv_w0), a0=row(rwkv_a0), k_k=row(rwkv_k_k), k_a=row(rwkv_k_a),
        r_k=row(rwkv_r_k), rw_ln_w=row(rwkv_ln_w), rw_ln_b=row(rwkv_ln_b), w_lora=w_lora.astype(BF16),
        conv_w=mlstm_conv_w, conv_b=row(mlstm_conv_b),
        b_if=row(jnp.pad(jnp.concatenate([mlstm_b_i, mlstm_b_f]), (0, pad_if))),
        ml_ln_w=row(mlstm_ln_w), ml_ln_b=row(mlstm_ln_b),
        w_pa=w_pa.astype(BF16), w_pb=w_pb.astype(BF16), w_pc=w_pc.astype(BF16), w_out=w_out.astype(BF16),
        ln1_w=row(ln1_w), ln1_b=row(ln1_b),
        w_router=jnp.pad(w_router, ((0, 0), (0, LANES - N_EXPERTS))).astype(BF16),
        router_bias=row(jnp.pad(router_bias, (0, LANES - N_EXPERTS))),
        w_e1=jnp.transpose(w_e1, (1, 0, 2)).reshape(D_MODEL, hid).astype(BF16),
        w_e3=jnp.transpose(w_e3, (1, 0, 2)).reshape(D_MODEL, hid).astype(BF16),
        w_e2=w_e2.reshape(hid, D_MODEL).astype(BF16),
        w_s1=w_s1.astype(BF16), w_s3=w_s3.astype(BF16), w_s2=w_s2.astype(BF16),
        ln2_w=row(ln2_w), ln2_b=row(ln2_b),
    )


def _pair_state(wkv):
    bsz = wkv.shape[0]
    s = wkv.reshape(bsz, RWKV_PAIRS, 2, RWKV_HEAD_DIM, RWKV_HEAD_DIM)
    z = jnp.zeros_like(s[:, :, 0])
    top = jnp.concatenate([s[:, :, 0], z], axis=-1)
    bot = jnp.concatenate([z, s[:, :, 1]], axis=-1)
    return jnp.concatenate([top, bot], axis=-2)


def _unpair_state(h_bd):
    bsz = h_bd.shape[0]
    hd = RWKV_HEAD_DIM
    heads = jnp.stack([h_bd[:, :, :hd, :hd], h_bd[:, :, hd:, hd:]], axis=2)
    return heads.reshape(bsz, RWKV_HEADS, hd, hd)


def _trunk(x3, mem_k, mem_v, shift_prev, wkv0, conv_prev, c0, n0, m0, w, chunk_rw, chunk_ml, tm_proj, tm_tail,
           tm_moe):
    bsz, t_len, _ = x3.shape
    n = bsz * t_len
    x = x3.reshape(n, D_MODEL)
    xb = x.astype(BF16)
    p_rw = _matmul_bias(xb, w["w_rw"], w["b_rw"], tm_proj, RWKV_PROJ // 2)
    p_ml = _matmul_bias(xb, w["w_ml"], w["b_ml"], tm_proj, ML_COLS // 3)
    p_tail = _matmul_bias(xb, w["w_tail"], w["b_tail"], tm_proj, TAIL_COLS // 2)

    p_rw3 = p_rw.reshape(bsz, t_len, RWKV_PROJ)
    p_ml3 = p_ml.reshape(bsz, t_len, ML_COLS)
    ya, h_bd = _rwkv(p_rw3, shift_prev, _pair_state(wkv0), w, chunk_rw)
    hb, c_t, n_t, m_t = _mlstm(p_ml3, conv_prev, c0, n0, m0.reshape(bsz, 1, MLSTM_HEADS), w, chunk_ml)
    x1 = _tail(x, ya.reshape(n, D_RWKV), hb.reshape(n, D_MLSTM), p_tail, mem_k, mem_v, w, tm_tail,
               t_len // tm_tail)
    y = _moe(x1, w, tm_moe)

    new_shift = p_rw3[:, t_len - 1:, :]
    new_conv = p_ml3[:, t_len - (CONV_W - 1):, :2 * D_MLSTM]
    states = (new_shift, _unpair_state(h_bd), new_conv, c_t, n_t, m_t.reshape(bsz, MLSTM_HEADS))
    return y.reshape(bsz, t_len, D_MODEL), states


def kernel(x_prompt, x_sample, state_rwkv_shift, state_rwkv_wkv, state_mlstm_conv, state_mlstm_c, state_mlstm_n, state_mlstm_m, cache_mem_k, cache_mem_v, mem_prompt, w_in, b_in, rwkv_mu, rwkv_w0, rwkv_w2, rwkv_a0, rwkv_a2, rwkv_g2, rwkv_k_k, rwkv_k_a, rwkv_r_k, rwkv_ln_w, rwkv_ln_b, mlstm_conv_w, mlstm_conv_b, mlstm_b_i, mlstm_b_f, mlstm_ln_w, mlstm_ln_b, w_mem_kv, w_pa, w_pb, w_pc, w_out, ln1_w, ln1_b, w_router, router_bias, w_e1, w_e3, w_e2, w_s1, w_s3, w_s2, ln2_w, ln2_b):
    weights = (w_in, b_in, rwkv_mu, rwkv_w0, rwkv_w2, rwkv_a0, rwkv_a2, rwkv_g2, rwkv_k_k, rwkv_k_a, rwkv_r_k,
               rwkv_ln_w, rwkv_ln_b, mlstm_conv_w, mlstm_conv_b, mlstm_b_i, mlstm_b_f, mlstm_ln_w, mlstm_ln_b,
               w_mem_kv, w_pa, w_pb, w_pc, w_out, ln1_w, ln1_b, w_router, router_bias, w_e1, w_e3, w_e2,
               w_s1, w_s3, w_s2, ln2_w, ln2_b)
    w = _prep_weights(*(t[0] for t in weights))
    bp, t_p, _ = x_prompt.shape
    bs, t_s, _ = x_sample.shape
    hd = MLSTM_HEAD_DIM

    mem_flat = mem_prompt.reshape(bp * N_MEM, D_MODEL).astype(BF16)
    zero_bias = jnp.zeros((1, D_MEM), F32)
    mk = _matmul_bias(mem_flat, w["w_mem_k"], zero_bias, bp * N_MEM, D_MEM).reshape(bp, N_MEM, D_MEM)
    mv = _matmul_bias(mem_flat, w["w_mem_v"], zero_bias, bp * N_MEM, D_MEM).reshape(bp, N_MEM, D_MEM)

    y_p, st_p = _trunk(
        x_prompt, mk, mv,
        jnp.zeros((bp, 1, RWKV_PROJ), F32), jnp.zeros((bp, RWKV_HEADS, RWKV_HEAD_DIM, RWKV_HEAD_DIM), F32),
        jnp.zeros((bp, CONV_W - 1, 2 * D_MLSTM), F32), jnp.zeros((bp, MLSTM_HEADS, hd, hd), F32),
        jnp.zeros((bp, MLSTM_HEADS, hd), F32), jnp.zeros((bp, MLSTM_HEADS), F32),
        w, chunk_rw=64, chunk_ml=min(MLSTM_CHUNK, t_p), tm_proj=1024, tm_tail=256, tm_moe=512)
    y_s, st_s = _trunk(
        x_sample, cache_mem_k[0].reshape(bs, N_MEM, D_MEM), cache_mem_v[0].reshape(bs, N_MEM, D_MEM),
        state_rwkv_shift[0], state_rwkv_wkv[0], state_mlstm_conv[0], state_mlstm_c[0], state_mlstm_n[0],
        state_mlstm_m[0],
        w, chunk_rw=t_s, chunk_ml=min(MLSTM_CHUNK, t_s), tm_proj=bs * t_s, tm_tail=t_s, tm_moe=bs * t_s)

    lead = lambda t: t[None]
    mem_shape = (1, bp, N_MEM, MEM_HEADS, MEM_HEAD_DIM)
    return (y_p, y_s, *(lead(t) for t in st_p), mk.reshape(mem_shape), mv.reshape(mem_shape),
            *(lead(t) for t in st_s))
```

```python
import functools
import itertools

import jax
import jax.numpy as jnp
from jax import lax
from jax.experimental import pallas as pl
from jax.experimental.pallas import tpu as pltpu

F32 = jnp.float32
BF16 = jnp.bfloat16

D_MODEL = 1024
DEPTH = 1
D_RWKV = 1024
RWKV_HEAD_DIM = 64
RWKV_HEADS = 16
W_LORA = 64
A_LORA = 64
G_LORA = 128
LORA_ALL = W_LORA + A_LORA + G_LORA
RWKV_PROJ = 3 * D_RWKV + LORA_ALL
RWKV_GN_EPS = 64e-5
D_MLSTM = 1024
MLSTM_HEADS = 4
MLSTM_HEAD_DIM = 256
CONV_W = 4
MLSTM_CHUNK = 64
N_MEM = 256
MEM_HEADS = 4
MEM_HEAD_DIM = 256
D_MEM = 1024
N_EXPERTS = 32
TOP_K = 4
EXPERT_HIDDEN = 128
ROUTED_SCALE = 2.5
DN_ALPHA = (2 * DEPTH) ** 0.25
LN_EPS = 1e-5

LANES = 128
RWKV_PAIRS = D_RWKV // LANES
ML_COLS = 4 * D_MLSTM + LANES
VMEM_LIMIT = 56 * 1024 * 1024
BATCH_GROUP = 2


def _dot(a, b):
    return jnp.dot(a, b, preferred_element_type=F32)


def _dot_nt(a, b):
    return lax.dot_general(a, b, (((1,), (1,)), ((), ())), preferred_element_type=F32)


def _dot_tn(a, b):
    return lax.dot_general(a, b, (((0,), (0,)), ((), ())), preferred_element_type=F32)


def _split(a):
    hi = a.astype(BF16)
    return hi, (a - hi.astype(F32)).astype(BF16)


def _split3(a):
    hi = a.astype(BF16)
    rest = a - hi.astype(F32)
    mid = rest.astype(BF16)
    return hi, mid, (rest - mid.astype(F32)).astype(BF16)


def _bdot(a, b):
    return _dot(a.astype(BF16), b.astype(BF16))


def _sigmoid(x):
    return 1.0 / (1.0 + jnp.exp(-x))


def _softplus(x):
    return jnp.maximum(x, 0.0) + jnp.log(1.0 + jnp.exp(-jnp.abs(x)))


def _silu(x):
    return x * _sigmoid(x)


def _layer_norm(x, w, b):
    mu = jnp.mean(x, axis=-1, keepdims=True)
    d = x - mu
    var = jnp.mean(d * d, axis=-1, keepdims=True)
    return d * lax.rsqrt(var + LN_EPS) * w + b


def _params(semantics):
    return pltpu.CompilerParams(dimension_semantics=semantics, vmem_limit_bytes=VMEM_LIMIT)


def _const_spec(shape):
    zeros = (0,) * len(shape)
    return pl.BlockSpec(shape, lambda *_: zeros, pipeline_mode=pl.Buffered(1))


def _mm_kernel(x_ref, w_ref, b_ref, o_ref):
    o_ref[...] = (_dot(x_ref[...].astype(BF16), w_ref[...]) + b_ref[...]).astype(o_ref.dtype)


def _matmul_bias(x, w, b, tm, tn, out_dtype=F32):
    n, k = x.shape
    nc = w.shape[1]
    return pl.pallas_call(
        _mm_kernel,
        grid=(nc // tn, n // tm),
        in_specs=[
            pl.BlockSpec((tm, k), lambda j, i: (i, 0)),
            pl.BlockSpec((k, tn), lambda j, i: (0, j)),
            pl.BlockSpec((1, tn), lambda j, i: (0, j)),
        ],
        out_specs=pl.BlockSpec((tm, tn), lambda j, i: (i, j)),
        out_shape=jax.ShapeDtypeStruct((n, nc), out_dtype),
        compiler_params=_params(("arbitrary", "arbitrary")),
        name="matmul_bias",
    )(x, w, b)


def _rwkv_kernel(p_ref, prev_ref, h0_ref, mu_ref, w0_ref, a0_ref, kk_ref, ka_ref, rk_ref, lnw_ref, lnb_ref,
                 wl_ref, y_ref, hout_ref, h_scr, xbuf, ar_scr, bk_scr, bkh_scr, v_scr, bonus_scr, g_scr, gamma_scr,
                 *, chunk, n_chunks, n_items):
    step = pl.program_id(0)
    L = chunk
    half = 2 * L
    prep_chunk = jnp.minimum(step, n_items - 1) % n_chunks
    solve_chunk = jnp.maximum(step - 1, 0) % n_chunks

    @pl.when(step == 0)
    def _():
        for ref in (ar_scr, bk_scr, bkh_scr, v_scr, bonus_scr, g_scr):
            ref[...] = jnp.zeros(ref.shape, ref.dtype)
        gamma_scr[...] = jnp.ones(gamma_scr.shape, F32)

    @pl.when(solve_chunk == 0)
    def _():
        h_scr[...] = h0_ref[...]

    @pl.when(prep_chunk == 0)
    def _():
        for bi in range(BATCH_GROUP):
            xbuf[bi, 0:1, :] = prev_ref[bi]

    ti = lax.broadcasted_iota(jnp.int32, (L, L), 0)
    tj = lax.broadcasted_iota(jnp.int32, (L, L), 1)
    ltri = (tj <= ti).astype(BF16)
    row_m = lax.broadcasted_iota(jnp.int32, (L, half), 0)
    col_m = lax.broadcasted_iota(jnp.int32, (L, half), 1)
    col_in = jnp.where(col_m >= L, col_m - L, col_m)
    strict = col_in < row_m
    incl = col_in <= row_m
    eye = (col_in == row_m).astype(F32)
    gi = lax.broadcasted_iota(jnp.int32, (LANES, LANES), 0)
    gj = lax.broadcasted_iota(jnp.int32, (LANES, LANES), 1)
    same_head = (gi >= RWKV_HEAD_DIM) == (gj >= RWKV_HEAD_DIM)
    head_ones = same_head.astype(BF16)
    first_nat = lax.broadcasted_iota(jnp.int32, (L, LANES), 1) < RWKV_HEAD_DIM
    first_sbs = col_m < L
    lane_l = lax.broadcasted_iota(jnp.int32, (L, LORA_ALL), 1)

    def diag(x, first):
        zero = jnp.zeros_like(x)
        return jnp.concatenate([jnp.where(first, x, zero), jnp.where(first, zero, x)], axis=0)

    def seg_sum(x):
        return _dot(x.astype(BF16), head_ones)

    units = [(bi, p) for bi in range(BATCH_GROUP) for p in range(RWKV_PAIRS)]
    idx = range(len(units))
    lane_of = lambda p: slice(p * LANES, (p + 1) * LANES)

    ar = [ar_scr[bi, :, lane_of(p)] for bi, p in units]
    bk = [bk_scr[bi, :, lane_of(p)] for bi, p in units]
    bkh = [bkh_scr[bi, :, lane_of(p)] for bi, p in units]
    v_b = [v_scr[bi, :, lane_of(p)] for bi, p in units]
    bonus = [bonus_scr[bi, :, lane_of(p)] for bi, p in units]
    gate = [g_scr[bi, :, lane_of(p)] for bi, p in units]
    gamma = [gamma_scr[bi, :, lane_of(p)] for bi, p in units]

    first_row = lax.broadcasted_iota(jnp.int32, (L, RWKV_PROJ), 0) == 0

    def prepare_row(bi):
        cur = p_ref[bi]
        prev = jnp.where(first_row, xbuf[bi, 0:1, :], pltpu.roll(cur, 1, 0))
        xbuf[bi, 0:1, :] = cur[L - 1:L, :]
        xr = cur + (prev - cur) * mu_ref[...]
        r_all = xr[:, 0:D_RWKV]
        k_all = xr[:, D_RWKV:2 * D_RWKV]
        v_all = xr[:, 2 * D_RWKV:3 * D_RWKV]
        slab = xr[:, 3 * D_RWKV:]
        act = jnp.where(lane_l < W_LORA, jnp.tanh(slab),
                        jnp.where(lane_l < W_LORA + A_LORA, slab, _sigmoid(slab)))
        yield
        lora = _bdot(act, wl_ref[...])
        yield
        w_log = -_softplus(-(w0_ref[...] + lora[:, 0:D_RWKV])) - 0.5
        lw = -jnp.exp(w_log)
        a_sig = _sigmoid(a0_ref[...] + lora[:, D_RWKV:2 * D_RWKV])
        lw_hi, lw_lo = _split(lw)
        yield
        kk0 = k_all * kk_ref[...]
        k2 = k_all * (1.0 + (a_sig - 1.0) * ka_ref[...])
        sums = jnp.concatenate([kk0 * kk0, r_all * k2 * rk_ref[...]], axis=0).astype(BF16)
        yield
        cum = _dot(ltri, lw_hi) + _dot(ltri, lw_lo)
        sums = jnp.concatenate([_dot(sums[:, lane_of(p)], head_ones) for p in range(RWKV_PAIRS)], axis=1)
        yield
        kk = kk0 * lax.rsqrt(jnp.maximum(sums[0:L], 1e-24))
        b_vec = kk * a_sig
        c_last = cum[L - 1:L, :]
        e_neg = jnp.exp(-cum)
        e_end = jnp.exp(c_last - cum)
        yield
        ar_scr[bi] = jnp.concatenate([-kk * jnp.exp(cum - lw), r_all * jnp.exp(cum)], axis=0).astype(BF16)
        bk_scr[bi] = jnp.concatenate([b_vec * e_neg, k2 * e_neg], axis=0).astype(BF16)
        yield
        bkh_scr[bi] = jnp.concatenate([b_vec * e_end, k2 * e_end], axis=0).astype(BF16)
        v_scr[bi] = v_all.astype(BF16)
        bonus_scr[bi] = sums[L:] * v_all
        g_scr[bi] = lora[:, 2 * D_RWKV:]
        gamma_scr[bi] = jnp.exp(c_last)

    segments = itertools.chain(*(prepare_row(bi) for bi in range(BATCH_GROUP)))

    def prepare_some(count):
        for _ in range(count):
            next(segments, None)

    n_sq = L.bit_length() - 1
    per_stage = 2
    bk_d = [jnp.concatenate([diag(bk[u][0:L], first_nat), diag(bk[u][L:], first_nat)], axis=0) for u in idx]
    v_d = [diag(v_b[u], first_nat) for u in idx]
    h_bd = [h_scr[bi, p] for bi, p in units]
    sc = [_dot_nt(ar[u], bk_d[u]) for u in idx]
    prepare_some(per_stage)
    arh = [_dot_nt(ar[u], h_bd[u].astype(BF16)) for u in idx]
    m_ab = [jnp.where(strict, sc[u][0:L, 0:half], 0.0) for u in idx]
    m_ak = [jnp.where(strict, sc[u][0:L, half:], 0.0).astype(BF16) for u in idx]
    n_rbk = [jnp.concatenate([jnp.where(incl, sc[u][L:, 0:half], 0.0),
                              jnp.where(incl, sc[u][L:, half:], 0.0)], axis=1).astype(BF16) for u in idx]
    prepare_some(per_stage)
    w0 = [arh[u][0:L] + _dot(m_ak[u], v_d[u]) for u in idx]
    inv = [eye + m_ab[u] for u in idx]
    m_b = [m_ab[u].astype(BF16) for u in idx]
    pw = [_dot(m_b[u], diag(m_b[u], first_sbs)) for u in idx]
    prepare_some(per_stage)
    for i in range(n_sq - 1):
        pw_d = [diag(pw[u].astype(BF16), first_sbs) for u in idx]
        if i + 1 < n_sq - 1:
            z = [_dot(jnp.concatenate([inv[u], pw[u]], axis=0).astype(BF16), pw_d[u]) for u in idx]
            inv = [inv[u] + z[u][0:L] for u in idx]
            pw = [z[u][L:] for u in idx]
        else:
            inv = [inv[u] + _dot(inv[u].astype(BF16), pw_d[u]) for u in idx]
        prepare_some(per_stage)
    c_b = [_dot(inv[u].astype(BF16), diag(w0[u].astype(BF16), first_nat)).astype(BF16) for u in idx]
    cv = [jnp.concatenate([c_b[u], v_b[u]], axis=0) for u in idx]
    y = [arh[u][L:] + _dot(n_rbk[u], jnp.concatenate([diag(c_b[u], first_nat), v_d[u]], axis=0)) for u in idx]
    for u, (bi, p) in enumerate(units):
        h_scr[bi, p] = h_bd[u] * gamma[u] + jnp.where(same_head, _dot_tn(cv[u], bkh[u]), 0.0)
    mean = [seg_sum(y[u]) * (1.0 / RWKV_HEAD_DIM) for u in idx]
    dev = [y[u] - mean[u] for u in idx]
    var = [seg_sum(dev[u] * dev[u]) * (1.0 / RWKV_HEAD_DIM) for u in idx]
    for u, (bi, p) in enumerate(units):
        s = lane_of(p)
        yn = dev[u] * lax.rsqrt(var[u] + RWKV_GN_EPS) * lnw_ref[:, s] + lnb_ref[:, s]
        y_ref[bi, :, s] = ((yn + bonus[u]) * gate[u]).astype(y_ref.dtype)
    prepare_some(8 * BATCH_GROUP)

    @pl.when((solve_chunk == n_chunks - 1) & (step > 0))
    def _():
        hout_ref[...] = h_scr[...]


def _rwkv(p_rw, shift_prev, h0_bd, w, chunk):
    batch, t_len, _ = p_rw.shape
    n_chunks = t_len // chunk
    bg = BATCH_GROUP
    n_items = (batch // bg) * n_chunks
    prep = lambda i: jnp.minimum(i, n_items - 1)
    solve = lambda i: jnp.maximum(i - 1, 0)
    vec = lambda n: pl.BlockSpec((1, n), lambda i: (0, 0))
    state = pl.BlockSpec((bg, RWKV_PAIRS, LANES, LANES), lambda i: (solve(i) // n_chunks, 0, 0, 0))
    kern = functools.partial(_rwkv_kernel, chunk=chunk, n_chunks=n_chunks, n_items=n_items)
    wide = lambda rows, dtype: pltpu.VMEM((bg, rows, D_RWKV), dtype)
    return pl.pallas_call(
        kern,
        grid=(n_items + 1,),
        in_specs=[
            pl.BlockSpec((bg, chunk, RWKV_PROJ), lambda i: (prep(i) // n_chunks, prep(i) % n_chunks, 0)),
            pl.BlockSpec((bg, 1, RWKV_PROJ), lambda i: (prep(i) // n_chunks, 0, 0)),
            state,
            vec(RWKV_PROJ), vec(D_RWKV), vec(D_RWKV), vec(D_RWKV), vec(D_RWKV), vec(D_RWKV), vec(D_RWKV),
            vec(D_RWKV),
            pl.BlockSpec((LORA_ALL, 3 * D_RWKV), lambda i: (0, 0)),
        ],
        out_specs=[
            pl.BlockSpec((bg, chunk, D_RWKV), lambda i: (solve(i) // n_chunks, solve(i) % n_chunks, 0)),
            state,
        ],
        out_shape=[
            jax.ShapeDtypeStruct((batch, t_len, D_RWKV), BF16),
            jax.ShapeDtypeStruct((batch, RWKV_PAIRS, LANES, LANES), F32),
        ],
        scratch_shapes=[
            pltpu.VMEM((bg, RWKV_PAIRS, LANES, LANES), F32),
            pltpu.VMEM((bg, 8, RWKV_PROJ), F32),
            wide(2 * chunk, BF16), wide(2 * chunk, BF16), wide(2 * chunk, BF16), wide(chunk, BF16),
            wide(chunk, F32), wide(chunk, F32), wide(1, F32),
        ],
        compiler_params=_params(("arbitrary",)),
        name="rwkv7_chunked",
    )(p_rw, shift_prev, h0_bd, w["mu"], w["w0"], w["a0"], w["k_k"], w["k_a"], w["r_k"], w["rw_ln_w"],
      w["rw_ln_b"], w["w_lora"])


def _mlstm_kernel(p_ref, conv_prev_ref, c0_ref, n0_ref, m0_ref, cw_ref, cb_ref, bif_ref, lnw_ref, lnb_ref,
                  h_ref, cout_ref, nout_ref, mout_ref, c_scr, n_scr, m_scr, xbuf, *, chunk, n_chunks):
    c_idx = pl.program_id(1)
    L = chunk
    d2 = 2 * D_MLSTM
    nh = MLSTM_HEADS

    @pl.when(c_idx == 0)
    def _():
        c_scr[...] = c0_ref[...]
        n_scr[...] = n0_ref[...]
        m_scr[...] = m0_ref[...]
        xbuf[:, 0:8 - (CONV_W - 1), :] = jnp.zeros((BATCH_GROUP, 8 - (CONV_W - 1), d2), F32)
        xbuf[:, 8 - (CONV_W - 1):8, :] = conv_prev_ref[...]

    row8 = lax.broadcasted_iota(jnp.int32, (8, d2), 0)
    ti = lax.broadcasted_iota(jnp.int32, (L, L), 0)
    tj = lax.broadcasted_iota(jnp.int32, (L, L), 1)
    causal = tj <= ti
    ltri = causal.astype(BF16)
    utri = (ti <= tj).astype(BF16)
    sel = (lax.broadcasted_iota(jnp.int32, (8, LANES), 0)
           == lax.broadcasted_iota(jnp.int32, (8, LANES), 1)).astype(BF16)
    lane_g = lax.broadcasted_iota(jnp.int32, (L, LANES), 1)
    is_f = (lane_g >= nh) & (lane_g < 2 * nh)

    rows = []
    for bi in range(BATCH_GROUP):
        x = p_ref[bi, :, 0:d2]
        tail8 = xbuf[bi]
        conv = cb_ref[...] + cw_ref[CONV_W - 1:CONV_W, :] * x
        for j in range(CONV_W - 1):
            lag = CONV_W - 1 - j
            rolled = pltpu.roll(x, lag, 0)
            top = jnp.where(row8 < lag, pltpu.roll(tail8, lag, 0), rolled[0:8])
            conv = conv + cw_ref[j:j + 1, :] * jnp.concatenate([top, rolled[8:]], axis=0)
        xbuf[bi] = x[L - 8:L, :]
        qk = _silu(conv)
        gates = p_ref[bi, :, 4 * D_MLSTM:] + bif_ref[...]
        glog = jnp.where(is_f, -_softplus(-gates), gates)
        parts = _split3(glog)
        b_cols = sum(_dot(ltri, t) for t in parts)
        g_rows = sum(_dot_nt(sel, t) for t in parts)
        b_rows = sum(_dot(t, utri) for t in _split3(g_rows))
        rows.append(dict(q=qk[:, 0:D_MLSTM], k=qk[:, D_MLSTM:d2] * (MLSTM_HEAD_DIM ** -0.5),
                         v=p_ref[bi, :, d2:d2 + D_MLSTM], o=_sigmoid(p_ref[bi, :, d2 + D_MLSTM:d2 + 2 * D_MLSTM]),
                         glog=glog, b_cols=b_cols, g_rows=g_rows, b_rows=b_rows))

    units = [(bi, h) for bi in range(BATCH_GROUP) for h in range(nh)]
    idx = range(len(units))
    head = lambda h: slice(h * MLSTM_HEAD_DIM, (h + 1) * MLSTM_HEAD_DIM)
    q = [rows[bi]["q"][:, head(h)] for bi, h in units]
    k = [rows[bi]["k"][:, head(h)] for bi, h in units]
    v = [rows[bi]["v"][:, head(h)] for bi, h in units]
    q_b = [t.astype(BF16) for t in q]
    k_b = [t.astype(BF16) for t in k]
    c_st = [c_scr[bi, h] for bi, h in units]
    n_st = [n_scr[bi, h:h + 1, :] for bi, h in units]
    m_prev = [m_scr[bi, :, h:h + 1] for bi, h in units]
    qk_t = [_dot_nt(q_b[u], k_b[u]) for u in idx]
    qc = [_dot_nt(q_b[u], c_st[u].astype(BF16)) for u in idx]
    s, w_inter, m_t, wk, carry, m_next = [], [], [], [], [], []
    for u, (bi, h) in enumerate(units):
        t = rows[bi]
        ig_row = t["g_rows"][h:h + 1, :]
        b_row = t["b_rows"][nh + h:nh + h + 1, :]
        ig_col = t["glog"][:, h:h + 1]
        b_col = t["b_cols"][:, nh + h:nh + h + 1]
        log_w = jnp.where(causal, b_col - b_row + ig_row, -jnp.inf)
        inter = b_col + m_prev[u]
        m_now = jnp.maximum(inter, jnp.max(log_w, axis=-1, keepdims=True))
        s.append(qk_t[u] * jnp.exp(log_w - m_now))
        w_inter.append(jnp.exp(inter - m_now))
        m_t.append(m_now)
        b_last = b_col[L - 1:L, :]
        g_col = b_last - b_col + ig_col
        m_new = jnp.maximum(b_last + m_prev[u], jnp.max(g_col, axis=0, keepdims=True))
        carry.append(jnp.exp(b_last + m_prev[u] - m_new))
        wk.append(jnp.exp(g_col - m_new))
        m_next.append(m_new)
    sv = [_dot(s[u].astype(BF16), v[u].astype(BF16)) for u in idx]
    upd = [_dot_tn((wk[u] * v[u]).astype(BF16), k_b[u]) for u in idx]
    qn = [jnp.sum(q[u] * n_st[u], axis=-1, keepdims=True) for u in idx]
    s_sum = [jnp.sum(s[u], axis=-1, keepdims=True) for u in idx]
    den = [jnp.maximum(jnp.abs(w_inter[u] * qn[u] + s_sum[u]), jnp.exp(-m_t[u])) for u in idx]
    hb = [rows[bi]["o"][:, head(h)] * ((w_inter[u] * qc[u] + sv[u]) / den[u]) for u, (bi, h) in enumerate(units)]
    mu = [jnp.mean(hb[u], axis=-1, keepdims=True) for u in idx]
    dev = [hb[u] - mu[u] for u in idx]
    var = [jnp.mean(dev[u] * dev[u], axis=-1, keepdims=True) for u in idx]
    n_new = [carry[u] * n_st[u] + jnp.sum(wk[u] * k[u], axis=0, keepdims=True) for u in idx]
    for u, (bi, h) in enumerate(units):
        sl = head(h)
        h_ref[bi, :, sl] = (dev[u] * lax.rsqrt(var[u] + LN_EPS) * lnw_ref[:, sl] + lnb_ref[:, sl]).astype(h_ref.dtype)
        c_scr[bi, h] = carry[u] * c_st[u] + upd[u]
        n_scr[bi, h:h + 1, :] = n_new[u]
        m_scr[bi, :, h:h + 1] = m_next[u]

    @pl.when(c_idx == n_chunks - 1)
    def _():
        cout_ref[...] = c_scr[...]
        nout_ref[...] = n_scr[...]
        mout_ref[...] = m_scr[...]


def _mlstm(p_ml, conv_prev, c0, n0, m0, w, chunk):
    batch, t_len, _ = p_ml.shape
    n_chunks = t_len // chunk
    bg = BATCH_GROUP
    seq = lambda b, c: (b, c, 0)
    vec = lambda n: pl.BlockSpec((1, n), lambda b, c: (0, 0))
    hd = MLSTM_HEAD_DIM
    kern = functools.partial(_mlstm_kernel, chunk=chunk, n_chunks=n_chunks)
    state_specs = [
        pl.BlockSpec((bg, MLSTM_HEADS, hd, hd), lambda b, c: (b, 0, 0, 0)),
        pl.BlockSpec((bg, MLSTM_HEADS, hd), lambda b, c: (b, 0, 0)),
        pl.BlockSpec((bg, 1, MLSTM_HEADS), lambda b, c: (b, 0, 0)),
    ]
    return pl.pallas_call(
        kern,
        grid=(batch // bg, n_chunks),
        in_specs=[
            pl.BlockSpec((bg, chunk, ML_COLS), seq),
            pl.BlockSpec((bg, CONV_W - 1, 2 * D_MLSTM), lambda b, c: (b, 0, 0)),
            *state_specs,
            pl.BlockSpec((CONV_W, 2 * D_MLSTM), lambda b, c: (0, 0)),
            vec(2 * D_MLSTM), vec(LANES), vec(D_MLSTM), vec(D_MLSTM),
        ],
        out_specs=[pl.BlockSpec((bg, chunk, D_MLSTM), seq), *state_specs],
        out_shape=[
            jax.ShapeDtypeStruct((batch, t_len, D_MLSTM), BF16),
            jax.ShapeDtypeStruct((batch, MLSTM_HEADS, hd, hd), F32),
            jax.ShapeDtypeStruct((batch, MLSTM_HEADS, hd), F32),
            jax.ShapeDtypeStruct((batch, 1, MLSTM_HEADS), F32),
        ],
        scratch_shapes=[
            pltpu.VMEM((bg, MLSTM_HEADS, hd, hd), F32),
            pltpu.VMEM((bg, MLSTM_HEADS, hd), F32),
            pltpu.VMEM((bg, 1, MLSTM_HEADS), F32),
            pltpu.VMEM((bg, 8, 2 * D_MLSTM), F32),
        ],
        compiler_params=_params(("arbitrary", "arbitrary")),
        name="mlstm_chunkwise",
    )(p_ml, conv_prev, c0, n0, m0, w["conv_w"], w["conv_b"], w["b_if"], w["ml_ln_w"], w["ml_ln_b"])


def _tail_kernel(x_ref, ya_ref, hb_ref, mq_ref, gate_ref, mk_ref, mv_ref, wpa_ref, wpb_ref, wpc_ref, wout_ref,
                 lnw_ref, lnb_ref, o_ref, *, rows):
    attn = []
    for b in range(mk_ref.shape[0]):
        q = mq_ref[b * rows:(b + 1) * rows, :]
        mk = mk_ref[b].astype(BF16)
        mv = mv_ref[b].astype(BF16)
        heads = []
        for h in range(MEM_HEADS):
            sl = slice(h * MEM_HEAD_DIM, (h + 1) * MEM_HEAD_DIM)
            s = _dot_nt(q[:, sl], mk[:, sl]) * (MEM_HEAD_DIM ** -0.5)
            e = jnp.exp(s - jnp.max(s, axis=-1, keepdims=True))
            prob = e / jnp.sum(e, axis=-1, keepdims=True)
            heads.append(_dot(prob.astype(BF16), mv[:, sl]))
        attn.append(jnp.concatenate(heads, axis=-1).astype(BF16))
    attn = attn[0] if len(attn) == 1 else jnp.concatenate(attn, axis=0)
    y_a = _dot(ya_ref[...], wpa_ref[...])
    y_b = _dot(hb_ref[...], wpb_ref[...])
    y_c = _dot(attn, wpc_ref[...])
    d = D_MODEL
    mixed = (_sigmoid(gate_ref[:, 0:d]) * y_a + _sigmoid(gate_ref[:, d:2 * d]) * y_b
             + _sigmoid(gate_ref[:, 2 * d:3 * d]) * y_c)
    u = _bdot(mixed, wout_ref[...])
    o_ref[...] = _layer_norm(DN_ALPHA * x_ref[...] + u, lnw_ref[...], lnb_ref[...])


def _tail(x, ya, hb, p_mq, p_gate, mem_k, mem_v, w, rows, batches):
    n = x.shape[0]
    tm = rows * batches
    tiles_per_batch = (n // mem_k.shape[0]) // rows if batches == 1 else 1
    row = lambda i: (i, 0)
    mem = pl.BlockSpec((batches, N_MEM, D_MEM), lambda i: (i // tiles_per_batch, 0, 0))
    tok = lambda cols: pl.BlockSpec((tm, cols), row)
    return pl.pallas_call(
        functools.partial(_tail_kernel, rows=rows),
        grid=(n // tm,),
        in_specs=[
            tok(D_MODEL), tok(D_RWKV), tok(D_MLSTM), tok(D_MEM), tok(3 * D_MODEL), mem, mem,
            _const_spec((D_RWKV, D_MODEL)), _const_spec((D_MLSTM, D_MODEL)), _const_spec((D_MEM, D_MODEL)),
            _const_spec((D_MODEL, D_MODEL)), _const_spec((1, D_MODEL)), _const_spec((1, D_MODEL)),
        ],
        out_specs=tok(D_MODEL),
        out_shape=jax.ShapeDtypeStruct((n, D_MODEL), F32),
        compiler_params=_params(("arbitrary",)),
        name="attn_merge_ln",
    )(x, ya, hb, p_mq, p_gate, mem_k, mem_v, w["w_pa"], w["w_pb"], w["w_pc"], w["w_out"], w["ln1_w"],
      w["ln1_b"])


def _moe_kernel(x_ref, wr_ref, rb_ref, w1_ref, w3_ref, w2_ref, ws1_ref, ws3_ref, ws2_ref, lnw_ref, lnb_ref, o_ref):
    x = x_ref[...]
    xb = x.astype(BF16)
    scores = _sigmoid(_dot(xb, wr_ref[...]))
    lane = lax.broadcasted_iota(jnp.int32, scores.shape, 1)
    work = jnp.where(lane < N_EXPERTS, scores + rb_ref[...], -jnp.inf)
    chosen = jnp.zeros(scores.shape, dtype=jnp.bool_)
    for _ in range(TOP_K):
        best = jnp.max(work, axis=-1, keepdims=True)
        first = jnp.min(jnp.where(work == best, lane, LANES), axis=-1, keepdims=True)
        pick = lane == first
        chosen = chosen | pick
        work = jnp.where(pick, -jnp.inf, work)
    sel = jnp.where(chosen, scores, 0.0)
    gates = sel / jnp.sum(sel, axis=-1, keepdims=True) * ROUTED_SCALE

    acc = _dot((_silu(_dot(xb, ws1_ref[...])) * _dot(xb, ws3_ref[...])).astype(BF16), ws2_ref[...])
    group = 8
    width = group * EXPERT_HIDDEN
    for s in range(N_EXPERTS // group):
        cols = slice(s * width, (s + 1) * width)
        hidden = _silu(_dot(xb, w1_ref[:, cols])) * _dot(xb, w3_ref[:, cols])
        gated = jnp.concatenate(
            [hidden[:, e * EXPERT_HIDDEN:(e + 1) * EXPERT_HIDDEN] * gates[:, s * group + e:s * group + e + 1]
             for e in range(group)], axis=-1)
        acc = acc + _dot(gated.astype(BF16), w2_ref[cols, :])
    o_ref[...] = _layer_norm(DN_ALPHA * x + acc, lnw_ref[...], lnb_ref[...])


def _moe(x, w, tm):
    n = x.shape[0]
    row = lambda i: (i, 0)
    hid = N_EXPERTS * EXPERT_HIDDEN
    return pl.pallas_call(
        _moe_kernel,
        grid=(n // tm,),
        in_specs=[
            pl.BlockSpec((tm, D_MODEL), row),
            _const_spec((D_MODEL, LANES)), _const_spec((1, LANES)),
            _const_spec((D_MODEL, hid)), _const_spec((D_MODEL, hid)), _const_spec((hid, D_MODEL)),
            _const_spec((D_MODEL, EXPERT_HIDDEN)), _const_spec((D_MODEL, EXPERT_HIDDEN)),
            _const_spec((EXPERT_HIDDEN, D_MODEL)),
            _const_spec((1, D_MODEL)), _const_spec((1, D_MODEL)),
        ],
        out_specs=pl.BlockSpec((tm, D_MODEL), row),
        out_shape=jax.ShapeDtypeStruct((n, D_MODEL), F32),
        compiler_params=_params(("arbitrary",)),
        name="moe_ln",
    )(x, w["w_router"], w["router_bias"], w["w_e1"], w["w_e3"], w["w_e2"], w["w_s1"], w["w_s3"], w["w_s2"],
      w["ln2_w"], w["ln2_b"])


def _prep_weights(w_in, b_in, rwkv_mu, rwkv_w0, rwkv_w2, rwkv_a0, rwkv_a2, rwkv_g2, rwkv_k_k, rwkv_k_a,
                  rwkv_r_k, rwkv_ln_w, rwkv_ln_b, mlstm_conv_w, mlstm_conv_b, mlstm_b_i, mlstm_b_f,
                  mlstm_ln_w, mlstm_ln_b, w_mem_kv, w_pa, w_pb, w_pc, w_out, ln1_w, ln1_b,
                  w_router, router_bias, w_e1, w_e3, w_e2, w_s1, w_s3, w_s2, ln2_w, ln2_b):
    off_ml = RWKV_PROJ
    off_if = off_ml + 4 * D_MLSTM
    off_mq = off_if + 2 * MLSTM_HEADS
    pad_if = LANES - 2 * MLSTM_HEADS
    row = lambda t: t.reshape(1, -1)
    hid = N_EXPERTS * EXPERT_HIDDEN
    w_lora = jnp.zeros((LORA_ALL, 3 * D_RWKV), F32)
    w_lora = w_lora.at[0:W_LORA, 0:D_RWKV].set(rwkv_w2)
    w_lora = w_lora.at[W_LORA:W_LORA + A_LORA, D_RWKV:2 * D_RWKV].set(rwkv_a2)
    w_lora = w_lora.at[W_LORA + A_LORA:, 2 * D_RWKV:].set(rwkv_g2)
    return dict(
        w_rw=w_in[:, :off_ml].astype(BF16), b_rw=row(b_in[:off_ml]),
        w_ml=jnp.pad(w_in[:, off_ml:off_mq], ((0, 0), (0, pad_if))).astype(BF16),
        b_ml=row(jnp.pad(b_in[off_ml:off_mq], (0, pad_if))),
        w_mq=w_in[:, off_mq:off_mq + D_MEM].astype(BF16), b_mq=row(b_in[off_mq:off_mq + D_MEM]),
        w_gate=w_in[:, off_mq + D_MEM:].astype(BF16), b_gate=row(b_in[off_mq + D_MEM:]),
        w_mem_k=w_mem_kv[:, :D_MEM].astype(BF16), w_mem_v=w_mem_kv[:, D_MEM:].astype(BF16),
        mu=row(rwkv_mu), w0=row(rwkv_w0), a0=row(rwkv_a0), k_k=row(rwkv_k_k), k_a=row(rwkv_k_a),
        r_k=row(rwkv_r_k), rw_ln_w=row(rwkv_ln_w), rw_ln_b=row(rwkv_ln_b), w_lora=w_lora.astype(BF16),
        conv_w=mlstm_conv_w, conv_b=row(mlstm_conv_b),
        b_if=row(jnp.pad(jnp.concatenate([mlstm_b_i, mlstm_b_f]), (0, pad_if))),
        ml_ln_w=row(mlstm_ln_w), ml_ln_b=row(mlstm_ln_b),
        w_pa=w_pa.astype(BF16), w_pb=w_pb.astype(BF16), w_pc=w_pc.astype(BF16), w_out=w_out.astype(BF16),
        ln1_w=row(ln1_w), ln1_b=row(ln1_b),
        w_router=jnp.pad(w_router, ((0, 0), (0, LANES - N_EXPERTS))).astype(BF16),
        router_bias=row(jnp.pad(router_bias, (0, LANES - N_EXPERTS))),
        w_e1=jnp.transpose(w_e1.astype(BF16), (1, 0, 2)).reshape(D_MODEL, hid),
        w_e3=jnp.transpose(w_e3.astype(BF16), (1, 0, 2)).reshape(D_MODEL, hid),
        w_e2=w_e2.reshape(hid, D_MODEL).astype(BF16),
        w_s1=w_s1.astype(BF16), w_s3=w_s3.astype(BF16), w_s2=w_s2.astype(BF16),
        ln2_w=row(ln2_w), ln2_b=row(ln2_b),
    )


def _pair_state(wkv):
    bsz = wkv.shape[0]
    s = wkv.reshape(bsz, RWKV_PAIRS, 2, RWKV_HEAD_DIM, RWKV_HEAD_DIM)
    z = jnp.zeros_like(s[:, :, 0])
    top = jnp.concatenate([s[:, :, 0], z], axis=-1)
    bot = jnp.concatenate([z, s[:, :, 1]], axis=-1)
    return jnp.concatenate([top, bot], axis=-2)


def _unpair_state(h_bd):
    bsz = h_bd.shape[0]
    hd = RWKV_HEAD_DIM
    heads = jnp.stack([h_bd[:, :, :hd, :hd], h_bd[:, :, hd:, hd:]], axis=2)
    return heads.reshape(bsz, RWKV_HEADS, hd, hd)


def _trunk(x3, mem_k, mem_v, shift_prev, wkv0, conv_prev, c0, n0, m0, w, chunk_rw, chunk_ml, tm_proj, tail_rows,
           tail_batches, tm_moe):
    bsz, t_len, _ = x3.shape
    n = bsz * t_len
    x = x3.reshape(n, D_MODEL)
    p_rw = _matmul_bias(x, w["w_rw"], w["b_rw"], tm_proj, RWKV_PROJ // 2)
    p_ml = _matmul_bias(x, w["w_ml"], w["b_ml"], tm_proj, ML_COLS // 3)
    p_mq = _matmul_bias(x, w["w_mq"], w["b_mq"], tm_proj, D_MEM, BF16)
    p_gate = _matmul_bias(x, w["w_gate"], w["b_gate"], tm_proj, 3 * D_MODEL // 2)

    p_rw3 = p_rw.reshape(bsz, t_len, RWKV_PROJ)
    p_ml3 = p_ml.reshape(bsz, t_len, ML_COLS)
    ya, h_bd = _rwkv(p_rw3, shift_prev, _pair_state(wkv0), w, chunk_rw)
    hb, c_t, n_t, m_t = _mlstm(p_ml3, conv_prev, c0, n0, m0.reshape(bsz, 1, MLSTM_HEADS), w, chunk_ml)
    x1 = _tail(x, ya.reshape(n, D_RWKV), hb.reshape(n, D_MLSTM), p_mq, p_gate, mem_k, mem_v, w, tail_rows,
               tail_batches)
    y = _moe(x1, w, tm_moe)

    new_shift = p_rw3[:, t_len - 1:, :]
    new_conv = p_ml3[:, t_len - (CONV_W - 1):, :2 * D_MLSTM]
    states = (new_shift, _unpair_state(h_bd), new_conv, c_t, n_t, m_t.reshape(bsz, MLSTM_HEADS))
    return y.reshape(bsz, t_len, D_MODEL), states


def kernel(x_prompt, x_sample, state_rwkv_shift, state_rwkv_wkv, state_mlstm_conv, state_mlstm_c, state_mlstm_n, state_mlstm_m, cache_mem_k, cache_mem_v, mem_prompt, w_in, b_in, rwkv_mu, rwkv_w0, rwkv_w2, rwkv_a0, rwkv_a2, rwkv_g2, rwkv_k_k, rwkv_k_a, rwkv_r_k, rwkv_ln_w, rwkv_ln_b, mlstm_conv_w, mlstm_conv_b, mlstm_b_i, mlstm_b_f, mlstm_ln_w, mlstm_ln_b, w_mem_kv, w_pa, w_pb, w_pc, w_out, ln1_w, ln1_b, w_router, router_bias, w_e1, w_e3, w_e2, w_s1, w_s3, w_s2, ln2_w, ln2_b):
    weights = (w_in, b_in, rwkv_mu, rwkv_w0, rwkv_w2, rwkv_a0, rwkv_a2, rwkv_g2, rwkv_k_k, rwkv_k_a, rwkv_r_k,
               rwkv_ln_w, rwkv_ln_b, mlstm_conv_w, mlstm_conv_b, mlstm_b_i, mlstm_b_f, mlstm_ln_w, mlstm_ln_b,
               w_mem_kv, w_pa, w_pb, w_pc, w_out, ln1_w, ln1_b, w_router, router_bias, w_e1, w_e3, w_e2,
               w_s1, w_s3, w_s2, ln2_w, ln2_b)
    w = _prep_weights(*(t[0] for t in weights))
    bp, t_p, _ = x_prompt.shape
    bs, t_s, _ = x_sample.shape
    hd = MLSTM_HEAD_DIM

    mem_flat = mem_prompt.reshape(bp * N_MEM, D_MODEL)
    zero_bias = jnp.zeros((1, D_MEM), F32)
    mk = _matmul_bias(mem_flat, w["w_mem_k"], zero_bias, bp * N_MEM, D_MEM).reshape(bp, N_MEM, D_MEM)
    mv = _matmul_bias(mem_flat, w["w_mem_v"], zero_bias, bp * N_MEM, D_MEM).reshape(bp, N_MEM, D_MEM)

    y_p, st_p = _trunk(
        x_prompt, mk, mv,
        jnp.zeros((bp, 1, RWKV_PROJ), F32), jnp.zeros((bp, RWKV_HEADS, RWKV_HEAD_DIM, RWKV_HEAD_DIM), F32),
        jnp.zeros((bp, CONV_W - 1, 2 * D_MLSTM), F32), jnp.zeros((bp, MLSTM_HEADS, hd, hd), F32),
        jnp.zeros((bp, MLSTM_HEADS, hd), F32), jnp.zeros((bp, MLSTM_HEADS), F32),
        w, chunk_rw=64, chunk_ml=min(MLSTM_CHUNK, t_p), tm_proj=1024, tail_rows=512, tail_batches=1, tm_moe=512)
    y_s, st_s = _trunk(
        x_sample, cache_mem_k[0].reshape(bs, N_MEM, D_MEM), cache_mem_v[0].reshape(bs, N_MEM, D_MEM),
        state_rwkv_shift[0], state_rwkv_wkv[0], state_mlstm_conv[0], state_mlstm_c[0], state_mlstm_n[0],
        state_mlstm_m[0],
        w, chunk_rw=t_s, chunk_ml=min(MLSTM_CHUNK, t_s), tm_proj=bs * t_s, tail_rows=t_s, tail_batches=8, tm_moe=bs * t_s)

    lead = lambda t: t[None]
    mem_shape = (1, bp, N_MEM, MEM_HEADS, MEM_HEAD_DIM)
    return (y_p, y_s, *(lead(t) for t in st_p), mk.reshape(mem_shape), mv.reshape(mem_shape),
            *(lead(t) for t in st_s))
```

```python
import functools
import itertools

import jax
import jax.numpy as jnp
from jax import lax
from jax.experimental import pallas as pl
from jax.experimental.pallas import tpu as pltpu

F32 = jnp.float32
BF16 = jnp.bfloat16

D_MODEL = 1024
DEPTH = 1
D_RWKV = 1024
RWKV_HEAD_DIM = 64
RWKV_HEADS = 16
W_LORA = 64
A_LORA = 64
G_LORA = 128
LORA_ALL = W_LORA + A_LORA + G_LORA
RWKV_PROJ = 3 * D_RWKV + LORA_ALL
RWKV_GN_EPS = 64e-5
DECAY_SCALE = 0.6065306597126334
D_MLSTM = 1024
MLSTM_HEADS = 4
MLSTM_HEAD_DIM = 256
CONV_W = 4
MLSTM_CHUNK = 64
N_MEM = 256
MEM_HEADS = 4
MEM_HEAD_DIM = 256
D_MEM = 1024
N_EXPERTS = 32
TOP_K = 4
EXPERT_HIDDEN = 128
ROUTED_SCALE = 2.5
DN_ALPHA = (2 * DEPTH) ** 0.25
LN_EPS = 1e-5

LANES = 128
RWKV_PAIRS = D_RWKV // LANES
ML_COLS = 4 * D_MLSTM + LANES
VMEM_LIMIT = 56 * 1024 * 1024
BATCH_GROUP = 2


def _dot(a, b):
    return jnp.dot(a, b, preferred_element_type=F32)


def _dot_nt(a, b):
    return lax.dot_general(a, b, (((1,), (1,)), ((), ())), preferred_element_type=F32)


def _dot_tn(a, b):
    return lax.dot_general(a, b, (((0,), (0,)), ((), ())), preferred_element_type=F32)


def _split(a):
    hi = a.astype(BF16)
    return hi, (a - hi.astype(F32)).astype(BF16)


def _split3(a):
    hi = a.astype(BF16)
    rest = a - hi.astype(F32)
    mid = rest.astype(BF16)
    return hi, mid, (rest - mid.astype(F32)).astype(BF16)


def _bdot(a, b):
    return _dot(a.astype(BF16), b.astype(BF16))


def _sigmoid(x):
    return 0.5 * jnp.tanh(0.5 * x) + 0.5


def _softplus(x):
    return jnp.maximum(x, 0.0) + jnp.log(1.0 + jnp.exp(-jnp.abs(x)))


def _silu(x):
    return x * _sigmoid(x)


def _layer_norm(x, w, b):
    mu = jnp.mean(x, axis=-1, keepdims=True)
    d = x - mu
    var = jnp.mean(d * d, axis=-1, keepdims=True)
    return d * lax.rsqrt(var + LN_EPS) * w + b


def _params(semantics):
    return pltpu.CompilerParams(dimension_semantics=semantics, vmem_limit_bytes=VMEM_LIMIT)


def _const_spec(shape):
    zeros = (0,) * len(shape)
    return pl.BlockSpec(shape, lambda *_: zeros, pipeline_mode=pl.Buffered(1))


def _mm_kernel(x_ref, w_ref, b_ref, o_ref):
    o_ref[...] = (_dot(x_ref[...].astype(BF16), w_ref[...]) + b_ref[...]).astype(o_ref.dtype)


def _matmul_bias(x, w, b, tm, tn, out_dtype=F32):
    n, k = x.shape
    nc = w.shape[1]
    return pl.pallas_call(
        _mm_kernel,
        grid=(n // tm, nc // tn),
        in_specs=[
            pl.BlockSpec((tm, k), lambda i, j: (i, 0)),
            pl.BlockSpec((k, tn), lambda i, j: (0, j)),
            pl.BlockSpec((1, tn), lambda i, j: (0, j)),
        ],
        out_specs=pl.BlockSpec((tm, tn), lambda i, j: (i, j)),
        out_shape=jax.ShapeDtypeStruct((n, nc), out_dtype),
        compiler_params=_params(("arbitrary", "arbitrary")),
        name="matmul_bias",
    )(x, w, b)


def _rwkv_body(p_ref, prev_ref, h0_ref, mu_ref, w0_ref, a0_ref, kk_ref, ka_ref, rk_ref, lnw_ref, lnb_ref,
               wl_ref, y_ref, hout_ref, h_scr, xbuf, ar_scr, bk_scr, bkh_scr, v_scr, bonus_scr, g_scr, gamma_scr,
               *, chunk, n_chunks, n_items, companion):
    step = pl.program_id(0)
    L = chunk
    half = 2 * L
    prep_chunk = jnp.minimum(step, n_items - 1) % n_chunks
    solve_chunk = jnp.maximum(step - 1, 0) % n_chunks

    @pl.when(step == 0)
    def _():
        for ref in (ar_scr, bk_scr, bkh_scr, v_scr, bonus_scr, g_scr):
            ref[...] = jnp.zeros(ref.shape, ref.dtype)
        gamma_scr[...] = jnp.ones(gamma_scr.shape, F32)

    @pl.when(solve_chunk == 0)
    def _():
        h_scr[...] = h0_ref[...]

    @pl.when(prep_chunk == 0)
    def _():
        for bi in range(BATCH_GROUP):
            xbuf[bi, 0:1, :] = prev_ref[bi]

    ti = lax.broadcasted_iota(jnp.int32, (L, L), 0)
    tj = lax.broadcasted_iota(jnp.int32, (L, L), 1)
    ltri = (tj <= ti).astype(BF16)
    row_m = lax.broadcasted_iota(jnp.int32, (L, half), 0)
    col_m = lax.broadcasted_iota(jnp.int32, (L, half), 1)
    col_in = jnp.where(col_m >= L, col_m - L, col_m)
    strict = col_in < row_m
    incl = col_in <= row_m
    eye = (col_in == row_m).astype(F32)
    gi = lax.broadcasted_iota(jnp.int32, (LANES, LANES), 0)
    gj = lax.broadcasted_iota(jnp.int32, (LANES, LANES), 1)
    same_head = (gi >= RWKV_HEAD_DIM) == (gj >= RWKV_HEAD_DIM)
    head_ones = same_head.astype(BF16)
    first_nat = lax.broadcasted_iota(jnp.int32, (L, LANES), 1) < RWKV_HEAD_DIM
    first_sbs = col_m < L
    lane_l = lax.broadcasted_iota(jnp.int32, (L, LORA_ALL), 1)

    def diag(x, first):
        zero = jnp.zeros_like(x)
        return jnp.concatenate([jnp.where(first, x, zero), jnp.where(first, zero, x)], axis=0)

    def seg_sum(x):
        return _dot(x.astype(BF16), head_ones)

    units = [(bi, p) for bi in range(BATCH_GROUP) for p in range(RWKV_PAIRS)]
    idx = range(len(units))
    lane_of = lambda p: slice(p * LANES, (p + 1) * LANES)

    ar = [ar_scr[bi, :, lane_of(p)] for bi, p in units]
    bk = [bk_scr[bi, :, lane_of(p)] for bi, p in units]
    bkh = [bkh_scr[bi, :, lane_of(p)] for bi, p in units]
    v_b = [v_scr[bi, :, lane_of(p)] for bi, p in units]
    bonus = [bonus_scr[bi, :, lane_of(p)] for bi, p in units]
    gate = [g_scr[bi, :, lane_of(p)] for bi, p in units]
    gamma = [gamma_scr[bi, :, lane_of(p)] for bi, p in units]

    first_row = lax.broadcasted_iota(jnp.int32, (L, RWKV_PROJ), 0) == 0

    def prepare_row(bi):
        cur = p_ref[bi]
        prev = jnp.where(first_row, xbuf[bi, 0:1, :], pltpu.roll(cur, 1, 0))
        xbuf[bi, 0:1, :] = cur[L - 1:L, :]
        xr = cur + (prev - cur) * mu_ref[...]
        r_all = xr[:, 0:D_RWKV]
        k_all = xr[:, D_RWKV:2 * D_RWKV]
        v_all = xr[:, 2 * D_RWKV:3 * D_RWKV]
        slab = xr[:, 3 * D_RWKV:]
        act = jnp.where(lane_l < W_LORA, jnp.tanh(slab),
                        jnp.where(lane_l < W_LORA + A_LORA, slab, _sigmoid(slab)))
        yield
        lora = _bdot(act, wl_ref[...])
        yield
        lw = -DECAY_SCALE * _sigmoid(w0_ref[...] + lora[:, 0:D_RWKV])
        a_sig = _sigmoid(a0_ref[...] + lora[:, D_RWKV:2 * D_RWKV])
        lw_hi, lw_lo = _split(lw)
        yield
        kk0 = k_all * kk_ref[...]
        k2 = k_all * (1.0 + (a_sig - 1.0) * ka_ref[...])
        sums = jnp.concatenate([kk0 * kk0, r_all * k2 * rk_ref[...]], axis=0).astype(BF16)
        yield
        cum = _dot(ltri, lw_hi) + _dot(ltri, lw_lo)
        sums = jnp.concatenate([_dot(sums[:, lane_of(p)], head_ones) for p in range(RWKV_PAIRS)], axis=1)
        yield
        kk = kk0 * lax.rsqrt(jnp.maximum(sums[0:L], 1e-24))
        b_vec = kk * a_sig
        c_last = cum[L - 1:L, :]
        e_neg = jnp.exp(-cum)
        e_end = jnp.exp(c_last - cum)
        yield
        ar_scr[bi] = jnp.concatenate([-kk * jnp.exp(cum - lw), r_all * jnp.exp(cum)], axis=0).astype(BF16)
        bk_scr[bi] = jnp.concatenate([b_vec * e_neg, k2 * e_neg], axis=0).astype(BF16)
        yield
        bkh_scr[bi] = jnp.concatenate([b_vec * e_end, k2 * e_end], axis=0).astype(BF16)
        v_scr[bi] = v_all.astype(BF16)
        bonus_scr[bi] = sums[L:] * v_all
        g_scr[bi] = lora[:, 2 * D_RWKV:]
        gamma_scr[bi] = jnp.exp(c_last)

    segments = itertools.chain(*(prepare_row(bi) for bi in range(BATCH_GROUP)))

    def prepare_some(count):
        for _ in range(count):
            next(segments, None)
            next(companion, None)

    n_sq = L.bit_length() - 1
    per_stage = 2
    bk_d = [jnp.concatenate([diag(bk[u][0:L], first_nat), diag(bk[u][L:], first_nat)], axis=0) for u in idx]
    v_d = [diag(v_b[u], first_nat) for u in idx]
    h_bd = [h_scr[bi, p] for bi, p in units]
    sc = [_dot_nt(ar[u], bk_d[u]) for u in idx]
    prepare_some(per_stage)
    arh = [_dot_nt(ar[u], h_bd[u].astype(BF16)) for u in idx]
    m_ab = [jnp.where(strict, sc[u][0:L, 0:half], 0.0) for u in idx]
    m_ak = [jnp.where(strict, sc[u][0:L, half:], 0.0).astype(BF16) for u in idx]
    n_rbk = [jnp.concatenate([jnp.where(incl, sc[u][L:, 0:half], 0.0),
                              jnp.where(incl, sc[u][L:, half:], 0.0)], axis=1).astype(BF16) for u in idx]
    prepare_some(per_stage)
    w0 = [arh[u][0:L] + _dot(m_ak[u], v_d[u]) for u in idx]
    inv = [eye + m_ab[u] for u in idx]
    m_b = [m_ab[u].astype(BF16) for u in idx]
    pw = [_dot(m_b[u], diag(m_b[u], first_sbs)) for u in idx]
    prepare_some(per_stage)
    for i in range(n_sq - 1):
        pw_d = [diag(pw[u].astype(BF16), first_sbs) for u in idx]
        if i + 1 < n_sq - 1:
            z = [_dot(jnp.concatenate([inv[u], pw[u]], axis=0).astype(BF16), pw_d[u]) for u in idx]
            inv = [inv[u] + z[u][0:L] for u in idx]
            pw = [z[u][L:] for u in idx]
        else:
            inv = [inv[u] + _dot(inv[u].astype(BF16), pw_d[u]) for u in idx]
        prepare_some(per_stage)
    c_b = [_dot(inv[u].astype(BF16), diag(w0[u].astype(BF16), first_nat)).astype(BF16) for u in idx]
    cv = [jnp.concatenate([c_b[u], v_b[u]], axis=0) for u in idx]
    y = [arh[u][L:] + _dot(n_rbk[u], jnp.concatenate([diag(c_b[u], first_nat), v_d[u]], axis=0)) for u in idx]
    for u, (bi, p) in enumerate(units):
        h_scr[bi, p] = h_bd[u] * gamma[u] + jnp.where(same_head, _dot_tn(cv[u], bkh[u]), 0.0)
    mean = [seg_sum(y[u]) * (1.0 / RWKV_HEAD_DIM) for u in idx]
    dev = [y[u] - mean[u] for u in idx]
    var = [seg_sum(dev[u] * dev[u]) * (1.0 / RWKV_HEAD_DIM) for u in idx]
    for u, (bi, p) in enumerate(units):
        s = lane_of(p)
        yn = dev[u] * lax.rsqrt(var[u] + RWKV_GN_EPS) * lnw_ref[:, s] + lnb_ref[:, s]
        y_ref[bi, :, s] = ((yn + bonus[u]) * gate[u]).astype(y_ref.dtype)
    prepare_some(8 * BATCH_GROUP)
    for _ in companion:
        pass

    @pl.when((solve_chunk == n_chunks - 1) & (step > 0))
    def _():
        hout_ref[...] = h_scr[...]


def _mlstm_stages(p_ref, conv_prev_ref, c0_ref, n0_ref, m0_ref, cw_ref, cb_ref, bif_ref, lnw_ref, lnb_ref,
                  h_ref, cout_ref, nout_ref, mout_ref, c_scr, n_scr, m_scr, xbuf, *, chunk, first, last):
    L = chunk
    d2 = 2 * D_MLSTM
    nh = MLSTM_HEADS

    @pl.when(first)
    def _():
        c_scr[...] = c0_ref[...]
        n_scr[...] = n0_ref[...]
        m_scr[...] = m0_ref[...]
        xbuf[:, 0:8 - (CONV_W - 1), :] = jnp.zeros((BATCH_GROUP, 8 - (CONV_W - 1), d2), F32)
        xbuf[:, 8 - (CONV_W - 1):8, :] = conv_prev_ref[...]

    yield

    row8 =lax.broadcasted_iota(jnp.int32, (8, d2), 0)
    ti = lax.broadcasted_iota(jnp.int32, (L, L), 0)
    tj = lax.broadcasted_iota(jnp.int32, (L, L), 1)
    causal = tj <= ti
    ltri = causal.astype(BF16)
    utri = (ti <= tj).astype(BF16)
    sel = (lax.broadcasted_iota(jnp.int32, (8, LANES), 0)
           == lax.broadcasted_iota(jnp.int32, (8, LANES), 1)).astype(BF16)
    lane_g = lax.broadcasted_iota(jnp.int32, (L, LANES), 1)
    is_f = (lane_g >= nh) & (lane_g < 2 * nh)

    rows = []
    for bi in range(BATCH_GROUP):
        x = p_ref[bi, :, 0:d2]
        tail8 = xbuf[bi]
        conv = cb_ref[...] + cw_ref[CONV_W - 1:CONV_W, :] * x
        for j in range(CONV_W - 1):
            lag = CONV_W - 1 - j
            rolled = pltpu.roll(x, lag, 0)
            top = jnp.where(row8 < lag, pltpu.roll(tail8, lag, 0), rolled[0:8])
            conv = conv + cw_ref[j:j + 1, :] * jnp.concatenate([top, rolled[8:]], axis=0)
        xbuf[bi] = x[L - 8:L, :]
        qk = _silu(conv)
        yield
        gates =p_ref[bi, :, 4 * D_MLSTM:] + bif_ref[...]
        glog = jnp.where(is_f, -_softplus(-gates), gates)
        parts = _split3(glog)
        b_cols = sum(_dot(ltri, t) for t in parts)
        g_rows = sum(_dot_nt(sel, t) for t in parts)
        b_rows = sum(_dot(t, utri) for t in _split3(g_rows))
        rows.append(dict(q=qk[:, 0:D_MLSTM], k=qk[:, D_MLSTM:d2] * (MLSTM_HEAD_DIM ** -0.5),
                         v=p_ref[bi, :, d2:d2 + D_MLSTM], o=_sigmoid(p_ref[bi, :, d2 + D_MLSTM:d2 + 2 * D_MLSTM]),
                         glog=glog, b_cols=b_cols, g_rows=g_rows, b_rows=b_rows))
        yield

    units = [(bi, h) for bi in range(BATCH_GROUP) for h in range(nh)]
    idx = range(len(units))
    head = lambda h: slice(h * MLSTM_HEAD_DIM, (h + 1) * MLSTM_HEAD_DIM)
    q = [rows[bi]["q"][:, head(h)] for bi, h in units]
    k = [rows[bi]["k"][:, head(h)] for bi, h in units]
    v = [rows[bi]["v"][:, head(h)] for bi, h in units]
    q_b = [t.astype(BF16) for t in q]
    k_b = [t.astype(BF16) for t in k]
    c_st = [c_scr[bi, h] for bi, h in units]
    n_st = [n_scr[bi, h:h + 1, :] for bi, h in units]
    m_prev = [m_scr[bi, :, h:h + 1] for bi, h in units]
    qk_t = [_dot_nt(q_b[u], k_b[u]) for u in idx]
    yield
    qc = [_dot_nt(q_b[u], c_st[u].astype(BF16)) for u in idx]
    yield
    s, w_inter, m_t, wk, carry, m_next = [], [], [], [], [], []
    for u, (bi, h) in enumerate(units):
        t = rows[bi]
        ig_row = t["g_rows"][h:h + 1, :]
        b_row = t["b_rows"][nh + h:nh + h + 1, :]
        ig_col = t["glog"][:, h:h + 1]
        b_col = t["b_cols"][:, nh + h:nh + h + 1]
        log_w = jnp.where(causal, b_col - b_row + ig_row, -jnp.inf)
        inter = b_col + m_prev[u]
        m_now = jnp.maximum(inter, jnp.max(log_w, axis=-1, keepdims=True))
        s.append(qk_t[u] * jnp.exp(log_w - m_now))
        w_inter.append(jnp.exp(inter - m_now))
        m_t.append(m_now)
        b_last = b_col[L - 1:L, :]
        g_col = b_last - b_col + ig_col
        m_new = jnp.maximum(b_last + m_prev[u], jnp.max(g_col, axis=0, keepdims=True))
        carry.append(jnp.exp(b_last + m_prev[u] - m_new))
        wk.append(jnp.exp(g_col - m_new))
        m_next.append(m_new)
        if u % 2 == 1:
            yield
    sv = [_dot(s[u].astype(BF16), v[u].astype(BF16)) for u in idx]
    yield
    upd = [_dot_tn((wk[u] * v[u]).astype(BF16), k_b[u]) for u in idx]
    yield
    qn =[jnp.sum(q[u] * n_st[u], axis=-1, keepdims=True) for u in idx]
    s_sum = [jnp.sum(s[u], axis=-1, keepdims=True) for u in idx]
    den = [jnp.maximum(jnp.abs(w_inter[u] * qn[u] + s_sum[u]), jnp.exp(-m_t[u])) for u in idx]
    yield
    hb = [rows[bi]["o"][:, head(h)] * ((w_inter[u] * qc[u] + sv[u]) / den[u]) for u, (bi, h) in enumerate(units)]
    yield
    mu = [jnp.mean(hb[u], axis=-1, keepdims=True) for u in idx]
    dev = [hb[u] - mu[u] for u in idx]
    var = [jnp.mean(dev[u] * dev[u], axis=-1, keepdims=True) for u in idx]
    yield
    n_new = [carry[u] * n_st[u] + jnp.sum(wk[u] * k[u], axis=0, keepdims=True) for u in idx]
    for u, (bi, h) in enumerate(units):
        sl = head(h)
        h_ref[bi, :, sl] = (dev[u] * lax.rsqrt(var[u] + LN_EPS) * lnw_ref[:, sl] + lnb_ref[:, sl]).astype(h_ref.dtype)
        c_scr[bi, h] = carry[u] * c_st[u] + upd[u]
        n_scr[bi, h:h + 1, :] = n_new[u]
        m_scr[bi, :, h:h + 1] = m_next[u]
        if u % 4 == 3:
            yield

    @pl.when(last)
    def _():
        cout_ref[...] = c_scr[...]
        nout_ref[...] = n_scr[...]
        mout_ref[...] = m_scr[...]


RW_REFS = (12, 2, 9)
ML_REFS = (10, 4, 4)


def _recurrent_kernel(*refs, chunk, n_chunks, n_items):
    groups, pos = [], 0
    for n_rw, n_ml in zip(RW_REFS, ML_REFS):
        groups.append((refs[pos:pos + n_rw], refs[pos + n_rw:pos + n_rw + n_ml]))
        pos += n_rw + n_ml
    rw_refs = [r for rw_part, _ in groups for r in rw_part]
    ml_refs = [r for _, ml_part in groups for r in ml_part]
    step = pl.program_id(0)
    solve_chunk = jnp.maximum(step - 1, 0) % n_chunks
    ml = _mlstm_stages(*ml_refs, chunk=chunk, first=solve_chunk == 0,
                       last=(solve_chunk == n_chunks - 1) & (step > 0))
    next(ml)
    _rwkv_body(*rw_refs, chunk=chunk, n_chunks=n_chunks, n_items=n_items, companion=ml)


def _recurrent(p_rw, shift_prev, h0_bd, p_ml, conv_prev, c0, n0, m0, w, chunk):
    batch, t_len, _ = p_rw.shape
    n_chunks = t_len // chunk
    bg = BATCH_GROUP
    hd = MLSTM_HEAD_DIM
    n_items = (batch // bg) * n_chunks
    prep = lambda i: jnp.minimum(i, n_items - 1)
    solve = lambda i: jnp.maximum(i - 1, 0)
    prep_seq = lambda i: (prep(i) // n_chunks, prep(i) % n_chunks, 0)
    solve_seq = lambda i: (solve(i) // n_chunks, solve(i) % n_chunks, 0)
    vec = lambda n: pl.BlockSpec((1, n), lambda i: (0, 0))
    group = lambda *dims: pl.BlockSpec((bg,) + dims, lambda i: (solve(i) // n_chunks,) + (0,) * len(dims))
    rw_state = group(RWKV_PAIRS, LANES, LANES)
    ml_states = [group(MLSTM_HEADS, hd, hd), group(MLSTM_HEADS, hd), group(1, MLSTM_HEADS)]
    wide = lambda rows, dtype: pltpu.VMEM((bg, rows, D_RWKV), dtype)
    kern = functools.partial(_recurrent_kernel, chunk=chunk, n_chunks=n_chunks, n_items=n_items)
    return pl.pallas_call(
        kern,
        grid=(n_items + 1,),
        in_specs=[
            pl.BlockSpec((bg, chunk, RWKV_PROJ), prep_seq),
            pl.BlockSpec((bg, 1, RWKV_PROJ), lambda i: (prep(i) // n_chunks, 0, 0)),
            rw_state,
            vec(RWKV_PROJ), vec(D_RWKV), vec(D_RWKV), vec(D_RWKV), vec(D_RWKV), vec(D_RWKV), vec(D_RWKV),
            vec(D_RWKV),
            pl.BlockSpec((LORA_ALL, 3 * D_RWKV), lambda i: (0, 0)),
            pl.BlockSpec((bg, chunk, ML_COLS), solve_seq),
            group(CONV_W - 1, 2 * D_MLSTM),
            *ml_states,
            pl.BlockSpec((CONV_W, 2 * D_MLSTM), lambda i: (0, 0)),
            vec(2 * D_MLSTM), vec(LANES), vec(D_MLSTM), vec(D_MLSTM),
        ],
        out_specs=[
            pl.BlockSpec((bg, chunk, D_RWKV), solve_seq), rw_state,
            pl.BlockSpec((bg, chunk, D_MLSTM), solve_seq), *ml_states,
        ],
        out_shape=[
            jax.ShapeDtypeStruct((batch, t_len, D_RWKV), BF16),
            jax.ShapeDtypeStruct((batch, RWKV_PAIRS, LANES, LANES), F32),
            jax.ShapeDtypeStruct((batch, t_len, D_MLSTM), BF16),
            jax.ShapeDtypeStruct((batch, MLSTM_HEADS, hd, hd), F32),
            jax.ShapeDtypeStruct((batch, MLSTM_HEADS, hd), F32),
            jax.ShapeDtypeStruct((batch, 1, MLSTM_HEADS), F32),
        ],
        scratch_shapes=[
            pltpu.VMEM((bg, RWKV_PAIRS, LANES, LANES), F32),
            pltpu.VMEM((bg, 8, RWKV_PROJ), F32),
            wide(2 * chunk, BF16), wide(2 * chunk, BF16), wide(2 * chunk, BF16), wide(chunk, BF16),
            wide(chunk, F32), wide(chunk, F32), wide(1, F32),
            pltpu.VMEM((bg, MLSTM_HEADS, hd, hd), F32),
            pltpu.VMEM((bg, MLSTM_HEADS, hd), F32),
            pltpu.VMEM((bg, 1, MLSTM_HEADS), F32),
            pltpu.VMEM((bg, 8, 2 * D_MLSTM), F32),
        ],
        compiler_params=_params(("arbitrary",)),
        name="rwkv7_mlstm_recurrent",
    )(p_rw, shift_prev, h0_bd, w["mu"], w["w0"], w["a0"], w["k_k"], w["k_a"], w["r_k"], w["rw_ln_w"],
      w["rw_ln_b"], w["w_lora"], p_ml, conv_prev, c0, n0, m0, w["conv_w"], w["conv_b"], w["b_if"],
      w["ml_ln_w"], w["ml_ln_b"])


def _tail_kernel(x_ref, ya_ref, hb_ref, mq_ref, gate_ref, mk_ref, mv_ref, wpa_ref, wpb_ref, wpc_ref, wout_ref,
                 lnw_ref, lnb_ref, o_ref, *, rows):
    attn = []
    for b in range(mk_ref.shape[0]):
        q = mq_ref[b * rows:(b + 1) * rows, :]
        mk = mk_ref[b].astype(BF16)
        mv = mv_ref[b].astype(BF16)
        heads = []
        for h in range(MEM_HEADS):
            sl = slice(h * MEM_HEAD_DIM, (h + 1) * MEM_HEAD_DIM)
            s = _dot_nt(q[:, sl], mk[:, sl]) * (MEM_HEAD_DIM ** -0.5)
            e = jnp.exp(s - jnp.max(s, axis=-1, keepdims=True))
            prob = e / jnp.sum(e, axis=-1, keepdims=True)
            heads.append(_dot(prob.astype(BF16), mv[:, sl]))
        attn.append(jnp.concatenate(heads, axis=-1).astype(BF16))
    attn = attn[0] if len(attn) == 1 else jnp.concatenate(attn, axis=0)
    y_a = _dot(ya_ref[...], wpa_ref[...])
    y_b = _dot(hb_ref[...], wpb_ref[...])
    y_c = _dot(attn, wpc_ref[...])
    d = D_MODEL
    mixed = (_sigmoid(gate_ref[:, 0:d]) * y_a + _sigmoid(gate_ref[:, d:2 * d]) * y_b
             + _sigmoid(gate_ref[:, 2 * d:3 * d]) * y_c)
    u = _bdot(mixed, wout_ref[...])
    o_ref[...] = _layer_norm(DN_ALPHA * x_ref[...] + u, lnw_ref[...], lnb_ref[...])


def _tail(x, ya, hb, p_mq, p_gate, mem_k, mem_v, w, rows, batches):
    n = x.shape[0]
    tm = rows * batches
    tiles_per_batch = (n // mem_k.shape[0]) // rows if batches == 1 else 1
    row = lambda i: (i, 0)
    mem = pl.BlockSpec((batches, N_MEM, D_MEM), lambda i: (i // tiles_per_batch, 0, 0))
    tok = lambda cols: pl.BlockSpec((tm, cols), row)
    return pl.pallas_call(
        functools.partial(_tail_kernel, rows=rows),
        grid=(n // tm,),
        in_specs=[
            tok(D_MODEL), tok(D_RWKV), tok(D_MLSTM), tok(D_MEM), tok(3 * D_MODEL), mem, mem,
            _const_spec((D_RWKV, D_MODEL)), _const_spec((D_MLSTM, D_MODEL)), _const_spec((D_MEM, D_MODEL)),
            _const_spec((D_MODEL, D_MODEL)), _const_spec((1, D_MODEL)), _const_spec((1, D_MODEL)),
        ],
        out_specs=tok(D_MODEL),
        out_shape=jax.ShapeDtypeStruct((n, D_MODEL), F32),
        compiler_params=_params(("arbitrary",)),
        name="attn_merge_ln",
    )(x, ya, hb, p_mq, p_gate, mem_k, mem_v, w["w_pa"], w["w_pb"], w["w_pc"], w["w_out"], w["ln1_w"],
      w["ln1_b"])


def _moe_kernel(x_ref, wr_ref, rb_ref, w1_ref, w3_ref, w2_ref, ws1_ref, ws3_ref, ws2_ref, lnw_ref, lnb_ref, o_ref):
    x = x_ref[...]
    xb = x.astype(BF16)
    scores = _sigmoid(_dot(xb, wr_ref[...]))
    lane = lax.broadcasted_iota(jnp.int32, scores.shape, 1)
    work = jnp.where(lane < N_EXPERTS, scores + rb_ref[...], -jnp.inf)
    chosen = jnp.zeros(scores.shape, dtype=jnp.bool_)
    for _ in range(TOP_K):
        best = jnp.max(work, axis=-1, keepdims=True)
        first = jnp.min(jnp.where(work == best, lane, LANES), axis=-1, keepdims=True)
        pick = lane == first
        chosen = chosen | pick
        work = jnp.where(pick, -jnp.inf, work)
    sel = jnp.where(chosen, scores, 0.0)
    gates = sel / jnp.sum(sel, axis=-1, keepdims=True) * ROUTED_SCALE

    acc = _dot((_silu(_dot(xb, ws1_ref[...])) * _dot(xb, ws3_ref[...])).astype(BF16), ws2_ref[...])
    group = 8
    width = group * EXPERT_HIDDEN
    for s in range(N_EXPERTS // group):
        cols = slice(s * width, (s + 1) * width)
        hidden = _silu(_dot(xb, w1_ref[:, cols])) * _dot(xb, w3_ref[:, cols])
        gated = jnp.concatenate(
            [hidden[:, e * EXPERT_HIDDEN:(e + 1) * EXPERT_HIDDEN] * gates[:, s * group + e:s * group + e + 1]
             for e in range(group)], axis=-1)
        acc = acc + _dot(gated.astype(BF16), w2_ref[cols, :])
    o_ref[...] = _layer_norm(DN_ALPHA * x + acc, lnw_ref[...], lnb_ref[...])


def _moe(x, w, tm):
    n = x.shape[0]
    row = lambda i: (i, 0)
    hid = N_EXPERTS * EXPERT_HIDDEN
    return pl.pallas_call(
        _moe_kernel,
        grid=(n // tm,),
        in_specs=[
            pl.BlockSpec((tm, D_MODEL), row),
            _const_spec((D_MODEL, LANES)), _const_spec((1, LANES)),
            _const_spec((D_MODEL, hid)), _const_spec((D_MODEL, hid)), _const_spec((hid, D_MODEL)),
            _const_spec((D_MODEL, EXPERT_HIDDEN)), _const_spec((D_MODEL, EXPERT_HIDDEN)),
            _const_spec((EXPERT_HIDDEN, D_MODEL)),
            _const_spec((1, D_MODEL)), _const_spec((1, D_MODEL)),
        ],
        out_specs=pl.BlockSpec((tm, D_MODEL), row),
        out_shape=jax.ShapeDtypeStruct((n, D_MODEL), F32),
        compiler_params=_params(("arbitrary",)),
        name="moe_ln",
    )(x, w["w_router"], w["router_bias"], w["w_e1"], w["w_e3"], w["w_e2"], w["w_s1"], w["w_s3"], w["w_s2"],
      w["ln2_w"], w["ln2_b"])


def _prep_weights(w_in, b_in, rwkv_mu, rwkv_w0, rwkv_w2, rwkv_a0, rwkv_a2, rwkv_g2, rwkv_k_k, rwkv_k_a,
                  rwkv_r_k, rwkv_ln_w, rwkv_ln_b, mlstm_conv_w, mlstm_conv_b, mlstm_b_i, mlstm_b_f,
                  mlstm_ln_w, mlstm_ln_b, w_mem_kv, w_pa, w_pb, w_pc, w_out, ln1_w, ln1_b,
                  w_router, router_bias, w_e1, w_e3, w_e2, w_s1, w_s3, w_s2, ln2_w, ln2_b):
    off_ml = RWKV_PROJ
    off_if = off_ml + 4 * D_MLSTM
    off_mq = off_if + 2 * MLSTM_HEADS
    pad_if = LANES - 2 * MLSTM_HEADS
    row = lambda t: t.reshape(1, -1)
    hid = N_EXPERTS * EXPERT_HIDDEN
    w_lora = jnp.zeros((LORA_ALL, 3 * D_RWKV), F32)
    w_lora = w_lora.at[0:W_LORA, 0:D_RWKV].set(rwkv_w2)
    w_lora = w_lora.at[W_LORA:W_LORA + A_LORA, D_RWKV:2 * D_RWKV].set(rwkv_a2)
    w_lora = w_lora.at[W_LORA + A_LORA:, 2 * D_RWKV:].set(rwkv_g2)
    return dict(
        w_rw=w_in[:, :off_ml].astype(BF16), b_rw=row(b_in[:off_ml]),
        w_ml=jnp.pad(w_in[:, off_ml:off_mq], ((0, 0), (0, pad_if))).astype(BF16),
        b_ml=row(jnp.pad(b_in[off_ml:off_mq], (0, pad_if))),
        w_mq=w_in[:, off_mq:off_mq + D_MEM].astype(BF16), b_mq=row(b_in[off_mq:off_mq + D_MEM]),
        w_gate=w_in[:, off_mq + D_MEM:].astype(BF16), b_gate=row(b_in[off_mq + D_MEM:]),
        w_mem_k=w_mem_kv[:, :D_MEM].astype(BF16), w_mem_v=w_mem_kv[:, D_MEM:].astype(BF16),
        mu=row(rwkv_mu), w0=row(rwkv_w0), a0=row(rwkv_a0), k_k=row(rwkv_k_k), k_a=row(rwkv_k_a),
        r_k=row(rwkv_r_k), rw_ln_w=row(rwkv_ln_w), rw_ln_b=row(rwkv_ln_b), w_lora=w_lora.astype(BF16),
        conv_w=mlstm_conv_w, conv_b=row(mlstm_conv_b),
        b_if=row(jnp.pad(jnp.concatenate([mlstm_b_i, mlstm_b_f]), (0, pad_if))),
        ml_ln_w=row(mlstm_ln_w), ml_ln_b=row(mlstm_ln_b),
        w_pa=w_pa.astype(BF16), w_pb=w_pb.astype(BF16), w_pc=w_pc.astype(BF16), w_out=w_out.astype(BF16),
        ln1_w=row(ln1_w), ln1_b=row(ln1_b),
        w_router=jnp.pad(w_router, ((0, 0), (0, LANES - N_EXPERTS))).astype(BF16),
        router_bias=row(jnp.pad(router_bias, (0, LANES - N_EXPERTS))),
        w_e1=jnp.transpose(w_e1.astype(BF16), (1, 0, 2)).reshape(D_MODEL, hid),
        w_e3=jnp.transpose(w_e3.astype(BF16), (1, 0, 2)).reshape(D_MODEL, hid),
        w_e2=w_e2.reshape(hid, D_MODEL).astype(BF16),
        w_s1=w_s1.astype(BF16), w_s3=w_s3.astype(BF16), w_s2=w_s2.astype(BF16),
        ln2_w=row(ln2_w), ln2_b=row(ln2_b),
    )


def _pair_state(wkv):
    bsz = wkv.shape[0]
    s = wkv.reshape(bsz, RWKV_PAIRS, 2, RWKV_HEAD_DIM, RWKV_HEAD_DIM)
    z = jnp.zeros_like(s[:, :, 0])
    top = jnp.concatenate([s[:, :, 0], z], axis=-1)
    bot = jnp.concatenate([z, s[:, :, 1]], axis=-1)
    return jnp.concatenate([top, bot], axis=-2)


def _unpair_state(h_bd):
    bsz = h_bd.shape[0]
    hd = RWKV_HEAD_DIM
    heads = jnp.stack([h_bd[:, :, :hd, :hd], h_bd[:, :, hd:, hd:]], axis=2)
    return heads.reshape(bsz, RWKV_HEADS, hd, hd)


def _trunk(x3, mem_k, mem_v, shift_prev, wkv0, conv_prev, c0, n0, m0, w, chunk, tm_proj, tail_rows, tail_batches,
           tm_moe):
    bsz, t_len, _ = x3.shape
    n = bsz * t_len
    x = x3.reshape(n, D_MODEL)
    p_rw = _matmul_bias(x, w["w_rw"], w["b_rw"], tm_proj, RWKV_PROJ // 2)
    p_ml = _matmul_bias(x, w["w_ml"], w["b_ml"], tm_proj, ML_COLS // 3)
    p_mq = _matmul_bias(x, w["w_mq"], w["b_mq"], tm_proj, D_MEM, BF16)
    p_gate = _matmul_bias(x, w["w_gate"], w["b_gate"], tm_proj, 3 * D_MODEL // 2)

    p_rw3 = p_rw.reshape(bsz, t_len, RWKV_PROJ)
    p_ml3 = p_ml.reshape(bsz, t_len, ML_COLS)
    ya, h_bd, hb, c_t, n_t, m_t = _recurrent(p_rw3, shift_prev, _pair_state(wkv0), p_ml3, conv_prev, c0, n0,
                                             m0.reshape(bsz, 1, MLSTM_HEADS), w, chunk)
    x1 = _tail(x, ya.reshape(n, D_RWKV), hb.reshape(n, D_MLSTM), p_mq, p_gate, mem_k, mem_v, w, tail_rows,
               tail_batches)
    y = _moe(x1, w, tm_moe)

    new_shift = p_rw3[:, t_len - 1:, :]
    new_conv = p_ml3[:, t_len - (CONV_W - 1):, :2 * D_MLSTM]
    states = (new_shift, _unpair_state(h_bd), new_conv, c_t, n_t, m_t.reshape(bsz, MLSTM_HEADS))
    return y.reshape(bsz, t_len, D_MODEL), states


def kernel(x_prompt, x_sample, state_rwkv_shift, state_rwkv_wkv, state_mlstm_conv, state_mlstm_c, state_mlstm_n, state_mlstm_m, cache_mem_k, cache_mem_v, mem_prompt, w_in, b_in, rwkv_mu, rwkv_w0, rwkv_w2, rwkv_a0, rwkv_a2, rwkv_g2, rwkv_k_k, rwkv_k_a, rwkv_r_k, rwkv_ln_w, rwkv_ln_b, mlstm_conv_w, mlstm_conv_b, mlstm_b_i, mlstm_b_f, mlstm_ln_w, mlstm_ln_b, w_mem_kv, w_pa, w_pb, w_pc, w_out, ln1_w, ln1_b, w_router, router_bias, w_e1, w_e3, w_e2, w_s1, w_s3, w_s2, ln2_w, ln2_b):
    weights = (w_in, b_in, rwkv_mu, rwkv_w0, rwkv_w2, rwkv_a0, rwkv_a2, rwkv_g2, rwkv_k_k, rwkv_k_a, rwkv_r_k,
               rwkv_ln_w, rwkv_ln_b, mlstm_conv_w, mlstm_conv_b, mlstm_b_i, mlstm_b_f, mlstm_ln_w, mlstm_ln_b,
               w_mem_kv, w_pa, w_pb, w_pc, w_out, ln1_w, ln1_b, w_router, router_bias, w_e1, w_e3, w_e2,
               w_s1, w_s3, w_s2, ln2_w, ln2_b)
    w = _prep_weights(*(t[0] for t in weights))
    bp, t_p, _ = x_prompt.shape
    bs, t_s, _ = x_sample.shape
    hd = MLSTM_HEAD_DIM

    mem_flat = mem_prompt.reshape(bp * N_MEM, D_MODEL)
    zero_bias = jnp.zeros((1, D_MEM), F32)
    mk = _matmul_bias(mem_flat, w["w_mem_k"], zero_bias, bp * N_MEM, D_MEM).reshape(bp, N_MEM, D_MEM)
    mv = _matmul_bias(mem_flat, w["w_mem_v"], zero_bias, bp * N_MEM, D_MEM).reshape(bp, N_MEM, D_MEM)

    y_p, st_p = _trunk(
        x_prompt, mk, mv,
        jnp.zeros((bp, 1, RWKV_PROJ), F32), jnp.zeros((bp, RWKV_HEADS, RWKV_HEAD_DIM, RWKV_HEAD_DIM), F32),
        jnp.zeros((bp, CONV_W - 1, 2 * D_MLSTM), F32), jnp.zeros((bp, MLSTM_HEADS, hd, hd), F32),
        jnp.zeros((bp, MLSTM_HEADS, hd), F32), jnp.zeros((bp, MLSTM_HEADS), F32),
        w, chunk=min(MLSTM_CHUNK, t_p), tm_proj=1024, tail_rows=512, tail_batches=1, tm_moe=512)
    y_s, st_s = _trunk(
        x_sample, cache_mem_k[0].reshape(bs, N_MEM, D_MEM), cache_mem_v[0].reshape(bs, N_MEM, D_MEM),
        state_rwkv_shift[0], state_rwkv_wkv[0], state_mlstm_conv[0], state_mlstm_c[0], state_mlstm_n[0],
        state_mlstm_m[0],
        w, chunk=min(MLSTM_CHUNK, t_s), tm_proj=bs * t_s, tail_rows=t_s, tail_batches=8, tm_moe=bs * t_s)

    lead = lambda t: t[None]
    mem_shape = (1, bp, N_MEM, MEM_HEADS, MEM_HEAD_DIM)
    return (y_p, y_s, *(lead(t) for t in st_p), mk.reshape(mem_shape), mv.reshape(mem_shape),
            *(lead(t) for t in st_s))
```

```python
import functools
import itertools

import jax
import jax.numpy as jnp
from jax import lax
from jax.experimental import pallas as pl
from jax.experimental.pallas import tpu as pltpu

F32 = jnp.float32
BF16 = jnp.bfloat16

D_MODEL = 1024
DEPTH = 1
D_RWKV = 1024
RWKV_HEAD_DIM = 64
RWKV_HEADS = 16
W_LORA = 64
A_LORA = 64
G_LORA = 128
LORA_ALL = W_LORA + A_LORA + G_LORA
RWKV_PROJ = 3 * D_RWKV + LORA_ALL
RWKV_GN_EPS = 64e-5
DECAY_SCALE = 0.6065306597126334
D_MLSTM = 1024
MLSTM_HEADS = 4
MLSTM_HEAD_DIM = 256
CONV_W = 4
MLSTM_CHUNK = 64
N_MEM = 256
MEM_HEADS = 4
MEM_HEAD_DIM = 256
D_MEM = 1024
N_EXPERTS = 32
TOP_K = 4
EXPERT_HIDDEN = 128
ROUTED_SCALE = 2.5
DN_ALPHA = (2 * DEPTH) ** 0.25
LN_EPS = 1e-5

LANES = 128
RWKV_PAIRS = D_RWKV // LANES
VO_COLS = 2 * D_MLSTM + LANES
VMEM_LIMIT = 56 * 1024 * 1024
BATCH_GROUP = 2


def _dot(a, b):
    return jnp.dot(a, b, preferred_element_type=F32)


def _dot_nt(a, b):
    return lax.dot_general(a, b, (((1,), (1,)), ((), ())), preferred_element_type=F32)


def _dot_tn(a, b):
    return lax.dot_general(a, b, (((0,), (0,)), ((), ())), preferred_element_type=F32)


def _split(a):
    hi = a.astype(BF16)
    return hi, (a - hi.astype(F32)).astype(BF16)


def _split3(a):
    hi = a.astype(BF16)
    rest = a - hi.astype(F32)
    mid = rest.astype(BF16)
    return hi, mid, (rest - mid.astype(F32)).astype(BF16)


def _bdot(a, b):
    return _dot(a.astype(BF16), b.astype(BF16))


def _sigmoid(x):
    return 0.5 * jnp.tanh(0.5 * x) + 0.5


def _softplus(x):
    return jnp.maximum(x, 0.0) + jnp.log(1.0 + jnp.exp(-jnp.abs(x)))


def _silu(x):
    return x * _sigmoid(x)


def _layer_norm(x, w, b):
    mu = jnp.mean(x, axis=-1, keepdims=True)
    d = x - mu
    var = jnp.mean(d * d, axis=-1, keepdims=True)
    return d * lax.rsqrt(var + LN_EPS) * w + b


def _params(semantics):
    return pltpu.CompilerParams(dimension_semantics=semantics, vmem_limit_bytes=VMEM_LIMIT)


def _const_spec(shape):
    zeros = (0,) * len(shape)
    return pl.BlockSpec(shape, lambda *_: zeros, pipeline_mode=pl.Buffered(1))


def _mm_kernel(x_ref, w_ref, b_ref, o_ref):
    o_ref[...] = (_dot(x_ref[...].astype(BF16), w_ref[...]) + b_ref[...]).astype(o_ref.dtype)


def _matmul_bias(x, w, b, tm, tn, out_dtype=F32):
    n, k = x.shape
    nc = w.shape[1]
    return pl.pallas_call(
        _mm_kernel,
        grid=(nc // tn, n // tm),
        in_specs=[
            pl.BlockSpec((tm, k), lambda j, i: (i, 0)),
            pl.BlockSpec((k, tn), lambda j, i: (0, j)),
            pl.BlockSpec((1, tn), lambda j, i: (0, j)),
        ],
        out_specs=pl.BlockSpec((tm, tn), lambda j, i: (i, j)),
        out_shape=jax.ShapeDtypeStruct((n, nc), out_dtype),
        compiler_params=_params(("arbitrary", "arbitrary")),
        name="matmul_bias",
    )(x, w, b)


def _qk_proj_kernel(x_ref, w_ref, b_ref, cw_ref, cb_ref, prev_ref, o_ref, tail_ref, raw, *, rows, tiles_per_batch):
    col_block, row_tile = pl.program_id(0), pl.program_id(1)
    n_seg = prev_ref.shape[0]
    tm, tn = o_ref.shape
    scale = jnp.where(col_block >= D_MLSTM // tn, MLSTM_HEAD_DIM ** -0.5, 1.0)
    pad = jnp.zeros((8 - (CONV_W - 1), LANES), F32)

    if n_seg == 1:
        @pl.when(row_tile % tiles_per_batch == 0)
        def _():
            raw[0:8 - (CONV_W - 1), :] = jnp.zeros((8 - (CONV_W - 1), tn), F32)
            raw[8 - (CONV_W - 1):8, :] = prev_ref[0]

    strip = min(rows, 64)
    sub = min(tm, 256)

    def project(k):
        rs = slice(k * sub, (k + 1) * sub)
        raw[8 + k * sub:8 + (k + 1) * sub, :] = _dot(x_ref[rs, :].astype(BF16), w_ref[...]) + b_ref[...]

    def conv_strips(k):
        for r0 in range(k * sub, (k + 1) * sub, strip):
            for c0 in range(0, tn, LANES):
                cols = slice(c0, c0 + LANES)
                if n_seg > 1 and r0 % rows == 0:
                    head = jnp.concatenate([pad, prev_ref[r0 // rows, :, cols]], axis=0)
                else:
                    head = raw[r0:r0 + 8, cols]
                x = raw[8 + r0:8 + r0 + strip, cols]
                ext = jnp.concatenate([head, x], axis=0)
                conv = cb_ref[:, cols] + cw_ref[CONV_W - 1:CONV_W, cols] * x
                for j in range(CONV_W - 1):
                    conv = conv + cw_ref[j:j + 1, cols] * pltpu.roll(ext, CONV_W - 1 - j, 0)[8:, :]
                o_ref[r0:r0 + strip, cols] = _silu(conv) * scale

    project(0)
    for k in range(tm // sub):
        if k + 1 < tm // sub:
            project(k + 1)
        conv_strips(k)

    for seg in range(n_seg):
        tail_ref[seg] = raw[(seg + 1) * rows:(seg + 1) * rows + 8, :]
    if n_seg == 1:
        raw[0:8, :] = raw[tm:tm + 8, :]


def _qk_proj(x, w, b, conv_w, conv_b, conv_prev, tm, tn):
    n, k = x.shape
    batch = conv_prev.shape[0]
    t_len = n // batch
    rows = min(tm, t_len)
    n_seg = tm // rows
    tiles_per_batch = t_len // rows
    nc = w.shape[1]
    seg = lambda j, i: (i // tiles_per_batch if n_seg == 1 else i, 0, j)
    return pl.pallas_call(
        functools.partial(_qk_proj_kernel, rows=rows, tiles_per_batch=tiles_per_batch),
        grid=(nc // tn, n // tm),
        in_specs=[
            pl.BlockSpec((tm, k), lambda j, i: (i, 0)),
            pl.BlockSpec((k, tn), lambda j, i: (0, j)),
            pl.BlockSpec((1, tn), lambda j, i: (0, j)),
            pl.BlockSpec((CONV_W, tn), lambda j, i: (0, j)),
            pl.BlockSpec((1, tn), lambda j, i: (0, j)),
            pl.BlockSpec((n_seg, CONV_W - 1, tn), seg),
        ],
        out_specs=[
            pl.BlockSpec((tm, tn), lambda j, i: (i, j)),
            pl.BlockSpec((n_seg, 8, tn), seg),
        ],
        out_shape=[jax.ShapeDtypeStruct((n, nc), F32), jax.ShapeDtypeStruct((batch, 8, nc), F32)],
        scratch_shapes=[pltpu.VMEM((tm + 8, tn), F32)],
        compiler_params=_params(("arbitrary", "arbitrary")),
        name="qk_proj_conv",
    )(x, w, b, conv_w, conv_b, conv_prev)


def _rwkv_body(p_ref, prev_ref, h0_ref, mu_ref, w0_ref, a0_ref, kk_ref, ka_ref, rk_ref, lnw_ref, lnb_ref,
               wl_ref, y_ref, hout_ref, h_scr, xbuf, ar_scr, bk_scr, bkh_scr, v_scr, bonus_scr, g_scr, gamma_scr,
               *, chunk, n_chunks, n_items, companion):
    step = pl.program_id(0)
    L = chunk
    half = 2 * L
    prep_chunk = jnp.minimum(step, n_items - 1) % n_chunks
    solve_chunk = jnp.maximum(step - 1, 0) % n_chunks

    @pl.when(step == 0)
    def _():
        for ref in (ar_scr, bk_scr, bkh_scr, v_scr, bonus_scr, g_scr):
            ref[...] = jnp.zeros(ref.shape, ref.dtype)
        gamma_scr[...] = jnp.ones(gamma_scr.shape, F32)

    @pl.when(solve_chunk == 0)
    def _():
        h_scr[...] = h0_ref[...]

    @pl.when(prep_chunk == 0)
    def _():
        for bi in range(BATCH_GROUP):
            xbuf[bi, 0:1, :] = prev_ref[bi]

    ti = lax.broadcasted_iota(jnp.int32, (L, L), 0)
    tj = lax.broadcasted_iota(jnp.int32, (L, L), 1)
    ltri = (tj <= ti).astype(BF16)
    row_m = lax.broadcasted_iota(jnp.int32, (L, half), 0)
    col_m = lax.broadcasted_iota(jnp.int32, (L, half), 1)
    col_in = jnp.where(col_m >= L, col_m - L, col_m)
    strict = col_in < row_m
    incl = col_in <= row_m
    eye = (col_in == row_m).astype(F32)
    gi = lax.broadcasted_iota(jnp.int32, (LANES, LANES), 0)
    gj = lax.broadcasted_iota(jnp.int32, (LANES, LANES), 1)
    same_head = (gi >= RWKV_HEAD_DIM) == (gj >= RWKV_HEAD_DIM)
    head_ones = same_head.astype(BF16)
    first_nat = lax.broadcasted_iota(jnp.int32, (L, LANES), 1) < RWKV_HEAD_DIM
    first_sbs = col_m < L
    lane_l = lax.broadcasted_iota(jnp.int32, (L, LORA_ALL), 1)

    def diag(x, first):
        zero = jnp.zeros_like(x)
        return jnp.concatenate([jnp.where(first, x, zero), jnp.where(first, zero, x)], axis=0)

    def seg_sum(x):
        return _dot(x.astype(BF16), head_ones)

    units = [(bi, p) for bi in range(BATCH_GROUP) for p in range(RWKV_PAIRS)]
    idx = range(len(units))
    lane_of = lambda p: slice(p * LANES, (p + 1) * LANES)

    ar = [ar_scr[bi, :, lane_of(p)] for bi, p in units]
    bk = [bk_scr[bi, :, lane_of(p)] for bi, p in units]
    bkh = [bkh_scr[bi, :, lane_of(p)] for bi, p in units]
    v_b = [v_scr[bi, :, lane_of(p)] for bi, p in units]
    bonus = [bonus_scr[bi, :, lane_of(p)] for bi, p in units]
    gate = [g_scr[bi, :, lane_of(p)] for bi, p in units]
    gamma = [gamma_scr[bi, :, lane_of(p)] for bi, p in units]

    first_row = lax.broadcasted_iota(jnp.int32, (L, RWKV_PROJ), 0) == 0

    def prepare_row(bi):
        cur = p_ref[bi]
        prev = jnp.where(first_row, xbuf[bi, 0:1, :], pltpu.roll(cur, 1, 0))
        xbuf[bi, 0:1, :] = cur[L - 1:L, :]
        xr = cur + (prev - cur) * mu_ref[...]
        r_all = xr[:, 0:D_RWKV]
        k_all = xr[:, D_RWKV:2 * D_RWKV]
        v_all = xr[:, 2 * D_RWKV:3 * D_RWKV]
        slab = xr[:, 3 * D_RWKV:]
        act = jnp.where(lane_l < W_LORA, jnp.tanh(slab),
                        jnp.where(lane_l < W_LORA + A_LORA, slab, _sigmoid(slab)))
        yield
        lora = _bdot(act, wl_ref[...])
        yield
        lw = -DECAY_SCALE * _sigmoid(w0_ref[...] + lora[:, 0:D_RWKV])
        a_sig = _sigmoid(a0_ref[...] + lora[:, D_RWKV:2 * D_RWKV])
        lw_hi, lw_lo = _split(lw)
        yield
        kk0 = k_all * kk_ref[...]
        k2 = k_all * (1.0 + (a_sig - 1.0) * ka_ref[...])
        sums = jnp.concatenate([kk0 * kk0, r_all * k2 * rk_ref[...]], axis=0).astype(BF16)
        yield
        cum = _dot(ltri, lw_hi) + _dot(ltri, lw_lo)
        sums = jnp.concatenate([_dot(sums[:, lane_of(p)], head_ones) for p in range(RWKV_PAIRS)], axis=1)
        yield
        kk = kk0 * lax.rsqrt(jnp.maximum(sums[0:L], 1e-24))
        b_vec = kk * a_sig
        c_last = cum[L - 1:L, :]
        e_neg = jnp.exp(-cum)
        e_end = jnp.exp(c_last - cum)
        yield
        ar_scr[bi] = jnp.concatenate([-kk * jnp.exp(cum - lw), r_all * jnp.exp(cum)], axis=0).astype(BF16)
        bk_scr[bi] = jnp.concatenate([b_vec * e_neg, k2 * e_neg], axis=0).astype(BF16)
        yield
        bkh_scr[bi] = jnp.concatenate([b_vec * e_end, k2 * e_end], axis=0).astype(BF16)
        v_scr[bi] = v_all.astype(BF16)
        bonus_scr[bi] = sums[L:] * v_all
        g_scr[bi] = lora[:, 2 * D_RWKV:]
        gamma_scr[bi] = jnp.exp(c_last)

    segments = itertools.chain(*(prepare_row(bi) for bi in range(BATCH_GROUP)))

    def prepare_some(count):
        for _ in range(count):
            next(segments, None)
            next(companion, None)

    n_sq = L.bit_length() - 1
    per_stage = 2
    bk_d = [jnp.concatenate([diag(bk[u][0:L], first_nat), diag(bk[u][L:], first_nat)], axis=0) for u in idx]
    v_d = [diag(v_b[u], first_nat) for u in idx]
    h_bd = [h_scr[bi, p] for bi, p in units]
    sc = [_dot_nt(ar[u], bk_d[u]) for u in idx]
    prepare_some(per_stage)
    arh = [_dot_nt(ar[u], h_bd[u].astype(BF16)) for u in idx]
    m_ab = [jnp.where(strict, sc[u][0:L, 0:half], 0.0) for u in idx]
    m_ak = [jnp.where(strict, sc[u][0:L, half:], 0.0).astype(BF16) for u in idx]
    n_rbk = [jnp.concatenate([jnp.where(incl, sc[u][L:, 0:half], 0.0),
                              jnp.where(incl, sc[u][L:, half:], 0.0)], axis=1).astype(BF16) for u in idx]
    prepare_some(per_stage)
    w0 = [arh[u][0:L] + _dot(m_ak[u], v_d[u]) for u in idx]
    inv = [eye + m_ab[u] for u in idx]
    m_b = [m_ab[u].astype(BF16) for u in idx]
    pw = [_dot(m_b[u], diag(m_b[u], first_sbs)) for u in idx]
    prepare_some(per_stage)
    for i in range(n_sq - 1):
        pw_d = [diag(pw[u].astype(BF16), first_sbs) for u in idx]
        if i + 1 < n_sq - 1:
            z = [_dot(jnp.concatenate([inv[u], pw[u]], axis=0).astype(BF16), pw_d[u]) for u in idx]
            inv = [inv[u] + z[u][0:L] for u in idx]
            pw = [z[u][L:] for u in idx]
        else:
            inv = [inv[u] + _dot(inv[u].astype(BF16), pw_d[u]) for u in idx]
        prepare_some(per_stage)
    c_b = [_dot(inv[u].astype(BF16), diag(w0[u].astype(BF16), first_nat)).astype(BF16) for u in idx]
    cv = [jnp.concatenate([c_b[u], v_b[u]], axis=0) for u in idx]
    y = [arh[u][L:] + _dot(n_rbk[u], jnp.concatenate([diag(c_b[u], first_nat), v_d[u]], axis=0)) for u in idx]
    for u, (bi, p) in enumerate(units):
        h_scr[bi, p] = h_bd[u] * gamma[u] + jnp.where(same_head, _dot_tn(cv[u], bkh[u]), 0.0)
    mean = [seg_sum(y[u]) * (1.0 / RWKV_HEAD_DIM) for u in idx]
    dev = [y[u] - mean[u] for u in idx]
    var = [seg_sum(dev[u] * dev[u]) * (1.0 / RWKV_HEAD_DIM) for u in idx]
    for u, (bi, p) in enumerate(units):
        s = lane_of(p)
        yn = dev[u] * lax.rsqrt(var[u] + RWKV_GN_EPS) * lnw_ref[:, s] + lnb_ref[:, s]
        y_ref[bi, :, s] = ((yn + bonus[u]) * gate[u]).astype(y_ref.dtype)
    prepare_some(8 * BATCH_GROUP)
    for _ in companion:
        pass

    @pl.when((solve_chunk == n_chunks - 1) & (step > 0))
    def _():
        hout_ref[...] = h_scr[...]


def _mlstm_stages(qk_ref, p_ref, c0_ref, n0_ref, m0_ref, bif_ref, lnw_ref, lnb_ref,
                  h_ref, cout_ref, nout_ref, mout_ref, c_scr, n_scr, m_scr, *, chunk, first, last):
    L = chunk
    d = D_MLSTM
    nh = MLSTM_HEADS

    @pl.when(first)
    def _():
        c_scr[...] = c0_ref[...]
        n_scr[...] = n0_ref[...]
        m_scr[...] = m0_ref[...]

    yield

    ti =lax.broadcasted_iota(jnp.int32, (L, L), 0)
    tj = lax.broadcasted_iota(jnp.int32, (L, L), 1)
    causal = tj <= ti
    ltri = causal.astype(BF16)
    utri = (ti <= tj).astype(BF16)
    sel = (lax.broadcasted_iota(jnp.int32, (8, LANES), 0)
           == lax.broadcasted_iota(jnp.int32, (8, LANES), 1)).astype(BF16)
    lane_g = lax.broadcasted_iota(jnp.int32, (L, LANES), 1)
    is_f = (lane_g >= nh) & (lane_g < 2 * nh)

    rows = []
    for bi in range(BATCH_GROUP):
        gates = p_ref[bi, :, 2 * d:] + bif_ref[...]
        glog = jnp.where(is_f, -_softplus(-gates), gates)
        parts = _split3(glog)
        b_cols = sum(_dot(ltri, t) for t in parts)
        g_rows = sum(_dot_nt(sel, t) for t in parts)
        b_rows = sum(_dot(t, utri) for t in _split3(g_rows))
        rows.append(dict(q=qk_ref[bi, :, 0:d], k=qk_ref[bi, :, d:2 * d], v=p_ref[bi, :, 0:d],
                         o=_sigmoid(p_ref[bi, :, d:2 * d]),
                         glog=glog, b_cols=b_cols, g_rows=g_rows, b_rows=b_rows))
        yield

    units = [(bi, h) for bi in range(BATCH_GROUP) for h in range(nh)]
    idx = range(len(units))
    head = lambda h: slice(h * MLSTM_HEAD_DIM, (h + 1) * MLSTM_HEAD_DIM)
    q = [rows[bi]["q"][:, head(h)] for bi, h in units]
    k = [rows[bi]["k"][:, head(h)] for bi, h in units]
    v = [rows[bi]["v"][:, head(h)] for bi, h in units]
    q_b = [t.astype(BF16) for t in q]
    k_b = [t.astype(BF16) for t in k]
    c_st = [c_scr[bi, h] for bi, h in units]
    n_st = [n_scr[bi, h:h + 1, :] for bi, h in units]
    m_prev = [m_scr[bi, :, h:h + 1] for bi, h in units]
    qk_t = [_dot_nt(q_b[u], k_b[u]) for u in idx]
    yield
    qc = [_dot_nt(q_b[u], c_st[u].astype(BF16)) for u in idx]
    yield
    s, w_inter, m_t, wk, carry, m_next = [], [], [], [], [], []
    for u, (bi, h) in enumerate(units):
        t = rows[bi]
        ig_row = t["g_rows"][h:h + 1, :]
        b_row = t["b_rows"][nh + h:nh + h + 1, :]
        ig_col = t["glog"][:, h:h + 1]
        b_col = t["b_cols"][:, nh + h:nh + h + 1]
        log_w = jnp.where(causal, b_col - b_row + ig_row, -jnp.inf)
        inter = b_col + m_prev[u]
        m_now = jnp.maximum(inter, jnp.max(log_w, axis=-1, keepdims=True))
        s.append(qk_t[u] * jnp.exp(log_w - m_now))
        w_inter.append(jnp.exp(inter - m_now))
        m_t.append(m_now)
        b_last = b_col[L - 1:L, :]
        g_col = b_last - b_col + ig_col
        m_new = jnp.maximum(b_last + m_prev[u], jnp.max(g_col, axis=0, keepdims=True))
        carry.append(jnp.exp(b_last + m_prev[u] - m_new))
        wk.append(jnp.exp(g_col - m_new))
        m_next.append(m_new)
        if u % 2 == 1:
            yield
    sv = [_dot(s[u].astype(BF16), v[u].astype(BF16)) for u in idx]
    yield
    upd = [_dot_tn((wk[u] * v[u]).astype(BF16), k_b[u]) for u in idx]
    yield
    qn =[jnp.sum(q[u] * n_st[u], axis=-1, keepdims=True) for u in idx]
    s_sum = [jnp.sum(s[u], axis=-1, keepdims=True) for u in idx]
    den = [jnp.maximum(jnp.abs(w_inter[u] * qn[u] + s_sum[u]), jnp.exp(-m_t[u])) for u in idx]
    yield
    hb = [rows[bi]["o"][:, head(h)] * ((w_inter[u] * qc[u] + sv[u]) / den[u]) for u, (bi, h) in enumerate(units)]
    yield
    mu = [jnp.mean(hb[u], axis=-1, keepdims=True) for u in idx]
    dev = [hb[u] - mu[u] for u in idx]
    var = [jnp.mean(dev[u] * dev[u], axis=-1, keepdims=True) for u in idx]
    yield
    n_new = [carry[u] * n_st[u] + jnp.sum(wk[u] * k[u], axis=0, keepdims=True) for u in idx]
    for u, (bi, h) in enumerate(units):
        sl = head(h)
        h_ref[bi, :, sl] = (dev[u] * lax.rsqrt(var[u] + LN_EPS) * lnw_ref[:, sl] + lnb_ref[:, sl]).astype(h_ref.dtype)
        c_scr[bi, h] = carry[u] * c_st[u] + upd[u]
        n_scr[bi, h:h + 1, :] = n_new[u]
        m_scr[bi, :, h:h + 1] = m_next[u]
        if u % 4 == 3:
            yield

    @pl.when(last)
    def _():
        cout_ref[...] = c_scr[...]
        nout_ref[...] = n_scr[...]
        mout_ref[...] = m_scr[...]


RW_REFS = (12, 2, 9)
ML_REFS = (8, 4, 3)


def _recurrent_kernel(*refs, chunk, n_chunks, n_items):
    groups, pos = [], 0
    for n_rw, n_ml in zip(RW_REFS, ML_REFS):
        groups.append((refs[pos:pos + n_rw], refs[pos + n_rw:pos + n_rw + n_ml]))
        pos += n_rw + n_ml
    rw_refs = [r for rw_part, _ in groups for r in rw_part]
    ml_refs = [r for _, ml_part in groups for r in ml_part]
    step = pl.program_id(0)
    solve_chunk = jnp.maximum(step - 1, 0) % n_chunks
    ml = _mlstm_stages(*ml_refs, chunk=chunk, first=solve_chunk == 0,
                       last=(solve_chunk == n_chunks - 1) & (step > 0))
    next(ml)
    _rwkv_body(*rw_refs, chunk=chunk, n_chunks=n_chunks, n_items=n_items, companion=ml)


def _recurrent(p_rw, shift_prev, h0_bd, qk, p_vo, c0, n0, m0, w, chunk):
    batch, t_len, _ = p_rw.shape
    n_chunks = t_len // chunk
    bg = BATCH_GROUP
    hd = MLSTM_HEAD_DIM
    n_items = (batch // bg) * n_chunks
    prep = lambda i: jnp.minimum(i, n_items - 1)
    solve = lambda i: jnp.maximum(i - 1, 0)
    prep_seq = lambda i: (prep(i) // n_chunks, prep(i) % n_chunks, 0)
    solve_seq = lambda i: (solve(i) // n_chunks, solve(i) % n_chunks, 0)
    vec = lambda n: pl.BlockSpec((1, n), lambda i: (0, 0))
    group = lambda *dims: pl.BlockSpec((bg,) + dims, lambda i: (solve(i) // n_chunks,) + (0,) * len(dims))
    rw_state = group(RWKV_PAIRS, LANES, LANES)
    ml_states = [group(MLSTM_HEADS, hd, hd), group(MLSTM_HEADS, hd), group(1, MLSTM_HEADS)]
    wide = lambda rows, dtype: pltpu.VMEM((bg, rows, D_RWKV), dtype)
    kern = functools.partial(_recurrent_kernel, chunk=chunk, n_chunks=n_chunks, n_items=n_items)
    return pl.pallas_call(
        kern,
        grid=(n_items + 1,),
        in_specs=[
            pl.BlockSpec((bg, chunk, RWKV_PROJ), prep_seq),
            pl.BlockSpec((bg, 1, RWKV_PROJ), lambda i: (prep(i) // n_chunks, 0, 0)),
            rw_state,
            vec(RWKV_PROJ), vec(D_RWKV), vec(D_RWKV), vec(D_RWKV), vec(D_RWKV), vec(D_RWKV), vec(D_RWKV),
            vec(D_RWKV),
            pl.BlockSpec((LORA_ALL, 3 * D_RWKV), lambda i: (0, 0)),
            pl.BlockSpec((bg, chunk, 2 * D_MLSTM), solve_seq),
            pl.BlockSpec((bg, chunk, VO_COLS), solve_seq),
            *ml_states,
            vec(LANES), vec(D_MLSTM), vec(D_MLSTM),
        ],
        out_specs=[
            pl.BlockSpec((bg, chunk, D_RWKV), solve_seq), rw_state,
            pl.BlockSpec((bg, chunk, D_MLSTM), solve_seq), *ml_states,
        ],
        out_shape=[
            jax.ShapeDtypeStruct((batch, t_len, D_RWKV), BF16),
            jax.ShapeDtypeStruct((batch, RWKV_PAIRS, LANES, LANES), F32),
            jax.ShapeDtypeStruct((batch, t_len, D_MLSTM), BF16),
            jax.ShapeDtypeStruct((batch, MLSTM_HEADS, hd, hd), F32),
            jax.ShapeDtypeStruct((batch, MLSTM_HEADS, hd), F32),
            jax.ShapeDtypeStruct((batch, 1, MLSTM_HEADS), F32),
        ],
        scratch_shapes=[
            pltpu.VMEM((bg, RWKV_PAIRS, LANES, LANES), F32),
            pltpu.VMEM((bg, 8, RWKV_PROJ), F32),
            wide(2 * chunk, BF16), wide(2 * chunk, BF16), wide(2 * chunk, BF16), wide(chunk, BF16),
            wide(chunk, F32), wide(chunk, F32), wide(1, F32),
            pltpu.VMEM((bg, MLSTM_HEADS, hd, hd), F32),
            pltpu.VMEM((bg, MLSTM_HEADS, hd), F32),
            pltpu.VMEM((bg, 1, MLSTM_HEADS), F32),
        ],
        compiler_params=_params(("arbitrary",)),
        name="rwkv7_mlstm_recurrent",
    )(p_rw, shift_prev, h0_bd, w["mu"], w["w0"], w["a0"], w["k_k"], w["k_a"], w["r_k"], w["rw_ln_w"],
      w["rw_ln_b"], w["w_lora"], qk, p_vo, c0, n0, m0, w["b_if"], w["ml_ln_w"], w["ml_ln_b"])


def _tail_kernel(x_ref, ya_ref, hb_ref, mq_ref, gate_ref, mk_ref, mv_ref, wpa_ref, wpb_ref, wpc_ref, wout_ref,
                 lnw_ref, lnb_ref, o_ref, *, rows):
    attn = []
    for b in range(mk_ref.shape[0]):
        q = mq_ref[b * rows:(b + 1) * rows, :]
        mk = mk_ref[b].astype(BF16)
        mv = mv_ref[b].astype(BF16)
        heads = []
        for h in range(MEM_HEADS):
            sl = slice(h * MEM_HEAD_DIM, (h + 1) * MEM_HEAD_DIM)
            s = _dot_nt(q[:, sl], mk[:, sl]) * (MEM_HEAD_DIM ** -0.5)
            e = jnp.exp(s - jnp.max(s, axis=-1, keepdims=True))
            prob = e / jnp.sum(e, axis=-1, keepdims=True)
            heads.append(_dot(prob.astype(BF16), mv[:, sl]))
        attn.append(jnp.concatenate(heads, axis=-1).astype(BF16))
    attn = attn[0] if len(attn) == 1 else jnp.concatenate(attn, axis=0)
    y_a = _dot(ya_ref[...], wpa_ref[...])
    y_b = _dot(hb_ref[...], wpb_ref[...])
    y_c = _dot(attn, wpc_ref[...])
    d = D_MODEL
    mixed = (_sigmoid(gate_ref[:, 0:d]) * y_a + _sigmoid(gate_ref[:, d:2 * d]) * y_b
             + _sigmoid(gate_ref[:, 2 * d:3 * d]) * y_c)
    u = _bdot(mixed, wout_ref[...])
    o_ref[...] = _layer_norm(DN_ALPHA * x_ref[...] + u, lnw_ref[...], lnb_ref[...])


def _tail(x, ya, hb, p_mq, p_gate, mem_k, mem_v, w, rows, batches):
    n = x.shape[0]
    tm = rows * batches
    tiles_per_batch = (n // mem_k.shape[0]) // rows if batches == 1 else 1
    row = lambda i: (i, 0)
    mem = pl.BlockSpec((batches, N_MEM, D_MEM), lambda i: (i // tiles_per_batch, 0, 0))
    tok = lambda cols: pl.BlockSpec((tm, cols), row)
    return pl.pallas_call(
        functools.partial(_tail_kernel, rows=rows),
        grid=(n // tm,),
        in_specs=[
            tok(D_MODEL), tok(D_RWKV), tok(D_MLSTM), tok(D_MEM), tok(3 * D_MODEL), mem, mem,
            _const_spec((D_RWKV, D_MODEL)), _const_spec((D_MLSTM, D_MODEL)), _const_spec((D_MEM, D_MODEL)),
            _const_spec((D_MODEL, D_MODEL)), _const_spec((1, D_MODEL)), _const_spec((1, D_MODEL)),
        ],
        out_specs=tok(D_MODEL),
        out_shape=jax.ShapeDtypeStruct((n, D_MODEL), F32),
        compiler_params=_params(("arbitrary",)),
        name="attn_merge_ln",
    )(x, ya, hb, p_mq, p_gate, mem_k, mem_v, w["w_pa"], w["w_pb"], w["w_pc"], w["w_out"], w["ln1_w"],
      w["ln1_b"])


def _moe_kernel(x_ref, wr_ref, rb_ref, w1_ref, w3_ref, w2_ref, ws1_ref, ws3_ref, ws2_ref, lnw_ref, lnb_ref, o_ref):
    x = x_ref[...]
    xb = x.astype(BF16)
    scores = _sigmoid(_dot(xb, wr_ref[...]))
    lane = lax.broadcasted_iota(jnp.int32, scores.shape, 1)
    work = jnp.where(lane < N_EXPERTS, scores + rb_ref[...], -jnp.inf)
    chosen = jnp.zeros(scores.shape, dtype=jnp.bool_)
    for _ in range(TOP_K):
        best = jnp.max(work, axis=-1, keepdims=True)
        first = jnp.min(jnp.where(work == best, lane, LANES), axis=-1, keepdims=True)
        pick = lane == first
        chosen = chosen | pick
        work = jnp.where(pick, -jnp.inf, work)
    sel = jnp.where(chosen, scores, 0.0)
    gates = sel / jnp.sum(sel, axis=-1, keepdims=True) * ROUTED_SCALE

    acc = _dot((_silu(_dot(xb, ws1_ref[...])) * _dot(xb, ws3_ref[...])).astype(BF16), ws2_ref[...])
    group = 8
    width = group * EXPERT_HIDDEN
    for s in range(N_EXPERTS // group):
        cols = slice(s * width, (s + 1) * width)
        hidden = _silu(_dot(xb, w1_ref[:, cols])) * _dot(xb, w3_ref[:, cols])
        gated = jnp.concatenate(
            [hidden[:, e * EXPERT_HIDDEN:(e + 1) * EXPERT_HIDDEN] * gates[:, s * group + e:s * group + e + 1]
             for e in range(group)], axis=-1)
        acc = acc + _dot(gated.astype(BF16), w2_ref[cols, :])
    o_ref[...] = _layer_norm(DN_ALPHA * x + acc, lnw_ref[...], lnb_ref[...])


def _moe(x, w, tm):
    n = x.shape[0]
    row = lambda i: (i, 0)
    hid = N_EXPERTS * EXPERT_HIDDEN
    return pl.pallas_call(
        _moe_kernel,
        grid=(n // tm,),
        in_specs=[
            pl.BlockSpec((tm, D_MODEL), row),
            _const_spec((D_MODEL, LANES)), _const_spec((1, LANES)),
            _const_spec((D_MODEL, hid)), _const_spec((D_MODEL, hid)), _const_spec((hid, D_MODEL)),
            _const_spec((D_MODEL, EXPERT_HIDDEN)), _const_spec((D_MODEL, EXPERT_HIDDEN)),
            _const_spec((EXPERT_HIDDEN, D_MODEL)),
            _const_spec((1, D_MODEL)), _const_spec((1, D_MODEL)),
        ],
        out_specs=pl.BlockSpec((tm, D_MODEL), row),
        out_shape=jax.ShapeDtypeStruct((n, D_MODEL), F32),
        compiler_params=_params(("arbitrary",)),
        name="moe_ln",
    )(x, w["w_router"], w["router_bias"], w["w_e1"], w["w_e3"], w["w_e2"], w["w_s1"], w["w_s3"], w["w_s2"],
      w["ln2_w"], w["ln2_b"])


def _prep_weights(w_in, b_in, rwkv_mu, rwkv_w0, rwkv_w2, rwkv_a0, rwkv_a2, rwkv_g2, rwkv_k_k, rwkv_k_a,
                  rwkv_r_k, rwkv_ln_w, rwkv_ln_b, mlstm_conv_w, mlstm_conv_b, mlstm_b_i, mlstm_b_f,
                  mlstm_ln_w, mlstm_ln_b, w_mem_kv, w_pa, w_pb, w_pc, w_out, ln1_w, ln1_b,
                  w_router, router_bias, w_e1, w_e3, w_e2, w_s1, w_s3, w_s2, ln2_w, ln2_b):
    off_ml = RWKV_PROJ
    off_if = off_ml + 4 * D_MLSTM
    off_mq = off_if + 2 * MLSTM_HEADS
    pad_if = LANES - 2 * MLSTM_HEADS
    row = lambda t: t.reshape(1, -1)
    hid = N_EXPERTS * EXPERT_HIDDEN
    w_lora = jnp.zeros((LORA_ALL, 3 * D_RWKV), F32)
    w_lora = w_lora.at[0:W_LORA, 0:D_RWKV].set(rwkv_w2)
    w_lora = w_lora.at[W_LORA:W_LORA + A_LORA, D_RWKV:2 * D_RWKV].set(rwkv_a2)
    w_lora = w_lora.at[W_LORA + A_LORA:, 2 * D_RWKV:].set(rwkv_g2)
    return dict(
        w_rw=w_in[:, :off_ml].astype(BF16), b_rw=row(b_in[:off_ml]),
        w_qk=w_in[:, off_ml:off_ml + 2 * D_MLSTM].astype(BF16), b_qk=row(b_in[off_ml:off_ml + 2 * D_MLSTM]),
        w_vo=jnp.pad(w_in[:, off_ml + 2 * D_MLSTM:off_mq], ((0, 0), (0, pad_if))).astype(BF16),
        b_vo=row(jnp.pad(b_in[off_ml + 2 * D_MLSTM:off_mq], (0, pad_if))),
        w_mq=w_in[:, off_mq:off_mq + D_MEM].astype(BF16), b_mq=row(b_in[off_mq:off_mq + D_MEM]),
        w_gate=w_in[:, off_mq + D_MEM:].astype(BF16), b_gate=row(b_in[off_mq + D_MEM:]),
        w_mem_k=w_mem_kv[:, :D_MEM].astype(BF16), w_mem_v=w_mem_kv[:, D_MEM:].astype(BF16),
        mu=row(rwkv_mu), w0=row(rwkv_w0), a0=row(rwkv_a0), k_k=row(rwkv_k_k), k_a=row(rwkv_k_a),
        r_k=row(rwkv_r_k), rw_ln_w=row(rwkv_ln_w), rw_ln_b=row(rwkv_ln_b), w_lora=w_lora.astype(BF16),
        conv_w=mlstm_conv_w, conv_b=row(mlstm_conv_b),
        b_if=row(jnp.pad(jnp.concatenate([mlstm_b_i, mlstm_b_f]), (0, pad_if))),
        ml_ln_w=row(mlstm_ln_w), ml_ln_b=row(mlstm_ln_b),
        w_pa=w_pa.astype(BF16), w_pb=w_pb.astype(BF16), w_pc=w_pc.astype(BF16), w_out=w_out.astype(BF16),
        ln1_w=row(ln1_w), ln1_b=row(ln1_b),
        w_router=jnp.pad(w_router, ((0, 0), (0, LANES - N_EXPERTS))).astype(BF16),
        router_bias=row(jnp.pad(router_bias, (0, LANES - N_EXPERTS))),
        w_e1=jnp.transpose(w_e1.astype(BF16), (1, 0, 2)).reshape(D_MODEL, hid),
        w_e3=jnp.transpose(w_e3.astype(BF16), (1, 0, 2)).reshape(D_MODEL, hid),
        w_e2=w_e2.reshape(hid, D_MODEL).astype(BF16),
        w_s1=w_s1.astype(BF16), w_s3=w_s3.astype(BF16), w_s2=w_s2.astype(BF16),
        ln2_w=row(ln2_w), ln2_b=row(ln2_b),
    )


def _pair_state(wkv):
    bsz = wkv.shape[0]
    s = wkv.reshape(bsz, RWKV_PAIRS, 2, RWKV_HEAD_DIM, RWKV_HEAD_DIM)
    z = jnp.zeros_like(s[:, :, 0])
    top = jnp.concatenate([s[:, :, 0], z], axis=-1)
    bot = jnp.concatenate([z, s[:, :, 1]], axis=-1)
    return jnp.concatenate([top, bot], axis=-2)


def _unpair_state(h_bd):
    bsz = h_bd.shape[0]
    hd = RWKV_HEAD_DIM
    heads = jnp.stack([h_bd[:, :, :hd, :hd], h_bd[:, :, hd:, hd:]], axis=2)
    return heads.reshape(bsz, RWKV_HEADS, hd, hd)


def _trunk(x3, mem_k, mem_v, shift_prev, wkv0, conv_prev, c0, n0, m0, w, chunk, tm_proj, tail_rows, tail_batches,
           tm_moe):
    bsz, t_len, _ = x3.shape
    n = bsz * t_len
    x = x3.reshape(n, D_MODEL)
    p_rw = _matmul_bias(x, w["w_rw"], w["b_rw"], tm_proj, RWKV_PROJ // 2)
    qk, qk_tail = _qk_proj(x, w["w_qk"], w["b_qk"], w["conv_w"], w["conv_b"], conv_prev, tm_proj, D_MLSTM)
    p_vo = _matmul_bias(x, w["w_vo"], w["b_vo"], tm_proj, VO_COLS)
    p_mq = _matmul_bias(x, w["w_mq"], w["b_mq"], tm_proj, D_MEM, BF16)
    p_gate = _matmul_bias(x, w["w_gate"], w["b_gate"], tm_proj, 3 * D_MODEL // 2)

    p_rw3 = p_rw.reshape(bsz, t_len, RWKV_PROJ)
    ya, h_bd, hb, c_t, n_t, m_t = _recurrent(
        p_rw3, shift_prev, _pair_state(wkv0), qk.reshape(bsz, t_len, 2 * D_MLSTM), p_vo.reshape(bsz, t_len, VO_COLS),
        c0, n0, m0.reshape(bsz, 1, MLSTM_HEADS), w, chunk)
    x1 = _tail(x, ya.reshape(n, D_RWKV), hb.reshape(n, D_MLSTM), p_mq, p_gate, mem_k, mem_v, w, tail_rows,
               tail_batches)
    y = _moe(x1, w, tm_moe)

    new_shift = p_rw3[:, t_len - 1:, :]
    new_conv = qk_tail[:, 8 - (CONV_W - 1):, :]
    states = (new_shift, _unpair_state(h_bd), new_conv, c_t, n_t, m_t.reshape(bsz, MLSTM_HEADS))
    return y.reshape(bsz, t_len, D_MODEL), states


def kernel(x_prompt, x_sample, state_rwkv_shift, state_rwkv_wkv, state_mlstm_conv, state_mlstm_c, state_mlstm_n, state_mlstm_m, cache_mem_k, cache_mem_v, mem_prompt, w_in, b_in, rwkv_mu, rwkv_w0, rwkv_w2, rwkv_a0, rwkv_a2, rwkv_g2, rwkv_k_k, rwkv_k_a, rwkv_r_k, rwkv_ln_w, rwkv_ln_b, mlstm_conv_w, mlstm_conv_b, mlstm_b_i, mlstm_b_f, mlstm_ln_w, mlstm_ln_b, w_mem_kv, w_pa, w_pb, w_pc, w_out, ln1_w, ln1_b, w_router, router_bias, w_e1, w_e3, w_e2, w_s1, w_s3, w_s2, ln2_w, ln2_b):
    weights = (w_in, b_in, rwkv_mu, rwkv_w0, rwkv_w2, rwkv_a0, rwkv_a2, rwkv_g2, rwkv_k_k, rwkv_k_a, rwkv_r_k,
               rwkv_ln_w, rwkv_ln_b, mlstm_conv_w, mlstm_conv_b, mlstm_b_i, mlstm_b_f, mlstm_ln_w, mlstm_ln_b,
               w_mem_kv, w_pa, w_pb, w_pc, w_out, ln1_w, ln1_b, w_router, router_bias, w_e1, w_e3, w_e2,
               w_s1, w_s3, w_s2, ln2_w, ln2_b)
    w = _prep_weights(*(t[0] for t in weights))
    bp, t_p, _ = x_prompt.shape
    bs, t_s, _ = x_sample.shape
    hd = MLSTM_HEAD_DIM

    mem_flat = mem_prompt.reshape(bp * N_MEM, D_MODEL)
    zero_bias = jnp.zeros((1, D_MEM), F32)
    mk = _matmul_bias(mem_flat, w["w_mem_k"], zero_bias, bp * N_MEM, D_MEM).reshape(bp, N_MEM, D_MEM)
    mv = _matmul_bias(mem_flat, w["w_mem_v"], zero_bias, bp * N_MEM, D_MEM).reshape(bp, N_MEM, D_MEM)

    y_p, st_p = _trunk(
        x_prompt, mk, mv,
        jnp.zeros((bp, 1, RWKV_PROJ), F32), jnp.zeros((bp, RWKV_HEADS, RWKV_HEAD_DIM, RWKV_HEAD_DIM), F32),
        jnp.zeros((bp, CONV_W - 1, 2 * D_MLSTM), F32), jnp.zeros((bp, MLSTM_HEADS, hd, hd), F32),
        jnp.zeros((bp, MLSTM_HEADS, hd), F32), jnp.zeros((bp, MLSTM_HEADS), F32),
        w, chunk=min(MLSTM_CHUNK, t_p), tm_proj=1024, tail_rows=512, tail_batches=1, tm_moe=512)
    y_s, st_s = _trunk(
        x_sample, cache_mem_k[0].reshape(bs, N_MEM, D_MEM), cache_mem_v[0].reshape(bs, N_MEM, D_MEM),
        state_rwkv_shift[0], state_rwkv_wkv[0], state_mlstm_conv[0], state_mlstm_c[0], state_mlstm_n[0],
        state_mlstm_m[0],
        w, chunk=min(MLSTM_CHUNK, t_s), tm_proj=bs * t_s, tail_rows=t_s, tail_batches=8, tm_moe=bs * t_s)

    lead = lambda t: t[None]
    mem_shape = (1, bp, N_MEM, MEM_HEADS, MEM_HEAD_DIM)
    return (y_p, y_s, *(lead(t) for t in st_p), mk.reshape(mem_shape), mv.reshape(mem_shape),
            *(lead(t) for t in st_s))
```

```python
import functools
import itertools

import jax
import jax.numpy as jnp
from jax import lax
from jax.experimental import pallas as pl
from jax.experimental.pallas import tpu as pltpu

F32 = jnp.float32
BF16 = jnp.bfloat16

D_MODEL = 1024
DEPTH = 1
D_RWKV = 1024
RWKV_HEAD_DIM = 64
RWKV_HEADS = 16
W_LORA = 64
A_LORA = 64
G_LORA = 128
LORA_ALL = W_LORA + A_LORA + G_LORA
RWKV_PROJ = 3 * D_RWKV + LORA_ALL
RWKV_GN_EPS = 64e-5
DECAY_SCALE = 0.6065306597126334
D_MLSTM = 1024
MLSTM_HEADS = 4
MLSTM_HEAD_DIM = 256
CONV_W = 4
MLSTM_CHUNK = 64
N_MEM = 256
MEM_HEADS = 4
MEM_HEAD_DIM = 256
D_MEM = 1024
N_EXPERTS = 32
TOP_K = 4
EXPERT_HIDDEN = 128
ROUTED_SCALE = 2.5
DN_ALPHA = (2 * DEPTH) ** 0.25
LN_EPS = 1e-5

LANES = 128
RWKV_PAIRS = D_RWKV // LANES
VO_COLS = 2 * D_MLSTM + LANES
VMEM_LIMIT = 56 * 1024 * 1024
BATCH_GROUP = 2


def _dot(a, b):
    return jnp.dot(a, b, preferred_element_type=F32)


def _dot_nt(a, b):
    return lax.dot_general(a, b, (((1,), (1,)), ((), ())), preferred_element_type=F32)


def _dot_tn(a, b):
    return lax.dot_general(a, b, (((0,), (0,)), ((), ())), preferred_element_type=F32)


def _split(a):
    hi = a.astype(BF16)
    return hi, (a - hi.astype(F32)).astype(BF16)


def _split3(a):
    hi = a.astype(BF16)
    rest = a - hi.astype(F32)
    mid = rest.astype(BF16)
    return hi, mid, (rest - mid.astype(F32)).astype(BF16)


def _bdot(a, b):
    return _dot(a.astype(BF16), b.astype(BF16))


def _sigmoid(x):
    return 0.5 * jnp.tanh(0.5 * x) + 0.5


def _softplus(x):
    return jnp.maximum(x, 0.0) + jnp.log(1.0 + jnp.exp(-jnp.abs(x)))


def _silu(x):
    return x * _sigmoid(x)


def _layer_norm(x, w, b):
    mu = jnp.mean(x, axis=-1, keepdims=True)
    d = x - mu
    var = jnp.mean(d * d, axis=-1, keepdims=True)
    return d * lax.rsqrt(var + LN_EPS) * w + b


def _params(semantics):
    return pltpu.CompilerParams(dimension_semantics=semantics, vmem_limit_bytes=VMEM_LIMIT)


def _const_spec(shape):
    zeros = (0,) * len(shape)
    return pl.BlockSpec(shape, lambda *_: zeros, pipeline_mode=pl.Buffered(1))


def _mm_kernel(x_ref, w_ref, b_ref, o_ref):
    o_ref[...] = (_dot(x_ref[...].astype(BF16), w_ref[...]) + b_ref[...]).astype(o_ref.dtype)


def _matmul_bias(x, w, b, tm, tn, out_dtype=F32):
    n, k = x.shape
    nc = w.shape[1]
    resident = dict(pipeline_mode=pl.Buffered(1)) if nc == tn else {}
    return pl.pallas_call(
        _mm_kernel,
        grid=(nc // tn, n // tm),
        in_specs=[
            pl.BlockSpec((tm, k), lambda j, i: (i, 0)),
            pl.BlockSpec((k, tn), lambda j, i: (0, j), **resident),
            pl.BlockSpec((1, tn), lambda j, i: (0, j), **resident),
        ],
        out_specs=pl.BlockSpec((tm, tn), lambda j, i: (i, j)),
        out_shape=jax.ShapeDtypeStruct((n, nc), out_dtype),
        compiler_params=_params(("arbitrary", "arbitrary")),
        name="matmul_bias",
    )(x, w, b)


def _qk_proj_kernel(x_ref, w_ref, b_ref, cw_ref, cb_ref, prev_ref, o_ref, tail_ref, raw, *, rows, tiles_per_batch):
    col_block, row_tile = pl.program_id(0), pl.program_id(1)
    n_seg = prev_ref.shape[0]
    tm, tn = o_ref.shape
    half_scale = jnp.where(col_block >= D_MLSTM // tn, 0.5 * MLSTM_HEAD_DIM ** -0.5, 0.5)
    pad = jnp.zeros((8 - (CONV_W - 1), LANES), F32)
    row8 = lax.broadcasted_iota(jnp.int32, (8, LANES), 0)

    if n_seg == 1:
        @pl.when(row_tile % tiles_per_batch == 0)
        def _():
            raw[0:8 - (CONV_W - 1), :] = jnp.zeros((8 - (CONV_W - 1), tn), F32)
            raw[8 - (CONV_W - 1):8, :] = prev_ref[0]

    strip = min(rows, 64)
    sub = min(tm, 256)

    def project(k):
        rs = slice(k * sub, (k + 1) * sub)
        raw[8 + k * sub:8 + (k + 1) * sub, :] = _dot(x_ref[rs, :].astype(BF16), w_ref[...]) + b_ref[...]

    def conv_strips(k):
        for r0 in range(k * sub, (k + 1) * sub, strip):
            for c0 in range(0, tn, LANES):
                cols = slice(c0, c0 + LANES)
                if n_seg > 1 and r0 % rows == 0:
                    head = jnp.concatenate([pad, prev_ref[r0 // rows, :, cols]], axis=0)
                else:
                    head = raw[r0:r0 + 8, cols]
                x = raw[8 + r0:8 + r0 + strip, cols]
                blocks = [head] + [x[r:r + 8, :] for r in range(0, strip, 8)]
                conv = cb_ref[:, cols] + cw_ref[CONV_W - 1:CONV_W, cols] * x
                for j in range(CONV_W - 1):
                    lag = CONV_W - 1 - j
                    turned = [pltpu.roll(blk, lag, 0) for blk in blocks]
                    shifted = jnp.concatenate([jnp.where(row8 < lag, turned[i], turned[i + 1])
                                               for i in range(len(blocks) - 1)], axis=0)
                    conv = conv + cw_ref[j:j + 1, cols] * shifted
                o_ref[r0:r0 + strip, cols] = conv * (half_scale * jnp.tanh(0.5 * conv) + half_scale)

    project(0)
    for k in range(tm // sub):
        if k + 1 < tm // sub:
            project(k + 1)
        conv_strips(k)

    for seg in range(n_seg):
        tail_ref[seg] = raw[(seg + 1) * rows:(seg + 1) * rows + 8, :]
    if n_seg == 1:
        raw[0:8, :] = raw[tm:tm + 8, :]


def _qk_proj(x, w, b, conv_w, conv_b, conv_prev, tm, tn):
    n, k = x.shape
    batch = conv_prev.shape[0]
    t_len = n // batch
    rows = min(tm, t_len)
    n_seg = tm // rows
    tiles_per_batch = t_len // rows
    nc = w.shape[1]
    seg = lambda j, i: (i // tiles_per_batch if n_seg == 1 else i, 0, j)
    return pl.pallas_call(
        functools.partial(_qk_proj_kernel, rows=rows, tiles_per_batch=tiles_per_batch),
        grid=(nc // tn, n // tm),
        in_specs=[
            pl.BlockSpec((tm, k), lambda j, i: (i, 0)),
            pl.BlockSpec((k, tn), lambda j, i: (0, j)),
            pl.BlockSpec((1, tn), lambda j, i: (0, j)),
            pl.BlockSpec((CONV_W, tn), lambda j, i: (0, j)),
            pl.BlockSpec((1, tn), lambda j, i: (0, j)),
            pl.BlockSpec((n_seg, CONV_W - 1, tn), seg),
        ],
        out_specs=[
            pl.BlockSpec((tm, tn), lambda j, i: (i, j)),
            pl.BlockSpec((n_seg, 8, tn), seg),
        ],
        out_shape=[jax.ShapeDtypeStruct((n, nc), F32), jax.ShapeDtypeStruct((batch, 8, nc), F32)],
        scratch_shapes=[pltpu.VMEM((tm + 8, tn), F32)],
        compiler_params=_params(("arbitrary", "arbitrary")),
        name="qk_proj_conv",
    )(x, w, b, conv_w, conv_b, conv_prev)


def _rwkv_body(p_ref, prev_ref, h0_ref, mu_ref, w0_ref, a0_ref, kk_ref, ka_ref, rk_ref, lnw_ref, lnb_ref,
               wl_ref, y_ref, hout_ref, h_scr, xbuf, ar_scr, bk_scr, bkh_scr, v_scr, bonus_scr, g_scr, gamma_scr,
               *, chunk, n_chunks, n_items, companion):
    step = pl.program_id(0)
    L = chunk
    half = 2 * L
    prep_chunk = jnp.minimum(step, n_items - 1) % n_chunks
    solve_chunk = jnp.maximum(step - 1, 0) % n_chunks

    @pl.when(step == 0)
    def _():
        for ref in (ar_scr, bk_scr, bkh_scr, v_scr, bonus_scr, g_scr):
            ref[...] = jnp.zeros(ref.shape, ref.dtype)
        gamma_scr[...] = jnp.ones(gamma_scr.shape, F32)

    @pl.when(solve_chunk == 0)
    def _():
        h_scr[...] = h0_ref[...]

    @pl.when(prep_chunk == 0)
    def _():
        for bi in range(BATCH_GROUP):
            xbuf[bi, 0:1, :] = prev_ref[bi]

    ti = lax.broadcasted_iota(jnp.int32, (L, L), 0)
    tj = lax.broadcasted_iota(jnp.int32, (L, L), 1)
    ltri = (tj <= ti).astype(BF16)
    row_m = lax.broadcasted_iota(jnp.int32, (L, half), 0)
    col_m = lax.broadcasted_iota(jnp.int32, (L, half), 1)
    col_in = jnp.where(col_m >= L, col_m - L, col_m)
    strict = col_in < row_m
    incl = col_in <= row_m
    eye = (col_in == row_m).astype(F32)
    gi = lax.broadcasted_iota(jnp.int32, (LANES, LANES), 0)
    gj = lax.broadcasted_iota(jnp.int32, (LANES, LANES), 1)
    same_head = (gi >= RWKV_HEAD_DIM) == (gj >= RWKV_HEAD_DIM)
    head_ones = same_head.astype(BF16)
    first_nat = lax.broadcasted_iota(jnp.int32, (L, LANES), 1) < RWKV_HEAD_DIM
    first_sbs = col_m < L
    lane_l = lax.broadcasted_iota(jnp.int32, (L, LORA_ALL), 1)

    def diag(x, first):
        zero = jnp.zeros_like(x)
        return jnp.concatenate([jnp.where(first, x, zero), jnp.where(first, zero, x)], axis=0)

    def seg_sum(x):
        return _dot(x.astype(BF16), head_ones)

    units = [(bi, p) for bi in range(BATCH_GROUP) for p in range(RWKV_PAIRS)]
    idx = range(len(units))
    lane_of = lambda p: slice(p * LANES, (p + 1) * LANES)

    ar = [ar_scr[bi, :, lane_of(p)] for bi, p in units]
    bk = [bk_scr[bi, :, lane_of(p)] for bi, p in units]
    bkh = [bkh_scr[bi, :, lane_of(p)] for bi, p in units]
    v_b = [v_scr[bi, :, lane_of(p)] for bi, p in units]
    bonus = [bonus_scr[bi, :, lane_of(p)] for bi, p in units]
    gate = [g_scr[bi, :, lane_of(p)] for bi, p in units]
    gamma = [gamma_scr[bi, :, lane_of(p)] for bi, p in units]

    first_row = lax.broadcasted_iota(jnp.int32, (L, RWKV_PROJ), 0) == 0

    def prepare_row(bi):
        cur = p_ref[bi]
        prev = jnp.where(first_row, xbuf[bi, 0:1, :], pltpu.roll(cur, 1, 0))
        xbuf[bi, 0:1, :] = cur[L - 1:L, :]
        xr = cur + (prev - cur) * mu_ref[...]
        r_all = xr[:, 0:D_RWKV]
        k_all = xr[:, D_RWKV:2 * D_RWKV]
        v_all = xr[:, 2 * D_RWKV:3 * D_RWKV]
        slab = xr[:, 3 * D_RWKV:]
        act = jnp.where(lane_l < W_LORA, jnp.tanh(slab),
                        jnp.where(lane_l < W_LORA + A_LORA, slab, _sigmoid(slab)))
        yield
        lora = _bdot(act, wl_ref[...])
        yield
        lw = -DECAY_SCALE * _sigmoid(w0_ref[...] + lora[:, 0:D_RWKV])
        a_sig = _sigmoid(a0_ref[...] + lora[:, D_RWKV:2 * D_RWKV])
        lw_hi, lw_lo = _split(lw)
        yield
        kk0 = k_all * kk_ref[...]
        k2 = k_all * (1.0 + (a_sig - 1.0) * ka_ref[...])
        sums = jnp.concatenate([kk0 * kk0, r_all * k2 * rk_ref[...]], axis=0).astype(BF16)
        yield
        cum = _dot(ltri, lw_hi) + _dot(ltri, lw_lo)
        sums = jnp.concatenate([_dot(sums[:, lane_of(p)], head_ones) for p in range(RWKV_PAIRS)], axis=1)
        yield
        kk = kk0 * lax.rsqrt(jnp.maximum(sums[0:L], 1e-24))
        b_vec = kk * a_sig
        c_last = cum[L - 1:L, :]
        e_neg = jnp.exp(-cum)
        e_end = jnp.exp(c_last - cum)
        yield
        ar_scr[bi] = jnp.concatenate([-kk * jnp.exp(cum - lw), r_all * jnp.exp(cum)], axis=0).astype(BF16)
        bk_scr[bi] = jnp.concatenate([b_vec * e_neg, k2 * e_neg], axis=0).astype(BF16)
        yield
        bkh_scr[bi] = jnp.concatenate([b_vec * e_end, k2 * e_end], axis=0).astype(BF16)
        v_scr[bi] = v_all.astype(BF16)
        bonus_scr[bi] = sums[L:] * v_all
        g_scr[bi] = lora[:, 2 * D_RWKV:]
        gamma_scr[bi] = jnp.exp(c_last)

    segments = itertools.chain(*(prepare_row(bi) for bi in range(BATCH_GROUP)))

    def prepare_some(count):
        for _ in range(count):
            next(segments, None)
            next(companion, None)

    n_sq = L.bit_length() - 1
    per_stage = 2
    bk_d = [jnp.concatenate([diag(bk[u][0:L], first_nat), diag(bk[u][L:], first_nat)], axis=0) for u in idx]
    v_d = [diag(v_b[u], first_nat) for u in idx]
    h_bd = [h_scr[bi, p] for bi, p in units]
    sc = [_dot_nt(ar[u], bk_d[u]) for u in idx]
    prepare_some(per_stage)
    arh = [_dot_nt(ar[u], h_bd[u].astype(BF16)) for u in idx]
    m_ab = [jnp.where(strict, sc[u][0:L, 0:half], 0.0) for u in idx]
    m_ak = [jnp.where(strict, sc[u][0:L, half:], 0.0).astype(BF16) for u in idx]
    n_rbk = [jnp.concatenate([jnp.where(incl, sc[u][L:, 0:half], 0.0),
                              jnp.where(incl, sc[u][L:, half:], 0.0)], axis=1).astype(BF16) for u in idx]
    prepare_some(per_stage)
    w0 = [arh[u][0:L] + _dot(m_ak[u], v_d[u]) for u in idx]
    inv = [eye + m_ab[u] for u in idx]
    m_b = [m_ab[u].astype(BF16) for u in idx]
    pw = [_dot(m_b[u], diag(m_b[u], first_sbs)) for u in idx]
    prepare_some(per_stage)
    for i in range(n_sq - 1):
        pw_d = [diag(pw[u].astype(BF16), first_sbs) for u in idx]
        if i + 1 < n_sq - 1:
            z = [_dot(jnp.concatenate([inv[u], pw[u]], axis=0).astype(BF16), pw_d[u]) for u in idx]
            inv = [inv[u] + z[u][0:L] for u in idx]
            pw = [z[u][L:] for u in idx]
        else:
            inv = [inv[u] + _dot(inv[u].astype(BF16), pw_d[u]) for u in idx]
        prepare_some(per_stage)
    c_b = [_dot(inv[u].astype(BF16), diag(w0[u].astype(BF16), first_nat)).astype(BF16) for u in idx]
    cv = [jnp.concatenate([c_b[u], v_b[u]], axis=0) for u in idx]
    y = [arh[u][L:] + _dot(n_rbk[u], jnp.concatenate([diag(c_b[u], first_nat), v_d[u]], axis=0)) for u in idx]
    for u, (bi, p) in enumerate(units):
        h_scr[bi, p] = h_bd[u] * gamma[u] + jnp.where(same_head, _dot_tn(cv[u], bkh[u]), 0.0)
    mean = [seg_sum(y[u]) * (1.0 / RWKV_HEAD_DIM) for u in idx]
    dev = [y[u] - mean[u] for u in idx]
    var = [seg_sum(dev[u] * dev[u]) * (1.0 / RWKV_HEAD_DIM) for u in idx]
    for u, (bi, p) in enumerate(units):
        s = lane_of(p)
        yn = dev[u] * lax.rsqrt(var[u] + RWKV_GN_EPS) * lnw_ref[:, s] + lnb_ref[:, s]
        y_ref[bi, :, s] = ((yn + bonus[u]) * gate[u]).astype(y_ref.dtype)
    prepare_some(8 * BATCH_GROUP)
    for _ in companion:
        pass

    @pl.when((solve_chunk == n_chunks - 1) & (step > 0))
    def _():
        hout_ref[...] = h_scr[...]


def _mlstm_stages(qk_ref, p_ref, c0_ref, n0_ref, m0_ref, bif_ref, lnw_ref, lnb_ref,
                  h_ref, cout_ref, nout_ref, mout_ref, c_scr, n_scr, m_scr, *, chunk, first, last):
    L = chunk
    d = D_MLSTM
    nh = MLSTM_HEADS

    @pl.when(first)
    def _():
        c_scr[...] = c0_ref[...]
        n_scr[...] = n0_ref[...]
        m_scr[...] = m0_ref[...]

    yield

    ti =lax.broadcasted_iota(jnp.int32, (L, L), 0)
    tj = lax.broadcasted_iota(jnp.int32, (L, L), 1)
    causal = tj <= ti
    ltri = causal.astype(BF16)
    utri = (ti <= tj).astype(BF16)
    sel = (lax.broadcasted_iota(jnp.int32, (8, LANES), 0)
           == lax.broadcasted_iota(jnp.int32, (8, LANES), 1)).astype(BF16)
    lane_g = lax.broadcasted_iota(jnp.int32, (L, LANES), 1)
    is_f = (lane_g >= nh) & (lane_g < 2 * nh)

    rows = []
    for bi in range(BATCH_GROUP):
        gates = p_ref[bi, :, 2 * d:] + bif_ref[...]
        glog = jnp.where(is_f, -_softplus(-gates), gates)
        parts = _split3(glog)
        b_cols = sum(_dot(ltri, t) for t in parts)
        g_rows = sum(_dot_nt(sel, t) for t in parts)
        b_rows = sum(_dot(t, utri) for t in _split3(g_rows))
        rows.append(dict(q=qk_ref[bi, :, 0:d], k=qk_ref[bi, :, d:2 * d], v=p_ref[bi, :, 0:d],
                         o=_sigmoid(p_ref[bi, :, d:2 * d]),
                         glog=glog, b_cols=b_cols, g_rows=g_rows, b_rows=b_rows))
        yield

    units = [(bi, h) for bi in range(BATCH_GROUP) for h in range(nh)]
    idx = range(len(units))
    head = lambda h: slice(h * MLSTM_HEAD_DIM, (h + 1) * MLSTM_HEAD_DIM)
    q = [rows[bi]["q"][:, head(h)] for bi, h in units]
    k = [rows[bi]["k"][:, head(h)] for bi, h in units]
    v = [rows[bi]["v"][:, head(h)] for bi, h in units]
    q_b = [t.astype(BF16) for t in q]
    k_b = [t.astype(BF16) for t in k]
    c_st = [c_scr[bi, h] for bi, h in units]
    n_st = [n_scr[bi, h:h + 1, :] for bi, h in units]
    m_prev = [m_scr[bi, :, h:h + 1] for bi, h in units]
    qk_t = [_dot_nt(q_b[u], k_b[u]) for u in idx]
    yield
    qc = [_dot_nt(q_b[u], c_st[u].astype(BF16)) for u in idx]
    yield
    s, w_inter, m_t, wk, carry, m_next = [], [], [], [], [], []
    for u, (bi, h) in enumerate(units):
        t = rows[bi]
        ig_row = t["g_rows"][h:h + 1, :]
        b_row = t["b_rows"][nh + h:nh + h + 1, :]
        ig_col = t["glog"][:, h:h + 1]
        b_col = t["b_cols"][:, nh + h:nh + h + 1]
        log_w = jnp.where(causal, b_col - b_row + ig_row, -jnp.inf)
        inter = b_col + m_prev[u]
        m_now = jnp.maximum(inter, jnp.max(log_w, axis=-1, keepdims=True))
        s.append(qk_t[u] * jnp.exp(log_w - m_now))
        w_inter.append(jnp.exp(inter - m_now))
        m_t.append(m_now)
        b_last = b_col[L - 1:L, :]
        g_col = b_last - b_col + ig_col
        m_new = jnp.maximum(b_last + m_prev[u], jnp.max(g_col, axis=0, keepdims=True))
        carry.append(jnp.exp(b_last + m_prev[u] - m_new))
        wk.append(jnp.exp(g_col - m_new))
        m_next.append(m_new)
        if u % 2 == 1:
            yield
    sv = [_dot(s[u].astype(BF16), v[u].astype(BF16)) for u in idx]
    yield
    upd = [_dot_tn((wk[u] * v[u]).astype(BF16), k_b[u]) for u in idx]
    yield
    qn =[jnp.sum(q[u] * n_st[u], axis=-1, keepdims=True) for u in idx]
    s_sum = [jnp.sum(s[u], axis=-1, keepdims=True) for u in idx]
    den = [jnp.maximum(jnp.abs(w_inter[u] * qn[u] + s_sum[u]), jnp.exp(-m_t[u])) for u in idx]
    yield
    hb = [rows[bi]["o"][:, head(h)] * ((w_inter[u] * qc[u] + sv[u]) / den[u]) for u, (bi, h) in enumerate(units)]
    yield
    mu = [jnp.mean(hb[u], axis=-1, keepdims=True) for u in idx]
    dev = [hb[u] - mu[u] for u in idx]
    var = [jnp.mean(dev[u] * dev[u], axis=-1, keepdims=True) for u in idx]
    yield
    n_new = [carry[u] * n_st[u] + jnp.sum(wk[u] * k[u], axis=0, keepdims=True) for u in idx]
    for u, (bi, h) in enumerate(units):
        sl = head(h)
        h_ref[bi, :, sl] = (dev[u] * lax.rsqrt(var[u] + LN_EPS) * lnw_ref[:, sl] + lnb_ref[:, sl]).astype(h_ref.dtype)
        c_scr[bi, h] = carry[u] * c_st[u] + upd[u]
        n_scr[bi, h:h + 1, :] = n_new[u]
        m_scr[bi, :, h:h + 1] = m_next[u]
        if u % 4 == 3:
            yield

    @pl.when(last)
    def _():
        cout_ref[...] = c_scr[...]
        nout_ref[...] = n_scr[...]
        mout_ref[...] = m_scr[...]


RW_REFS = (12, 2, 9)
ML_REFS = (8, 4, 3)


def _recurrent_kernel(*refs, chunk, n_chunks, n_items):
    groups, pos = [], 0
    for n_rw, n_ml in zip(RW_REFS, ML_REFS):
        groups.append((refs[pos:pos + n_rw], refs[pos + n_rw:pos + n_rw + n_ml]))
        pos += n_rw + n_ml
    rw_refs = [r for rw_part, _ in groups for r in rw_part]
    ml_refs = [r for _, ml_part in groups for r in ml_part]
    step = pl.program_id(0)
    solve_chunk = jnp.maximum(step - 1, 0) % n_chunks
    ml = _mlstm_stages(*ml_refs, chunk=chunk, first=solve_chunk == 0,
                       last=(solve_chunk == n_chunks - 1) & (step > 0))
    next(ml)
    _rwkv_body(*rw_refs, chunk=chunk, n_chunks=n_chunks, n_items=n_items, companion=ml)


def _recurrent(p_rw, shift_prev, h0_bd, qk, p_vo, c0, n0, m0, w, chunk):
    batch, t_len, _ = p_rw.shape
    n_chunks = t_len // chunk
    bg = BATCH_GROUP
    hd = MLSTM_HEAD_DIM
    n_items = (batch // bg) * n_chunks
    prep = lambda i: jnp.minimum(i, n_items - 1)
    solve = lambda i: jnp.maximum(i - 1, 0)
    prep_seq = lambda i: (prep(i) // n_chunks, prep(i) % n_chunks, 0)
    solve_seq = lambda i: (solve(i) // n_chunks, solve(i) % n_chunks, 0)
    vec = lambda n: pl.BlockSpec((1, n), lambda i: (0, 0))
    group = lambda *dims: pl.BlockSpec((bg,) + dims, lambda i: (solve(i) // n_chunks,) + (0,) * len(dims))
    rw_state = group(RWKV_PAIRS, LANES, LANES)
    ml_states = [group(MLSTM_HEADS, hd, hd), group(MLSTM_HEADS, hd), group(1, MLSTM_HEADS)]
    wide = lambda rows, dtype: pltpu.VMEM((bg, rows, D_RWKV), dtype)
    kern = functools.partial(_recurrent_kernel, chunk=chunk, n_chunks=n_chunks, n_items=n_items)
    return pl.pallas_call(
        kern,
        grid=(n_items + 1,),
        in_specs=[
            pl.BlockSpec((bg, chunk, RWKV_PROJ), prep_seq),
            pl.BlockSpec((bg, 1, RWKV_PROJ), lambda i: (prep(i) // n_chunks, 0, 0)),
            rw_state,
            vec(RWKV_PROJ), vec(D_RWKV), vec(D_RWKV), vec(D_RWKV), vec(D_RWKV), vec(D_RWKV), vec(D_RWKV),
            vec(D_RWKV),
            pl.BlockSpec((LORA_ALL, 3 * D_RWKV), lambda i: (0, 0)),
            pl.BlockSpec((bg, chunk, 2 * D_MLSTM), solve_seq),
            pl.BlockSpec((bg, chunk, VO_COLS), solve_seq),
            *ml_states,
            vec(LANES), vec(D_MLSTM), vec(D_MLSTM),
        ],
        out_specs=[
            pl.BlockSpec((bg, chunk, D_RWKV), solve_seq), rw_state,
            pl.BlockSpec((bg, chunk, D_MLSTM), solve_seq), *ml_states,
        ],
        out_shape=[
            jax.ShapeDtypeStruct((batch, t_len, D_RWKV), BF16),
            jax.ShapeDtypeStruct((batch, RWKV_PAIRS, LANES, LANES), F32),
            jax.ShapeDtypeStruct((batch, t_len, D_MLSTM), BF16),
            jax.ShapeDtypeStruct((batch, MLSTM_HEADS, hd, hd), F32),
            jax.ShapeDtypeStruct((batch, MLSTM_HEADS, hd), F32),
            jax.ShapeDtypeStruct((batch, 1, MLSTM_HEADS), F32),
        ],
        scratch_shapes=[
            pltpu.VMEM((bg, RWKV_PAIRS, LANES, LANES), F32),
            pltpu.VMEM((bg, 8, RWKV_PROJ), F32),
            wide(2 * chunk, BF16), wide(2 * chunk, BF16), wide(2 * chunk, BF16), wide(chunk, BF16),
            wide(chunk, F32), wide(chunk, F32), wide(1, F32),
            pltpu.VMEM((bg, MLSTM_HEADS, hd, hd), F32),
            pltpu.VMEM((bg, MLSTM_HEADS, hd), F32),
            pltpu.VMEM((bg, 1, MLSTM_HEADS), F32),
        ],
        compiler_params=_params(("arbitrary",)),
        name="rwkv7_mlstm_recurrent",
    )(p_rw, shift_prev, h0_bd, w["mu"], w["w0"], w["a0"], w["k_k"], w["k_a"], w["r_k"], w["rw_ln_w"],
      w["rw_ln_b"], w["w_lora"], qk, p_vo, c0, n0, m0, w["b_if"], w["ml_ln_w"], w["ml_ln_b"])


def _tail_kernel(x_ref, ya_ref, hb_ref, mq_ref, gate_ref, mk_ref, mv_ref, wpa_ref, wpb_ref, wpc_ref, wout_ref,
                 lnw_ref, lnb_ref, o_ref, *, rows):
    attn = []
    for b in range(mk_ref.shape[0]):
        q = mq_ref[b * rows:(b + 1) * rows, :]
        mk = mk_ref[b].astype(BF16)
        mv = mv_ref[b].astype(BF16)
        heads = []
        for h in range(MEM_HEADS):
            sl = slice(h * MEM_HEAD_DIM, (h + 1) * MEM_HEAD_DIM)
            s = _dot_nt(q[:, sl], mk[:, sl]) * (MEM_HEAD_DIM ** -0.5)
            e = jnp.exp(s - jnp.max(s, axis=-1, keepdims=True))
            prob = e / jnp.sum(e, axis=-1, keepdims=True)
            heads.append(_dot(prob.astype(BF16), mv[:, sl]))
        attn.append(jnp.concatenate(heads, axis=-1).astype(BF16))
    attn = attn[0] if len(attn) == 1 else jnp.concatenate(attn, axis=0)
    y_a = _dot(ya_ref[...], wpa_ref[...])
    y_b = _dot(hb_ref[...], wpb_ref[...])
    y_c = _dot(attn, wpc_ref[...])
    d = D_MODEL
    mixed = (_sigmoid(gate_ref[:, 0:d]) * y_a + _sigmoid(gate_ref[:, d:2 * d]) * y_b
             + _sigmoid(gate_ref[:, 2 * d:3 * d]) * y_c)
    u = _bdot(mixed, wout_ref[...])
    o_ref[...] = _layer_norm(DN_ALPHA * x_ref[...] + u, lnw_ref[...], lnb_ref[...])


def _tail(x, ya, hb, p_mq, p_gate, mem_k, mem_v, w, rows, batches):
    n = x.shape[0]
    tm = rows * batches
    tiles_per_batch = (n // mem_k.shape[0]) // rows if batches == 1 else 1
    row = lambda i: (i, 0)
    mem = pl.BlockSpec((batches, N_MEM, D_MEM), lambda i: (i // tiles_per_batch, 0, 0))
    tok = lambda cols: pl.BlockSpec((tm, cols), row)
    return pl.pallas_call(
        functools.partial(_tail_kernel, rows=rows),
        grid=(n // tm,),
        in_specs=[
            tok(D_MODEL), tok(D_RWKV), tok(D_MLSTM), tok(D_MEM), tok(3 * D_MODEL), mem, mem,
            _const_spec((D_RWKV, D_MODEL)), _const_spec((D_MLSTM, D_MODEL)), _const_spec((D_MEM, D_MODEL)),
            _const_spec((D_MODEL, D_MODEL)), _const_spec((1, D_MODEL)), _const_spec((1, D_MODEL)),
        ],
        out_specs=tok(D_MODEL),
        out_shape=jax.ShapeDtypeStruct((n, D_MODEL), F32),
        compiler_params=_params(("arbitrary",)),
        name="attn_merge_ln",
    )(x, ya, hb, p_mq, p_gate, mem_k, mem_v, w["w_pa"], w["w_pb"], w["w_pc"], w["w_out"], w["ln1_w"],
      w["ln1_b"])


def _moe_kernel(x_ref, wr_ref, rb_ref, w1_ref, w3_ref, w2_ref, ws1_ref, ws3_ref, ws2_ref, lnw_ref, lnb_ref, o_ref):
    x = x_ref[...]
    xb = x.astype(BF16)
    scores = _sigmoid(_dot(xb, wr_ref[...]))
    lane = lax.broadcasted_iota(jnp.int32, scores.shape, 1)
    work = jnp.where(lane < N_EXPERTS, scores + rb_ref[...], -jnp.inf)
    chosen = jnp.zeros(scores.shape, dtype=jnp.bool_)
    for _ in range(TOP_K):
        best = jnp.max(work, axis=-1, keepdims=True)
        first = jnp.min(jnp.where(work == best, lane, LANES), axis=-1, keepdims=True)
        pick = lane == first
        chosen = chosen | pick
        work = jnp.where(pick, -jnp.inf, work)
    sel = jnp.where(chosen, scores, 0.0)
    gates = sel / jnp.sum(sel, axis=-1, keepdims=True) * ROUTED_SCALE

    acc = _dot((_silu(_dot(xb, ws1_ref[...])) * _dot(xb, ws3_ref[...])).astype(BF16), ws2_ref[...])
    group = 8
    width = group * EXPERT_HIDDEN
    for s in range(N_EXPERTS // group):
        cols = slice(s * width, (s + 1) * width)
        hidden = _silu(_dot(xb, w1_ref[:, cols])) * _dot(xb, w3_ref[:, cols])
        gated = jnp.concatenate(
            [hidden[:, e * EXPERT_HIDDEN:(e + 1) * EXPERT_HIDDEN] * gates[:, s * group + e:s * group + e + 1]
             for e in range(group)], axis=-1)
        acc = acc + _dot(gated.astype(BF16), w2_ref[cols, :])
    o_ref[...] = _layer_norm(DN_ALPHA * x + acc, lnw_ref[...], lnb_ref[...])


def _moe(x, w, tm):
    n = x.shape[0]
    row = lambda i: (i, 0)
    hid = N_EXPERTS * EXPERT_HIDDEN
    return pl.pallas_call(
        _moe_kernel,
        grid=(n // tm,),
        in_specs=[
            pl.BlockSpec((tm, D_MODEL), row),
            _const_spec((D_MODEL, LANES)), _const_spec((1, LANES)),
            _const_spec((D_MODEL, hid)), _const_spec((D_MODEL, hid)), _const_spec((hid, D_MODEL)),
            _const_spec((D_MODEL, EXPERT_HIDDEN)), _const_spec((D_MODEL, EXPERT_HIDDEN)),
            _const_spec((EXPERT_HIDDEN, D_MODEL)),
            _const_spec((1, D_MODEL)), _const_spec((1, D_MODEL)),
        ],
        out_specs=pl.BlockSpec((tm, D_MODEL), row),
        out_shape=jax.ShapeDtypeStruct((n, D_MODEL), F32),
        compiler_params=_params(("arbitrary",)),
        name="moe_ln",
    )(x, w["w_router"], w["router_bias"], w["w_e1"], w["w_e3"], w["w_e2"], w["w_s1"], w["w_s3"], w["w_s2"],
      w["ln2_w"], w["ln2_b"])


def _prep_weights(w_in, b_in, rwkv_mu, rwkv_w0, rwkv_w2, rwkv_a0, rwkv_a2, rwkv_g2, rwkv_k_k, rwkv_k_a,
                  rwkv_r_k, rwkv_ln_w, rwkv_ln_b, mlstm_conv_w, mlstm_conv_b, mlstm_b_i, mlstm_b_f,
                  mlstm_ln_w, mlstm_ln_b, w_mem_kv, w_pa, w_pb, w_pc, w_out, ln1_w, ln1_b,
                  w_router, router_bias, w_e1, w_e3, w_e2, w_s1, w_s3, w_s2, ln2_w, ln2_b):
    off_ml = RWKV_PROJ
    off_if = off_ml + 4 * D_MLSTM
    off_mq = off_if + 2 * MLSTM_HEADS
    pad_if = LANES - 2 * MLSTM_HEADS
    row = lambda t: t.reshape(1, -1)
    hid = N_EXPERTS * EXPERT_HIDDEN
    w_lora = jnp.zeros((LORA_ALL, 3 * D_RWKV), F32)
    w_lora = w_lora.at[0:W_LORA, 0:D_RWKV].set(rwkv_w2)
    w_lora = w_lora.at[W_LORA:W_LORA + A_LORA, D_RWKV:2 * D_RWKV].set(rwkv_a2)
    w_lora = w_lora.at[W_LORA + A_LORA:, 2 * D_RWKV:].set(rwkv_g2)
    return dict(
        w_rw=w_in[:, :off_ml].astype(BF16), b_rw=row(b_in[:off_ml]),
        w_qk=w_in[:, off_ml:off_ml + 2 * D_MLSTM].astype(BF16), b_qk=row(b_in[off_ml:off_ml + 2 * D_MLSTM]),
        w_vo=jnp.pad(w_in[:, off_ml + 2 * D_MLSTM:off_mq], ((0, 0), (0, pad_if))).astype(BF16),
        b_vo=row(jnp.pad(b_in[off_ml + 2 * D_MLSTM:off_mq], (0, pad_if))),
        w_mq=w_in[:, off_mq:off_mq + D_MEM].astype(BF16), b_mq=row(b_in[off_mq:off_mq + D_MEM]),
        w_gate=w_in[:, off_mq + D_MEM:].astype(BF16), b_gate=row(b_in[off_mq + D_MEM:]),
        w_mem_k=w_mem_kv[:, :D_MEM].astype(BF16), w_mem_v=w_mem_kv[:, D_MEM:].astype(BF16),
        mu=row(rwkv_mu), w0=row(rwkv_w0), a0=row(rwkv_a0), k_k=row(rwkv_k_k), k_a=row(rwkv_k_a),
        r_k=row(rwkv_r_k), rw_ln_w=row(rwkv_ln_w), rw_ln_b=row(rwkv_ln_b), w_lora=w_lora.astype(BF16),
        conv_w=mlstm_conv_w, conv_b=row(mlstm_conv_b),
        b_if=row(jnp.pad(jnp.concatenate([mlstm_b_i, mlstm_b_f]), (0, pad_if))),
        ml_ln_w=row(mlstm_ln_w), ml_ln_b=row(mlstm_ln_b),
        w_pa=w_pa.astype(BF16), w_pb=w_pb.astype(BF16), w_pc=w_pc.astype(BF16), w_out=w_out.astype(BF16),
        ln1_w=row(ln1_w), ln1_b=row(ln1_b),
        w_router=jnp.pad(w_router, ((0, 0), (0, LANES - N_EXPERTS))).astype(BF16),
        router_bias=row(jnp.pad(router_bias, (0, LANES - N_EXPERTS))),
        w_e1=jnp.transpose(w_e1.astype(BF16), (1, 0, 2)).reshape(D_MODEL, hid),
        w_e3=jnp.transpose(w_e3.astype(BF16), (1, 0, 2)).reshape(D_MODEL, hid),
        w_e2=w_e2.reshape(hid, D_MODEL).astype(BF16),
        w_s1=w_s1.astype(BF16), w_s3=w_s3.astype(BF16), w_s2=w_s2.astype(BF16),
        ln2_w=row(ln2_w), ln2_b=row(ln2_b),
    )


def _pair_state(wkv):
    bsz = wkv.shape[0]
    s = wkv.reshape(bsz, RWKV_PAIRS, 2, RWKV_HEAD_DIM, RWKV_HEAD_DIM)
    z = jnp.zeros_like(s[:, :, 0])
    top = jnp.concatenate([s[:, :, 0], z], axis=-1)
    bot = jnp.concatenate([z, s[:, :, 1]], axis=-1)
    return jnp.concatenate([top, bot], axis=-2)


def _unpair_state(h_bd):
    bsz = h_bd.shape[0]
    hd = RWKV_HEAD_DIM
    heads = jnp.stack([h_bd[:, :, :hd, :hd], h_bd[:, :, hd:, hd:]], axis=2)
    return heads.reshape(bsz, RWKV_HEADS, hd, hd)


def _trunk(x3, mem_k, mem_v, shift_prev, wkv0, conv_prev, c0, n0, m0, w, chunk, tm_proj, tail_rows, tail_batches,
           tm_moe):
    bsz, t_len, _ = x3.shape
    n = bsz * t_len
    x = x3.reshape(n, D_MODEL)
    p_rw = _matmul_bias(x, w["w_rw"], w["b_rw"], tm_proj, RWKV_PROJ)
    qk, qk_tail = _qk_proj(x, w["w_qk"], w["b_qk"], w["conv_w"], w["conv_b"], conv_prev, tm_proj, D_MLSTM)
    p_vo = _matmul_bias(x, w["w_vo"], w["b_vo"], tm_proj, VO_COLS)
    p_mq = _matmul_bias(x, w["w_mq"], w["b_mq"], tm_proj, D_MEM, BF16)
    p_gate = _matmul_bias(x, w["w_gate"], w["b_gate"], tm_proj, 3 * D_MODEL)

    p_rw3 = p_rw.reshape(bsz, t_len, RWKV_PROJ)
    ya, h_bd, hb, c_t, n_t, m_t = _recurrent(
        p_rw3, shift_prev, _pair_state(wkv0), qk.reshape(bsz, t_len, 2 * D_MLSTM), p_vo.reshape(bsz, t_len, VO_COLS),
        c0, n0, m0.reshape(bsz, 1, MLSTM_HEADS), w, chunk)
    x1 = _tail(x, ya.reshape(n, D_RWKV), hb.reshape(n, D_MLSTM), p_mq, p_gate, mem_k, mem_v, w, tail_rows,
               tail_batches)
    y = _moe(x1, w, tm_moe)

    new_shift = p_rw3[:, t_len - 1:, :]
    new_conv = qk_tail[:, 8 - (CONV_W - 1):, :]
    states = (new_shift, _unpair_state(h_bd), new_conv, c_t, n_t, m_t.reshape(bsz, MLSTM_HEADS))
    return y.reshape(bsz, t_len, D_MODEL), states


def kernel(x_prompt, x_sample, state_rwkv_shift, state_rwkv_wkv, state_mlstm_conv, state_mlstm_c, state_mlstm_n, state_mlstm_m, cache_mem_k, cache_mem_v, mem_prompt, w_in, b_in, rwkv_mu, rwkv_w0, rwkv_w2, rwkv_a0, rwkv_a2, rwkv_g2, rwkv_k_k, rwkv_k_a, rwkv_r_k, rwkv_ln_w, rwkv_ln_b, mlstm_conv_w, mlstm_conv_b, mlstm_b_i, mlstm_b_f, mlstm_ln_w, mlstm_ln_b, w_mem_kv, w_pa, w_pb, w_pc, w_out, ln1_w, ln1_b, w_router, router_bias, w_e1, w_e3, w_e2, w_s1, w_s3, w_s2, ln2_w, ln2_b):
    weights = (w_in, b_in, rwkv_mu, rwkv_w0, rwkv_w2, rwkv_a0, rwkv_a2, rwkv_g2, rwkv_k_k, rwkv_k_a, rwkv_r_k,
               rwkv_ln_w, rwkv_ln_b, mlstm_conv_w, mlstm_conv_b, mlstm_b_i, mlstm_b_f, mlstm_ln_w, mlstm_ln_b,
               w_mem_kv, w_pa, w_pb, w_pc, w_out, ln1_w, ln1_b, w_router, router_bias, w_e1, w_e3, w_e2,
               w_s1, w_s3, w_s2, ln2_w, ln2_b)
    w = _prep_weights(*(t[0] for t in weights))
    bp, t_p, _ = x_prompt.shape
    bs, t_s, _ = x_sample.shape
    hd = MLSTM_HEAD_DIM

    mem_flat = mem_prompt.reshape(bp * N_MEM, D_MODEL)
    zero_bias = jnp.zeros((1, D_MEM), F32)
    mk = _matmul_bias(mem_flat, w["w_mem_k"], zero_bias, bp * N_MEM, D_MEM).reshape(bp, N_MEM, D_MEM)
    mv = _matmul_bias(mem_flat, w["w_mem_v"], zero_bias, bp * N_MEM, D_MEM).reshape(bp, N_MEM, D_MEM)

    y_p, st_p = _trunk(
        x_prompt, mk, mv,
        jnp.zeros((bp, 1, RWKV_PROJ), F32), jnp.zeros((bp, RWKV_HEADS, RWKV_HEAD_DIM, RWKV_HEAD_DIM), F32),
        jnp.zeros((bp, CONV_W - 1, 2 * D_MLSTM), F32), jnp.zeros((bp, MLSTM_HEADS, hd, hd), F32),
        jnp.zeros((bp, MLSTM_HEADS, hd), F32), jnp.zeros((bp, MLSTM_HEADS), F32),
        w, chunk=min(MLSTM_CHUNK, t_p), tm_proj=1024, tail_rows=512, tail_batches=1, tm_moe=512)
    y_s, st_s = _trunk(
        x_sample, cache_mem_k[0].reshape(bs, N_MEM, D_MEM), cache_mem_v[0].reshape(bs, N_MEM, D_MEM),
        state_rwkv_shift[0], state_rwkv_wkv[0], state_mlstm_conv[0], state_mlstm_c[0], state_mlstm_n[0],
        state_mlstm_m[0],
        w, chunk=min(MLSTM_CHUNK, t_s), tm_proj=bs * t_s, tail_rows=t_s, tail_batches=8, tm_moe=bs * t_s)

    lead = lambda t: t[None]
    mem_shape = (1, bp, N_MEM, MEM_HEADS, MEM_HEAD_DIM)
    return (y_p, y_s, *(lead(t) for t in st_p), mk.reshape(mem_shape), mv.reshape(mem_shape),
            *(lead(t) for t in st_s))
```

```python
import functools
import itertools

import jax
import jax.numpy as jnp
from jax import lax
from jax.experimental import pallas as pl
from jax.experimental.pallas import tpu as pltpu

F32 = jnp.float32
BF16 = jnp.bfloat16

D_MODEL = 1024
DEPTH = 1
D_RWKV = 1024
RWKV_HEAD_DIM = 64
RWKV_HEADS = 16
W_LORA = 64
A_LORA = 64
G_LORA = 128
LORA_ALL = W_LORA + A_LORA + G_LORA
RWKV_PROJ = 3 * D_RWKV + LORA_ALL
RWKV_GN_EPS = 64e-5
DECAY_SCALE = 0.6065306597126334
D_MLSTM = 1024
MLSTM_HEADS = 4
MLSTM_HEAD_DIM = 256
CONV_W = 4
MLSTM_CHUNK = 64
N_MEM = 256
MEM_HEADS = 4
MEM_HEAD_DIM = 256
D_MEM = 1024
N_EXPERTS = 32
TOP_K = 4
EXPERT_HIDDEN = 128
ROUTED_SCALE = 2.5
DN_ALPHA = (2 * DEPTH) ** 0.25
LN_EPS = 1e-5

LANES = 128
RWKV_PAIRS = D_RWKV // LANES
VO_COLS = 2 * D_MLSTM + LANES
VMEM_LIMIT = 56 * 1024 * 1024
BATCH_GROUP = 2


def _dot(a, b):
    return jnp.dot(a, b, preferred_element_type=F32)


def _dot_nt(a, b):
    return lax.dot_general(a, b, (((1,), (1,)), ((), ())), preferred_element_type=F32)


def _dot_tn(a, b):
    return lax.dot_general(a, b, (((0,), (0,)), ((), ())), preferred_element_type=F32)


def _split(a):
    hi = a.astype(BF16)
    return hi, (a - hi.astype(F32)).astype(BF16)


def _split3(a):
    hi = a.astype(BF16)
    rest = a - hi.astype(F32)
    mid = rest.astype(BF16)
    return hi, mid, (rest - mid.astype(F32)).astype(BF16)


def _bdot(a, b):
    return _dot(a.astype(BF16), b.astype(BF16))


def _sigmoid(x):
    return 0.5 * jnp.tanh(0.5 * x) + 0.5


def _softplus(x):
    return jnp.maximum(x, 0.0) + jnp.log(1.0 + jnp.exp(-jnp.abs(x)))


def _silu(x):
    return x * _sigmoid(x)


def _layer_norm(x, w, b):
    mu = jnp.mean(x, axis=-1, keepdims=True)
    d = x - mu
    var = jnp.mean(d * d, axis=-1, keepdims=True)
    return d * lax.rsqrt(var + LN_EPS) * w + b


def _params(semantics):
    return pltpu.CompilerParams(dimension_semantics=semantics, vmem_limit_bytes=VMEM_LIMIT)


def _const_spec(shape):
    zeros = (0,) * len(shape)
    return pl.BlockSpec(shape, lambda *_: zeros, pipeline_mode=pl.Buffered(1))


def _mm_kernel(x_ref, w_ref, b_ref, o_ref):
    o_ref[...] = (_dot(x_ref[...].astype(BF16), w_ref[...]) + b_ref[...]).astype(o_ref.dtype)


def _matmul_bias(x, w, b, tm, tn, out_dtype=F32):
    n, k = x.shape
    nc = w.shape[1]
    resident = dict(pipeline_mode=pl.Buffered(1)) if nc == tn else {}
    return pl.pallas_call(
        _mm_kernel,
        grid=(nc // tn, n // tm),
        in_specs=[
            pl.BlockSpec((tm, k), lambda j, i: (i, 0)),
            pl.BlockSpec((k, tn), lambda j, i: (0, j), **resident),
            pl.BlockSpec((1, tn), lambda j, i: (0, j), **resident),
        ],
        out_specs=pl.BlockSpec((tm, tn), lambda j, i: (i, j)),
        out_shape=jax.ShapeDtypeStruct((n, nc), out_dtype),
        compiler_params=_params(("arbitrary", "arbitrary")),
        name="matmul_bias",
    )(x, w, b)


def _qk_proj_kernel(x_ref, w_ref, b_ref, cw_ref, cb_ref, prev_ref, o_ref, tail_ref, raw, *, rows, tiles_per_batch):
    col_block, row_tile = pl.program_id(0), pl.program_id(1)
    n_seg = prev_ref.shape[0]
    tm, tn = o_ref.shape
    half_scale = jnp.where(col_block >= D_MLSTM // tn, 0.5 * MLSTM_HEAD_DIM ** -0.5, 0.5)
    pad = jnp.zeros((8 - (CONV_W - 1), LANES), F32)
    row8 = lax.broadcasted_iota(jnp.int32, (8, LANES), 0)

    if n_seg == 1:
        @pl.when(row_tile % tiles_per_batch == 0)
        def _():
            raw[0:8 - (CONV_W - 1), :] = jnp.zeros((8 - (CONV_W - 1), tn), F32)
            raw[8 - (CONV_W - 1):8, :] = prev_ref[0]

    strip = min(rows, 64)
    sub = min(tm, 256)

    def project(k):
        rs = slice(k * sub, (k + 1) * sub)
        raw[8 + k * sub:8 + (k + 1) * sub, :] = _dot(x_ref[rs, :].astype(BF16), w_ref[...]) + b_ref[...]

    def conv_strips(k):
        for r0 in range(k * sub, (k + 1) * sub, strip):
            for c0 in range(0, tn, LANES):
                cols = slice(c0, c0 + LANES)
                if n_seg > 1 and r0 % rows == 0:
                    head = jnp.concatenate([pad, prev_ref[r0 // rows, :, cols]], axis=0)
                else:
                    head = raw[r0:r0 + 8, cols]
                x = raw[8 + r0:8 + r0 + strip, cols]
                blocks = [head] + [x[r:r + 8, :] for r in range(0, strip, 8)]
                conv = cb_ref[:, cols] + cw_ref[CONV_W - 1:CONV_W, cols] * x
                for j in range(CONV_W - 1):
                    lag = CONV_W - 1 - j
                    turned = [pltpu.roll(blk, lag, 0) for blk in blocks]
                    shifted = jnp.concatenate([jnp.where(row8 < lag, turned[i], turned[i + 1])
                                               for i in range(len(blocks) - 1)], axis=0)
                    conv = conv + cw_ref[j:j + 1, cols] * shifted
                o_ref[r0:r0 + strip, cols] = conv * (half_scale * jnp.tanh(0.5 * conv) + half_scale)

    project(0)
    for k in range(tm // sub):
        if k + 1 < tm // sub:
            project(k + 1)
        conv_strips(k)

    for seg in range(n_seg):
        tail_ref[seg] = raw[(seg + 1) * rows:(seg + 1) * rows + 8, :]
    if n_seg == 1:
        raw[0:8, :] = raw[tm:tm + 8, :]


def _qk_proj(x, w, b, conv_w, conv_b, conv_prev, tm, tn):
    n, k = x.shape
    batch = conv_prev.shape[0]
    t_len = n // batch
    rows = min(tm, t_len)
    n_seg = tm // rows
    tiles_per_batch = t_len // rows
    nc = w.shape[1]
    seg = lambda j, i: (i // tiles_per_batch if n_seg == 1 else i, 0, j)
    return pl.pallas_call(
        functools.partial(_qk_proj_kernel, rows=rows, tiles_per_batch=tiles_per_batch),
        grid=(nc // tn, n // tm),
        in_specs=[
            pl.BlockSpec((tm, k), lambda j, i: (i, 0)),
            pl.BlockSpec((k, tn), lambda j, i: (0, j)),
            pl.BlockSpec((1, tn), lambda j, i: (0, j)),
            pl.BlockSpec((CONV_W, tn), lambda j, i: (0, j)),
            pl.BlockSpec((1, tn), lambda j, i: (0, j)),
            pl.BlockSpec((n_seg, CONV_W - 1, tn), seg),
        ],
        out_specs=[
            pl.BlockSpec((tm, tn), lambda j, i: (i, j)),
            pl.BlockSpec((n_seg, 8, tn), seg),
        ],
        out_shape=[jax.ShapeDtypeStruct((n, nc), F32), jax.ShapeDtypeStruct((batch, 8, nc), F32)],
        scratch_shapes=[pltpu.VMEM((tm + 8, tn), F32)],
        compiler_params=_params(("arbitrary", "arbitrary")),
        name="qk_proj_conv",
    )(x, w, b, conv_w, conv_b, conv_prev)


def _rwkv_body(p_ref, prev_ref, h0_ref, mu_ref, w0_ref, a0_ref, kk_ref, ka_ref, rk_ref, lnw_ref, lnb_ref,
               wl_ref, y_ref, hout_ref, h_scr, xbuf, ar_scr, bk_scr, bkh_scr, v_scr, bonus_scr, g_scr, gamma_scr,
               *, chunk, n_chunks, n_items, companion):
    step = pl.program_id(0)
    L = chunk
    half = 2 * L
    prep_chunk = jnp.minimum(step, n_items - 1) % n_chunks
    solve_chunk = jnp.maximum(step - 1, 0) % n_chunks

    @pl.when(step == 0)
    def _():
        for ref in (ar_scr, bk_scr, bkh_scr, v_scr, bonus_scr, g_scr):
            ref[...] = jnp.zeros(ref.shape, ref.dtype)
        gamma_scr[...] = jnp.ones(gamma_scr.shape, F32)

    @pl.when(solve_chunk == 0)
    def _():
        h_scr[...] = h0_ref[...]

    @pl.when(prep_chunk == 0)
    def _():
        for bi in range(BATCH_GROUP):
            xbuf[bi, 0:1, :] = prev_ref[bi]

    ti = lax.broadcasted_iota(jnp.int32, (L, L), 0)
    tj = lax.broadcasted_iota(jnp.int32, (L, L), 1)
    ltri = (tj <= ti).astype(BF16)
    row_m = lax.broadcasted_iota(jnp.int32, (L, half), 0)
    col_m = lax.broadcasted_iota(jnp.int32, (L, half), 1)
    col_in = jnp.where(col_m >= L, col_m - L, col_m)
    strict = col_in < row_m
    incl = col_in <= row_m
    eye = (col_in == row_m).astype(F32)
    gi = lax.broadcasted_iota(jnp.int32, (LANES, LANES), 0)
    gj = lax.broadcasted_iota(jnp.int32, (LANES, LANES), 1)
    same_head = (gi >= RWKV_HEAD_DIM) == (gj >= RWKV_HEAD_DIM)
    head_ones = same_head.astype(BF16)
    first_nat = lax.broadcasted_iota(jnp.int32, (L, LANES), 1) < RWKV_HEAD_DIM
    first_sbs = col_m < L
    lane_l = lax.broadcasted_iota(jnp.int32, (L, LORA_ALL), 1)

    def diag(x, first):
        zero = jnp.zeros_like(x)
        return jnp.concatenate([jnp.where(first, x, zero), jnp.where(first, zero, x)], axis=0)

    def seg_sum(x):
        return _dot(x.astype(BF16), head_ones)

    units = [(bi, p) for bi in range(BATCH_GROUP) for p in range(RWKV_PAIRS)]
    idx = range(len(units))
    lane_of = lambda p: slice(p * LANES, (p + 1) * LANES)

    ar = [ar_scr[bi, :, lane_of(p)] for bi, p in units]
    bk = [bk_scr[bi, :, lane_of(p)] for bi, p in units]
    bkh = [bkh_scr[bi, :, lane_of(p)] for bi, p in units]
    v_b = [v_scr[bi, :, lane_of(p)] for bi, p in units]
    bonus = [bonus_scr[bi, :, lane_of(p)] for bi, p in units]
    gate = [g_scr[bi, :, lane_of(p)] for bi, p in units]
    gamma = [gamma_scr[bi, :, lane_of(p)] for bi, p in units]

    first_row = lax.broadcasted_iota(jnp.int32, (L, RWKV_PROJ), 0) == 0

    def prepare_row(bi):
        cur = p_ref[bi]
        prev = jnp.where(first_row, xbuf[bi, 0:1, :], pltpu.roll(cur, 1, 0))
        xbuf[bi, 0:1, :] = cur[L - 1:L, :]
        xr = cur + (prev - cur) * mu_ref[...]
        r_all = xr[:, 0:D_RWKV]
        k_all = xr[:, D_RWKV:2 * D_RWKV]
        v_all = xr[:, 2 * D_RWKV:3 * D_RWKV]
        slab = xr[:, 3 * D_RWKV:]
        act = jnp.where(lane_l < W_LORA, jnp.tanh(slab),
                        jnp.where(lane_l < W_LORA + A_LORA, slab, _sigmoid(slab)))
        yield
        lora = _bdot(act, wl_ref[...])
        yield
        lw = -DECAY_SCALE * _sigmoid(w0_ref[...] + lora[:, 0:D_RWKV])
        a_sig = _sigmoid(a0_ref[...] + lora[:, D_RWKV:2 * D_RWKV])
        lw_hi, lw_lo = _split(lw)
        yield
        kk0 = k_all * kk_ref[...]
        k2 = k_all * (1.0 + (a_sig - 1.0) * ka_ref[...])
        sums = jnp.concatenate([kk0 * kk0, r_all * k2 * rk_ref[...]], axis=0).astype(BF16)
        yield
        cum = _dot(ltri, lw_hi) + _dot(ltri, lw_lo)
        sums = jnp.concatenate([_dot(sums[:, lane_of(p)], head_ones) for p in range(RWKV_PAIRS)], axis=1)
        yield
        kk = kk0 * lax.rsqrt(jnp.maximum(sums[0:L], 1e-24))
        b_vec = kk * a_sig
        c_last = cum[L - 1:L, :]
        e_neg = jnp.exp(-cum)
        e_end = jnp.exp(c_last - cum)
        yield
        ar_scr[bi] = jnp.concatenate([-kk * jnp.exp(cum - lw), r_all * jnp.exp(cum)], axis=0).astype(BF16)
        bk_scr[bi] = jnp.concatenate([b_vec * e_neg, k2 * e_neg], axis=0).astype(BF16)
        yield
        bkh_scr[bi] = jnp.concatenate([b_vec * e_end, k2 * e_end], axis=0).astype(BF16)
        v_scr[bi] = v_all.astype(BF16)
        bonus_scr[bi] = sums[L:] * v_all
        g_scr[bi] = lora[:, 2 * D_RWKV:]
        gamma_scr[bi] = jnp.exp(c_last)

    segments = itertools.chain(*(prepare_row(bi) for bi in range(BATCH_GROUP)))

    def prepare_some(count):
        for _ in range(count):
            next(segments, None)
            next(companion, None)

    n_sq = L.bit_length() - 1
    per_stage = 2
    bk_d = [jnp.concatenate([diag(bk[u][0:L], first_nat), diag(bk[u][L:], first_nat)], axis=0) for u in idx]
    v_d = [diag(v_b[u], first_nat) for u in idx]
    h_bd = [h_scr[bi, p] for bi, p in units]
    sc = [_dot_nt(ar[u], bk_d[u]) for u in idx]
    prepare_some(per_stage)
    arh = [_dot_nt(ar[u], h_bd[u].astype(BF16)) for u in idx]
    m_ab = [jnp.where(strict, sc[u][0:L, 0:half], 0.0) for u in idx]
    m_ak = [jnp.where(strict, sc[u][0:L, half:], 0.0).astype(BF16) for u in idx]
    n_rbk = [jnp.concatenate([jnp.where(incl, sc[u][L:, 0:half], 0.0),
                              jnp.where(incl, sc[u][L:, half:], 0.0)], axis=1).astype(BF16) for u in idx]
    prepare_some(per_stage)
    w0 = [arh[u][0:L] + _dot(m_ak[u], v_d[u]) for u in idx]
    inv = [eye + m_ab[u] for u in idx]
    m_b = [m_ab[u].astype(BF16) for u in idx]
    pw = [_dot(m_b[u], diag(m_b[u], first_sbs)) for u in idx]
    prepare_some(per_stage)
    for i in range(n_sq - 1):
        pw_d = [diag(pw[u].astype(BF16), first_sbs) for u in idx]
        if i + 1 < n_sq - 1:
            z = [_dot(jnp.concatenate([inv[u], pw[u]], axis=0).astype(BF16), pw_d[u]) for u in idx]
            inv = [inv[u] + z[u][0:L] for u in idx]
            pw = [z[u][L:] for u in idx]
        else:
            inv = [inv[u] + _dot(inv[u].astype(BF16), pw_d[u]) for u in idx]
        prepare_some(per_stage)
    c_b = [_dot(inv[u].astype(BF16), diag(w0[u].astype(BF16), first_nat)).astype(BF16) for u in idx]
    cv = [jnp.concatenate([c_b[u], v_b[u]], axis=0) for u in idx]
    y = [arh[u][L:] + _dot(n_rbk[u], jnp.concatenate([diag(c_b[u], first_nat), v_d[u]], axis=0)) for u in idx]
    for u, (bi, p) in enumerate(units):
        h_scr[bi, p] = h_bd[u] * gamma[u] + jnp.where(same_head, _dot_tn(cv[u], bkh[u]), 0.0)
    mean = [seg_sum(y[u]) * (1.0 / RWKV_HEAD_DIM) for u in idx]
    dev = [y[u] - mean[u] for u in idx]
    var = [seg_sum(dev[u] * dev[u]) * (1.0 / RWKV_HEAD_DIM) for u in idx]
    for u, (bi, p) in enumerate(units):
        s = lane_of(p)
        yn = dev[u] * lax.rsqrt(var[u] + RWKV_GN_EPS) * lnw_ref[:, s] + lnb_ref[:, s]
        y_ref[bi, :, s] = ((yn + bonus[u]) * gate[u]).astype(y_ref.dtype)
    prepare_some(8 * BATCH_GROUP)
    for _ in companion:
        pass

    @pl.when((solve_chunk == n_chunks - 1) & (step > 0))
    def _():
        hout_ref[...] = h_scr[...]


def _mlstm_stages(qk_ref, p_ref, c0_ref, n0_ref, m0_ref, bif_ref, lnw_ref, lnb_ref,
                  h_ref, cout_ref, nout_ref, mout_ref, c_scr, n_scr, m_scr, *, chunk, first, last):
    L = chunk
    d = D_MLSTM
    nh = MLSTM_HEADS

    @pl.when(first)
    def _():
        c_scr[...] = c0_ref[...]
        n_scr[...] = n0_ref[...]
        m_scr[...] = m0_ref[...]

    yield

    ti =lax.broadcasted_iota(jnp.int32, (L, L), 0)
    tj = lax.broadcasted_iota(jnp.int32, (L, L), 1)
    causal = tj <= ti
    ltri = causal.astype(BF16)
    utri = (ti <= tj).astype(BF16)
    sel = (lax.broadcasted_iota(jnp.int32, (8, LANES), 0)
           == lax.broadcasted_iota(jnp.int32, (8, LANES), 1)).astype(BF16)
    lane_g = lax.broadcasted_iota(jnp.int32, (L, LANES), 1)
    is_f = (lane_g >= nh) & (lane_g < 2 * nh)

    rows = []
    for bi in range(BATCH_GROUP):
        gates = p_ref[bi, :, 2 * d:] + bif_ref[...]
        glog = jnp.where(is_f, -_softplus(-gates), gates)
        parts = _split3(glog)
        b_cols = sum(_dot(ltri, t) for t in parts)
        g_rows = sum(_dot_nt(sel, t) for t in parts)
        b_rows = sum(_dot(t, utri) for t in _split3(g_rows))
        rows.append(dict(q=qk_ref[bi, :, 0:d], k=qk_ref[bi, :, d:2 * d], v=p_ref[bi, :, 0:d],
                         o=_sigmoid(p_ref[bi, :, d:2 * d]),
                         glog=glog, b_cols=b_cols, g_rows=g_rows, b_rows=b_rows))
        yield

    units = [(bi, h) for bi in range(BATCH_GROUP) for h in range(nh)]
    idx = range(len(units))
    head = lambda h: slice(h * MLSTM_HEAD_DIM, (h + 1) * MLSTM_HEAD_DIM)
    q = [rows[bi]["q"][:, head(h)] for bi, h in units]
    k = [rows[bi]["k"][:, head(h)] for bi, h in units]
    v = [rows[bi]["v"][:, head(h)] for bi, h in units]
    q_b = [t.astype(BF16) for t in q]
    k_b = [t.astype(BF16) for t in k]
    c_st = [c_scr[bi, h] for bi, h in units]
    n_st = [n_scr[bi, h:h + 1, :] for bi, h in units]
    m_prev = [m_scr[bi, :, h:h + 1] for bi, h in units]
    qk_t = [_dot_nt(q_b[u], k_b[u]) for u in idx]
    yield
    qc = [_dot_nt(q_b[u], c_st[u].astype(BF16)) for u in idx]
    yield
    s, w_inter, m_t, wk, carry, m_next = [], [], [], [], [], []
    for u, (bi, h) in enumerate(units):
        t = rows[bi]
        ig_row = t["g_rows"][h:h + 1, :]
        b_row = t["b_rows"][nh + h:nh + h + 1, :]
        ig_col = t["glog"][:, h:h + 1]
        b_col = t["b_cols"][:, nh + h:nh + h + 1]
        log_w = jnp.where(causal, b_col - b_row + ig_row, -jnp.inf)
        inter = b_col + m_prev[u]
        m_now = jnp.maximum(inter, jnp.max(log_w, axis=-1, keepdims=True))
        s.append(qk_t[u] * jnp.exp(log_w - m_now))
        w_inter.append(jnp.exp(inter - m_now))
        m_t.append(m_now)
        b_last = b_col[L - 1:L, :]
        g_col = b_last - b_col + ig_col
        m_new = jnp.maximum(b_last + m_prev[u], jnp.max(g_col, axis=0, keepdims=True))
        carry.append(jnp.exp(b_last + m_prev[u] - m_new))
        wk.append(jnp.exp(g_col - m_new))
        m_next.append(m_new)
        if u % 2 == 1:
            yield
    sv = [_dot(s[u].astype(BF16), v[u].astype(BF16)) for u in idx]
    yield
    upd = [_dot_tn((wk[u] * v[u]).astype(BF16), k_b[u]) for u in idx]
    yield
    qn =[jnp.sum(q[u] * n_st[u], axis=-1, keepdims=True) for u in idx]
    s_sum = [jnp.sum(s[u], axis=-1, keepdims=True) for u in idx]
    den = [jnp.maximum(jnp.abs(w_inter[u] * qn[u] + s_sum[u]), jnp.exp(-m_t[u])) for u in idx]
    yield
    hb = [rows[bi]["o"][:, head(h)] * ((w_inter[u] * qc[u] + sv[u]) / den[u]) for u, (bi, h) in enumerate(units)]
    yield
    mu = [jnp.mean(hb[u], axis=-1, keepdims=True) for u in idx]
    dev = [hb[u] - mu[u] for u in idx]
    var = [jnp.mean(dev[u] * dev[u], axis=-1, keepdims=True) for u in idx]
    yield
    n_new = [carry[u] * n_st[u] + jnp.sum(wk[u] * k[u], axis=0, keepdims=True) for u in idx]
    for u, (bi, h) in enumerate(units):
        sl = head(h)
        h_ref[bi, :, sl] = (dev[u] * lax.rsqrt(var[u] + LN_EPS) * lnw_ref[:, sl] + lnb_ref[:, sl]).astype(h_ref.dtype)
        c_scr[bi, h] = carry[u] * c_st[u] + upd[u]
        n_scr[bi, h:h + 1, :] = n_new[u]
        m_scr[bi, :, h:h + 1] = m_next[u]
        if u % 4 == 3:
            yield

    @pl.when(last)
    def _():
        cout_ref[...] = c_scr[...]
        nout_ref[...] = n_scr[...]
        mout_ref[...] = m_scr[...]


RW_REFS = (12, 2, 9)
ML_REFS = (8, 4, 3)


def _recurrent_kernel(*refs, chunk, n_chunks, n_items):
    groups, pos = [], 0
    for n_rw, n_ml in zip(RW_REFS, ML_REFS):
        groups.append((refs[pos:pos + n_rw], refs[pos + n_rw:pos + n_rw + n_ml]))
        pos += n_rw + n_ml
    rw_refs = [r for rw_part, _ in groups for r in rw_part]
    ml_refs = [r for _, ml_part in groups for r in ml_part]
    step = pl.program_id(0)
    solve_chunk = jnp.maximum(step - 1, 0) % n_chunks
    ml = _mlstm_stages(*ml_refs, chunk=chunk, first=solve_chunk == 0,
                       last=(solve_chunk == n_chunks - 1) & (step > 0))
    next(ml)
    _rwkv_body(*rw_refs, chunk=chunk, n_chunks=n_chunks, n_items=n_items, companion=ml)


def _recurrent(p_rw, shift_prev, h0_bd, qk, p_vo, c0, n0, m0, w, chunk):
    batch, t_len, _ = p_rw.shape
    n_chunks = t_len // chunk
    bg = BATCH_GROUP
    hd = MLSTM_HEAD_DIM
    n_items = (batch // bg) * n_chunks
    prep = lambda i: jnp.minimum(i, n_items - 1)
    solve = lambda i: jnp.maximum(i - 1, 0)
    prep_seq = lambda i: (prep(i) // n_chunks, prep(i) % n_chunks, 0)
    solve_seq = lambda i: (solve(i) // n_chunks, solve(i) % n_chunks, 0)
    vec = lambda n: pl.BlockSpec((1, n), lambda i: (0, 0))
    group = lambda *dims: pl.BlockSpec((bg,) + dims, lambda i: (solve(i) // n_chunks,) + (0,) * len(dims))
    rw_state = group(RWKV_PAIRS, LANES, LANES)
    ml_states = [group(MLSTM_HEADS, hd, hd), group(MLSTM_HEADS, hd), group(1, MLSTM_HEADS)]
    wide = lambda rows, dtype: pltpu.VMEM((bg, rows, D_RWKV), dtype)
    kern = functools.partial(_recurrent_kernel, chunk=chunk, n_chunks=n_chunks, n_items=n_items)
    return pl.pallas_call(
        kern,
        grid=(n_items + 1,),
        in_specs=[
            pl.BlockSpec((bg, chunk, RWKV_PROJ), prep_seq),
            pl.BlockSpec((bg, 1, RWKV_PROJ), lambda i: (prep(i) // n_chunks, 0, 0)),
            rw_state,
            vec(RWKV_PROJ), vec(D_RWKV), vec(D_RWKV), vec(D_RWKV), vec(D_RWKV), vec(D_RWKV), vec(D_RWKV),
            vec(D_RWKV),
            pl.BlockSpec((LORA_ALL, 3 * D_RWKV), lambda i: (0, 0)),
            pl.BlockSpec((bg, chunk, 2 * D_MLSTM), solve_seq),
            pl.BlockSpec((bg, chunk, VO_COLS), solve_seq),
            *ml_states,
            vec(LANES), vec(D_MLSTM), vec(D_MLSTM),
        ],
        out_specs=[
            pl.BlockSpec((bg, chunk, D_RWKV), solve_seq), rw_state,
            pl.BlockSpec((bg, chunk, D_MLSTM), solve_seq), *ml_states,
        ],
        out_shape=[
            jax.ShapeDtypeStruct((batch, t_len, D_RWKV), BF16),
            jax.ShapeDtypeStruct((batch, RWKV_PAIRS, LANES, LANES), F32),
            jax.ShapeDtypeStruct((batch, t_len, D_MLSTM), BF16),
            jax.ShapeDtypeStruct((batch, MLSTM_HEADS, hd, hd), F32),
            jax.ShapeDtypeStruct((batch, MLSTM_HEADS, hd), F32),
            jax.ShapeDtypeStruct((batch, 1, MLSTM_HEADS), F32),
        ],
        scratch_shapes=[
            pltpu.VMEM((bg, RWKV_PAIRS, LANES, LANES), F32),
            pltpu.VMEM((bg, 8, RWKV_PROJ), F32),
            wide(2 * chunk, BF16), wide(2 * chunk, BF16), wide(2 * chunk, BF16), wide(chunk, BF16),
            wide(chunk, F32), wide(chunk, F32), wide(1, F32),
            pltpu.VMEM((bg, MLSTM_HEADS, hd, hd), F32),
            pltpu.VMEM((bg, MLSTM_HEADS, hd), F32),
            pltpu.VMEM((bg, 1, MLSTM_HEADS), F32),
        ],
        compiler_params=_params(("arbitrary",)),
        name="rwkv7_mlstm_recurrent",
    )(p_rw, shift_prev, h0_bd, w["mu"], w["w0"], w["a0"], w["k_k"], w["k_a"], w["r_k"], w["rw_ln_w"],
      w["rw_ln_b"], w["w_lora"], qk, p_vo, c0, n0, m0, w["b_if"], w["ml_ln_w"], w["ml_ln_b"])


def _tail_kernel(x_ref, ya_ref, hb_ref, mq_ref, gate_ref, mk_ref, mv_ref, wpa_ref, wpb_ref, wpc_ref, wout_ref,
                 lnw_ref, lnb_ref, o_ref, *, rows):
    n_seg = mk_ref.shape[0]
    d = D_MODEL
    head = lambda h: slice(h * MEM_HEAD_DIM, (h + 1) * MEM_HEAD_DIM)
    units = [(b, h) for b in range(n_seg) for h in range(MEM_HEADS)]
    idx = range(len(units))
    mk = [mk_ref[b].astype(BF16) for b in range(n_seg)]
    mv = [mv_ref[b].astype(BF16) for b in range(n_seg)]
    s = [_dot_nt(mq_ref[b * rows:(b + 1) * rows, head(h)], mk[b][:, head(h)]) * (MEM_HEAD_DIM ** -0.5)
         for b, h in units]
    y_a = _sigmoid(gate_ref[:, 0:d]) * _dot(ya_ref[...], wpa_ref[...])
    top = [jnp.max(s[u], axis=-1, keepdims=True) for u in idx]
    e = [jnp.exp(s[u] - top[u]) for u in idx]
    y_b = _sigmoid(gate_ref[:, d:2 * d]) * _dot(hb_ref[...], wpb_ref[...])
    den = [jnp.sum(e[u], axis=-1, keepdims=True) for u in idx]
    prob = [(e[u] / den[u]).astype(BF16) for u in idx]
    out = [_dot(prob[u], mv[b][:, head(h)]).astype(BF16) for u, (b, h) in enumerate(units)]
    attn = [jnp.concatenate(out[b * MEM_HEADS:(b + 1) * MEM_HEADS], axis=-1) for b in range(n_seg)]
    attn = attn[0] if n_seg == 1 else jnp.concatenate(attn, axis=0)
    y_c = _sigmoid(gate_ref[:, 2 * d:3 * d]) * _dot(attn, wpc_ref[...])
    mixed = y_a + y_b + y_c
    u = _bdot(mixed, wout_ref[...])
    o_ref[...] = _layer_norm(DN_ALPHA * x_ref[...] + u, lnw_ref[...], lnb_ref[...])


def _tail(x, ya, hb, p_mq, p_gate, mem_k, mem_v, w, rows, batches):
    n = x.shape[0]
    tm = rows * batches
    tiles_per_batch = (n // mem_k.shape[0]) // rows if batches == 1 else 1
    row = lambda i: (i, 0)
    mem = pl.BlockSpec((batches, N_MEM, D_MEM), lambda i: (i // tiles_per_batch, 0, 0))
    tok = lambda cols: pl.BlockSpec((tm, cols), row)
    return pl.pallas_call(
        functools.partial(_tail_kernel, rows=rows),
        grid=(n // tm,),
        in_specs=[
            tok(D_MODEL), tok(D_RWKV), tok(D_MLSTM), tok(D_MEM), tok(3 * D_MODEL), mem, mem,
            _const_spec((D_RWKV, D_MODEL)), _const_spec((D_MLSTM, D_MODEL)), _const_spec((D_MEM, D_MODEL)),
            _const_spec((D_MODEL, D_MODEL)), _const_spec((1, D_MODEL)), _const_spec((1, D_MODEL)),
        ],
        out_specs=tok(D_MODEL),
        out_shape=jax.ShapeDtypeStruct((n, D_MODEL), F32),
        compiler_params=_params(("arbitrary",)),
        name="attn_merge_ln",
    )(x, ya, hb, p_mq, p_gate, mem_k, mem_v, w["w_pa"], w["w_pb"], w["w_pc"], w["w_out"], w["ln1_w"],
      w["ln1_b"])


def _moe_kernel(x_ref, wr_ref, rb_ref, w1_ref, w3_ref, w2_ref, ws1_ref, ws3_ref, ws2_ref, lnw_ref, lnb_ref, o_ref):
    x = x_ref[...]
    xb = x.astype(BF16)
    group = 8
    width = group * EXPERT_HIDDEN
    n_slabs = N_EXPERTS // group

    def up_project(s):
        cols = slice(s * width, (s + 1) * width)
        return _dot(xb, w1_ref[:, cols]), _dot(xb, w3_ref[:, cols])

    logits = _dot(xb, wr_ref[...])
    ahead = up_project(0)
    scores = _sigmoid(logits)
    lane = lax.broadcasted_iota(jnp.int32, scores.shape, 1)
    work = jnp.where(lane < N_EXPERTS, scores + rb_ref[...], -jnp.inf)
    chosen = jnp.zeros(scores.shape, dtype=jnp.bool_)
    for _ in range(TOP_K):
        best = jnp.max(work, axis=-1, keepdims=True)
        first = jnp.min(jnp.where(work == best, lane, LANES), axis=-1, keepdims=True)
        pick = lane == first
        chosen = chosen | pick
        work = jnp.where(pick, -jnp.inf, work)
    sel = jnp.where(chosen, scores, 0.0)
    gates = sel / jnp.sum(sel, axis=-1, keepdims=True) * ROUTED_SCALE

    acc = _dot((_silu(_dot(xb, ws1_ref[...])) * _dot(xb, ws3_ref[...])).astype(BF16), ws2_ref[...])
    for s in range(n_slabs):
        cols = slice(s * width, (s + 1) * width)
        h1, h3 = ahead
        if s + 1 < n_slabs:
            ahead = up_project(s + 1)
        hidden = _silu(h1) * h3
        gated = jnp.concatenate(
            [hidden[:, e * EXPERT_HIDDEN:(e + 1) * EXPERT_HIDDEN] * gates[:, s * group + e:s * group + e + 1]
             for e in range(group)], axis=-1)
        acc = acc + _dot(gated.astype(BF16), w2_ref[cols, :])
    o_ref[...] = _layer_norm(DN_ALPHA * x + acc, lnw_ref[...], lnb_ref[...])


def _moe(x, w, tm):
    n = x.shape[0]
    row = lambda i: (i, 0)
    hid = N_EXPERTS * EXPERT_HIDDEN
    return pl.pallas_call(
        _moe_kernel,
        grid=(n // tm,),
        in_specs=[
            pl.BlockSpec((tm, D_MODEL), row),
            _const_spec((D_MODEL, LANES)), _const_spec((1, LANES)),
            _const_spec((D_MODEL, hid)), _const_spec((D_MODEL, hid)), _const_spec((hid, D_MODEL)),
            _const_spec((D_MODEL, EXPERT_HIDDEN)), _const_spec((D_MODEL, EXPERT_HIDDEN)),
            _const_spec((EXPERT_HIDDEN, D_MODEL)),
            _const_spec((1, D_MODEL)), _const_spec((1, D_MODEL)),
        ],
        out_specs=pl.BlockSpec((tm, D_MODEL), row),
        out_shape=jax.ShapeDtypeStruct((n, D_MODEL), F32),
        compiler_params=_params(("arbitrary",)),
        name="moe_ln",
    )(x, w["w_router"], w["router_bias"], w["w_e1"], w["w_e3"], w["w_e2"], w["w_s1"], w["w_s3"], w["w_s2"],
      w["ln2_w"], w["ln2_b"])


def _prep_weights(w_in, b_in, rwkv_mu, rwkv_w0, rwkv_w2, rwkv_a0, rwkv_a2, rwkv_g2, rwkv_k_k, rwkv_k_a,
                  rwkv_r_k, rwkv_ln_w, rwkv_ln_b, mlstm_conv_w, mlstm_conv_b, mlstm_b_i, mlstm_b_f,
                  mlstm_ln_w, mlstm_ln_b, w_mem_kv, w_pa, w_pb, w_pc, w_out, ln1_w, ln1_b,
                  w_router, router_bias, w_e1, w_e3, w_e2, w_s1, w_s3, w_s2, ln2_w, ln2_b):
    off_ml = RWKV_PROJ
    off_if = off_ml + 4 * D_MLSTM
    off_mq = off_if + 2 * MLSTM_HEADS
    pad_if = LANES - 2 * MLSTM_HEADS
    row = lambda t: t.reshape(1, -1)
    hid = N_EXPERTS * EXPERT_HIDDEN
    w_lora = jnp.zeros((LORA_ALL, 3 * D_RWKV), F32)
    w_lora = w_lora.at[0:W_LORA, 0:D_RWKV].set(rwkv_w2)
    w_lora = w_lora.at[W_LORA:W_LORA + A_LORA, D_RWKV:2 * D_RWKV].set(rwkv_a2)
    w_lora = w_lora.at[W_LORA + A_LORA:, 2 * D_RWKV:].set(rwkv_g2)
    return dict(
        w_rw=w_in[:, :off_ml].astype(BF16), b_rw=row(b_in[:off_ml]),
        w_qk=w_in[:, off_ml:off_ml + 2 * D_MLSTM].astype(BF16), b_qk=row(b_in[off_ml:off_ml + 2 * D_MLSTM]),
        w_vo=jnp.pad(w_in[:, off_ml + 2 * D_MLSTM:off_mq], ((0, 0), (0, pad_if))).astype(BF16),
        b_vo=row(jnp.pad(b_in[off_ml + 2 * D_MLSTM:off_mq], (0, pad_if))),
        w_mq=w_in[:, off_mq:off_mq + D_MEM].astype(BF16), b_mq=row(b_in[off_mq:off_mq + D_MEM]),
        w_gate=w_in[:, off_mq + D_MEM:].astype(BF16), b_gate=row(b_in[off_mq + D_MEM:]),
        w_mem_k=w_mem_kv[:, :D_MEM].astype(BF16), w_mem_v=w_mem_kv[:, D_MEM:].astype(BF16),
        mu=row(rwkv_mu), w0=row(rwkv_w0), a0=row(rwkv_a0), k_k=row(rwkv_k_k), k_a=row(rwkv_k_a),
        r_k=row(rwkv_r_k), rw_ln_w=row(rwkv_ln_w), rw_ln_b=row(rwkv_ln_b), w_lora=w_lora.astype(BF16),
        conv_w=mlstm_conv_w, conv_b=row(mlstm_conv_b),
        b_if=row(jnp.pad(jnp.concatenate([mlstm_b_i, mlstm_b_f]), (0, pad_if))),
        ml_ln_w=row(mlstm_ln_w), ml_ln_b=row(mlstm_ln_b),
        w_pa=w_pa.astype(BF16), w_pb=w_pb.astype(BF16), w_pc=w_pc.astype(BF16), w_out=w_out.astype(BF16),
        ln1_w=row(ln1_w), ln1_b=row(ln1_b),
        w_router=jnp.pad(w_router, ((0, 0), (0, LANES - N_EXPERTS))).astype(BF16),
        router_bias=row(jnp.pad(router_bias, (0, LANES - N_EXPERTS))),
        w_e1=jnp.transpose(w_e1.astype(BF16), (1, 0, 2)).reshape(D_MODEL, hid),
        w_e3=jnp.transpose(w_e3.astype(BF16), (1, 0, 2)).reshape(D_MODEL, hid),
        w_e2=w_e2.reshape(hid, D_MODEL).astype(BF16),
        w_s1=w_s1.astype(BF16), w_s3=w_s3.astype(BF16), w_s2=w_s2.astype(BF16),
        ln2_w=row(ln2_w), ln2_b=row(ln2_b),
    )


def _pair_state(wkv):
    bsz = wkv.shape[0]
    s = wkv.reshape(bsz, RWKV_PAIRS, 2, RWKV_HEAD_DIM, RWKV_HEAD_DIM)
    z = jnp.zeros_like(s[:, :, 0])
    top = jnp.concatenate([s[:, :, 0], z], axis=-1)
    bot = jnp.concatenate([z, s[:, :, 1]], axis=-1)
    return jnp.concatenate([top, bot], axis=-2)


def _unpair_state(h_bd):
    bsz = h_bd.shape[0]
    hd = RWKV_HEAD_DIM
    heads = jnp.stack([h_bd[:, :, :hd, :hd], h_bd[:, :, hd:, hd:]], axis=2)
    return heads.reshape(bsz, RWKV_HEADS, hd, hd)


def _trunk(x3, mem_k, mem_v, shift_prev, wkv0, conv_prev, c0, n0, m0, w, chunk, tm_proj, tail_rows, tail_batches,
           tm_moe):
    bsz, t_len, _ = x3.shape
    n = bsz * t_len
    x = x3.reshape(n, D_MODEL)
    p_rw = _matmul_bias(x, w["w_rw"], w["b_rw"], tm_proj, RWKV_PROJ)
    qk, qk_tail = _qk_proj(x, w["w_qk"], w["b_qk"], w["conv_w"], w["conv_b"], conv_prev, tm_proj, D_MLSTM)
    p_vo = _matmul_bias(x, w["w_vo"], w["b_vo"], tm_proj, VO_COLS)
    p_mq = _matmul_bias(x, w["w_mq"], w["b_mq"], tm_proj, D_MEM, BF16)
    p_gate = _matmul_bias(x, w["w_gate"], w["b_gate"], tm_proj, 3 * D_MODEL)

    p_rw3 = p_rw.reshape(bsz, t_len, RWKV_PROJ)
    ya, h_bd, hb, c_t, n_t, m_t = _recurrent(
        p_rw3, shift_prev, _pair_state(wkv0), qk.reshape(bsz, t_len, 2 * D_MLSTM), p_vo.reshape(bsz, t_len, VO_COLS),
        c0, n0, m0.reshape(bsz, 1, MLSTM_HEADS), w, chunk)
    x1 = _tail(x, ya.reshape(n, D_RWKV), hb.reshape(n, D_MLSTM), p_mq, p_gate, mem_k, mem_v, w, tail_rows,
               tail_batches)
    y = _moe(x1, w, tm_moe)

    new_shift = p_rw3[:, t_len - 1:, :]
    new_conv = qk_tail[:, 8 - (CONV_W - 1):, :]
    states = (new_shift, _unpair_state(h_bd), new_conv, c_t, n_t, m_t.reshape(bsz, MLSTM_HEADS))
    return y.reshape(bsz, t_len, D_MODEL), states


def kernel(x_prompt, x_sample, state_rwkv_shift, state_rwkv_wkv, state_mlstm_conv, state_mlstm_c, state_mlstm_n, state_mlstm_m, cache_mem_k, cache_mem_v, mem_prompt, w_in, b_in, rwkv_mu, rwkv_w0, rwkv_w2, rwkv_a0, rwkv_a2, rwkv_g2, rwkv_k_k, rwkv_k_a, rwkv_r_k, rwkv_ln_w, rwkv_ln_b, mlstm_conv_w, mlstm_conv_b, mlstm_b_i, mlstm_b_f, mlstm_ln_w, mlstm_ln_b, w_mem_kv, w_pa, w_pb, w_pc, w_out, ln1_w, ln1_b, w_router, router_bias, w_e1, w_e3, w_e2, w_s1, w_s3, w_s2, ln2_w, ln2_b):
    weights = (w_in, b_in, rwkv_mu, rwkv_w0, rwkv_w2, rwkv_a0, rwkv_a2, rwkv_g2, rwkv_k_k, rwkv_k_a, rwkv_r_k,
               rwkv_ln_w, rwkv_ln_b, mlstm_conv_w, mlstm_conv_b, mlstm_b_i, mlstm_b_f, mlstm_ln_w, mlstm_ln_b,
               w_mem_kv, w_pa, w_pb, w_pc, w_out, ln1_w, ln1_b, w_router, router_bias, w_e1, w_e3, w_e2,
               w_s1, w_s3, w_s2, ln2_w, ln2_b)
    w = _prep_weights(*(t[0] for t in weights))
    bp, t_p, _ = x_prompt.shape
    bs, t_s, _ = x_sample.shape
    hd = MLSTM_HEAD_DIM

    mem_flat = mem_prompt.reshape(bp * N_MEM, D_MODEL)
    zero_bias = jnp.zeros((1, D_MEM), F32)
    mk = _matmul_bias(mem_flat, w["w_mem_k"], zero_bias, bp * N_MEM, D_MEM).reshape(bp, N_MEM, D_MEM)
    mv = _matmul_bias(mem_flat, w["w_mem_v"], zero_bias, bp * N_MEM, D_MEM).reshape(bp, N_MEM, D_MEM)

    y_p, st_p = _trunk(
        x_prompt, mk, mv,
        jnp.zeros((bp, 1, RWKV_PROJ), F32), jnp.zeros((bp, RWKV_HEADS, RWKV_HEAD_DIM, RWKV_HEAD_DIM), F32),
        jnp.zeros((bp, CONV_W - 1, 2 * D_MLSTM), F32), jnp.zeros((bp, MLSTM_HEADS, hd, hd), F32),
        jnp.zeros((bp, MLSTM_HEADS, hd), F32), jnp.zeros((bp, MLSTM_HEADS), F32),
        w, chunk=min(MLSTM_CHUNK, t_p), tm_proj=1024, tail_rows=512, tail_batches=1, tm_moe=512)
    y_s, st_s = _trunk(
        x_sample, cache_mem_k[0].reshape(bs, N_MEM, D_MEM), cache_mem_v[0].reshape(bs, N_MEM, D_MEM),
        state_rwkv_shift[0], state_rwkv_wkv[0], state_mlstm_conv[0], state_mlstm_c[0], state_mlstm_n[0],
        state_mlstm_m[0],
        w, chunk=min(MLSTM_CHUNK, t_s), tm_proj=bs * t_s, tail_rows=t_s, tail_batches=4, tm_moe=bs * t_s)

    lead = lambda t: t[None]
    mem_shape = (1, bp, N_MEM, MEM_HEADS, MEM_HEAD_DIM)
    return (y_p, y_s, *(lead(t) for t in st_p), mk.reshape(mem_shape), mv.reshape(mem_shape),
            *(lead(t) for t in st_s))
```

```python
import functools
import itertools

import jax
import jax.numpy as jnp
from jax import lax
from jax.experimental import pallas as pl
from jax.experimental.pallas import tpu as pltpu

F32 = jnp.float32
BF16 = jnp.bfloat16

D_MODEL = 1024
DEPTH = 1
D_RWKV = 1024
RWKV_HEAD_DIM = 64
RWKV_HEADS = 16
W_LORA = 64
A_LORA = 64
G_LORA = 128
LORA_ALL = W_LORA + A_LORA + G_LORA
RWKV_PROJ = 3 * D_RWKV + LORA_ALL
RWKV_GN_EPS = 64e-5
DECAY_SCALE = 0.6065306597126334
D_MLSTM = 1024
MLSTM_HEADS = 4
MLSTM_HEAD_DIM = 256
CONV_W = 4
MLSTM_CHUNK = 64
N_MEM = 256
MEM_HEADS = 4
MEM_HEAD_DIM = 256
D_MEM = 1024
N_EXPERTS = 32
TOP_K = 4
EXPERT_HIDDEN = 128
ROUTED_SCALE = 2.5
DN_ALPHA = (2 * DEPTH) ** 0.25
LN_EPS = 1e-5

LANES = 128
RWKV_PAIRS = D_RWKV // LANES
VO_COLS = 2 * D_MLSTM + LANES
VMEM_LIMIT = 56 * 1024 * 1024
BATCH_GROUP = 2


def _dot(a, b):
    return jnp.dot(a, b, preferred_element_type=F32)


def _dot_nt(a, b):
    return lax.dot_general(a, b, (((1,), (1,)), ((), ())), preferred_element_type=F32)


def _dot_tn(a, b):
    return lax.dot_general(a, b, (((0,), (0,)), ((), ())), preferred_element_type=F32)


def _split(a):
    hi = a.astype(BF16)
    return hi, (a - hi.astype(F32)).astype(BF16)


def _split3(a):
    hi = a.astype(BF16)
    rest = a - hi.astype(F32)
    mid = rest.astype(BF16)
    return hi, mid, (rest - mid.astype(F32)).astype(BF16)


def _bdot(a, b):
    return _dot(a.astype(BF16), b.astype(BF16))


def _sigmoid(x):
    return 0.5 * jnp.tanh(0.5 * x) + 0.5


def _softplus(x):
    return jnp.maximum(x, 0.0) + jnp.log(1.0 + jnp.exp(-jnp.abs(x)))


def _silu(x):
    return x * _sigmoid(x)


def _layer_norm(x, w, b):
    mu = jnp.mean(x, axis=-1, keepdims=True)
    d = x - mu
    var = jnp.mean(d * d, axis=-1, keepdims=True)
    return d * lax.rsqrt(var + LN_EPS) * w + b


def _params(semantics):
    return pltpu.CompilerParams(dimension_semantics=semantics, vmem_limit_bytes=VMEM_LIMIT)


def _const_spec(shape):
    zeros = (0,) * len(shape)
    return pl.BlockSpec(shape, lambda *_: zeros, pipeline_mode=pl.Buffered(1))


def _mm_kernel(x_ref, w_ref, b_ref, o_ref):
    o_ref[...] = (_dot(x_ref[...].astype(BF16), w_ref[...]) + b_ref[...]).astype(o_ref.dtype)


def _matmul_bias(x, w, b, tm, tn, out_dtype=F32):
    n, k = x.shape
    nc = w.shape[1]
    resident = dict(pipeline_mode=pl.Buffered(1)) if nc == tn else {}
    return pl.pallas_call(
        _mm_kernel,
        grid=(nc // tn, n // tm),
        in_specs=[
            pl.BlockSpec((tm, k), lambda j, i: (i, 0)),
            pl.BlockSpec((k, tn), lambda j, i: (0, j), **resident),
            pl.BlockSpec((1, tn), lambda j, i: (0, j), **resident),
        ],
        out_specs=pl.BlockSpec((tm, tn), lambda j, i: (i, j)),
        out_shape=jax.ShapeDtypeStruct((n, nc), out_dtype),
        compiler_params=_params(("arbitrary", "arbitrary")),
        name="matmul_bias",
    )(x, w, b)


def _mlstm_proj_kernel(x_ref, w_ref, b_ref, cw_ref, cb_ref, prev_ref, w2_ref, b2_ref, o_ref, tail_ref, o2_ref, raw,
                       *, rows, tiles_per_batch):
    row_tile = pl.program_id(0)
    n_seg = prev_ref.shape[0]
    tm, tn = o_ref.shape
    n2 = o2_ref.shape[1]
    pad = jnp.zeros((8 - (CONV_W - 1), LANES), F32)
    row8 = lax.broadcasted_iota(jnp.int32, (8, LANES), 0)

    if n_seg == 1:
        @pl.when(row_tile % tiles_per_batch == 0)
        def _():
            raw[0:8 - (CONV_W - 1), :] = jnp.zeros((8 - (CONV_W - 1), tn), F32)
            raw[8 - (CONV_W - 1):8, :] = prev_ref[0]

    xb = x_ref[...].astype(BF16)
    raw[8:8 + tm, :] = _dot(xb, w_ref[...]) + b_ref[...]
    strip = min(rows, 64)

    def conv_strips(r_lo, r_hi):
        for r0 in range(r_lo, r_hi, strip):
            for c0 in range(0, tn, LANES):
                cols = slice(c0, c0 + LANES)
                half_scale = 0.5 * MLSTM_HEAD_DIM ** -0.5 if c0 >= D_MLSTM else 0.5
                if n_seg > 1 and r0 % rows == 0:
                    head = jnp.concatenate([pad, prev_ref[r0 // rows, :, cols]], axis=0)
                else:
                    head = raw[r0:r0 + 8, cols]
                x = raw[8 + r0:8 + r0 + strip, cols]
                blocks = [head] + [x[r:r + 8, :] for r in range(0, strip, 8)]
                conv = cb_ref[:, cols] + cw_ref[CONV_W - 1:CONV_W, cols] * x
                for j in range(CONV_W - 1):
                    lag = CONV_W - 1 - j
                    turned = [pltpu.roll(blk, lag, 0) for blk in blocks]
                    shifted = jnp.concatenate([jnp.where(row8 < lag, turned[i], turned[i + 1])
                                               for i in range(len(blocks) - 1)], axis=0)
                    conv = conv + cw_ref[j:j + 1, cols] * shifted
                o_ref[r0:r0 + strip, cols] = conv * (half_scale * jnp.tanh(0.5 * conv) + half_scale)

    n_parts = 4
    col_step = -(-n2 // (n_parts * LANES)) * LANES
    row_step = tm // n_parts
    for part in range(n_parts):
        cs = slice(part * col_step, min((part + 1) * col_step, n2))
        o2_ref[:, cs] = _dot(xb, w2_ref[:, cs]) + b2_ref[:, cs]
        conv_strips(part * row_step, (part + 1) * row_step)

    for seg in range(n_seg):
        tail_ref[seg] = raw[(seg + 1) * rows:(seg + 1) * rows + 8, :]
    if n_seg == 1:
        raw[0:8, :] = raw[tm:tm + 8, :]


def _mlstm_proj(x, w, b, conv_w, conv_b, conv_prev, w2, b2, tm):
    n, k = x.shape
    batch = conv_prev.shape[0]
    t_len = n // batch
    rows = min(tm, t_len)
    n_seg = tm // rows
    tiles_per_batch = t_len // rows
    nc, n2 = w.shape[1], w2.shape[1]
    seg = lambda i: (i // tiles_per_batch if n_seg == 1 else i, 0, 0)
    return pl.pallas_call(
        functools.partial(_mlstm_proj_kernel, rows=rows, tiles_per_batch=tiles_per_batch),
        grid=(n // tm,),
        in_specs=[
            pl.BlockSpec((tm, k), lambda i: (i, 0)),
            _const_spec((k, nc)), _const_spec((1, nc)), _const_spec((CONV_W, nc)), _const_spec((1, nc)),
            pl.BlockSpec((n_seg, CONV_W - 1, nc), seg),
            _const_spec((k, n2)), _const_spec((1, n2)),
        ],
        out_specs=[
            pl.BlockSpec((tm, nc), lambda i: (i, 0)),
            pl.BlockSpec((n_seg, 8, nc), seg),
            pl.BlockSpec((tm, n2), lambda i: (i, 0)),
        ],
        out_shape=[jax.ShapeDtypeStruct((n, nc), F32), jax.ShapeDtypeStruct((batch, 8, nc), F32),
                   jax.ShapeDtypeStruct((n, n2), F32)],
        scratch_shapes=[pltpu.VMEM((tm + 8, nc), F32)],
        compiler_params=_params(("arbitrary",)),
        name="mlstm_proj_conv",
    )(x, w, b, conv_w, conv_b, conv_prev, w2, b2)


def _rwkv_body(p_ref, prev_ref, h0_ref, mu_ref, w0_ref, a0_ref, kk_ref, ka_ref, rk_ref, lnw_ref, lnb_ref,
               wl_ref, y_ref, hout_ref, h_scr, xbuf, ar_scr, bk_scr, bkh_scr, v_scr, bonus_scr, g_scr, gamma_scr,
               *, chunk, n_chunks, n_items, companion):
    step = pl.program_id(0)
    L = chunk
    half = 2 * L
    prep_chunk = jnp.minimum(step, n_items - 1) % n_chunks
    solve_chunk = jnp.maximum(step - 1, 0) % n_chunks

    @pl.when(step == 0)
    def _():
        for ref in (ar_scr, bk_scr, bkh_scr, v_scr, bonus_scr, g_scr):
            ref[...] = jnp.zeros(ref.shape, ref.dtype)
        gamma_scr[...] = jnp.ones(gamma_scr.shape, F32)

    @pl.when(solve_chunk == 0)
    def _():
        h_scr[...] = h0_ref[...]

    @pl.when(prep_chunk == 0)
    def _():
        for bi in range(BATCH_GROUP):
            xbuf[bi, 0:1, :] = prev_ref[bi]

    ti = lax.broadcasted_iota(jnp.int32, (L, L), 0)
    tj = lax.broadcasted_iota(jnp.int32, (L, L), 1)
    ltri = (tj <= ti).astype(BF16)
    row_m = lax.broadcasted_iota(jnp.int32, (L, half), 0)
    col_m = lax.broadcasted_iota(jnp.int32, (L, half), 1)
    col_in = jnp.where(col_m >= L, col_m - L, col_m)
    strict = col_in < row_m
    incl = col_in <= row_m
    eye = (col_in == row_m).astype(F32)
    gi = lax.broadcasted_iota(jnp.int32, (LANES, LANES), 0)
    gj = lax.broadcasted_iota(jnp.int32, (LANES, LANES), 1)
    same_head = (gi >= RWKV_HEAD_DIM) == (gj >= RWKV_HEAD_DIM)
    head_ones = same_head.astype(BF16)
    first_nat = lax.broadcasted_iota(jnp.int32, (L, LANES), 1) < RWKV_HEAD_DIM
    first_sbs = col_m < L
    lane_l = lax.broadcasted_iota(jnp.int32, (L, LORA_ALL), 1)

    def diag(x, first):
        zero = jnp.zeros_like(x)
        return jnp.concatenate([jnp.where(first, x, zero), jnp.where(first, zero, x)], axis=0)

    def seg_sum(x):
        return _dot(x.astype(BF16), head_ones)

    units = [(bi, p) for bi in range(BATCH_GROUP) for p in range(RWKV_PAIRS)]
    idx = range(len(units))
    lane_of = lambda p: slice(p * LANES, (p + 1) * LANES)

    ar = [ar_scr[bi, :, lane_of(p)] for bi, p in units]
    bk = [bk_scr[bi, :, lane_of(p)] for bi, p in units]
    bkh = [bkh_scr[bi, :, lane_of(p)] for bi, p in units]
    v_b = [v_scr[bi, :, lane_of(p)] for bi, p in units]
    bonus = [bonus_scr[bi, :, lane_of(p)] for bi, p in units]
    gate = [g_scr[bi, :, lane_of(p)] for bi, p in units]
    gamma = [gamma_scr[bi, :, lane_of(p)] for bi, p in units]

    first_row = lax.broadcasted_iota(jnp.int32, (L, RWKV_PROJ), 0) == 0

    def prepare_row(bi):
        cur = p_ref[bi]
        prev = jnp.where(first_row, xbuf[bi, 0:1, :], pltpu.roll(cur, 1, 0))
        xbuf[bi, 0:1, :] = cur[L - 1:L, :]
        xr = cur + (prev - cur) * mu_ref[...]
        r_all = xr[:, 0:D_RWKV]
        k_all = xr[:, D_RWKV:2 * D_RWKV]
        v_all = xr[:, 2 * D_RWKV:3 * D_RWKV]
        slab = xr[:, 3 * D_RWKV:]
        act = jnp.where(lane_l < W_LORA, jnp.tanh(slab),
                        jnp.where(lane_l < W_LORA + A_LORA, slab, _sigmoid(slab)))
        yield
        lora = _bdot(act, wl_ref[...])
        yield
        lw = -DECAY_SCALE * _sigmoid(w0_ref[...] + lora[:, 0:D_RWKV])
        a_sig = _sigmoid(a0_ref[...] + lora[:, D_RWKV:2 * D_RWKV])
        lw_hi, lw_lo = _split(lw)
        yield
        kk0 = k_all * kk_ref[...]
        k2 = k_all * (1.0 + (a_sig - 1.0) * ka_ref[...])
        sums = jnp.concatenate([kk0 * kk0, r_all * k2 * rk_ref[...]], axis=0).astype(BF16)
        yield
        cum = _dot(ltri, lw_hi) + _dot(ltri, lw_lo)
        sums = jnp.concatenate([_dot(sums[:, lane_of(p)], head_ones) for p in range(RWKV_PAIRS)], axis=1)
        yield
        kk = kk0 * lax.rsqrt(jnp.maximum(sums[0:L], 1e-24))
        b_vec = kk * a_sig
        c_last = cum[L - 1:L, :]
        e_neg = jnp.exp(-cum)
        e_end = jnp.exp(c_last - cum)
        yield
        ar_scr[bi] = jnp.concatenate([-kk * jnp.exp(cum - lw), r_all * jnp.exp(cum)], axis=0).astype(BF16)
        bk_scr[bi] = jnp.concatenate([b_vec * e_neg, k2 * e_neg], axis=0).astype(BF16)
        yield
        bkh_scr[bi] = jnp.concatenate([b_vec * e_end, k2 * e_end], axis=0).astype(BF16)
        v_scr[bi] = v_all.astype(BF16)
        bonus_scr[bi] = sums[L:] * v_all
        g_scr[bi] = lora[:, 2 * D_RWKV:]
        gamma_scr[bi] = jnp.exp(c_last)

    segments = itertools.chain(*(prepare_row(bi) for bi in range(BATCH_GROUP)))

    def prepare_some(count):
        for _ in range(count):
            next(segments, None)
            next(companion, None)

    n_sq = L.bit_length() - 1
    per_stage = 2
    bk_d = [jnp.concatenate([diag(bk[u][0:L], first_nat), diag(bk[u][L:], first_nat)], axis=0) for u in idx]
    v_d = [diag(v_b[u], first_nat) for u in idx]
    h_bd = [h_scr[bi, p] for bi, p in units]
    sc = [_dot_nt(ar[u], bk_d[u]) for u in idx]
    prepare_some(per_stage)
    arh = [_dot_nt(ar[u], h_bd[u].astype(BF16)) for u in idx]
    m_ab = [jnp.where(strict, sc[u][0:L, 0:half], 0.0) for u in idx]
    m_ak = [jnp.where(strict, sc[u][0:L, half:], 0.0).astype(BF16) for u in idx]
    n_rbk = [jnp.concatenate([jnp.where(incl, sc[u][L:, 0:half], 0.0),
                              jnp.where(incl, sc[u][L:, half:], 0.0)], axis=1).astype(BF16) for u in idx]
    prepare_some(per_stage)
    w0 = [arh[u][0:L] + _dot(m_ak[u], v_d[u]) for u in idx]
    inv = [eye + m_ab[u] for u in idx]
    m_b = [m_ab[u].astype(BF16) for u in idx]
    pw = [_dot(m_b[u], diag(m_b[u], first_sbs)) for u in idx]
    prepare_some(per_stage)
    for i in range(n_sq - 1):
        pw_d = [diag(pw[u].astype(BF16), first_sbs) for u in idx]
        if i + 1 < n_sq - 1:
            z = [_dot(jnp.concatenate([inv[u], pw[u]], axis=0).astype(BF16), pw_d[u]) for u in idx]
            inv = [inv[u] + z[u][0:L] for u in idx]
            pw = [z[u][L:] for u in idx]
        else:
            inv = [inv[u] + _dot(inv[u].astype(BF16), pw_d[u]) for u in idx]
        prepare_some(per_stage)
    c_b = [_dot(inv[u].astype(BF16), diag(w0[u].astype(BF16), first_nat)).astype(BF16) for u in idx]
    cv = [jnp.concatenate([c_b[u], v_b[u]], axis=0) for u in idx]
    y = [arh[u][L:] + _dot(n_rbk[u], jnp.concatenate([diag(c_b[u], first_nat), v_d[u]], axis=0)) for u in idx]
    for u, (bi, p) in enumerate(units):
        h_scr[bi, p] = h_bd[u] * gamma[u] + jnp.where(same_head, _dot_tn(cv[u], bkh[u]), 0.0)
    mean = [seg_sum(y[u]) * (1.0 / RWKV_HEAD_DIM) for u in idx]
    dev = [y[u] - mean[u] for u in idx]
    var = [seg_sum(dev[u] * dev[u]) * (1.0 / RWKV_HEAD_DIM) for u in idx]
    for u, (bi, p) in enumerate(units):
        s = lane_of(p)
        yn = dev[u] * lax.rsqrt(var[u] + RWKV_GN_EPS) * lnw_ref[:, s] + lnb_ref[:, s]
        y_ref[bi, :, s] = ((yn + bonus[u]) * gate[u]).astype(y_ref.dtype)
    prepare_some(8 * BATCH_GROUP)
    for _ in companion:
        pass

    @pl.when((solve_chunk == n_chunks - 1) & (step > 0))
    def _():
        hout_ref[...] = h_scr[...]


def _mlstm_stages(qk_ref, p_ref, c0_ref, n0_ref, m0_ref, bif_ref, lnw_ref, lnb_ref,
                  h_ref, cout_ref, nout_ref, mout_ref, c_scr, n_scr, m_scr, *, chunk, first, last):
    L = chunk
    d = D_MLSTM
    nh = MLSTM_HEADS

    @pl.when(first)
    def _():
        c_scr[...] = c0_ref[...]
        n_scr[...] = n0_ref[...]
        m_scr[...] = m0_ref[...]

    yield

    ti =lax.broadcasted_iota(jnp.int32, (L, L), 0)
    tj = lax.broadcasted_iota(jnp.int32, (L, L), 1)
    causal = tj <= ti
    ltri = causal.astype(BF16)
    utri = (ti <= tj).astype(BF16)
    sel = (lax.broadcasted_iota(jnp.int32, (8, LANES), 0)
           == lax.broadcasted_iota(jnp.int32, (8, LANES), 1)).astype(BF16)
    lane_g = lax.broadcasted_iota(jnp.int32, (L, LANES), 1)
    is_f = (lane_g >= nh) & (lane_g < 2 * nh)

    rows = []
    for bi in range(BATCH_GROUP):
        gates = p_ref[bi, :, 2 * d:] + bif_ref[...]
        glog = jnp.where(is_f, -_softplus(-gates), gates)
        parts = _split3(glog)
        b_cols = sum(_dot(ltri, t) for t in parts)
        g_rows = sum(_dot_nt(sel, t) for t in parts)
        b_rows = sum(_dot(t, utri) for t in _split3(g_rows))
        rows.append(dict(q=qk_ref[bi, :, 0:d], k=qk_ref[bi, :, d:2 * d], v=p_ref[bi, :, 0:d],
                         o=_sigmoid(p_ref[bi, :, d:2 * d]),
                         glog=glog, b_cols=b_cols, g_rows=g_rows, b_rows=b_rows))
        yield

    units = [(bi, h) for bi in range(BATCH_GROUP) for h in range(nh)]
    idx = range(len(units))
    head = lambda h: slice(h * MLSTM_HEAD_DIM, (h + 1) * MLSTM_HEAD_DIM)
    q = [rows[bi]["q"][:, head(h)] for bi, h in units]
    k = [rows[bi]["k"][:, head(h)] for bi, h in units]
    v = [rows[bi]["v"][:, head(h)] for bi, h in units]
    q_b = [t.astype(BF16) for t in q]
    k_b = [t.astype(BF16) for t in k]
    c_st = [c_scr[bi, h] for bi, h in units]
    n_st = [n_scr[bi, h:h + 1, :] for bi, h in units]
    m_prev = [m_scr[bi, :, h:h + 1] for bi, h in units]
    qk_t = [_dot_nt(q_b[u], k_b[u]) for u in idx]
    yield
    qc = [_dot_nt(q_b[u], c_st[u].astype(BF16)) for u in idx]
    yield
    s, w_inter, m_t, wk, carry, m_next = [], [], [], [], [], []
    for u, (bi, h) in enumerate(units):
        t = rows[bi]
        ig_row = t["g_rows"][h:h + 1, :]
        b_row = t["b_rows"][nh + h:nh + h + 1, :]
        ig_col = t["glog"][:, h:h + 1]
        b_col = t["b_cols"][:, nh + h:nh + h + 1]
        log_w = jnp.where(causal, b_col - b_row + ig_row, -jnp.inf)
        inter = b_col + m_prev[u]
        m_now = jnp.maximum(inter, jnp.max(log_w, axis=-1, keepdims=True))
        s.append(qk_t[u] * jnp.exp(log_w - m_now))
        w_inter.append(jnp.exp(inter - m_now))
        m_t.append(m_now)
        b_last = b_col[L - 1:L, :]
        g_col = b_last - b_col + ig_col
        m_new = jnp.maximum(b_last + m_prev[u], jnp.max(g_col, axis=0, keepdims=True))
        carry.append(jnp.exp(b_last + m_prev[u] - m_new))
        wk.append(jnp.exp(g_col - m_new))
        m_next.append(m_new)
        if u % 2 == 1:
            yield
    sv = [_dot(s[u].astype(BF16), v[u].astype(BF16)) for u in idx]
    yield
    upd = [_dot_tn((wk[u] * v[u]).astype(BF16), k_b[u]) for u in idx]
    yield
    qn =[jnp.sum(q[u] * n_st[u], axis=-1, keepdims=True) for u in idx]
    s_sum = [jnp.sum(s[u], axis=-1, keepdims=True) for u in idx]
    den = [jnp.maximum(jnp.abs(w_inter[u] * qn[u] + s_sum[u]), jnp.exp(-m_t[u])) for u in idx]
    yield
    hb = [rows[bi]["o"][:, head(h)] * ((w_inter[u] * qc[u] + sv[u]) / den[u]) for u, (bi, h) in enumerate(units)]
    yield
    mu = [jnp.mean(hb[u], axis=-1, keepdims=True) for u in idx]
    dev = [hb[u] - mu[u] for u in idx]
    var = [jnp.mean(dev[u] * dev[u], axis=-1, keepdims=True) for u in idx]
    yield
    n_new = [carry[u] * n_st[u] + jnp.sum(wk[u] * k[u], axis=0, keepdims=True) for u in idx]
    for u, (bi, h) in enumerate(units):
        sl = head(h)
        h_ref[bi, :, sl] = (dev[u] * lax.rsqrt(var[u] + LN_EPS) * lnw_ref[:, sl] + lnb_ref[:, sl]).astype(h_ref.dtype)
        c_scr[bi, h] = carry[u] * c_st[u] + upd[u]
        n_scr[bi, h:h + 1, :] = n_new[u]
        m_scr[bi, :, h:h + 1] = m_next[u]
        if u % 4 == 3:
            yield

    @pl.when(last)
    def _():
        cout_ref[...] = c_scr[...]
        nout_ref[...] = n_scr[...]
        mout_ref[...] = m_scr[...]


RW_REFS = (12, 2, 9)
ML_REFS = (8, 4, 3)


def _recurrent_kernel(*refs, chunk, n_chunks, n_items):
    groups, pos = [], 0
    for n_rw, n_ml in zip(RW_REFS, ML_REFS):
        groups.append((refs[pos:pos + n_rw], refs[pos + n_rw:pos + n_rw + n_ml]))
        pos += n_rw + n_ml
    rw_refs = [r for rw_part, _ in groups for r in rw_part]
    ml_refs = [r for _, ml_part in groups for r in ml_part]
    step = pl.program_id(0)
    solve_chunk = jnp.maximum(step - 1, 0) % n_chunks
    ml = _mlstm_stages(*ml_refs, chunk=chunk, first=solve_chunk == 0,
                       last=(solve_chunk == n_chunks - 1) & (step > 0))
    next(ml)
    _rwkv_body(*rw_refs, chunk=chunk, n_chunks=n_chunks, n_items=n_items, companion=ml)


def _recurrent(p_rw, shift_prev, h0_bd, qk, p_vo, c0, n0, m0, w, chunk):
    batch, t_len, _ = p_rw.shape
    n_chunks = t_len // chunk
    bg = BATCH_GROUP
    hd = MLSTM_HEAD_DIM
    n_items = (batch // bg) * n_chunks
    prep = lambda i: jnp.minimum(i, n_items - 1)
    solve = lambda i: jnp.maximum(i - 1, 0)
    prep_seq = lambda i: (prep(i) // n_chunks, prep(i) % n_chunks, 0)
    solve_seq = lambda i: (solve(i) // n_chunks, solve(i) % n_chunks, 0)
    vec = lambda n: pl.BlockSpec((1, n), lambda i: (0, 0))
    group = lambda *dims: pl.BlockSpec((bg,) + dims, lambda i: (solve(i) // n_chunks,) + (0,) * len(dims))
    rw_state = group(RWKV_PAIRS, LANES, LANES)
    ml_states = [group(MLSTM_HEADS, hd, hd), group(MLSTM_HEADS, hd), group(1, MLSTM_HEADS)]
    wide = lambda rows, dtype: pltpu.VMEM((bg, rows, D_RWKV), dtype)
    kern = functools.partial(_recurrent_kernel, chunk=chunk, n_chunks=n_chunks, n_items=n_items)
    return pl.pallas_call(
        kern,
        grid=(n_items + 1,),
        in_specs=[
            pl.BlockSpec((bg, chunk, RWKV_PROJ), prep_seq),
            pl.BlockSpec((bg, 1, RWKV_PROJ), lambda i: (prep(i) // n_chunks, 0, 0)),
            rw_state,
            vec(RWKV_PROJ), vec(D_RWKV), vec(D_RWKV), vec(D_RWKV), vec(D_RWKV), vec(D_RWKV), vec(D_RWKV),
            vec(D_RWKV),
            pl.BlockSpec((LORA_ALL, 3 * D_RWKV), lambda i: (0, 0)),
            pl.BlockSpec((bg, chunk, 2 * D_MLSTM), solve_seq),
            pl.BlockSpec((bg, chunk, VO_COLS), solve_seq),
            *ml_states,
            vec(LANES), vec(D_MLSTM), vec(D_MLSTM),
        ],
        out_specs=[
            pl.BlockSpec((bg, chunk, D_RWKV), solve_seq), rw_state,
            pl.BlockSpec((bg, chunk, D_MLSTM), solve_seq), *ml_states,
        ],
        out_shape=[
            jax.ShapeDtypeStruct((batch, t_len, D_RWKV), BF16),
            jax.ShapeDtypeStruct((batch, RWKV_PAIRS, LANES, LANES), F32),
            jax.ShapeDtypeStruct((batch, t_len, D_MLSTM), BF16),
            jax.ShapeDtypeStruct((batch, MLSTM_HEADS, hd, hd), F32),
            jax.ShapeDtypeStruct((batch, MLSTM_HEADS, hd), F32),
            jax.ShapeDtypeStruct((batch, 1, MLSTM_HEADS), F32),
        ],
        scratch_shapes=[
            pltpu.VMEM((bg, RWKV_PAIRS, LANES, LANES), F32),
            pltpu.VMEM((bg, 8, RWKV_PROJ), F32),
            wide(2 * chunk, BF16), wide(2 * chunk, BF16), wide(2 * chunk, BF16), wide(chunk, BF16),
            wide(chunk, F32), wide(chunk, F32), wide(1, F32),
            pltpu.VMEM((bg, MLSTM_HEADS, hd, hd), F32),
            pltpu.VMEM((bg, MLSTM_HEADS, hd), F32),
            pltpu.VMEM((bg, 1, MLSTM_HEADS), F32),
        ],
        compiler_params=_params(("arbitrary",)),
        name="rwkv7_mlstm_recurrent",
    )(p_rw, shift_prev, h0_bd, w["mu"], w["w0"], w["a0"], w["k_k"], w["k_a"], w["r_k"], w["rw_ln_w"],
      w["rw_ln_b"], w["w_lora"], qk, p_vo, c0, n0, m0, w["b_if"], w["ml_ln_w"], w["ml_ln_b"])


def _tail_kernel(x_ref, ya_ref, hb_ref, mq_ref, gate_ref, mk_ref, mv_ref, wpa_ref, wpb_ref, wpc_ref, wout_ref,
                 lnw_ref, lnb_ref, o_ref, *, rows):
    n_seg = mk_ref.shape[0]
    d = D_MODEL
    head = lambda h: slice(h * MEM_HEAD_DIM, (h + 1) * MEM_HEAD_DIM)
    units = [(b, h) for b in range(n_seg) for h in range(MEM_HEADS)]
    idx = range(len(units))
    mk = [mk_ref[b].astype(BF16) for b in range(n_seg)]
    mv = [mv_ref[b].astype(BF16) for b in range(n_seg)]
    s = [_dot_nt(mq_ref[b * rows:(b + 1) * rows, head(h)], mk[b][:, head(h)]) * (MEM_HEAD_DIM ** -0.5)
         for b, h in units]
    y_a = _sigmoid(gate_ref[:, 0:d]) * _dot(ya_ref[...], wpa_ref[...])
    top = [jnp.max(s[u], axis=-1, keepdims=True) for u in idx]
    e = [jnp.exp(s[u] - top[u]) for u in idx]
    y_b = _sigmoid(gate_ref[:, d:2 * d]) * _dot(hb_ref[...], wpb_ref[...])
    den = [jnp.sum(e[u], axis=-1, keepdims=True) for u in idx]
    prob = [(e[u] / den[u]).astype(BF16) for u in idx]
    out = [_dot(prob[u], mv[b][:, head(h)]).astype(BF16) for u, (b, h) in enumerate(units)]
    attn = [jnp.concatenate(out[b * MEM_HEADS:(b + 1) * MEM_HEADS], axis=-1) for b in range(n_seg)]
    attn = attn[0] if n_seg == 1 else jnp.concatenate(attn, axis=0)
    y_c = _sigmoid(gate_ref[:, 2 * d:3 * d]) * _dot(attn, wpc_ref[...])
    mixed = y_a + y_b + y_c
    u = _bdot(mixed, wout_ref[...])
    o_ref[...] = _layer_norm(DN_ALPHA * x_ref[...] + u, lnw_ref[...], lnb_ref[...])


def _tail(x, ya, hb, p_mq, p_gate, mem_k, mem_v, w, rows, batches):
    n = x.shape[0]
    tm = rows * batches
    tiles_per_batch = (n // mem_k.shape[0]) // rows if batches == 1 else 1
    row = lambda i: (i, 0)
    mem = pl.BlockSpec((batches, N_MEM, D_MEM), lambda i: (i // tiles_per_batch, 0, 0))
    tok = lambda cols: pl.BlockSpec((tm, cols), row)
    return pl.pallas_call(
        functools.partial(_tail_kernel, rows=rows),
        grid=(n // tm,),
        in_specs=[
            tok(D_MODEL), tok(D_RWKV), tok(D_MLSTM), tok(D_MEM), tok(3 * D_MODEL), mem, mem,
            _const_spec((D_RWKV, D_MODEL)), _const_spec((D_MLSTM, D_MODEL)), _const_spec((D_MEM, D_MODEL)),
            _const_spec((D_MODEL, D_MODEL)), _const_spec((1, D_MODEL)), _const_spec((1, D_MODEL)),
        ],
        out_specs=tok(D_MODEL),
        out_shape=jax.ShapeDtypeStruct((n, D_MODEL), F32),
        compiler_params=_params(("arbitrary",)),
        name="attn_merge_ln",
    )(x, ya, hb, p_mq, p_gate, mem_k, mem_v, w["w_pa"], w["w_pb"], w["w_pc"], w["w_out"], w["ln1_w"],
      w["ln1_b"])


def _moe_kernel(x_ref, wr_ref, rb_ref, w1_ref, w3_ref, w2_ref, ws1_ref, ws3_ref, ws2_ref, lnw_ref, lnb_ref, o_ref):
    x = x_ref[...]
    xb = x.astype(BF16)
    group = 8
    width = group * EXPERT_HIDDEN
    n_slabs = N_EXPERTS // group

    def up_project(s):
        cols = slice(s * width, (s + 1) * width)
        return _dot(xb, w1_ref[:, cols]), _dot(xb, w3_ref[:, cols])

    logits = _dot(xb, wr_ref[...])
    ahead = up_project(0)
    scores = _sigmoid(logits)
    lane = lax.broadcasted_iota(jnp.int32, scores.shape, 1)
    work = jnp.where(lane < N_EXPERTS, scores + rb_ref[...], -jnp.inf)
    chosen = jnp.zeros(scores.shape, dtype=jnp.bool_)
    for _ in range(TOP_K):
        best = jnp.max(work, axis=-1, keepdims=True)
        first = jnp.min(jnp.where(work == best, lane, LANES), axis=-1, keepdims=True)
        pick = lane == first
        chosen = chosen | pick
        work = jnp.where(pick, -jnp.inf, work)
    sel = jnp.where(chosen, scores, 0.0)
    gates = sel / jnp.sum(sel, axis=-1, keepdims=True) * ROUTED_SCALE

    acc = _dot((_silu(_dot(xb, ws1_ref[...])) * _dot(xb, ws3_ref[...])).astype(BF16), ws2_ref[...])
    for s in range(n_slabs):
        cols = slice(s * width, (s + 1) * width)
        h1, h3 = ahead
        if s + 1 < n_slabs:
            ahead = up_project(s + 1)
        hidden = _silu(h1) * h3
        gated = jnp.concatenate(
            [hidden[:, e * EXPERT_HIDDEN:(e + 1) * EXPERT_HIDDEN] * gates[:, s * group + e:s * group + e + 1]
             for e in range(group)], axis=-1)
        acc = acc + _dot(gated.astype(BF16), w2_ref[cols, :])
    o_ref[...] = _layer_norm(DN_ALPHA * x + acc, lnw_ref[...], lnb_ref[...])


def _moe(x, w, tm):
    n = x.shape[0]
    row = lambda i: (i, 0)
    hid = N_EXPERTS * EXPERT_HIDDEN
    return pl.pallas_call(
        _moe_kernel,
        grid=(n // tm,),
        in_specs=[
            pl.BlockSpec((tm, D_MODEL), row),
            _const_spec((D_MODEL, LANES)), _const_spec((1, LANES)),
            _const_spec((D_MODEL, hid)), _const_spec((D_MODEL, hid)), _const_spec((hid, D_MODEL)),
            _const_spec((D_MODEL, EXPERT_HIDDEN)), _const_spec((D_MODEL, EXPERT_HIDDEN)),
            _const_spec((EXPERT_HIDDEN, D_MODEL)),
            _const_spec((1, D_MODEL)), _const_spec((1, D_MODEL)),
        ],
        out_specs=pl.BlockSpec((tm, D_MODEL), row),
        out_shape=jax.ShapeDtypeStruct((n, D_MODEL), F32),
        compiler_params=_params(("arbitrary",)),
        name="moe_ln",
    )(x, w["w_router"], w["router_bias"], w["w_e1"], w["w_e3"], w["w_e2"], w["w_s1"], w["w_s3"], w["w_s2"],
      w["ln2_w"], w["ln2_b"])


def _prep_weights(w_in, b_in, rwkv_mu, rwkv_w0, rwkv_w2, rwkv_a0, rwkv_a2, rwkv_g2, rwkv_k_k, rwkv_k_a,
                  rwkv_r_k, rwkv_ln_w, rwkv_ln_b, mlstm_conv_w, mlstm_conv_b, mlstm_b_i, mlstm_b_f,
                  mlstm_ln_w, mlstm_ln_b, w_mem_kv, w_pa, w_pb, w_pc, w_out, ln1_w, ln1_b,
                  w_router, router_bias, w_e1, w_e3, w_e2, w_s1, w_s3, w_s2, ln2_w, ln2_b):
    off_ml = RWKV_PROJ
    off_if = off_ml + 4 * D_MLSTM
    off_mq = off_if + 2 * MLSTM_HEADS
    pad_if = LANES - 2 * MLSTM_HEADS
    row = lambda t: t.reshape(1, -1)
    hid = N_EXPERTS * EXPERT_HIDDEN
    w_lora = jnp.zeros((LORA_ALL, 3 * D_RWKV), F32)
    w_lora = w_lora.at[0:W_LORA, 0:D_RWKV].set(rwkv_w2)
    w_lora = w_lora.at[W_LORA:W_LORA + A_LORA, D_RWKV:2 * D_RWKV].set(rwkv_a2)
    w_lora = w_lora.at[W_LORA + A_LORA:, 2 * D_RWKV:].set(rwkv_g2)
    return dict(
        w_rw=w_in[:, :off_ml].astype(BF16), b_rw=row(b_in[:off_ml]),
        w_qk=w_in[:, off_ml:off_ml + 2 * D_MLSTM].astype(BF16), b_qk=row(b_in[off_ml:off_ml + 2 * D_MLSTM]),
        w_vo=jnp.pad(w_in[:, off_ml + 2 * D_MLSTM:off_mq], ((0, 0), (0, pad_if))).astype(BF16),
        b_vo=row(jnp.pad(b_in[off_ml + 2 * D_MLSTM:off_mq], (0, pad_if))),
        w_mq=w_in[:, off_mq:off_mq + D_MEM].astype(BF16), b_mq=row(b_in[off_mq:off_mq + D_MEM]),
        w_gate=w_in[:, off_mq + D_MEM:].astype(BF16), b_gate=row(b_in[off_mq + D_MEM:]),
        w_mem_k=w_mem_kv[:, :D_MEM].astype(BF16), w_mem_v=w_mem_kv[:, D_MEM:].astype(BF16),
        mu=row(rwkv_mu), w0=row(rwkv_w0), a0=row(rwkv_a0), k_k=row(rwkv_k_k), k_a=row(rwkv_k_a),
        r_k=row(rwkv_r_k), rw_ln_w=row(rwkv_ln_w), rw_ln_b=row(rwkv_ln_b), w_lora=w_lora.astype(BF16),
        conv_w=mlstm_conv_w, conv_b=row(mlstm_conv_b),
        b_if=row(jnp.pad(jnp.concatenate([mlstm_b_i, mlstm_b_f]), (0, pad_if))),
        ml_ln_w=row(mlstm_ln_w), ml_ln_b=row(mlstm_ln_b),
        w_pa=w_pa.astype(BF16), w_pb=w_pb.astype(BF16), w_pc=w_pc.astype(BF16), w_out=w_out.astype(BF16),
        ln1_w=row(ln1_w), ln1_b=row(ln1_b),
        w_router=jnp.pad(w_router, ((0, 0), (0, LANES - N_EXPERTS))).astype(BF16),
        router_bias=row(jnp.pad(router_bias, (0, LANES - N_EXPERTS))),
        w_e1=jnp.transpose(w_e1.astype(BF16), (1, 0, 2)).reshape(D_MODEL, hid),
        w_e3=jnp.transpose(w_e3.astype(BF16), (1, 0, 2)).reshape(D_MODEL, hid),
        w_e2=w_e2.reshape(hid, D_MODEL).astype(BF16),
        w_s1=w_s1.astype(BF16), w_s3=w_s3.astype(BF16), w_s2=w_s2.astype(BF16),
        ln2_w=row(ln2_w), ln2_b=row(ln2_b),
    )


def _pair_state(wkv):
    bsz = wkv.shape[0]
    s = wkv.reshape(bsz, RWKV_PAIRS, 2, RWKV_HEAD_DIM, RWKV_HEAD_DIM)
    z = jnp.zeros_like(s[:, :, 0])
    top = jnp.concatenate([s[:, :, 0], z], axis=-1)
    bot = jnp.concatenate([z, s[:, :, 1]], axis=-1)
    return jnp.concatenate([top, bot], axis=-2)


def _unpair_state(h_bd):
    bsz = h_bd.shape[0]
    hd = RWKV_HEAD_DIM
    heads = jnp.stack([h_bd[:, :, :hd, :hd], h_bd[:, :, hd:, hd:]], axis=2)
    return heads.reshape(bsz, RWKV_HEADS, hd, hd)


def _trunk(x3, mem_k, mem_v, shift_prev, wkv0, conv_prev, c0, n0, m0, w, chunk, tm_proj, tail_rows, tail_batches,
           tm_moe):
    bsz, t_len, _ = x3.shape
    n = bsz * t_len
    x = x3.reshape(n, D_MODEL)
    p_rw = _matmul_bias(x, w["w_rw"], w["b_rw"], tm_proj, RWKV_PROJ)
    qk, qk_tail, p_vo = _mlstm_proj(x, w["w_qk"], w["b_qk"], w["conv_w"], w["conv_b"], conv_prev, w["w_vo"],
                                    w["b_vo"], min(tm_proj, 512))
    p_mq = _matmul_bias(x, w["w_mq"], w["b_mq"], tm_proj, D_MEM, BF16)
    p_gate = _matmul_bias(x, w["w_gate"], w["b_gate"], tm_proj, 3 * D_MODEL)

    p_rw3 = p_rw.reshape(bsz, t_len, RWKV_PROJ)
    ya, h_bd, hb, c_t, n_t, m_t = _recurrent(
        p_rw3, shift_prev, _pair_state(wkv0), qk.reshape(bsz, t_len, 2 * D_MLSTM), p_vo.reshape(bsz, t_len, VO_COLS),
        c0, n0, m0.reshape(bsz, 1, MLSTM_HEADS), w, chunk)
    x1 = _tail(x, ya.reshape(n, D_RWKV), hb.reshape(n, D_MLSTM), p_mq, p_gate, mem_k, mem_v, w, tail_rows,
               tail_batches)
    y = _moe(x1, w, tm_moe)

    new_shift = p_rw3[:, t_len - 1:, :]
    new_conv = qk_tail[:, 8 - (CONV_W - 1):, :]
    states = (new_shift, _unpair_state(h_bd), new_conv, c_t, n_t, m_t.reshape(bsz, MLSTM_HEADS))
    return y.reshape(bsz, t_len, D_MODEL), states


def kernel(x_prompt, x_sample, state_rwkv_shift, state_rwkv_wkv, state_mlstm_conv, state_mlstm_c, state_mlstm_n, state_mlstm_m, cache_mem_k, cache_mem_v, mem_prompt, w_in, b_in, rwkv_mu, rwkv_w0, rwkv_w2, rwkv_a0, rwkv_a2, rwkv_g2, rwkv_k_k, rwkv_k_a, rwkv_r_k, rwkv_ln_w, rwkv_ln_b, mlstm_conv_w, mlstm_conv_b, mlstm_b_i, mlstm_b_f, mlstm_ln_w, mlstm_ln_b, w_mem_kv, w_pa, w_pb, w_pc, w_out, ln1_w, ln1_b, w_router, router_bias, w_e1, w_e3, w_e2, w_s1, w_s3, w_s2, ln2_w, ln2_b):
    weights = (w_in, b_in, rwkv_mu, rwkv_w0, rwkv_w2, rwkv_a0, rwkv_a2, rwkv_g2, rwkv_k_k, rwkv_k_a, rwkv_r_k,
               rwkv_ln_w, rwkv_ln_b, mlstm_conv_w, mlstm_conv_b, mlstm_b_i, mlstm_b_f, mlstm_ln_w, mlstm_ln_b,
               w_mem_kv, w_pa, w_pb, w_pc, w_out, ln1_w, ln1_b, w_router, router_bias, w_e1, w_e3, w_e2,
               w_s1, w_s3, w_s2, ln2_w, ln2_b)
    w = _prep_weights(*(t[0] for t in weights))
    bp, t_p, _ = x_prompt.shape
    bs, t_s, _ = x_sample.shape
    hd = MLSTM_HEAD_DIM

    mem_flat = mem_prompt.reshape(bp * N_MEM, D_MODEL)
    zero_bias = jnp.zeros((1, D_MEM), F32)
    mk = _matmul_bias(mem_flat, w["w_mem_k"], zero_bias, bp * N_MEM, D_MEM).reshape(bp, N_MEM, D_MEM)
    mv = _matmul_bias(mem_flat, w["w_mem_v"], zero_bias, bp * N_MEM, D_MEM).reshape(bp, N_MEM, D_MEM)

    y_p, st_p = _trunk(
        x_prompt, mk, mv,
        jnp.zeros((bp, 1, RWKV_PROJ), F32), jnp.zeros((bp, RWKV_HEADS, RWKV_HEAD_DIM, RWKV_HEAD_DIM), F32),
        jnp.zeros((bp, CONV_W - 1, 2 * D_MLSTM), F32), jnp.zeros((bp, MLSTM_HEADS, hd, hd), F32),
        jnp.zeros((bp, MLSTM_HEADS, hd), F32), jnp.zeros((bp, MLSTM_HEADS), F32),
        w, chunk=min(MLSTM_CHUNK, t_p), tm_proj=1024, tail_rows=512, tail_batches=1, tm_moe=512)
    y_s, st_s = _trunk(
        x_sample, cache_mem_k[0].reshape(bs, N_MEM, D_MEM), cache_mem_v[0].reshape(bs, N_MEM, D_MEM),
        state_rwkv_shift[0], state_rwkv_wkv[0], state_mlstm_conv[0], state_mlstm_c[0], state_mlstm_n[0],
        state_mlstm_m[0],
        w, chunk=min(MLSTM_CHUNK, t_s), tm_proj=bs * t_s, tail_rows=t_s, tail_batches=4, tm_moe=bs * t_s)

    lead = lambda t: t[None]
    mem_shape = (1, bp, N_MEM, MEM_HEADS, MEM_HEAD_DIM)
    return (y_p, y_s, *(lead(t) for t in st_p), mk.reshape(mem_shape), mv.reshape(mem_shape),
            *(lead(t) for t in st_s))
```

```python
import functools
import itertools

import jax
import jax.numpy as jnp
from jax import lax
from jax.experimental import pallas as pl
from jax.experimental.pallas import tpu as pltpu

F32 = jnp.float32
BF16 = jnp.bfloat16

D_MODEL = 1024
DEPTH = 1
D_RWKV = 1024
RWKV_HEAD_DIM = 64
RWKV_HEADS = 16
W_LORA = 64
A_LORA = 64
G_LORA = 128
LORA_ALL = W_LORA + A_LORA + G_LORA
RWKV_PROJ = 3 * D_RWKV + LORA_ALL
RWKV_GN_EPS = 64e-5
DECAY_SCALE_LOG2 = 0.6065306597126334 * 1.4426950408889634
D_MLSTM = 1024
MLSTM_HEADS = 4
MLSTM_HEAD_DIM = 256
CONV_W = 4
MLSTM_CHUNK = 64
N_MEM = 256
MEM_HEADS = 4
MEM_HEAD_DIM = 256
D_MEM = 1024
N_EXPERTS = 32
TOP_K = 4
EXPERT_HIDDEN = 128
ROUTED_SCALE = 2.5
DN_ALPHA = (2 * DEPTH) ** 0.25
LN_EPS = 1e-5

LANES = 128
RWKV_PAIRS = D_RWKV // LANES
VO_COLS = 2 * D_MLSTM + LANES
VMEM_LIMIT = 56 * 1024 * 1024
BATCH_GROUP = 2


def _dot(a, b):
    return jnp.dot(a, b, preferred_element_type=F32)


def _dot_nt(a, b):
    return lax.dot_general(a, b, (((1,), (1,)), ((), ())), preferred_element_type=F32)


def _dot_tn(a, b):
    return lax.dot_general(a, b, (((0,), (0,)), ((), ())), preferred_element_type=F32)


def _split(a):
    hi = a.astype(BF16)
    return hi, (a - hi.astype(F32)).astype(BF16)


def _split3(a):
    hi = a.astype(BF16)
    rest = a - hi.astype(F32)
    mid = rest.astype(BF16)
    return hi, mid, (rest - mid.astype(F32)).astype(BF16)


def _bdot(a, b):
    return _dot(a.astype(BF16), b.astype(BF16))


def _sigmoid(x):
    return 0.5 * jnp.tanh(0.5 * x) + 0.5


def _softplus(x):
    return jnp.maximum(x, 0.0) + jnp.log(1.0 + jnp.exp(-jnp.abs(x)))


def _silu(x):
    return x * _sigmoid(x)


def _layer_norm(x, w, b):
    mu = jnp.mean(x, axis=-1, keepdims=True)
    d = x - mu
    var = jnp.mean(d * d, axis=-1, keepdims=True)
    return d * lax.rsqrt(var + LN_EPS) * w + b


def _params(semantics):
    return pltpu.CompilerParams(dimension_semantics=semantics, vmem_limit_bytes=VMEM_LIMIT)


def _const_spec(shape):
    zeros = (0,) * len(shape)
    return pl.BlockSpec(shape, lambda *_: zeros, pipeline_mode=pl.Buffered(1))


def _mm_kernel(x_ref, w_ref, b_ref, o_ref):
    o_ref[...] = (_dot(x_ref[...].astype(BF16), w_ref[...]) + b_ref[...]).astype(o_ref.dtype)


def _matmul_bias(x, w, b, tm, tn, out_dtype=F32):
    n, k = x.shape
    nc = w.shape[1]
    resident = dict(pipeline_mode=pl.Buffered(1)) if nc == tn else {}
    return pl.pallas_call(
        _mm_kernel,
        grid=(nc // tn, n // tm),
        in_specs=[
            pl.BlockSpec((tm, k), lambda j, i: (i, 0)),
            pl.BlockSpec((k, tn), lambda j, i: (0, j), **resident),
            pl.BlockSpec((1, tn), lambda j, i: (0, j), **resident),
        ],
        out_specs=pl.BlockSpec((tm, tn), lambda j, i: (i, j)),
        out_shape=jax.ShapeDtypeStruct((n, nc), out_dtype),
        compiler_params=_params(("arbitrary", "arbitrary")),
        name="matmul_bias",
    )(x, w, b)


def _mlstm_proj_kernel(x_ref, w_ref, b_ref, cw_ref, cb_ref, prev_ref, w2_ref, b2_ref, o_ref, tail_ref, o2_ref, carry,
                       *, rows, tiles_per_batch):
    row_tile = pl.program_id(0)
    n_seg = prev_ref.shape[0]
    tm, tn = o_ref.shape
    n2 = o2_ref.shape[1]
    pad = jnp.zeros((8 - (CONV_W - 1), LANES), F32)
    row8 = lax.broadcasted_iota(jnp.int32, (8, LANES), 0)

    if n_seg == 1:
        @pl.when(row_tile % tiles_per_batch == 0)
        def _():
            carry[0:8 - (CONV_W - 1), :] = jnp.zeros((8 - (CONV_W - 1), tn), F32)
            carry[8 - (CONV_W - 1):8, :] = prev_ref[0]

    xb = x_ref[...].astype(BF16)
    n_parts = 4
    q_step = tn // n_parts
    for part in range(n_parts):
        cs = slice(part * q_step, (part + 1) * q_step)
        o_ref[:, cs] = _dot(xb, w_ref[:, cs]) + b_ref[:, cs]
    for seg in range(n_seg):
        tail_ref[seg] = o_ref[(seg + 1) * rows - 8:(seg + 1) * rows, :]
    strip = min(rows, 64)

    def conv_strips(r_lo, r_hi):
        for r0 in reversed(range(r_lo, r_hi, strip)):
            for c0 in range(0, tn, LANES):
                cols = slice(c0, c0 + LANES)
                half_scale = 0.5 * MLSTM_HEAD_DIM ** -0.5 if c0 >= D_MLSTM else 0.5
                if n_seg > 1 and r0 % rows == 0:
                    head = jnp.concatenate([pad, prev_ref[r0 // rows, :, cols]], axis=0)
                elif r0 == 0:
                    head = carry[:, cols]
                else:
                    head = o_ref[r0 - 8:r0, cols]
                x = o_ref[r0:r0 + strip, cols]
                blocks = [head] + [x[r:r + 8, :] for r in range(0, strip, 8)]
                conv = cb_ref[:, cols] + cw_ref[CONV_W - 1:CONV_W, cols] * x
                for j in range(CONV_W - 1):
                    lag = CONV_W - 1 - j
                    turned = [pltpu.roll(blk, lag, 0) for blk in blocks]
                    shifted = jnp.concatenate([jnp.where(row8 < lag, turned[i], turned[i + 1])
                                               for i in range(len(blocks) - 1)], axis=0)
                    conv = conv + cw_ref[j:j + 1, cols] * shifted
                o_ref[r0:r0 + strip, cols] = conv * (half_scale * jnp.tanh(0.5 * conv) + half_scale)

    col_step = -(-n2 // (n_parts * LANES)) * LANES
    row_step = tm // n_parts
    for part in reversed(range(n_parts)):
        cs = slice(part * col_step, min((part + 1) * col_step, n2))
        o2_ref[:, cs] = _dot(xb, w2_ref[:, cs]) + b2_ref[:, cs]
        conv_strips(part * row_step, (part + 1) * row_step)

    if n_seg == 1:
        carry[...] = tail_ref[0]


def _mlstm_proj(x, w, b, conv_w, conv_b, conv_prev, w2, b2, tm):
    n, k = x.shape
    batch = conv_prev.shape[0]
    t_len = n // batch
    rows = min(tm, t_len)
    n_seg = tm // rows
    tiles_per_batch = t_len // rows
    nc, n2 = w.shape[1], w2.shape[1]
    seg = lambda i: (i // tiles_per_batch if n_seg == 1 else i, 0, 0)
    return pl.pallas_call(
        functools.partial(_mlstm_proj_kernel, rows=rows, tiles_per_batch=tiles_per_batch),
        grid=(n // tm,),
        in_specs=[
            pl.BlockSpec((tm, k), lambda i: (i, 0)),
            _const_spec((k, nc)), _const_spec((1, nc)), _const_spec((CONV_W, nc)), _const_spec((1, nc)),
            pl.BlockSpec((n_seg, CONV_W - 1, nc), seg),
            _const_spec((k, n2)), _const_spec((1, n2)),
        ],
        out_specs=[
            pl.BlockSpec((tm, nc), lambda i: (i, 0)),
            pl.BlockSpec((n_seg, 8, nc), seg),
            pl.BlockSpec((tm, n2), lambda i: (i, 0)),
        ],
        out_shape=[jax.ShapeDtypeStruct((n, nc), F32), jax.ShapeDtypeStruct((batch, 8, nc), F32),
                   jax.ShapeDtypeStruct((n, n2), F32)],
        scratch_shapes=[pltpu.VMEM((8, nc), F32)],
        compiler_params=_params(("arbitrary",)),
        name="mlstm_proj_conv",
    )(x, w, b, conv_w, conv_b, conv_prev, w2, b2)


def _rwkv_body(p_ref, prev_ref, h0_ref, mu_ref, w0_ref, a0_ref, kk_ref, ka_ref, rk_ref, lnw_ref, lnb_ref,
               wl_ref, y_ref, hout_ref, h_scr, xbuf, ar_scr, bk_scr, bkh_scr, v_scr, bonus_scr, g_scr, gamma_scr,
               *, chunk, n_chunks, n_items, companion):
    step = pl.program_id(0)
    L = chunk
    half = 2 * L
    prep_chunk = jnp.minimum(step, n_items - 1) % n_chunks
    solve_chunk = jnp.maximum(step - 1, 0) % n_chunks

    @pl.when(step == 0)
    def _():
        for ref in (ar_scr, bk_scr, bkh_scr, v_scr, bonus_scr, g_scr):
            ref[...] = jnp.zeros(ref.shape, ref.dtype)
        gamma_scr[...] = jnp.ones(gamma_scr.shape, F32)

    @pl.when(solve_chunk == 0)
    def _():
        h_scr[...] = h0_ref[...]

    @pl.when(prep_chunk == 0)
    def _():
        for bi in range(BATCH_GROUP):
            xbuf[bi, 0:1, :] = prev_ref[bi]

    ti = lax.broadcasted_iota(jnp.int32, (L, L), 0)
    tj = lax.broadcasted_iota(jnp.int32, (L, L), 1)
    ltri = (tj <= ti).astype(BF16)
    row_m = lax.broadcasted_iota(jnp.int32, (L, half), 0)
    col_m = lax.broadcasted_iota(jnp.int32, (L, half), 1)
    col_in = jnp.where(col_m >= L, col_m - L, col_m)
    strict = col_in < row_m
    incl = col_in <= row_m
    eye = (col_in == row_m).astype(F32)
    gi = lax.broadcasted_iota(jnp.int32, (LANES, LANES), 0)
    gj = lax.broadcasted_iota(jnp.int32, (LANES, LANES), 1)
    same_head = (gi >= RWKV_HEAD_DIM) == (gj >= RWKV_HEAD_DIM)
    head_ones = same_head.astype(BF16)
    first_nat = lax.broadcasted_iota(jnp.int32, (L, LANES), 1) < RWKV_HEAD_DIM
    first_sbs = col_m < L
    lane_l = lax.broadcasted_iota(jnp.int32, (L, LORA_ALL), 1)

    def diag(x, first):
        zero = jnp.zeros_like(x)
        return jnp.concatenate([jnp.where(first, x, zero), jnp.where(first, zero, x)], axis=0)

    def seg_sum(x):
        return _dot(x.astype(BF16), head_ones)

    units = [(bi, p) for bi in range(BATCH_GROUP) for p in range(RWKV_PAIRS)]
    idx = range(len(units))
    lane_of = lambda p: slice(p * LANES, (p + 1) * LANES)

    ar = [ar_scr[bi, :, lane_of(p)] for bi, p in units]
    bk = [bk_scr[bi, :, lane_of(p)] for bi, p in units]
    bkh = [bkh_scr[bi, :, lane_of(p)] for bi, p in units]
    v_b = [v_scr[bi, :, lane_of(p)] for bi, p in units]
    bonus = [bonus_scr[bi, :, lane_of(p)] for bi, p in units]
    gate = [g_scr[bi, :, lane_of(p)] for bi, p in units]
    gamma = [gamma_scr[bi, :, lane_of(p)] for bi, p in units]

    first_row = lax.broadcasted_iota(jnp.int32, (L, RWKV_PROJ), 0) == 0

    def prepare_row(bi):
        cur = p_ref[bi]
        prev = jnp.where(first_row, xbuf[bi, 0:1, :], pltpu.roll(cur, 1, 0))
        xbuf[bi, 0:1, :] = cur[L - 1:L, :]
        xr = cur + (prev - cur) * mu_ref[...]
        r_all = xr[:, 0:D_RWKV]
        k_all = xr[:, D_RWKV:2 * D_RWKV]
        v_all = xr[:, 2 * D_RWKV:3 * D_RWKV]
        slab = xr[:, 3 * D_RWKV:]
        act = jnp.where(lane_l < W_LORA, jnp.tanh(slab),
                        jnp.where(lane_l < W_LORA + A_LORA, slab, _sigmoid(slab)))
        yield
        lora = _bdot(act, wl_ref[...])
        yield
        lw = -DECAY_SCALE_LOG2 * _sigmoid(w0_ref[...] + lora[:, 0:D_RWKV])
        a_sig = _sigmoid(a0_ref[...] + lora[:, D_RWKV:2 * D_RWKV])
        lw_hi, lw_lo = _split(lw)
        yield
        kk0 = k_all * kk_ref[...]
        k2 = k_all * (1.0 + (a_sig - 1.0) * ka_ref[...])
        sums = jnp.concatenate([kk0 * kk0, r_all * k2 * rk_ref[...]], axis=0).astype(BF16)
        yield
        cum = _dot(ltri, lw_hi) + _dot(ltri, lw_lo)
        sums = jnp.concatenate([_dot(sums[:, lane_of(p)], head_ones) for p in range(RWKV_PAIRS)], axis=1)
        yield
        kk = kk0 * lax.rsqrt(jnp.maximum(sums[0:L], 1e-24))
        b_vec = kk * a_sig
        c_last = cum[L - 1:L, :]
        e_neg = jnp.exp2(-cum)
        e_end = jnp.exp2(c_last - cum)
        yield
        ar_scr[bi] = jnp.concatenate([-kk * jnp.exp2(cum - lw), r_all * jnp.exp2(cum)], axis=0).astype(BF16)
        bk_scr[bi] = jnp.concatenate([b_vec * e_neg, k2 * e_neg], axis=0).astype(BF16)
        yield
        bkh_scr[bi] = jnp.concatenate([b_vec * e_end, k2 * e_end], axis=0).astype(BF16)
        v_scr[bi] = v_all.astype(BF16)
        bonus_scr[bi] = sums[L:] * v_all
        g_scr[bi] = lora[:, 2 * D_RWKV:]
        gamma_scr[bi] = jnp.exp2(c_last)

    segments = itertools.chain(*(prepare_row(bi) for bi in range(BATCH_GROUP)))

    def prepare_some(count):
        for _ in range(count):
            next(segments, None)
            next(companion, None)

    n_sq = L.bit_length() - 1
    per_stage = 2
    bk_d = [jnp.concatenate([diag(bk[u][0:L], first_nat), diag(bk[u][L:], first_nat)], axis=0) for u in idx]
    v_d = [diag(v_b[u], first_nat) for u in idx]
    h_bd = [h_scr[bi, p] for bi, p in units]
    sc = [_dot_nt(ar[u], bk_d[u]) for u in idx]
    prepare_some(per_stage)
    arh = [_dot_nt(ar[u], h_bd[u].astype(BF16)) for u in idx]
    m_ab = [jnp.where(strict, sc[u][0:L, 0:half], 0.0) for u in idx]
    m_ak = [jnp.where(strict, sc[u][0:L, half:], 0.0).astype(BF16) for u in idx]
    n_rbk = [jnp.concatenate([jnp.where(incl, sc[u][L:, 0:half], 0.0),
                              jnp.where(incl, sc[u][L:, half:], 0.0)], axis=1).astype(BF16) for u in idx]
    prepare_some(per_stage)
    w0 = [arh[u][0:L] + _dot(m_ak[u], v_d[u]) for u in idx]
    inv = [eye + m_ab[u] for u in idx]
    m_b = [m_ab[u].astype(BF16) for u in idx]
    pw = [_dot(m_b[u], diag(m_b[u], first_sbs)) for u in idx]
    prepare_some(per_stage)
    for i in range(n_sq - 1):
        pw_d = [diag(pw[u].astype(BF16), first_sbs) for u in idx]
        if i + 1 < n_sq - 1:
            z = [_dot(jnp.concatenate([inv[u], pw[u]], axis=0).astype(BF16), pw_d[u]) for u in idx]
            inv = [inv[u] + z[u][0:L] for u in idx]
            pw = [z[u][L:] for u in idx]
        else:
            inv = [inv[u] + _dot(inv[u].astype(BF16), pw_d[u]) for u in idx]
        prepare_some(per_stage)
    c_b = [_dot(inv[u].astype(BF16), diag(w0[u].astype(BF16), first_nat)).astype(BF16) for u in idx]
    cv = [jnp.concatenate([c_b[u], v_b[u]], axis=0) for u in idx]
    y = [arh[u][L:] + _dot(n_rbk[u], jnp.concatenate([diag(c_b[u], first_nat), v_d[u]], axis=0)) for u in idx]
    for u, (bi, p) in enumerate(units):
        h_scr[bi, p] = h_bd[u] * gamma[u] + jnp.where(same_head, _dot_tn(cv[u], bkh[u]), 0.0)
    mean = [seg_sum(y[u]) * (1.0 / RWKV_HEAD_DIM) for u in idx]
    dev = [y[u] - mean[u] for u in idx]
    var = [seg_sum(dev[u] * dev[u]) * (1.0 / RWKV_HEAD_DIM) for u in idx]
    for u, (bi, p) in enumerate(units):
        s = lane_of(p)
        yn = dev[u] * lax.rsqrt(var[u] + RWKV_GN_EPS) * lnw_ref[:, s] + lnb_ref[:, s]
        y_ref[bi, :, s] = ((yn + bonus[u]) * gate[u]).astype(y_ref.dtype)
    prepare_some(8 * BATCH_GROUP)
    for _ in companion:
        pass

    @pl.when((solve_chunk == n_chunks - 1) & (step > 0))
    def _():
        hout_ref[...] = h_scr[...]


def _mlstm_stages(qk_ref, p_ref, c0_ref, n0_ref, m0_ref, bif_ref, lnw_ref, lnb_ref,
                  h_ref, cout_ref, nout_ref, mout_ref, c_scr, n_scr, m_scr, *, chunk, first, last):
    L = chunk
    d = D_MLSTM
    nh = MLSTM_HEADS

    @pl.when(first)
    def _():
        c_scr[...] = c0_ref[...]
        n_scr[...] = n0_ref[...]
        m_scr[...] = m0_ref[...]

    yield

    ti =lax.broadcasted_iota(jnp.int32, (L, L), 0)
    tj = lax.broadcasted_iota(jnp.int32, (L, L), 1)
    causal = tj <= ti
    ltri = causal.astype(BF16)
    utri = (ti <= tj).astype(BF16)
    sel = (lax.broadcasted_iota(jnp.int32, (8, LANES), 0)
           == lax.broadcasted_iota(jnp.int32, (8, LANES), 1)).astype(BF16)
    lane_g = lax.broadcasted_iota(jnp.int32, (L, LANES), 1)
    is_f = (lane_g >= nh) & (lane_g < 2 * nh)

    rows = []
    for bi in range(BATCH_GROUP):
        gates = p_ref[bi, :, 2 * d:] + bif_ref[...]
        glog = jnp.where(is_f, -_softplus(-gates), gates)
        parts = _split3(glog)
        b_cols = sum(_dot(ltri, t) for t in parts)
        g_rows = sum(_dot_nt(sel, t) for t in parts)
        b_rows = sum(_dot(t, utri) for t in _split3(g_rows))
        rows.append(dict(q=qk_ref[bi, :, 0:d], k=qk_ref[bi, :, d:2 * d], v=p_ref[bi, :, 0:d],
                         o=_sigmoid(p_ref[bi, :, d:2 * d]),
                         glog=glog, b_cols=b_cols, g_rows=g_rows, b_rows=b_rows))
        yield

    units = [(bi, h) for bi in range(BATCH_GROUP) for h in range(nh)]
    idx = range(len(units))
    head = lambda h: slice(h * MLSTM_HEAD_DIM, (h + 1) * MLSTM_HEAD_DIM)
    q = [rows[bi]["q"][:, head(h)] for bi, h in units]
    k = [rows[bi]["k"][:, head(h)] for bi, h in units]
    v = [rows[bi]["v"][:, head(h)] for bi, h in units]
    q_b = [t.astype(BF16) for t in q]
    k_b = [t.astype(BF16) for t in k]
    c_st = [c_scr[bi, h] for bi, h in units]
    n_st = [n_scr[bi, h:h + 1, :] for bi, h in units]
    m_prev = [m_scr[bi, :, h:h + 1] for bi, h in units]
    qk_t = [_dot_nt(q_b[u], k_b[u]) for u in idx]
    yield
    qc = [_dot_nt(q_b[u], c_st[u].astype(BF16)) for u in idx]
    yield
    s, w_inter, m_t, wk, carry, m_next = [], [], [], [], [], []
    for u, (bi, h) in enumerate(units):
        t = rows[bi]
        ig_row = t["g_rows"][h:h + 1, :]
        b_row = t["b_rows"][nh + h:nh + h + 1, :]
        ig_col = t["glog"][:, h:h + 1]
        b_col = t["b_cols"][:, nh + h:nh + h + 1]
        log_w = jnp.where(causal, b_col - b_row + ig_row, -jnp.inf)
        inter = b_col + m_prev[u]
        m_now = jnp.maximum(inter, jnp.max(log_w, axis=-1, keepdims=True))
        s.append(qk_t[u] * jnp.exp(log_w - m_now))
        w_inter.append(jnp.exp(inter - m_now))
        m_t.append(m_now)
        b_last = b_col[L - 1:L, :]
        g_col = b_last - b_col + ig_col
        m_new = jnp.maximum(b_last + m_prev[u], jnp.max(g_col, axis=0, keepdims=True))
        carry.append(jnp.exp(b_last + m_prev[u] - m_new))
        wk.append(jnp.exp(g_col - m_new))
        m_next.append(m_new)
        if u % 2 == 1:
            yield
    sv = [_dot(s[u].astype(BF16), v[u].astype(BF16)) for u in idx]
    yield
    upd = [_dot_tn((wk[u] * v[u]).astype(BF16), k_b[u]) for u in idx]
    yield
    qn =[jnp.sum(q[u] * n_st[u], axis=-1, keepdims=True) for u in idx]
    s_sum = [jnp.sum(s[u], axis=-1, keepdims=True) for u in idx]
    den = [jnp.maximum(jnp.abs(w_inter[u] * qn[u] + s_sum[u]), jnp.exp(-m_t[u])) for u in idx]
    yield
    hb = [rows[bi]["o"][:, head(h)] * ((w_inter[u] * qc[u] + sv[u]) / den[u]) for u, (bi, h) in enumerate(units)]
    yield
    mu = [jnp.mean(hb[u], axis=-1, keepdims=True) for u in idx]
    dev = [hb[u] - mu[u] for u in idx]
    var = [jnp.mean(dev[u] * dev[u], axis=-1, keepdims=True) for u in idx]
    yield
    n_new = [carry[u] * n_st[u] + jnp.sum(wk[u] * k[u], axis=0, keepdims=True) for u in idx]
    for u, (bi, h) in enumerate(units):
        sl = head(h)
        h_ref[bi, :, sl] = (dev[u] * lax.rsqrt(var[u] + LN_EPS) * lnw_ref[:, sl] + lnb_ref[:, sl]).astype(h_ref.dtype)
        c_scr[bi, h] = carry[u] * c_st[u] + upd[u]
        n_scr[bi, h:h + 1, :] = n_new[u]
        m_scr[bi, :, h:h + 1] = m_next[u]
        if u % 4 == 3:
            yield

    @pl.when(last)
    def _():
        cout_ref[...] = c_scr[...]
        nout_ref[...] = n_scr[...]
        mout_ref[...] = m_scr[...]


RW_REFS = (12, 2, 9)
ML_REFS = (8, 4, 3)


def _recurrent_kernel(*refs, chunk, n_chunks, n_items):
    groups, pos = [], 0
    for n_rw, n_ml in zip(RW_REFS, ML_REFS):
        groups.append((refs[pos:pos + n_rw], refs[pos + n_rw:pos + n_rw + n_ml]))
        pos += n_rw + n_ml
    rw_refs = [r for rw_part, _ in groups for r in rw_part]
    ml_refs = [r for _, ml_part in groups for r in ml_part]
    step = pl.program_id(0)
    solve_chunk = jnp.maximum(step - 1, 0) % n_chunks
    ml = _mlstm_stages(*ml_refs, chunk=chunk, first=solve_chunk == 0,
                       last=(solve_chunk == n_chunks - 1) & (step > 0))
    next(ml)
    _rwkv_body(*rw_refs, chunk=chunk, n_chunks=n_chunks, n_items=n_items, companion=ml)


def _recurrent(p_rw, shift_prev, h0_bd, qk, p_vo, c0, n0, m0, w, chunk):
    batch, t_len, _ = p_rw.shape
    n_chunks = t_len // chunk
    bg = BATCH_GROUP
    hd = MLSTM_HEAD_DIM
    n_items = (batch // bg) * n_chunks
    prep = lambda i: jnp.minimum(i, n_items - 1)
    solve = lambda i: jnp.maximum(i - 1, 0)
    prep_seq = lambda i: (prep(i) // n_chunks, prep(i) % n_chunks, 0)
    solve_seq = lambda i: (solve(i) // n_chunks, solve(i) % n_chunks, 0)
    vec = lambda n: pl.BlockSpec((1, n), lambda i: (0, 0))
    group = lambda *dims: pl.BlockSpec((bg,) + dims, lambda i: (solve(i) // n_chunks,) + (0,) * len(dims))
    rw_state = group(RWKV_PAIRS, LANES, LANES)
    ml_states = [group(MLSTM_HEADS, hd, hd), group(MLSTM_HEADS, hd), group(1, MLSTM_HEADS)]
    wide = lambda rows, dtype: pltpu.VMEM((bg, rows, D_RWKV), dtype)
    kern = functools.partial(_recurrent_kernel, chunk=chunk, n_chunks=n_chunks, n_items=n_items)
    return pl.pallas_call(
        kern,
        grid=(n_items + 1,),
        in_specs=[
            pl.BlockSpec((bg, chunk, RWKV_PROJ), prep_seq),
            pl.BlockSpec((bg, 1, RWKV_PROJ), lambda i: (prep(i) // n_chunks, 0, 0)),
            rw_state,
            vec(RWKV_PROJ), vec(D_RWKV), vec(D_RWKV), vec(D_RWKV), vec(D_RWKV), vec(D_RWKV), vec(D_RWKV),
            vec(D_RWKV),
            pl.BlockSpec((LORA_ALL, 3 * D_RWKV), lambda i: (0, 0)),
            pl.BlockSpec((bg, chunk, 2 * D_MLSTM), solve_seq),
            pl.BlockSpec((bg, chunk, VO_COLS), solve_seq),
            *ml_states,
            vec(LANES), vec(D_MLSTM), vec(D_MLSTM),
        ],
        out_specs=[
            pl.BlockSpec((bg, chunk, D_RWKV), solve_seq), rw_state,
            pl.BlockSpec((bg, chunk, D_MLSTM), solve_seq), *ml_states,
        ],
        out_shape=[
            jax.ShapeDtypeStruct((batch, t_len, D_RWKV), BF16),
            jax.ShapeDtypeStruct((batch, RWKV_PAIRS, LANES, LANES), F32),
            jax.ShapeDtypeStruct((batch, t_len, D_MLSTM), BF16),
            jax.ShapeDtypeStruct((batch, MLSTM_HEADS, hd, hd), F32),
            jax.ShapeDtypeStruct((batch, MLSTM_HEADS, hd), F32),
            jax.ShapeDtypeStruct((batch, 1, MLSTM_HEADS), F32),
        ],
        scratch_shapes=[
            pltpu.VMEM((bg, RWKV_PAIRS, LANES, LANES), F32),
            pltpu.VMEM((bg, 8, RWKV_PROJ), F32),
            wide(2 * chunk, BF16), wide(2 * chunk, BF16), wide(2 * chunk, BF16), wide(chunk, BF16),
            wide(chunk, F32), wide(chunk, F32), wide(1, F32),
            pltpu.VMEM((bg, MLSTM_HEADS, hd, hd), F32),
            pltpu.VMEM((bg, MLSTM_HEADS, hd), F32),
            pltpu.VMEM((bg, 1, MLSTM_HEADS), F32),
        ],
        compiler_params=_params(("arbitrary",)),
        name="rwkv7_mlstm_recurrent",
    )(p_rw, shift_prev, h0_bd, w["mu"], w["w0"], w["a0"], w["k_k"], w["k_a"], w["r_k"], w["rw_ln_w"],
      w["rw_ln_b"], w["w_lora"], qk, p_vo, c0, n0, m0, w["b_if"], w["ml_ln_w"], w["ml_ln_b"])


def _tail_kernel(x_ref, ya_ref, hb_ref, mq_ref, gate_ref, mk_ref, mv_ref, wpa_ref, wpb_ref, wpc_ref, wout_ref,
                 lnw_ref, lnb_ref, o_ref, *, rows):
    n_seg = mk_ref.shape[0]
    d = D_MODEL
    head = lambda h: slice(h * MEM_HEAD_DIM, (h + 1) * MEM_HEAD_DIM)
    units = [(b, h) for b in range(n_seg) for h in range(MEM_HEADS)]
    idx = range(len(units))
    mk = [mk_ref[b].astype(BF16) for b in range(n_seg)]
    mv = [mv_ref[b].astype(BF16) for b in range(n_seg)]
    s = [_dot_nt(mq_ref[b * rows:(b + 1) * rows, head(h)], mk[b][:, head(h)]) * (MEM_HEAD_DIM ** -0.5)
         for b, h in units]
    y_a = _sigmoid(gate_ref[:, 0:d]) * _dot(ya_ref[...], wpa_ref[...])
    top = [jnp.max(s[u], axis=-1, keepdims=True) for u in idx]
    e = [jnp.exp(s[u] - top[u]) for u in idx]
    y_b = _sigmoid(gate_ref[:, d:2 * d]) * _dot(hb_ref[...], wpb_ref[...])
    den = [jnp.sum(e[u], axis=-1, keepdims=True) for u in idx]
    prob = [(e[u] / den[u]).astype(BF16) for u in idx]
    out = [_dot(prob[u], mv[b][:, head(h)]).astype(BF16) for u, (b, h) in enumerate(units)]
    attn = [jnp.concatenate(out[b * MEM_HEADS:(b + 1) * MEM_HEADS], axis=-1) for b in range(n_seg)]
    attn = attn[0] if n_seg == 1 else jnp.concatenate(attn, axis=0)
    y_c = _sigmoid(gate_ref[:, 2 * d:3 * d]) * _dot(attn, wpc_ref[...])
    mixed = y_a + y_b + y_c
    u = _bdot(mixed, wout_ref[...])
    o_ref[...] = _layer_norm(DN_ALPHA * x_ref[...] + u, lnw_ref[...], lnb_ref[...])


def _tail(x, ya, hb, p_mq, p_gate, mem_k, mem_v, w, rows, batches):
    n = x.shape[0]
    tm = rows * batches
    tiles_per_batch = (n // mem_k.shape[0]) // rows if batches == 1 else 1
    row = lambda i: (i, 0)
    mem = pl.BlockSpec((batches, N_MEM, D_MEM), lambda i: (i // tiles_per_batch, 0, 0))
    tok = lambda cols: pl.BlockSpec((tm, cols), row)
    return pl.pallas_call(
        functools.partial(_tail_kernel, rows=rows),
        grid=(n // tm,),
        in_specs=[
            tok(D_MODEL), tok(D_RWKV), tok(D_MLSTM), tok(D_MEM), tok(3 * D_MODEL), mem, mem,
            _const_spec((D_RWKV, D_MODEL)), _const_spec((D_MLSTM, D_MODEL)), _const_spec((D_MEM, D_MODEL)),
            _const_spec((D_MODEL, D_MODEL)), _const_spec((1, D_MODEL)), _const_spec((1, D_MODEL)),
        ],
        out_specs=tok(D_MODEL),
        out_shape=jax.ShapeDtypeStruct((n, D_MODEL), F32),
        compiler_params=_params(("arbitrary",)),
        name="attn_merge_ln",
    )(x, ya, hb, p_mq, p_gate, mem_k, mem_v, w["w_pa"], w["w_pb"], w["w_pc"], w["w_out"], w["ln1_w"],
      w["ln1_b"])


def _moe_kernel(x_ref, wr_ref, rb_ref, w1_ref, w3_ref, w2_ref, ws1_ref, ws3_ref, ws2_ref, lnw_ref, lnb_ref, o_ref):
    x = x_ref[...]
    xb = x.astype(BF16)
    group = 8
    width = group * EXPERT_HIDDEN
    n_slabs = N_EXPERTS // group

    def up_project(s):
        cols = slice(s * width, (s + 1) * width)
        return _dot(xb, w1_ref[:, cols]), _dot(xb, w3_ref[:, cols])

    logits = _dot(xb, wr_ref[...])
    ahead = up_project(0)
    scores = _sigmoid(logits)
    lane = lax.broadcasted_iota(jnp.int32, scores.shape, 1)
    work = jnp.where(lane < N_EXPERTS, scores + rb_ref[...], -jnp.inf)
    chosen = jnp.zeros(scores.shape, dtype=jnp.bool_)
    for _ in range(TOP_K):
        best = jnp.max(work, axis=-1, keepdims=True)
        first = jnp.min(jnp.where(work == best, lane, LANES), axis=-1, keepdims=True)
        pick = lane == first
        chosen = chosen | pick
        work = jnp.where(pick, -jnp.inf, work)
    sel = jnp.where(chosen, scores, 0.0)
    gates = sel / jnp.sum(sel, axis=-1, keepdims=True) * ROUTED_SCALE

    acc = _dot((_silu(_dot(xb, ws1_ref[...])) * _dot(xb, ws3_ref[...])).astype(BF16), ws2_ref[...])
    for s in range(n_slabs):
        cols = slice(s * width, (s + 1) * width)
        h1, h3 = ahead
        if s + 1 < n_slabs:
            ahead = up_project(s + 1)
        hidden = _silu(h1) * h3
        gated = jnp.concatenate(
            [hidden[:, e * EXPERT_HIDDEN:(e + 1) * EXPERT_HIDDEN] * gates[:, s * group + e:s * group + e + 1]
             for e in range(group)], axis=-1)
        acc = acc + _dot(gated.astype(BF16), w2_ref[cols, :])
    o_ref[...] = _layer_norm(DN_ALPHA * x + acc, lnw_ref[...], lnb_ref[...])


def _moe(x, w, tm):
    n = x.shape[0]
    row = lambda i: (i, 0)
    hid = N_EXPERTS * EXPERT_HIDDEN
    return pl.pallas_call(
        _moe_kernel,
        grid=(n // tm,),
        in_specs=[
            pl.BlockSpec((tm, D_MODEL), row),
            _const_spec((D_MODEL, LANES)), _const_spec((1, LANES)),
            _const_spec((D_MODEL, hid)), _const_spec((D_MODEL, hid)), _const_spec((hid, D_MODEL)),
            _const_spec((D_MODEL, EXPERT_HIDDEN)), _const_spec((D_MODEL, EXPERT_HIDDEN)),
            _const_spec((EXPERT_HIDDEN, D_MODEL)),
            _const_spec((1, D_MODEL)), _const_spec((1, D_MODEL)),
        ],
        out_specs=pl.BlockSpec((tm, D_MODEL), row),
        out_shape=jax.ShapeDtypeStruct((n, D_MODEL), F32),
        compiler_params=_params(("arbitrary",)),
        name="moe_ln",
    )(x, w["w_router"], w["router_bias"], w["w_e1"], w["w_e3"], w["w_e2"], w["w_s1"], w["w_s3"], w["w_s2"],
      w["ln2_w"], w["ln2_b"])


def _prep_weights(w_in, b_in, rwkv_mu, rwkv_w0, rwkv_w2, rwkv_a0, rwkv_a2, rwkv_g2, rwkv_k_k, rwkv_k_a,
                  rwkv_r_k, rwkv_ln_w, rwkv_ln_b, mlstm_conv_w, mlstm_conv_b, mlstm_b_i, mlstm_b_f,
                  mlstm_ln_w, mlstm_ln_b, w_mem_kv, w_pa, w_pb, w_pc, w_out, ln1_w, ln1_b,
                  w_router, router_bias, w_e1, w_e3, w_e2, w_s1, w_s3, w_s2, ln2_w, ln2_b):
    off_ml = RWKV_PROJ
    off_if = off_ml + 4 * D_MLSTM
    off_mq = off_if + 2 * MLSTM_HEADS
    pad_if = LANES - 2 * MLSTM_HEADS
    row = lambda t: t.reshape(1, -1)
    hid = N_EXPERTS * EXPERT_HIDDEN
    w_lora = jnp.zeros((LORA_ALL, 3 * D_RWKV), F32)
    w_lora = w_lora.at[0:W_LORA, 0:D_RWKV].set(rwkv_w2)
    w_lora = w_lora.at[W_LORA:W_LORA + A_LORA, D_RWKV:2 * D_RWKV].set(rwkv_a2)
    w_lora = w_lora.at[W_LORA + A_LORA:, 2 * D_RWKV:].set(rwkv_g2)
    return dict(
        w_rw=w_in[:, :off_ml].astype(BF16), b_rw=row(b_in[:off_ml]),
        w_qk=w_in[:, off_ml:off_ml + 2 * D_MLSTM].astype(BF16), b_qk=row(b_in[off_ml:off_ml + 2 * D_MLSTM]),
        w_vo=jnp.pad(w_in[:, off_ml + 2 * D_MLSTM:off_mq], ((0, 0), (0, pad_if))).astype(BF16),
        b_vo=row(jnp.pad(b_in[off_ml + 2 * D_MLSTM:off_mq], (0, pad_if))),
        w_mq=w_in[:, off_mq:off_mq + D_MEM].astype(BF16), b_mq=row(b_in[off_mq:off_mq + D_MEM]),
        w_gate=w_in[:, off_mq + D_MEM:].astype(BF16), b_gate=row(b_in[off_mq + D_MEM:]),
        w_mem_k=w_mem_kv[:, :D_MEM].astype(BF16), w_mem_v=w_mem_kv[:, D_MEM:].astype(BF16),
        mu=row(rwkv_mu), w0=row(rwkv_w0), a0=row(rwkv_a0), k_k=row(rwkv_k_k), k_a=row(rwkv_k_a),
        r_k=row(rwkv_r_k), rw_ln_w=row(rwkv_ln_w), rw_ln_b=row(rwkv_ln_b), w_lora=w_lora.astype(BF16),
        conv_w=mlstm_conv_w, conv_b=row(mlstm_conv_b),
        b_if=row(jnp.pad(jnp.concatenate([mlstm_b_i, mlstm_b_f]), (0, pad_if))),
        ml_ln_w=row(mlstm_ln_w), ml_ln_b=row(mlstm_ln_b),
        w_pa=w_pa.astype(BF16), w_pb=w_pb.astype(BF16), w_pc=w_pc.astype(BF16), w_out=w_out.astype(BF16),
        ln1_w=row(ln1_w), ln1_b=row(ln1_b),
        w_router=jnp.pad(w_router, ((0, 0), (0, LANES - N_EXPERTS))).astype(BF16),
        router_bias=row(jnp.pad(router_bias, (0, LANES - N_EXPERTS))),
        w_e1=jnp.transpose(w_e1.astype(BF16), (1, 0, 2)).reshape(D_MODEL, hid),
        w_e3=jnp.transpose(w_e3.astype(BF16), (1, 0, 2)).reshape(D_MODEL, hid),
        w_e2=w_e2.reshape(hid, D_MODEL).astype(BF16),
        w_s1=w_s1.astype(BF16), w_s3=w_s3.astype(BF16), w_s2=w_s2.astype(BF16),
        ln2_w=row(ln2_w), ln2_b=row(ln2_b),
    )


def _pair_state(wkv):
    bsz = wkv.shape[0]
    s = wkv.reshape(bsz, RWKV_PAIRS, 2, RWKV_HEAD_DIM, RWKV_HEAD_DIM)
    z = jnp.zeros_like(s[:, :, 0])
    top = jnp.concatenate([s[:, :, 0], z], axis=-1)
    bot = jnp.concatenate([z, s[:, :, 1]], axis=-1)
    return jnp.concatenate([top, bot], axis=-2)


def _unpair_state(h_bd):
    bsz = h_bd.shape[0]
    hd = RWKV_HEAD_DIM
    heads = jnp.stack([h_bd[:, :, :hd, :hd], h_bd[:, :, hd:, hd:]], axis=2)
    return heads.reshape(bsz, RWKV_HEADS, hd, hd)


def _trunk(x3, mem_k, mem_v, shift_prev, wkv0, conv_prev, c0, n0, m0, w, chunk, tm_proj, tail_rows, tail_batches,
           tm_moe):
    bsz, t_len, _ = x3.shape
    n = bsz * t_len
    x = x3.reshape(n, D_MODEL)
    p_rw = _matmul_bias(x, w["w_rw"], w["b_rw"], tm_proj, RWKV_PROJ)
    qk, qk_tail, p_vo = _mlstm_proj(x, w["w_qk"], w["b_qk"], w["conv_w"], w["conv_b"], conv_prev, w["w_vo"],
                                    w["b_vo"], tm_proj)
    p_mq = _matmul_bias(x, w["w_mq"], w["b_mq"], tm_proj, D_MEM, BF16)
    p_gate = _matmul_bias(x, w["w_gate"], w["b_gate"], tm_proj, 3 * D_MODEL)

    p_rw3 = p_rw.reshape(bsz, t_len, RWKV_PROJ)
    ya, h_bd, hb, c_t, n_t, m_t = _recurrent(
        p_rw3, shift_prev, _pair_state(wkv0), qk.reshape(bsz, t_len, 2 * D_MLSTM), p_vo.reshape(bsz, t_len, VO_COLS),
        c0, n0, m0.reshape(bsz, 1, MLSTM_HEADS), w, chunk)
    x1 = _tail(x, ya.reshape(n, D_RWKV), hb.reshape(n, D_MLSTM), p_mq, p_gate, mem_k, mem_v, w, tail_rows,
               tail_batches)
    y = _moe(x1, w, tm_moe)

    new_shift = p_rw3[:, t_len - 1:, :]
    new_conv = qk_tail[:, 8 - (CONV_W - 1):, :]
    states = (new_shift, _unpair_state(h_bd), new_conv, c_t, n_t, m_t.reshape(bsz, MLSTM_HEADS))
    return y.reshape(bsz, t_len, D_MODEL), states


def kernel(x_prompt, x_sample, state_rwkv_shift, state_rwkv_wkv, state_mlstm_conv, state_mlstm_c, state_mlstm_n, state_mlstm_m, cache_mem_k, cache_mem_v, mem_prompt, w_in, b_in, rwkv_mu, rwkv_w0, rwkv_w2, rwkv_a0, rwkv_a2, rwkv_g2, rwkv_k_k, rwkv_k_a, rwkv_r_k, rwkv_ln_w, rwkv_ln_b, mlstm_conv_w, mlstm_conv_b, mlstm_b_i, mlstm_b_f, mlstm_ln_w, mlstm_ln_b, w_mem_kv, w_pa, w_pb, w_pc, w_out, ln1_w, ln1_b, w_router, router_bias, w_e1, w_e3, w_e2, w_s1, w_s3, w_s2, ln2_w, ln2_b):
    weights = (w_in, b_in, rwkv_mu, rwkv_w0, rwkv_w2, rwkv_a0, rwkv_a2, rwkv_g2, rwkv_k_k, rwkv_k_a, rwkv_r_k,
               rwkv_ln_w, rwkv_ln_b, mlstm_conv_w, mlstm_conv_b, mlstm_b_i, mlstm_b_f, mlstm_ln_w, mlstm_ln_b,
               w_mem_kv, w_pa, w_pb, w_pc, w_out, ln1_w, ln1_b, w_router, router_bias, w_e1, w_e3, w_e2,
               w_s1, w_s3, w_s2, ln2_w, ln2_b)
    w = _prep_weights(*(t[0] for t in weights))
    bp, t_p, _ = x_prompt.shape
    bs, t_s, _ = x_sample.shape
    hd = MLSTM_HEAD_DIM

    mem_flat = mem_prompt.reshape(bp * N_MEM, D_MODEL)
    zero_bias = jnp.zeros((1, D_MEM), F32)
    mk = _matmul_bias(mem_flat, w["w_mem_k"], zero_bias, bp * N_MEM, D_MEM).reshape(bp, N_MEM, D_MEM)
    mv = _matmul_bias(mem_flat, w["w_mem_v"], zero_bias, bp * N_MEM, D_MEM).reshape(bp, N_MEM, D_MEM)

    y_p, st_p = _trunk(
        x_prompt, mk, mv,
        jnp.zeros((bp, 1, RWKV_PROJ), F32), jnp.zeros((bp, RWKV_HEADS, RWKV_HEAD_DIM, RWKV_HEAD_DIM), F32),
        jnp.zeros((bp, CONV_W - 1, 2 * D_MLSTM), F32), jnp.zeros((bp, MLSTM_HEADS, hd, hd), F32),
        jnp.zeros((bp, MLSTM_HEADS, hd), F32), jnp.zeros((bp, MLSTM_HEADS), F32),
        w, chunk=min(MLSTM_CHUNK, t_p), tm_proj=1024, tail_rows=512, tail_batches=1, tm_moe=512)
    y_s, st_s = _trunk(
        x_sample, cache_mem_k[0].reshape(bs, N_MEM, D_MEM), cache_mem_v[0].reshape(bs, N_MEM, D_MEM),
        state_rwkv_shift[0], state_rwkv_wkv[0], state_mlstm_conv[0], state_mlstm_c[0], state_mlstm_n[0],
        state_mlstm_m[0],
        w, chunk=min(MLSTM_CHUNK, t_s), tm_proj=bs * t_s, tail_rows=t_s, tail_batches=4, tm_moe=bs * t_s)

    lead = lambda t: t[None]
    mem_shape = (1, bp, N_MEM, MEM_HEADS, MEM_HEAD_DIM)
    return (y_p, y_s, *(lead(t) for t in st_p), mk.reshape(mem_shape), mv.reshape(mem_shape),
            *(lead(t) for t in st_s))
```

```python
import functools
import itertools

import jax
import jax.numpy as jnp
from jax import lax
from jax.experimental import pallas as pl
from jax.experimental.pallas import tpu as pltpu

F32 = jnp.float32
BF16 = jnp.bfloat16

D_MODEL = 1024
DEPTH = 1
D_RWKV = 1024
RWKV_HEAD_DIM = 64
RWKV_HEADS = 16
W_LORA = 64
A_LORA = 64
G_LORA = 128
LORA_ALL = W_LORA + A_LORA + G_LORA
RWKV_PROJ = 3 * D_RWKV + LORA_ALL
RWKV_GN_EPS = 64e-5
DECAY_SCALE_LOG2 = 0.6065306597126334 * 1.4426950408889634
D_MLSTM = 1024
MLSTM_HEADS = 4
MLSTM_HEAD_DIM = 256
CONV_W = 4
MLSTM_CHUNK = 64
N_MEM = 256
MEM_HEADS = 4
MEM_HEAD_DIM = 256
D_MEM = 1024
N_EXPERTS = 32
TOP_K = 4
EXPERT_HIDDEN = 128
ROUTED_SCALE = 2.5
DN_ALPHA = (2 * DEPTH) ** 0.25
LN_EPS = 1e-5

LANES = 128
RWKV_PAIRS = D_RWKV // LANES
VO_COLS = 2 * D_MLSTM + LANES
VMEM_LIMIT = 56 * 1024 * 1024
BATCH_GROUP = 2


def _dot(a, b):
    return jnp.dot(a, b, preferred_element_type=F32)


def _dot_nt(a, b):
    return lax.dot_general(a, b, (((1,), (1,)), ((), ())), preferred_element_type=F32)


def _dot_tn(a, b):
    return lax.dot_general(a, b, (((0,), (0,)), ((), ())), preferred_element_type=F32)


def _split(a):
    hi = a.astype(BF16)
    return hi, (a - hi.astype(F32)).astype(BF16)


def _split3(a):
    hi = a.astype(BF16)
    rest = a - hi.astype(F32)
    mid = rest.astype(BF16)
    return hi, mid, (rest - mid.astype(F32)).astype(BF16)


def _bdot(a, b):
    return _dot(a.astype(BF16), b.astype(BF16))


def _sigmoid(x):
    return 0.5 * jnp.tanh(0.5 * x) + 0.5


def _softplus(x):
    return jnp.maximum(x, 0.0) + jnp.log(1.0 + jnp.exp(-jnp.abs(x)))


def _silu(x):
    return x * _sigmoid(x)


def _layer_norm(x, w, b):
    mu = jnp.mean(x, axis=-1, keepdims=True)
    d = x - mu
    var = jnp.mean(d * d, axis=-1, keepdims=True)
    return d * lax.rsqrt(var + LN_EPS) * w + b


def _params(semantics):
    return pltpu.CompilerParams(dimension_semantics=semantics, vmem_limit_bytes=VMEM_LIMIT)


def _const_spec(shape):
    zeros = (0,) * len(shape)
    return pl.BlockSpec(shape, lambda *_: zeros, pipeline_mode=pl.Buffered(1))


def _mm_kernel(x_ref, w_ref, b_ref, o_ref):
    o_ref[...] = (_dot(x_ref[...].astype(BF16), w_ref[...]) + b_ref[...]).astype(o_ref.dtype)


def _matmul_bias(x, w, b, tm, tn, out_dtype=F32):
    n, k = x.shape
    nc = w.shape[1]
    resident = dict(pipeline_mode=pl.Buffered(1)) if nc == tn else {}
    return pl.pallas_call(
        _mm_kernel,
        grid=(nc // tn, n // tm),
        in_specs=[
            pl.BlockSpec((tm, k), lambda j, i: (i, 0)),
            pl.BlockSpec((k, tn), lambda j, i: (0, j), **resident),
            pl.BlockSpec((1, tn), lambda j, i: (0, j), **resident),
        ],
        out_specs=pl.BlockSpec((tm, tn), lambda j, i: (i, j)),
        out_shape=jax.ShapeDtypeStruct((n, nc), out_dtype),
        compiler_params=_params(("arbitrary", "arbitrary")),
        name="matmul_bias",
    )(x, w, b)


def _mm2_kernel(x_ref, w1_ref, b1_ref, w2_ref, b2_ref, o1_ref, o2_ref):
    xb = x_ref[...].astype(BF16)
    o1_ref[...] = (_dot(xb, w1_ref[...]) + b1_ref[...]).astype(o1_ref.dtype)
    o2_ref[...] = (_dot(xb, w2_ref[...]) + b2_ref[...]).astype(o2_ref.dtype)


def _matmul_bias2(x, w1, b1, w2, b2, tm, dtype1, dtype2):
    n, k = x.shape
    n1, n2 = w1.shape[1], w2.shape[1]
    row = lambda i: (i, 0)
    return pl.pallas_call(
        _mm2_kernel,
        grid=(n // tm,),
        in_specs=[pl.BlockSpec((tm, k), row), _const_spec((k, n1)), _const_spec((1, n1)), _const_spec((k, n2)),
                  _const_spec((1, n2))],
        out_specs=[pl.BlockSpec((tm, n1), row), pl.BlockSpec((tm, n2), row)],
        out_shape=[jax.ShapeDtypeStruct((n, n1), dtype1), jax.ShapeDtypeStruct((n, n2), dtype2)],
        compiler_params=_params(("arbitrary",)),
        name="matmul_bias_pair",
    )(x, w1, b1, w2, b2)


def _mlstm_proj_kernel(x_ref, w_ref, b_ref, cw_ref, cb_ref, prev_ref, w2_ref, b2_ref, o_ref, tail_ref, o2_ref, carry,
                       *, rows, tiles_per_batch):
    row_tile = pl.program_id(0)
    n_seg = prev_ref.shape[0]
    tm, tn = o_ref.shape
    n2 = o2_ref.shape[1]
    pad = jnp.zeros((8 - (CONV_W - 1), LANES), F32)
    row8 = lax.broadcasted_iota(jnp.int32, (8, LANES), 0)

    if n_seg == 1:
        @pl.when(row_tile % tiles_per_batch == 0)
        def _():
            carry[0:8 - (CONV_W - 1), :] = jnp.zeros((8 - (CONV_W - 1), tn), F32)
            carry[8 - (CONV_W - 1):8, :] = prev_ref[0]

    xb = x_ref[...].astype(BF16)
    n_parts = 4
    q_step = tn // n_parts
    for part in range(n_parts):
        cs = slice(part * q_step, (part + 1) * q_step)
        o_ref[:, cs] = _dot(xb, w_ref[:, cs]) + b_ref[:, cs]
    for seg in range(n_seg):
        tail_ref[seg] = o_ref[(seg + 1) * rows - 8:(seg + 1) * rows, :]
    strip = min(rows, 64)

    def conv_strips(r_lo, r_hi):
        for r0 in reversed(range(r_lo, r_hi, strip)):
            for c0 in range(0, tn, LANES):
                cols = slice(c0, c0 + LANES)
                half_scale = 0.5 * MLSTM_HEAD_DIM ** -0.5 if c0 >= D_MLSTM else 0.5
                if n_seg > 1 and r0 % rows == 0:
                    head = jnp.concatenate([pad, prev_ref[r0 // rows, :, cols]], axis=0)
                elif r0 == 0:
                    head = carry[:, cols]
                else:
                    head = o_ref[r0 - 8:r0, cols]
                x = o_ref[r0:r0 + strip, cols]
                blocks = [head] + [x[r:r + 8, :] for r in range(0, strip, 8)]
                conv = cb_ref[:, cols] + cw_ref[CONV_W - 1:CONV_W, cols] * x
                for j in range(CONV_W - 1):
                    lag = CONV_W - 1 - j
                    turned = [pltpu.roll(blk, lag, 0) for blk in blocks]
                    shifted = jnp.concatenate([jnp.where(row8 < lag, turned[i], turned[i + 1])
                                               for i in range(len(blocks) - 1)], axis=0)
                    conv = conv + cw_ref[j:j + 1, cols] * shifted
                o_ref[r0:r0 + strip, cols] = conv * (half_scale * jnp.tanh(0.5 * conv) + half_scale)

    col_step = -(-n2 // (n_parts * LANES)) * LANES
    row_step = tm // n_parts
    for part in reversed(range(n_parts)):
        cs = slice(part * col_step, min((part + 1) * col_step, n2))
        o2_ref[:, cs] = _dot(xb, w2_ref[:, cs]) + b2_ref[:, cs]
        conv_strips(part * row_step, (part + 1) * row_step)

    if n_seg == 1:
        carry[...] = tail_ref[0]


def _mlstm_proj(x, w, b, conv_w, conv_b, conv_prev, w2, b2, tm):
    n, k = x.shape
    batch = conv_prev.shape[0]
    t_len = n // batch
    rows = min(tm, t_len)
    n_seg = tm // rows
    tiles_per_batch = t_len // rows
    nc, n2 = w.shape[1], w2.shape[1]
    seg = lambda i: (i // tiles_per_batch if n_seg == 1 else i, 0, 0)
    return pl.pallas_call(
        functools.partial(_mlstm_proj_kernel, rows=rows, tiles_per_batch=tiles_per_batch),
        grid=(n // tm,),
        in_specs=[
            pl.BlockSpec((tm, k), lambda i: (i, 0)),
            _const_spec((k, nc)), _const_spec((1, nc)), _const_spec((CONV_W, nc)), _const_spec((1, nc)),
            pl.BlockSpec((n_seg, CONV_W - 1, nc), seg),
            _const_spec((k, n2)), _const_spec((1, n2)),
        ],
        out_specs=[
            pl.BlockSpec((tm, nc), lambda i: (i, 0)),
            pl.BlockSpec((n_seg, 8, nc), seg),
            pl.BlockSpec((tm, n2), lambda i: (i, 0)),
        ],
        out_shape=[jax.ShapeDtypeStruct((n, nc), F32), jax.ShapeDtypeStruct((batch, 8, nc), F32),
                   jax.ShapeDtypeStruct((n, n2), F32)],
        scratch_shapes=[pltpu.VMEM((8, nc), F32)],
        compiler_params=_params(("arbitrary",)),
        name="mlstm_proj_conv",
    )(x, w, b, conv_w, conv_b, conv_prev, w2, b2)


def _rwkv_body(p_ref, prev_ref, h0_ref, mu_ref, w0_ref, a0_ref, kk_ref, ka_ref, rk_ref, lnw_ref, lnb_ref,
               wl_ref, y_ref, hout_ref, h_scr, xbuf, ar_scr, bk_scr, bkh_scr, v_scr, bonus_scr, g_scr, gamma_scr,
               *, chunk, n_chunks, n_items, companion):
    step = pl.program_id(0)
    L = chunk
    half = 2 * L
    prep_chunk = jnp.minimum(step, n_items - 1) % n_chunks
    solve_chunk = jnp.maximum(step - 1, 0) % n_chunks

    @pl.when(step == 0)
    def _():
        for ref in (ar_scr, bk_scr, bkh_scr, v_scr, bonus_scr, g_scr):
            ref[...] = jnp.zeros(ref.shape, ref.dtype)
        gamma_scr[...] = jnp.ones(gamma_scr.shape, F32)

    @pl.when(solve_chunk == 0)
    def _():
        h_scr[...] = h0_ref[...]

    @pl.when(prep_chunk == 0)
    def _():
        for bi in range(BATCH_GROUP):
            xbuf[bi, 0:1, :] = prev_ref[bi]

    ti = lax.broadcasted_iota(jnp.int32, (L, L), 0)
    tj = lax.broadcasted_iota(jnp.int32, (L, L), 1)
    ltri = (tj <= ti).astype(BF16)
    row_m = lax.broadcasted_iota(jnp.int32, (L, half), 0)
    col_m = lax.broadcasted_iota(jnp.int32, (L, half), 1)
    col_in = jnp.where(col_m >= L, col_m - L, col_m)
    strict = col_in < row_m
    incl = col_in <= row_m
    eye = (col_in == row_m).astype(F32)
    gi = lax.broadcasted_iota(jnp.int32, (LANES, LANES), 0)
    gj = lax.broadcasted_iota(jnp.int32, (LANES, LANES), 1)
    same_head = (gi >= RWKV_HEAD_DIM) == (gj >= RWKV_HEAD_DIM)
    head_ones = same_head.astype(BF16)
    first_nat = lax.broadcasted_iota(jnp.int32, (L, LANES), 1) < RWKV_HEAD_DIM
    first_sbs = col_m < L
    lane_l = lax.broadcasted_iota(jnp.int32, (L, LORA_ALL), 1)

    def diag(x, first):
        zero = jnp.zeros_like(x)
        return jnp.concatenate([jnp.where(first, x, zero), jnp.where(first, zero, x)], axis=0)

    def seg_sum(x):
        return _dot(x.astype(BF16), head_ones)

    units = [(bi, p) for bi in range(BATCH_GROUP) for p in range(RWKV_PAIRS)]
    idx = range(len(units))
    lane_of = lambda p: slice(p * LANES, (p + 1) * LANES)

    ar = [ar_scr[bi, :, lane_of(p)] for bi, p in units]
    bk = [bk_scr[bi, :, lane_of(p)] for bi, p in units]
    bkh = [bkh_scr[bi, :, lane_of(p)] for bi, p in units]
    v_b = [v_scr[bi, :, lane_of(p)] for bi, p in units]
    bonus = [bonus_scr[bi, :, lane_of(p)] for bi, p in units]
    gate = [g_scr[bi, :, lane_of(p)] for bi, p in units]
    gamma = [gamma_scr[bi, :, lane_of(p)] for bi, p in units]

    first_row = lax.broadcasted_iota(jnp.int32, (L, RWKV_PROJ), 0) == 0

    def prepare_row(bi):
        cur = p_ref[bi]
        prev = jnp.where(first_row, xbuf[bi, 0:1, :], pltpu.roll(cur, 1, 0))
        xbuf[bi, 0:1, :] = cur[L - 1:L, :]
        xr = cur + (prev - cur) * mu_ref[...]
        r_all = xr[:, 0:D_RWKV]
        k_all = xr[:, D_RWKV:2 * D_RWKV]
        v_all = xr[:, 2 * D_RWKV:3 * D_RWKV]
        slab = xr[:, 3 * D_RWKV:]
        act = jnp.where(lane_l < W_LORA, jnp.tanh(slab),
                        jnp.where(lane_l < W_LORA + A_LORA, slab, _sigmoid(slab)))
        yield
        lora = _bdot(act, wl_ref[...])
        yield
        lw = -DECAY_SCALE_LOG2 * _sigmoid(w0_ref[...] + lora[:, 0:D_RWKV])
        a_sig = _sigmoid(a0_ref[...] + lora[:, D_RWKV:2 * D_RWKV])
        lw_hi, lw_lo = _split(lw)
        yield
        kk0 = k_all * kk_ref[...]
        k2 = k_all * (1.0 + (a_sig - 1.0) * ka_ref[...])
        sums = jnp.concatenate([kk0 * kk0, r_all * k2 * rk_ref[...]], axis=0).astype(BF16)
        yield
        cum = _dot(ltri, lw_hi) + _dot(ltri, lw_lo)
        sums = jnp.concatenate([_dot(sums[:, lane_of(p)], head_ones) for p in range(RWKV_PAIRS)], axis=1)
        yield
        kk = kk0 * lax.rsqrt(jnp.maximum(sums[0:L], 1e-24))
        b_vec = kk * a_sig
        c_last = cum[L - 1:L, :]
        e_neg = jnp.exp2(-cum)
        e_end = jnp.exp2(c_last - cum)
        yield
        ar_scr[bi] = jnp.concatenate([-kk * jnp.exp2(cum - lw), r_all * jnp.exp2(cum)], axis=0).astype(BF16)
        bk_scr[bi] = jnp.concatenate([b_vec * e_neg, k2 * e_neg], axis=0).astype(BF16)
        yield
        bkh_scr[bi] = jnp.concatenate([b_vec * e_end, k2 * e_end], axis=0).astype(BF16)
        v_scr[bi] = v_all.astype(BF16)
        bonus_scr[bi] = sums[L:] * v_all
        g_scr[bi] = lora[:, 2 * D_RWKV:]
        gamma_scr[bi] = jnp.exp2(c_last)

    segments = itertools.chain(*(prepare_row(bi) for bi in range(BATCH_GROUP)))

    def prepare_some(count):
        for _ in range(count):
            next(segments, None)
            next(companion, None)

    n_sq = L.bit_length() - 1
    per_stage = 2
    bk_d = [jnp.concatenate([diag(bk[u][0:L], first_nat), diag(bk[u][L:], first_nat)], axis=0) for u in idx]
    v_d = [diag(v_b[u], first_nat) for u in idx]
    h_bd = [h_scr[bi, p] for bi, p in units]
    sc = [_dot_nt(ar[u], bk_d[u]) for u in idx]
    prepare_some(per_stage)
    arh = [_dot_nt(ar[u], h_bd[u].astype(BF16)) for u in idx]
    m_ab = [jnp.where(strict, sc[u][0:L, 0:half], 0.0) for u in idx]
    m_ak = [jnp.where(strict, sc[u][0:L, half:], 0.0).astype(BF16) for u in idx]
    n_rbk = [jnp.concatenate([jnp.where(incl, sc[u][L:, 0:half], 0.0),
                              jnp.where(incl, sc[u][L:, half:], 0.0)], axis=1).astype(BF16) for u in idx]
    prepare_some(per_stage)
    w0 = [arh[u][0:L] + _dot(m_ak[u], v_d[u]) for u in idx]
    inv = [eye + m_ab[u] for u in idx]
    m_b = [m_ab[u].astype(BF16) for u in idx]
    pw = [_dot(m_b[u], diag(m_b[u], first_sbs)) for u in idx]
    prepare_some(per_stage)
    for i in range(n_sq - 1):
        pw_d = [diag(pw[u].astype(BF16), first_sbs) for u in idx]
        if i + 1 < n_sq - 1:
            z = [_dot(jnp.concatenate([inv[u], pw[u]], axis=0).astype(BF16), pw_d[u]) for u in idx]
            inv = [inv[u] + z[u][0:L] for u in idx]
            pw = [z[u][L:] for u in idx]
        else:
            inv = [inv[u] + _dot(inv[u].astype(BF16), pw_d[u]) for u in idx]
        prepare_some(per_stage)
    c_b = [_dot(inv[u].astype(BF16), diag(w0[u].astype(BF16), first_nat)).astype(BF16) for u in idx]
    cv = [jnp.concatenate([c_b[u], v_b[u]], axis=0) for u in idx]
    y = [arh[u][L:] + _dot(n_rbk[u], jnp.concatenate([diag(c_b[u], first_nat), v_d[u]], axis=0)) for u in idx]
    for u, (bi, p) in enumerate(units):
        h_scr[bi, p] = h_bd[u] * gamma[u] + jnp.where(same_head, _dot_tn(cv[u], bkh[u]), 0.0)
    mean = [seg_sum(y[u]) * (1.0 / RWKV_HEAD_DIM) for u in idx]
    dev = [y[u] - mean[u] for u in idx]
    var = [seg_sum(dev[u] * dev[u]) * (1.0 / RWKV_HEAD_DIM) for u in idx]
    for u, (bi, p) in enumerate(units):
        s = lane_of(p)
        yn = dev[u] * lax.rsqrt(var[u] + RWKV_GN_EPS) * lnw_ref[:, s] + lnb_ref[:, s]
        y_ref[bi, :, s] = ((yn + bonus[u]) * gate[u]).astype(y_ref.dtype)
    prepare_some(8 * BATCH_GROUP)
    for _ in companion:
        pass

    @pl.when((solve_chunk == n_chunks - 1) & (step > 0))
    def _():
        hout_ref[...] = h_scr[...]


def _mlstm_stages(qk_ref, p_ref, c0_ref, n0_ref, m0_ref, bif_ref, lnw_ref, lnb_ref,
                  h_ref, cout_ref, nout_ref, mout_ref, c_scr, n_scr, m_scr, *, chunk, first, last):
    L = chunk
    d = D_MLSTM
    nh = MLSTM_HEADS

    @pl.when(first)
    def _():
        c_scr[...] = c0_ref[...]
        n_scr[...] = n0_ref[...]
        m_scr[...] = m0_ref[...]

    yield

    ti =lax.broadcasted_iota(jnp.int32, (L, L), 0)
    tj = lax.broadcasted_iota(jnp.int32, (L, L), 1)
    causal = tj <= ti
    ltri = causal.astype(BF16)
    utri = (ti <= tj).astype(BF16)
    sel = (lax.broadcasted_iota(jnp.int32, (8, LANES), 0)
           == lax.broadcasted_iota(jnp.int32, (8, LANES), 1)).astype(BF16)
    lane_g = lax.broadcasted_iota(jnp.int32, (L, LANES), 1)
    is_f = (lane_g >= nh) & (lane_g < 2 * nh)

    rows = []
    for bi in range(BATCH_GROUP):
        gates = p_ref[bi, :, 2 * d:] + bif_ref[...]
        glog = jnp.where(is_f, -_softplus(-gates), gates)
        parts = _split3(glog)
        b_cols = sum(_dot(ltri, t) for t in parts)
        g_rows = sum(_dot_nt(sel, t) for t in parts)
        b_rows = sum(_dot(t, utri) for t in _split3(g_rows))
        rows.append(dict(q=qk_ref[bi, :, 0:d], k=qk_ref[bi, :, d:2 * d], v=p_ref[bi, :, 0:d],
                         o=_sigmoid(p_ref[bi, :, d:2 * d]),
                         glog=glog, b_cols=b_cols, g_rows=g_rows, b_rows=b_rows))
        yield

    units = [(bi, h) for bi in range(BATCH_GROUP) for h in range(nh)]
    idx = range(len(units))
    head = lambda h: slice(h * MLSTM_HEAD_DIM, (h + 1) * MLSTM_HEAD_DIM)
    q = [rows[bi]["q"][:, head(h)] for bi, h in units]
    k = [rows[bi]["k"][:, head(h)] for bi, h in units]
    v = [rows[bi]["v"][:, head(h)] for bi, h in units]
    q_b = [t.astype(BF16) for t in q]
    k_b = [t.astype(BF16) for t in k]
    c_st = [c_scr[bi, h] for bi, h in units]
    n_st = [n_scr[bi, h:h + 1, :] for bi, h in units]
    m_prev = [m_scr[bi, :, h:h + 1] for bi, h in units]
    qk_t = [_dot_nt(q_b[u], k_b[u]) for u in idx]
    yield
    qc = [_dot_nt(q_b[u], c_st[u].astype(BF16)) for u in idx]
    yield
    s, w_inter, m_t, wk, carry, m_next = [], [], [], [], [], []
    for u, (bi, h) in enumerate(units):
        t = rows[bi]
        ig_row = t["g_rows"][h:h + 1, :]
        b_row = t["b_rows"][nh + h:nh + h + 1, :]
        ig_col = t["glog"][:, h:h + 1]
        b_col = t["b_cols"][:, nh + h:nh + h + 1]
        log_w = jnp.where(causal, b_col - b_row + ig_row, -jnp.inf)
        inter = b_col + m_prev[u]
        m_now = jnp.maximum(inter, jnp.max(log_w, axis=-1, keepdims=True))
        s.append(qk_t[u] * jnp.exp(log_w - m_now))
        w_inter.append(jnp.exp(inter - m_now))
        m_t.append(m_now)
        b_last = b_col[L - 1:L, :]
        g_col = b_last - b_col + ig_col
        m_new = jnp.maximum(b_last + m_prev[u], jnp.max(g_col, axis=0, keepdims=True))
        carry.append(jnp.exp(b_last + m_prev[u] - m_new))
        wk.append(jnp.exp(g_col - m_new))
        m_next.append(m_new)
        if u % 2 == 1:
            yield
    sv = [_dot(s[u].astype(BF16), v[u].astype(BF16)) for u in idx]
    yield
    upd = [_dot_tn((wk[u] * v[u]).astype(BF16), k_b[u]) for u in idx]
    yield
    qn =[jnp.sum(q[u] * n_st[u], axis=-1, keepdims=True) for u in idx]
    s_sum = [jnp.sum(s[u], axis=-1, keepdims=True) for u in idx]
    den = [jnp.maximum(jnp.abs(w_inter[u] * qn[u] + s_sum[u]), jnp.exp(-m_t[u])) for u in idx]
    yield
    hb = [rows[bi]["o"][:, head(h)] * ((w_inter[u] * qc[u] + sv[u]) / den[u]) for u, (bi, h) in enumerate(units)]
    yield
    mu = [jnp.mean(hb[u], axis=-1, keepdims=True) for u in idx]
    dev = [hb[u] - mu[u] for u in idx]
    var = [jnp.mean(dev[u] * dev[u], axis=-1, keepdims=True) for u in idx]
    yield
    n_new = [carry[u] * n_st[u] + jnp.sum(wk[u] * k[u], axis=0, keepdims=True) for u in idx]
    for u, (bi, h) in enumerate(units):
        sl = head(h)
        h_ref[bi, :, sl] = (dev[u] * lax.rsqrt(var[u] + LN_EPS) * lnw_ref[:, sl] + lnb_ref[:, sl]).astype(h_ref.dtype)
        c_scr[bi, h] = carry[u] * c_st[u] + upd[u]
        n_scr[bi, h:h + 1, :] = n_new[u]
        m_scr[bi, :, h:h + 1] = m_next[u]
        if u % 4 == 3:
            yield

    @pl.when(last)
    def _():
        cout_ref[...] = c_scr[...]
        nout_ref[...] = n_scr[...]
        mout_ref[...] = m_scr[...]


RW_REFS = (12, 2, 9)
ML_REFS = (8, 4, 3)


def _recurrent_kernel(*refs, chunk, n_chunks, n_items):
    groups, pos = [], 0
    for n_rw, n_ml in zip(RW_REFS, ML_REFS):
        groups.append((refs[pos:pos + n_rw], refs[pos + n_rw:pos + n_rw + n_ml]))
        pos += n_rw + n_ml
    rw_refs = [r for rw_part, _ in groups for r in rw_part]
    ml_refs = [r for _, ml_part in groups for r in ml_part]
    step = pl.program_id(0)
    solve_chunk = jnp.maximum(step - 1, 0) % n_chunks
    ml = _mlstm_stages(*ml_refs, chunk=chunk, first=solve_chunk == 0,
                       last=(solve_chunk == n_chunks - 1) & (step > 0))
    next(ml)
    _rwkv_body(*rw_refs, chunk=chunk, n_chunks=n_chunks, n_items=n_items, companion=ml)


def _recurrent(p_rw, shift_prev, h0_bd, qk, p_vo, c0, n0, m0, w, chunk):
    batch, t_len, _ = p_rw.shape
    n_chunks = t_len // chunk
    bg = BATCH_GROUP
    hd = MLSTM_HEAD_DIM
    n_items = (batch // bg) * n_chunks
    prep = lambda i: jnp.minimum(i, n_items - 1)
    solve = lambda i: jnp.maximum(i - 1, 0)
    prep_seq = lambda i: (prep(i) // n_chunks, prep(i) % n_chunks, 0)
    solve_seq = lambda i: (solve(i) // n_chunks, solve(i) % n_chunks, 0)
    vec = lambda n: pl.BlockSpec((1, n), lambda i: (0, 0))
    group = lambda *dims: pl.BlockSpec((bg,) + dims, lambda i: (solve(i) // n_chunks,) + (0,) * len(dims))
    rw_state = group(RWKV_PAIRS, LANES, LANES)
    ml_states = [group(MLSTM_HEADS, hd, hd), group(MLSTM_HEADS, hd), group(1, MLSTM_HEADS)]
    wide = lambda rows, dtype: pltpu.VMEM((bg, rows, D_RWKV), dtype)
    kern = functools.partial(_recurrent_kernel, chunk=chunk, n_chunks=n_chunks, n_items=n_items)
    return pl.pallas_call(
        kern,
        grid=(n_items + 1,),
        in_specs=[
            pl.BlockSpec((bg, chunk, RWKV_PROJ), prep_seq),
            pl.BlockSpec((bg, 1, RWKV_PROJ), lambda i: (prep(i) // n_chunks, 0, 0)),
            rw_state,
            vec(RWKV_PROJ), vec(D_RWKV), vec(D_RWKV), vec(D_RWKV), vec(D_RWKV), vec(D_RWKV), vec(D_RWKV),
            vec(D_RWKV),
            pl.BlockSpec((LORA_ALL, 3 * D_RWKV), lambda i: (0, 0)),
            pl.BlockSpec((bg, chunk, 2 * D_MLSTM), solve_seq),
            pl.BlockSpec((bg, chunk, VO_COLS), solve_seq),
            *ml_states,
            vec(LANES), vec(D_MLSTM), vec(D_MLSTM),
        ],
        out_specs=[
            pl.BlockSpec((bg, chunk, D_RWKV), solve_seq), rw_state,
            pl.BlockSpec((bg, chunk, D_MLSTM), solve_seq), *ml_states,
        ],
        out_shape=[
            jax.ShapeDtypeStruct((batch, t_len, D_RWKV), BF16),
            jax.ShapeDtypeStruct((batch, RWKV_PAIRS, LANES, LANES), F32),
            jax.ShapeDtypeStruct((batch, t_len, D_MLSTM), BF16),
            jax.ShapeDtypeStruct((batch, MLSTM_HEADS, hd, hd), F32),
            jax.ShapeDtypeStruct((batch, MLSTM_HEADS, hd), F32),
            jax.ShapeDtypeStruct((batch, 1, MLSTM_HEADS), F32),
        ],
        scratch_shapes=[
            pltpu.VMEM((bg, RWKV_PAIRS, LANES, LANES), F32),
            pltpu.VMEM((bg, 8, RWKV_PROJ), F32),
            wide(2 * chunk, BF16), wide(2 * chunk, BF16), wide(2 * chunk, BF16), wide(chunk, BF16),
            wide(chunk, F32), wide(chunk, F32), wide(1, F32),
            pltpu.VMEM((bg, MLSTM_HEADS, hd, hd), F32),
            pltpu.VMEM((bg, MLSTM_HEADS, hd), F32),
            pltpu.VMEM((bg, 1, MLSTM_HEADS), F32),
        ],
        compiler_params=_params(("arbitrary",)),
        name="rwkv7_mlstm_recurrent",
    )(p_rw, shift_prev, h0_bd, w["mu"], w["w0"], w["a0"], w["k_k"], w["k_a"], w["r_k"], w["rw_ln_w"],
      w["rw_ln_b"], w["w_lora"], qk, p_vo, c0, n0, m0, w["b_if"], w["ml_ln_w"], w["ml_ln_b"])


def _tail_kernel(x_ref, ya_ref, hb_ref, mq_ref, gate_ref, mk_ref, mv_ref, wpa_ref, wpb_ref, wpc_ref, wout_ref,
                 lnw_ref, lnb_ref, o_ref, *, rows):
    n_seg = mk_ref.shape[0]
    d = D_MODEL
    head = lambda h: slice(h * MEM_HEAD_DIM, (h + 1) * MEM_HEAD_DIM)
    units = [(b, h) for b in range(n_seg) for h in range(MEM_HEADS)]
    idx = range(len(units))
    mk = [mk_ref[b].astype(BF16) for b in range(n_seg)]
    mv = [mv_ref[b].astype(BF16) for b in range(n_seg)]
    s = [_dot_nt(mq_ref[b * rows:(b + 1) * rows, head(h)], mk[b][:, head(h)]) * (MEM_HEAD_DIM ** -0.5)
         for b, h in units]
    y_a = _sigmoid(gate_ref[:, 0:d]) * _dot(ya_ref[...], wpa_ref[...])
    top = [jnp.max(s[u], axis=-1, keepdims=True) for u in idx]
    e = [jnp.exp(s[u] - top[u]) for u in idx]
    y_b = _sigmoid(gate_ref[:, d:2 * d]) * _dot(hb_ref[...], wpb_ref[...])
    den = [jnp.sum(e[u], axis=-1, keepdims=True) for u in idx]
    prob = [(e[u] / den[u]).astype(BF16) for u in idx]
    out = [_dot(prob[u], mv[b][:, head(h)]).astype(BF16) for u, (b, h) in enumerate(units)]
    attn = [jnp.concatenate(out[b * MEM_HEADS:(b + 1) * MEM_HEADS], axis=-1) for b in range(n_seg)]
    attn = attn[0] if n_seg == 1 else jnp.concatenate(attn, axis=0)
    y_c = _sigmoid(gate_ref[:, 2 * d:3 * d]) * _dot(attn, wpc_ref[...])
    mixed = y_a + y_b + y_c
    u = _bdot(mixed, wout_ref[...])
    o_ref[...] = _layer_norm(DN_ALPHA * x_ref[...] + u, lnw_ref[...], lnb_ref[...])


def _tail(x, ya, hb, p_mq, p_gate, mem_k, mem_v, w, rows, batches):
    n = x.shape[0]
    tm = rows * batches
    tiles_per_batch = (n // mem_k.shape[0]) // rows if batches == 1 else 1
    row = lambda i: (i, 0)
    mem = pl.BlockSpec((batches, N_MEM, D_MEM), lambda i: (i // tiles_per_batch, 0, 0))
    tok = lambda cols: pl.BlockSpec((tm, cols), row)
    return pl.pallas_call(
        functools.partial(_tail_kernel, rows=rows),
        grid=(n // tm,),
        in_specs=[
            tok(D_MODEL), tok(D_RWKV), tok(D_MLSTM), tok(D_MEM), tok(3 * D_MODEL), mem, mem,
            _const_spec((D_RWKV, D_MODEL)), _const_spec((D_MLSTM, D_MODEL)), _const_spec((D_MEM, D_MODEL)),
            _const_spec((D_MODEL, D_MODEL)), _const_spec((1, D_MODEL)), _const_spec((1, D_MODEL)),
        ],
        out_specs=tok(D_MODEL),
        out_shape=jax.ShapeDtypeStruct((n, D_MODEL), F32),
        compiler_params=_params(("arbitrary",)),
        name="attn_merge_ln",
    )(x, ya, hb, p_mq, p_gate, mem_k, mem_v, w["w_pa"], w["w_pb"], w["w_pc"], w["w_out"], w["ln1_w"],
      w["ln1_b"])


def _expert_copy(src_hbm, dst, sem, e):
    return pltpu.make_async_copy(src_hbm.at[e], dst.at[:, pl.ds(e * EXPERT_HIDDEN, EXPERT_HIDDEN)], sem)


def _moe_kernel(x_ref, wr_ref, rb_ref, w1_hbm, w3_hbm, w2_ref, ws1_ref, ws3_ref, ws2_ref, lnw_ref, lnb_ref, o_ref,
                w1_ref, w3_ref, sems):
    @pl.when(pl.program_id(0) == 0)
    def _():
        copies = [_expert_copy(src, dst, sems.at[i], e)
                  for i, (src, dst) in enumerate(((w1_hbm, w1_ref), (w3_hbm, w3_ref))) for e in range(N_EXPERTS)]
        for cp in copies:
            cp.start()
        for cp in copies:
            cp.wait()

    x = x_ref[...]
    xb = x.astype(BF16)
    group = 8
    width = group * EXPERT_HIDDEN
    n_slabs = N_EXPERTS // group

    def up_project(s):
        cols = slice(s * width, (s + 1) * width)
        return _dot(xb, w1_ref[:, cols]), _dot(xb, w3_ref[:, cols])

    logits = _dot(xb, wr_ref[...])
    ahead = up_project(0)
    scores = _sigmoid(logits)
    lane = lax.broadcasted_iota(jnp.int32, scores.shape, 1)
    work = jnp.where(lane < N_EXPERTS, scores + rb_ref[...], -jnp.inf)
    chosen = jnp.zeros(scores.shape, dtype=jnp.bool_)
    for _ in range(TOP_K):
        best = jnp.max(work, axis=-1, keepdims=True)
        first = jnp.min(jnp.where(work == best, lane, LANES), axis=-1, keepdims=True)
        pick = lane == first
        chosen = chosen | pick
        work = jnp.where(pick, -jnp.inf, work)
    sel = jnp.where(chosen, scores, 0.0)
    gates = sel / jnp.sum(sel, axis=-1, keepdims=True) * ROUTED_SCALE

    acc = _dot((_silu(_dot(xb, ws1_ref[...])) * _dot(xb, ws3_ref[...])).astype(BF16), ws2_ref[...])
    for s in range(n_slabs):
        cols = slice(s * width, (s + 1) * width)
        h1, h3 = ahead
        if s + 1 < n_slabs:
            ahead = up_project(s + 1)
        hidden = _silu(h1) * h3
        gated = jnp.concatenate(
            [hidden[:, e * EXPERT_HIDDEN:(e + 1) * EXPERT_HIDDEN] * gates[:, s * group + e:s * group + e + 1]
             for e in range(group)], axis=-1)
        acc = acc + _dot(gated.astype(BF16), w2_ref[cols, :])
    o_ref[...] = _layer_norm(DN_ALPHA * x + acc, lnw_ref[...], lnb_ref[...])


def _moe(x, w, tm):
    n = x.shape[0]
    row = lambda i: (i, 0)
    hid = N_EXPERTS * EXPERT_HIDDEN
    return pl.pallas_call(
        _moe_kernel,
        grid=(n // tm,),
        in_specs=[
            pl.BlockSpec((tm, D_MODEL), row),
            _const_spec((D_MODEL, LANES)), _const_spec((1, LANES)),
            pl.BlockSpec(memory_space=pl.ANY), pl.BlockSpec(memory_space=pl.ANY), _const_spec((hid, D_MODEL)),
            _const_spec((D_MODEL, EXPERT_HIDDEN)), _const_spec((D_MODEL, EXPERT_HIDDEN)),
            _const_spec((EXPERT_HIDDEN, D_MODEL)),
            _const_spec((1, D_MODEL)), _const_spec((1, D_MODEL)),
        ],
        out_specs=pl.BlockSpec((tm, D_MODEL), row),
        out_shape=jax.ShapeDtypeStruct((n, D_MODEL), F32),
        scratch_shapes=[pltpu.VMEM((D_MODEL, hid), BF16), pltpu.VMEM((D_MODEL, hid), BF16),
                        pltpu.SemaphoreType.DMA((2,))],
        compiler_params=_params(("arbitrary",)),
        name="moe_ln",
    )(x, w["w_router"], w["router_bias"], w["w_e1"], w["w_e3"], w["w_e2"], w["w_s1"], w["w_s3"], w["w_s2"],
      w["ln2_w"], w["ln2_b"])


def _prep_weights(w_in, b_in, rwkv_mu, rwkv_w0, rwkv_w2, rwkv_a0, rwkv_a2, rwkv_g2, rwkv_k_k, rwkv_k_a,
                  rwkv_r_k, rwkv_ln_w, rwkv_ln_b, mlstm_conv_w, mlstm_conv_b, mlstm_b_i, mlstm_b_f,
                  mlstm_ln_w, mlstm_ln_b, w_mem_kv, w_pa, w_pb, w_pc, w_out, ln1_w, ln1_b,
                  w_router, router_bias, w_e1, w_e3, w_e2, w_s1, w_s3, w_s2, ln2_w, ln2_b):
    off_ml = RWKV_PROJ
    off_if = off_ml + 4 * D_MLSTM
    off_mq = off_if + 2 * MLSTM_HEADS
    pad_if = LANES - 2 * MLSTM_HEADS
    row = lambda t: t.reshape(1, -1)
    hid = N_EXPERTS * EXPERT_HIDDEN
    w_lora = jnp.zeros((LORA_ALL, 3 * D_RWKV), F32)
    w_lora = w_lora.at[0:W_LORA, 0:D_RWKV].set(rwkv_w2)
    w_lora = w_lora.at[W_LORA:W_LORA + A_LORA, D_RWKV:2 * D_RWKV].set(rwkv_a2)
    w_lora = w_lora.at[W_LORA + A_LORA:, 2 * D_RWKV:].set(rwkv_g2)
    return dict(
        w_rw=w_in[:, :off_ml].astype(BF16), b_rw=row(b_in[:off_ml]),
        w_qk=w_in[:, off_ml:off_ml + 2 * D_MLSTM].astype(BF16), b_qk=row(b_in[off_ml:off_ml + 2 * D_MLSTM]),
        w_vo=jnp.pad(w_in[:, off_ml + 2 * D_MLSTM:off_mq], ((0, 0), (0, pad_if))).astype(BF16),
        b_vo=row(jnp.pad(b_in[off_ml + 2 * D_MLSTM:off_mq], (0, pad_if))),
        w_mq=w_in[:, off_mq:off_mq + D_MEM].astype(BF16), b_mq=row(b_in[off_mq:off_mq + D_MEM]),
        w_gate=w_in[:, off_mq + D_MEM:].astype(BF16), b_gate=row(b_in[off_mq + D_MEM:]),
        w_mem_k=w_mem_kv[:, :D_MEM].astype(BF16), w_mem_v=w_mem_kv[:, D_MEM:].astype(BF16),
        mu=row(rwkv_mu), w0=row(rwkv_w0), a0=row(rwkv_a0), k_k=row(rwkv_k_k), k_a=row(rwkv_k_a),
        r_k=row(rwkv_r_k), rw_ln_w=row(rwkv_ln_w), rw_ln_b=row(rwkv_ln_b), w_lora=w_lora.astype(BF16),
        conv_w=mlstm_conv_w, conv_b=row(mlstm_conv_b),
        b_if=row(jnp.pad(jnp.concatenate([mlstm_b_i, mlstm_b_f]), (0, pad_if))),
        ml_ln_w=row(mlstm_ln_w), ml_ln_b=row(mlstm_ln_b),
        w_pa=w_pa.astype(BF16), w_pb=w_pb.astype(BF16), w_pc=w_pc.astype(BF16), w_out=w_out.astype(BF16),
        ln1_w=row(ln1_w), ln1_b=row(ln1_b),
        w_router=jnp.pad(w_router, ((0, 0), (0, LANES - N_EXPERTS))).astype(BF16),
        router_bias=row(jnp.pad(router_bias, (0, LANES - N_EXPERTS))),
        w_e1=w_e1.astype(BF16), w_e3=w_e3.astype(BF16),
        w_e2=w_e2.reshape(hid, D_MODEL).astype(BF16),
        w_s1=w_s1.astype(BF16), w_s3=w_s3.astype(BF16), w_s2=w_s2.astype(BF16),
        ln2_w=row(ln2_w), ln2_b=row(ln2_b),
    )


def _pair_state(wkv):
    bsz = wkv.shape[0]
    s = wkv.reshape(bsz, RWKV_PAIRS, 2, RWKV_HEAD_DIM, RWKV_HEAD_DIM)
    z = jnp.zeros_like(s[:, :, 0])
    top = jnp.concatenate([s[:, :, 0], z], axis=-1)
    bot = jnp.concatenate([z, s[:, :, 1]], axis=-1)
    return jnp.concatenate([top, bot], axis=-2)


def _unpair_state(h_bd):
    bsz = h_bd.shape[0]
    hd = RWKV_HEAD_DIM
    heads = jnp.stack([h_bd[:, :, :hd, :hd], h_bd[:, :, hd:, hd:]], axis=2)
    return heads.reshape(bsz, RWKV_HEADS, hd, hd)


def _trunk(x3, mem_k, mem_v, shift_prev, wkv0, conv_prev, c0, n0, m0, w, chunk, tm_proj, tail_rows, tail_batches,
           tm_moe):
    bsz, t_len, _ = x3.shape
    n = bsz * t_len
    x = x3.reshape(n, D_MODEL)
    p_rw = _matmul_bias(x, w["w_rw"], w["b_rw"], tm_proj, RWKV_PROJ)
    qk, qk_tail, p_vo = _mlstm_proj(x, w["w_qk"], w["b_qk"], w["conv_w"], w["conv_b"], conv_prev, w["w_vo"],
                                    w["b_vo"], tm_proj)
    p_mq, p_gate = _matmul_bias2(x, w["w_mq"], w["b_mq"], w["w_gate"], w["b_gate"], tm_proj, BF16, F32)

    p_rw3 = p_rw.reshape(bsz, t_len, RWKV_PROJ)
    ya, h_bd, hb, c_t, n_t, m_t = _recurrent(
        p_rw3, shift_prev, _pair_state(wkv0), qk.reshape(bsz, t_len, 2 * D_MLSTM), p_vo.reshape(bsz, t_len, VO_COLS),
        c0, n0, m0.reshape(bsz, 1, MLSTM_HEADS), w, chunk)
    x1 = _tail(x, ya.reshape(n, D_RWKV), hb.reshape(n, D_MLSTM), p_mq, p_gate, mem_k, mem_v, w, tail_rows,
               tail_batches)
    y = _moe(x1, w, tm_moe)

    new_shift = p_rw3[:, t_len - 1:, :]
    new_conv = qk_tail[:, 8 - (CONV_W - 1):, :]
    states = (new_shift, _unpair_state(h_bd), new_conv, c_t, n_t, m_t.reshape(bsz, MLSTM_HEADS))
    return y.reshape(bsz, t_len, D_MODEL), states


def kernel(x_prompt, x_sample, state_rwkv_shift, state_rwkv_wkv, state_mlstm_conv, state_mlstm_c, state_mlstm_n, state_mlstm_m, cache_mem_k, cache_mem_v, mem_prompt, w_in, b_in, rwkv_mu, rwkv_w0, rwkv_w2, rwkv_a0, rwkv_a2, rwkv_g2, rwkv_k_k, rwkv_k_a, rwkv_r_k, rwkv_ln_w, rwkv_ln_b, mlstm_conv_w, mlstm_conv_b, mlstm_b_i, mlstm_b_f, mlstm_ln_w, mlstm_ln_b, w_mem_kv, w_pa, w_pb, w_pc, w_out, ln1_w, ln1_b, w_router, router_bias, w_e1, w_e3, w_e2, w_s1, w_s3, w_s2, ln2_w, ln2_b):
    weights = (w_in, b_in, rwkv_mu, rwkv_w0, rwkv_w2, rwkv_a0, rwkv_a2, rwkv_g2, rwkv_k_k, rwkv_k_a, rwkv_r_k,
               rwkv_ln_w, rwkv_ln_b, mlstm_conv_w, mlstm_conv_b, mlstm_b_i, mlstm_b_f, mlstm_ln_w, mlstm_ln_b,
               w_mem_kv, w_pa, w_pb, w_pc, w_out, ln1_w, ln1_b, w_router, router_bias, w_e1, w_e3, w_e2,
               w_s1, w_s3, w_s2, ln2_w, ln2_b)
    w = _prep_weights(*(t[0] for t in weights))
    bp, t_p, _ = x_prompt.shape
    bs, t_s, _ = x_sample.shape
    hd = MLSTM_HEAD_DIM

    mem_flat = mem_prompt.reshape(bp * N_MEM, D_MODEL)
    zero_bias = jnp.zeros((1, D_MEM), F32)
    mk = _matmul_bias(mem_flat, w["w_mem_k"], zero_bias, bp * N_MEM, D_MEM).reshape(bp, N_MEM, D_MEM)
    mv = _matmul_bias(mem_flat, w["w_mem_v"], zero_bias, bp * N_MEM, D_MEM).reshape(bp, N_MEM, D_MEM)

    y_p, st_p = _trunk(
        x_prompt, mk, mv,
        jnp.zeros((bp, 1, RWKV_PROJ), F32), jnp.zeros((bp, RWKV_HEADS, RWKV_HEAD_DIM, RWKV_HEAD_DIM), F32),
        jnp.zeros((bp, CONV_W - 1, 2 * D_MLSTM), F32), jnp.zeros((bp, MLSTM_HEADS, hd, hd), F32),
        jnp.zeros((bp, MLSTM_HEADS, hd), F32), jnp.zeros((bp, MLSTM_HEADS), F32),
        w, chunk=min(MLSTM_CHUNK, t_p), tm_proj=1024, tail_rows=512, tail_batches=1, tm_moe=512)
    y_s, st_s = _trunk(
        x_sample, cache_mem_k[0].reshape(bs, N_MEM, D_MEM), cache_mem_v[0].reshape(bs, N_MEM, D_MEM),
        state_rwkv_shift[0], state_rwkv_wkv[0], state_mlstm_conv[0], state_mlstm_c[0], state_mlstm_n[0],
        state_mlstm_m[0],
        w, chunk=min(MLSTM_CHUNK, t_s), tm_proj=bs * t_s, tail_rows=t_s, tail_batches=4, tm_moe=bs * t_s)

    lead = lambda t: t[None]
    mem_shape = (1, bp, N_MEM, MEM_HEADS, MEM_HEAD_DIM)
    return (y_p, y_s, *(lead(t) for t in st_p), mk.reshape(mem_shape), mv.reshape(mem_shape),
            *(lead(t) for t in st_s))
```

```python
import functools
import itertools

import jax
import jax.numpy as jnp
from jax import lax
from jax.experimental import pallas as pl
from jax.experimental.pallas import tpu as pltpu

F32 = jnp.float32
BF16 = jnp.bfloat16

D_MODEL = 1024
DEPTH = 1
D_RWKV = 1024
RWKV_HEAD_DIM = 64
RWKV_HEADS = 16
W_LORA = 64
A_LORA = 64
G_LORA = 128
LORA_ALL = W_LORA + A_LORA + G_LORA
RWKV_PROJ = 3 * D_RWKV + LORA_ALL
RWKV_GN_EPS = 64e-5
DECAY_SCALE_LOG2 = 0.6065306597126334 * 1.4426950408889634
D_MLSTM = 1024
MLSTM_HEADS = 4
MLSTM_HEAD_DIM = 256
CONV_W = 4
MLSTM_CHUNK = 64
N_MEM = 256
MEM_HEADS = 4
MEM_HEAD_DIM = 256
D_MEM = 1024
N_EXPERTS = 32
TOP_K = 4
EXPERT_HIDDEN = 128
ROUTED_SCALE = 2.5
DN_ALPHA = (2 * DEPTH) ** 0.25
LN_EPS = 1e-5

LANES = 128
RWKV_PAIRS = D_RWKV // LANES
VO_COLS = 2 * D_MLSTM + LANES
VMEM_LIMIT = 56 * 1024 * 1024
BATCH_GROUP = 2


def _dot(a, b):
    return jnp.dot(a, b, preferred_element_type=F32)


def _dot_nt(a, b):
    return lax.dot_general(a, b, (((1,), (1,)), ((), ())), preferred_element_type=F32)


def _dot_tn(a, b):
    return lax.dot_general(a, b, (((0,), (0,)), ((), ())), preferred_element_type=F32)


def _split(a):
    hi = a.astype(BF16)
    return hi, (a - hi.astype(F32)).astype(BF16)


def _split3(a):
    hi = a.astype(BF16)
    rest = a - hi.astype(F32)
    mid = rest.astype(BF16)
    return hi, mid, (rest - mid.astype(F32)).astype(BF16)


def _bdot(a, b):
    return _dot(a.astype(BF16), b.astype(BF16))


def _sigmoid(x):
    return 0.5 * jnp.tanh(0.5 * x) + 0.5


def _softplus(x):
    return jnp.maximum(x, 0.0) + jnp.log(1.0 + jnp.exp(-jnp.abs(x)))


def _silu(x):
    return x * _sigmoid(x)


def _layer_norm(x, w, b):
    mu = jnp.mean(x, axis=-1, keepdims=True)
    d = x - mu
    var = jnp.mean(d * d, axis=-1, keepdims=True)
    return d * lax.rsqrt(var + LN_EPS) * w + b


def _params(semantics):
    return pltpu.CompilerParams(dimension_semantics=semantics, vmem_limit_bytes=VMEM_LIMIT)


def _const_spec(shape):
    zeros = (0,) * len(shape)
    return pl.BlockSpec(shape, lambda *_: zeros, pipeline_mode=pl.Buffered(1))


def _mm_kernel(x_ref, w_ref, b_ref, o_ref):
    o_ref[...] = (_dot(x_ref[...].astype(BF16), w_ref[...]) + b_ref[...]).astype(o_ref.dtype)


def _matmul_bias(x, w, b, tm, tn, out_dtype=F32):
    n, k = x.shape
    nc = w.shape[1]
    resident = dict(pipeline_mode=pl.Buffered(1)) if nc == tn else {}
    return pl.pallas_call(
        _mm_kernel,
        grid=(nc // tn, n // tm),
        in_specs=[
            pl.BlockSpec((tm, k), lambda j, i: (i, 0)),
            pl.BlockSpec((k, tn), lambda j, i: (0, j), **resident),
            pl.BlockSpec((1, tn), lambda j, i: (0, j), **resident),
        ],
        out_specs=pl.BlockSpec((tm, tn), lambda j, i: (i, j)),
        out_shape=jax.ShapeDtypeStruct((n, nc), out_dtype),
        compiler_params=_params(("arbitrary", "arbitrary")),
        name="matmul_bias",
    )(x, w, b)


def _mm2_kernel(x_ref, w1_ref, b1_ref, w2_ref, b2_ref, o1_ref, o2_ref):
    xb = x_ref[...].astype(BF16)
    o1_ref[...] = (_dot(xb, w1_ref[...]) + b1_ref[...]).astype(o1_ref.dtype)
    o2_ref[...] = (_dot(xb, w2_ref[...]) + b2_ref[...]).astype(o2_ref.dtype)


def _matmul_bias2(x, w1, b1, w2, b2, tm, dtype1, dtype2):
    n, k = x.shape
    n1, n2 = w1.shape[1], w2.shape[1]
    row = lambda i: (i, 0)
    return pl.pallas_call(
        _mm2_kernel,
        grid=(n // tm,),
        in_specs=[pl.BlockSpec((tm, k), row), _const_spec((k, n1)), _const_spec((1, n1)), _const_spec((k, n2)),
                  _const_spec((1, n2))],
        out_specs=[pl.BlockSpec((tm, n1), row), pl.BlockSpec((tm, n2), row)],
        out_shape=[jax.ShapeDtypeStruct((n, n1), dtype1), jax.ShapeDtypeStruct((n, n2), dtype2)],
        compiler_params=_params(("arbitrary",)),
        name="matmul_bias_pair",
    )(x, w1, b1, w2, b2)


def _mlstm_proj_kernel(x_ref, w_ref, b_ref, cw_ref, cb_ref, prev_ref, w2_ref, b2_ref, o_ref, tail_ref, o2_ref, carry,
                       *, rows, tiles_per_batch):
    row_tile = pl.program_id(0)
    n_seg = prev_ref.shape[0]
    tm, tn = o_ref.shape
    n2 = o2_ref.shape[1]
    pad = jnp.zeros((8 - (CONV_W - 1), LANES), F32)
    row8 = lax.broadcasted_iota(jnp.int32, (8, LANES), 0)

    if n_seg == 1:
        @pl.when(row_tile % tiles_per_batch == 0)
        def _():
            carry[0:8 - (CONV_W - 1), :] = jnp.zeros((8 - (CONV_W - 1), tn), F32)
            carry[8 - (CONV_W - 1):8, :] = prev_ref[0]

    xb = x_ref[...].astype(BF16)
    n_parts = 4
    q_step = tn // n_parts
    for part in range(n_parts):
        cs = slice(part * q_step, (part + 1) * q_step)
        o_ref[:, cs] = _dot(xb, w_ref[:, cs]) + b_ref[:, cs]
    for seg in range(n_seg):
        tail_ref[seg] = o_ref[(seg + 1) * rows - 8:(seg + 1) * rows, :]
    strip = min(rows, 64)

    def conv_strips(r_lo, r_hi):
        for r0 in reversed(range(r_lo, r_hi, strip)):
            for c0 in range(0, tn, LANES):
                cols = slice(c0, c0 + LANES)
                half_scale = 0.5 * MLSTM_HEAD_DIM ** -0.5 if c0 >= D_MLSTM else 0.5
                if n_seg > 1 and r0 % rows == 0:
                    head = jnp.concatenate([pad, prev_ref[r0 // rows, :, cols]], axis=0)
                elif r0 == 0:
                    head = carry[:, cols]
                else:
                    head = o_ref[r0 - 8:r0, cols]
                x = o_ref[r0:r0 + strip, cols]
                blocks = [head] + [x[r:r + 8, :] for r in range(0, strip, 8)]
                conv = cb_ref[:, cols] + cw_ref[CONV_W - 1:CONV_W, cols] * x
                for j in range(CONV_W - 1):
                    lag = CONV_W - 1 - j
                    turned = [pltpu.roll(blk, lag, 0) for blk in blocks]
                    shifted = jnp.concatenate([jnp.where(row8 < lag, turned[i], turned[i + 1])
                                               for i in range(len(blocks) - 1)], axis=0)
                    conv = conv + cw_ref[j:j + 1, cols] * shifted
                o_ref[r0:r0 + strip, cols] = conv * (half_scale * jnp.tanh(0.5 * conv) + half_scale)

    col_step = -(-n2 // (n_parts * LANES)) * LANES
    row_step = tm // n_parts
    for part in reversed(range(n_parts)):
        cs = slice(part * col_step, min((part + 1) * col_step, n2))
        o2_ref[:, cs] = _dot(xb, w2_ref[:, cs]) + b2_ref[:, cs]
        conv_strips(part * row_step, (part + 1) * row_step)

    if n_seg == 1:
        carry[...] = tail_ref[0]


def _mlstm_proj(x, w, b, conv_w, conv_b, conv_prev, w2, b2, tm):
    n, k = x.shape
    batch = conv_prev.shape[0]
    t_len = n // batch
    rows = min(tm, t_len)
    n_seg = tm // rows
    tiles_per_batch = t_len // rows
    nc, n2 = w.shape[1], w2.shape[1]
    seg = lambda i: (i // tiles_per_batch if n_seg == 1 else i, 0, 0)
    return pl.pallas_call(
        functools.partial(_mlstm_proj_kernel, rows=rows, tiles_per_batch=tiles_per_batch),
        grid=(n // tm,),
        in_specs=[
            pl.BlockSpec((tm, k), lambda i: (i, 0)),
            _const_spec((k, nc)), _const_spec((1, nc)), _const_spec((CONV_W, nc)), _const_spec((1, nc)),
            pl.BlockSpec((n_seg, CONV_W - 1, nc), seg),
            _const_spec((k, n2)), _const_spec((1, n2)),
        ],
        out_specs=[
            pl.BlockSpec((tm, nc), lambda i: (i, 0)),
            pl.BlockSpec((n_seg, 8, nc), seg),
            pl.BlockSpec((tm, n2), lambda i: (i, 0)),
        ],
        out_shape=[jax.ShapeDtypeStruct((n, nc), F32), jax.ShapeDtypeStruct((batch, 8, nc), F32),
                   jax.ShapeDtypeStruct((n, n2), F32)],
        scratch_shapes=[pltpu.VMEM((8, nc), F32)],
        compiler_params=_params(("arbitrary",)),
        name="mlstm_proj_conv",
    )(x, w, b, conv_w, conv_b, conv_prev, w2, b2)


def _rwkv_body(p_ref, prev_ref, h0_ref, mu_ref, w0_ref, a0_ref, kk_ref, ka_ref, rk_ref, lnw_ref, lnb_ref,
               wl_ref, y_ref, hout_ref, h_scr, xbuf, ar_scr, bk_scr, bkh_scr, v_scr, bonus_scr, g_scr, gamma_scr,
               *, chunk, n_chunks, n_items, companion):
    step = pl.program_id(0)
    L = chunk
    half = 2 * L
    prep_chunk = jnp.minimum(step, n_items - 1) % n_chunks
    solve_chunk = jnp.maximum(step - 1, 0) % n_chunks

    @pl.when(step == 0)
    def _():
        for ref in (ar_scr, bk_scr, bkh_scr, v_scr, bonus_scr, g_scr):
            ref[...] = jnp.zeros(ref.shape, ref.dtype)
        gamma_scr[...] = jnp.ones(gamma_scr.shape, F32)

    @pl.when(solve_chunk == 0)
    def _():
        h_scr[...] = h0_ref[...]

    @pl.when(prep_chunk == 0)
    def _():
        for bi in range(BATCH_GROUP):
            xbuf[bi, 0:1, :] = prev_ref[bi]

    ti = lax.broadcasted_iota(jnp.int32, (L, L), 0)
    tj = lax.broadcasted_iota(jnp.int32, (L, L), 1)
    ltri = (tj <= ti).astype(BF16)
    row_m = lax.broadcasted_iota(jnp.int32, (L, half), 0)
    col_m = lax.broadcasted_iota(jnp.int32, (L, half), 1)
    col_in = jnp.where(col_m >= L, col_m - L, col_m)
    strict = col_in < row_m
    incl = col_in <= row_m
    eye = (col_in == row_m).astype(F32)
    gi = lax.broadcasted_iota(jnp.int32, (LANES, LANES), 0)
    gj = lax.broadcasted_iota(jnp.int32, (LANES, LANES), 1)
    same_head = (gi >= RWKV_HEAD_DIM) == (gj >= RWKV_HEAD_DIM)
    head_ones = same_head.astype(BF16)
    first_nat = lax.broadcasted_iota(jnp.int32, (L, LANES), 1) < RWKV_HEAD_DIM
    first_sbs = col_m < L
    lane_l = lax.broadcasted_iota(jnp.int32, (L, LORA_ALL), 1)

    def diag(x, first):
        zero = jnp.zeros_like(x)
        return jnp.concatenate([jnp.where(first, x, zero), jnp.where(first, zero, x)], axis=0)

    def seg_sum(x):
        return _dot(x.astype(BF16), head_ones)

    units = [(bi, p) for bi in range(BATCH_GROUP) for p in range(RWKV_PAIRS)]
    idx = range(len(units))
    lane_of = lambda p: slice(p * LANES, (p + 1) * LANES)

    ar = [ar_scr[bi, :, lane_of(p)] for bi, p in units]
    bk = [bk_scr[bi, :, lane_of(p)] for bi, p in units]
    bkh = [bkh_scr[bi, :, lane_of(p)] for bi, p in units]
    v_b = [v_scr[bi, :, lane_of(p)] for bi, p in units]
    bonus = [bonus_scr[bi, :, lane_of(p)] for bi, p in units]
    gate = [g_scr[bi, :, lane_of(p)] for bi, p in units]
    gamma = [gamma_scr[bi, :, lane_of(p)] for bi, p in units]

    first_row = lax.broadcasted_iota(jnp.int32, (L, RWKV_PROJ), 0) == 0

    def prepare_row(bi):
        cur = p_ref[bi]
        prev = jnp.where(first_row, xbuf[bi, 0:1, :], pltpu.roll(cur, 1, 0))
        xbuf[bi, 0:1, :] = cur[L - 1:L, :]
        xr = cur + (prev - cur) * mu_ref[...]
        r_all = xr[:, 0:D_RWKV]
        k_all = xr[:, D_RWKV:2 * D_RWKV]
        v_all = xr[:, 2 * D_RWKV:3 * D_RWKV]
        slab = xr[:, 3 * D_RWKV:]
        act = jnp.where(lane_l < W_LORA, jnp.tanh(slab),
                        jnp.where(lane_l < W_LORA + A_LORA, slab, _sigmoid(slab)))
        yield
        lora = _bdot(act, wl_ref[...])
        yield
        lw = -DECAY_SCALE_LOG2 * _sigmoid(w0_ref[...] + lora[:, 0:D_RWKV])
        a_sig = _sigmoid(a0_ref[...] + lora[:, D_RWKV:2 * D_RWKV])
        lw_hi, lw_lo = _split(lw)
        yield
        kk0 = k_all * kk_ref[...]
        k2 = k_all * (1.0 + (a_sig - 1.0) * ka_ref[...])
        sums = jnp.concatenate([kk0 * kk0, r_all * k2 * rk_ref[...]], axis=0).astype(BF16)
        yield
        cum = _dot(ltri, lw_hi) + _dot(ltri, lw_lo)
        sums = jnp.concatenate([_dot(sums[:, lane_of(p)], head_ones) for p in range(RWKV_PAIRS)], axis=1)
        yield
        kk = kk0 * lax.rsqrt(jnp.maximum(sums[0:L], 1e-24))
        b_vec = kk * a_sig
        c_last = cum[L - 1:L, :]
        e_neg = jnp.exp2(-cum)
        e_end = jnp.exp2(c_last - cum)
        yield
        ar_scr[bi] = jnp.concatenate([-kk * jnp.exp2(cum - lw), r_all * jnp.exp2(cum)], axis=0).astype(BF16)
        bk_scr[bi] = jnp.concatenate([b_vec * e_neg, k2 * e_neg], axis=0).astype(BF16)
        yield
        bkh_scr[bi] = jnp.concatenate([b_vec * e_end, k2 * e_end], axis=0).astype(BF16)
        v_scr[bi] = v_all.astype(BF16)
        bonus_scr[bi] = sums[L:] * v_all
        g_scr[bi] = lora[:, 2 * D_RWKV:]
        gamma_scr[bi] = jnp.exp2(c_last)

    segments = itertools.chain(*(prepare_row(bi) for bi in range(BATCH_GROUP)))

    def prepare_some(count):
        for _ in range(count):
            next(segments, None)
            next(companion, None)

    n_sq = L.bit_length() - 1
    per_stage = 2
    bk_d = [jnp.concatenate([diag(bk[u][0:L], first_nat), diag(bk[u][L:], first_nat)], axis=0) for u in idx]
    v_d = [diag(v_b[u], first_nat) for u in idx]
    h_bd = [h_scr[bi, p] for bi, p in units]
    sc = [_dot_nt(ar[u], bk_d[u]) for u in idx]
    prepare_some(per_stage)
    arh = [_dot_nt(ar[u], h_bd[u].astype(BF16)) for u in idx]
    m_ab = [jnp.where(strict, sc[u][0:L, 0:half], 0.0) for u in idx]
    m_ak = [jnp.where(strict, sc[u][0:L, half:], 0.0).astype(BF16) for u in idx]
    n_rbk = [jnp.concatenate([jnp.where(incl, sc[u][L:, 0:half], 0.0),
                              jnp.where(incl, sc[u][L:, half:], 0.0)], axis=1).astype(BF16) for u in idx]
    prepare_some(per_stage)
    w0 = [arh[u][0:L] + _dot(m_ak[u], v_d[u]) for u in idx]
    inv = [eye + m_ab[u] for u in idx]
    m_b = [m_ab[u].astype(BF16) for u in idx]
    pw = [_dot(m_b[u], diag(m_b[u], first_sbs)) for u in idx]
    prepare_some(per_stage)
    for i in range(n_sq - 1):
        pw_d = [diag(pw[u].astype(BF16), first_sbs) for u in idx]
        if i + 1 < n_sq - 1:
            z = [_dot(jnp.concatenate([inv[u], pw[u]], axis=0).astype(BF16), pw_d[u]) for u in idx]
            inv = [inv[u] + z[u][0:L] for u in idx]
            pw = [z[u][L:] for u in idx]
        else:
            inv = [inv[u] + _dot(inv[u].astype(BF16), pw_d[u]) for u in idx]
        prepare_some(per_stage)
    c_b = [_dot(inv[u].astype(BF16), diag(w0[u].astype(BF16), first_nat)).astype(BF16) for u in idx]
    cv = [jnp.concatenate([c_b[u], v_b[u]], axis=0) for u in idx]
    y = [arh[u][L:] + _dot(n_rbk[u], jnp.concatenate([diag(c_b[u], first_nat), v_d[u]], axis=0)) for u in idx]
    for u, (bi, p) in enumerate(units):
        h_scr[bi, p] = h_bd[u] * gamma[u] + jnp.where(same_head, _dot_tn(cv[u], bkh[u]), 0.0)
    mean = [seg_sum(y[u]) * (1.0 / RWKV_HEAD_DIM) for u in idx]
    dev = [y[u] - mean[u] for u in idx]
    var = [seg_sum(dev[u] * dev[u]) * (1.0 / RWKV_HEAD_DIM) for u in idx]
    for u, (bi, p) in enumerate(units):
        s = lane_of(p)
        yn = dev[u] * lax.rsqrt(var[u] + RWKV_GN_EPS) * lnw_ref[:, s] + lnb_ref[:, s]
        y_ref[bi, :, s] = ((yn + bonus[u]) * gate[u]).astype(y_ref.dtype)
    prepare_some(8 * BATCH_GROUP)
    for _ in companion:
        pass

    @pl.when((solve_chunk == n_chunks - 1) & (step > 0))
    def _():
        hout_ref[...] = h_scr[...]


def _mlstm_stages(qk_ref, p_ref, c0_ref, n0_ref, m0_ref, bif_ref, lnw_ref, lnb_ref,
                  h_ref, cout_ref, nout_ref, mout_ref, c_scr, n_scr, m_scr, *, chunk, first, last):
    L = chunk
    d = D_MLSTM
    nh = MLSTM_HEADS

    @pl.when(first)
    def _():
        c_scr[...] = c0_ref[...]
        n_scr[...] = n0_ref[...]
        m_scr[...] = m0_ref[...]

    yield

    ti =lax.broadcasted_iota(jnp.int32, (L, L), 0)
    tj = lax.broadcasted_iota(jnp.int32, (L, L), 1)
    causal = tj <= ti
    ltri = causal.astype(BF16)
    utri = (ti <= tj).astype(BF16)
    sel = (lax.broadcasted_iota(jnp.int32, (8, LANES), 0)
           == lax.broadcasted_iota(jnp.int32, (8, LANES), 1)).astype(BF16)
    lane_g = lax.broadcasted_iota(jnp.int32, (L, LANES), 1)
    is_f = (lane_g >= nh) & (lane_g < 2 * nh)

    rows = []
    for bi in range(BATCH_GROUP):
        gates = p_ref[bi, :, 2 * d:] + bif_ref[...]
        glog = jnp.where(is_f, -_softplus(-gates), gates)
        parts = _split3(glog)
        b_cols = sum(_dot(ltri, t) for t in parts)
        g_rows = sum(_dot_nt(sel, t) for t in parts)
        b_rows = sum(_dot(t, utri) for t in _split3(g_rows))
        rows.append(dict(q=qk_ref[bi, :, 0:d], k=qk_ref[bi, :, d:2 * d], v=p_ref[bi, :, 0:d],
                         o=_sigmoid(p_ref[bi, :, d:2 * d]),
                         glog=glog, b_cols=b_cols, g_rows=g_rows, b_rows=b_rows))
        yield

    units = [(bi, h) for bi in range(BATCH_GROUP) for h in range(nh)]
    idx = range(len(units))
    head = lambda h: slice(h * MLSTM_HEAD_DIM, (h + 1) * MLSTM_HEAD_DIM)
    q = [rows[bi]["q"][:, head(h)] for bi, h in units]
    k = [rows[bi]["k"][:, head(h)] for bi, h in units]
    v = [rows[bi]["v"][:, head(h)] for bi, h in units]
    q_b = [t.astype(BF16) for t in q]
    k_b = [t.astype(BF16) for t in k]
    c_st = [c_scr[bi, h] for bi, h in units]
    n_st = [n_scr[bi, h:h + 1, :] for bi, h in units]
    m_prev = [m_scr[bi, :, h:h + 1] for bi, h in units]
    qk_t = [_dot_nt(q_b[u], k_b[u]) for u in idx]
    yield
    qc = [_dot_nt(q_b[u], c_st[u].astype(BF16)) for u in idx]
    yield
    s, w_inter, m_t, wk, carry, m_next = [], [], [], [], [], []
    for u, (bi, h) in enumerate(units):
        t = rows[bi]
        ig_row = t["g_rows"][h:h + 1, :]
        b_row = t["b_rows"][nh + h:nh + h + 1, :]
        ig_col = t["glog"][:, h:h + 1]
        b_col = t["b_cols"][:, nh + h:nh + h + 1]
        log_w = jnp.where(causal, b_col - b_row + ig_row, -jnp.inf)
        inter = b_col + m_prev[u]
        m_now = jnp.maximum(inter, jnp.max(log_w, axis=-1, keepdims=True))
        s.append(qk_t[u] * jnp.exp(log_w - m_now))
        w_inter.append(jnp.exp(inter - m_now))
        m_t.append(m_now)
        b_last = b_col[L - 1:L, :]
        g_col = b_last - b_col + ig_col
        m_new = jnp.maximum(b_last + m_prev[u], jnp.max(g_col, axis=0, keepdims=True))
        carry.append(jnp.exp(b_last + m_prev[u] - m_new))
        wk.append(jnp.exp(g_col - m_new))
        m_next.append(m_new)
        if u % 2 == 1:
            yield
    sv = [_dot(s[u].astype(BF16), v[u].astype(BF16)) for u in idx]
    yield
    upd = [_dot_tn((wk[u] * v[u]).astype(BF16), k_b[u]) for u in idx]
    yield
    qn =[jnp.sum(q[u] * n_st[u], axis=-1, keepdims=True) for u in idx]
    s_sum = [jnp.sum(s[u], axis=-1, keepdims=True) for u in idx]
    den = [jnp.maximum(jnp.abs(w_inter[u] * qn[u] + s_sum[u]), jnp.exp(-m_t[u])) for u in idx]
    yield
    hb = [rows[bi]["o"][:, head(h)] * ((w_inter[u] * qc[u] + sv[u]) / den[u]) for u, (bi, h) in enumerate(units)]
    yield
    mu = [jnp.mean(hb[u], axis=-1, keepdims=True) for u in idx]
    dev = [hb[u] - mu[u] for u in idx]
    var = [jnp.mean(dev[u] * dev[u], axis=-1, keepdims=True) for u in idx]
    yield
    n_new = [carry[u] * n_st[u] + jnp.sum(wk[u] * k[u], axis=0, keepdims=True) for u in idx]
    for u, (bi, h) in enumerate(units):
        sl = head(h)
        h_ref[bi, :, sl] = (dev[u] * lax.rsqrt(var[u] + LN_EPS) * lnw_ref[:, sl] + lnb_ref[:, sl]).astype(h_ref.dtype)
        c_scr[bi, h] = carry[u] * c_st[u] + upd[u]
        n_scr[bi, h:h + 1, :] = n_new[u]
        m_scr[bi, :, h:h + 1] = m_next[u]
        if u % 4 == 3:
            yield

    @pl.when(last)
    def _():
        cout_ref[...] = c_scr[...]
        nout_ref[...] = n_scr[...]
        mout_ref[...] = m_scr[...]


RW_REFS = (12, 2, 9)
ML_REFS = (8, 4, 3)


def _recurrent_kernel(*refs, chunk, n_chunks, n_items):
    groups, pos = [], 0
    for n_rw, n_ml in zip(RW_REFS, ML_REFS):
        groups.append((refs[pos:pos + n_rw], refs[pos + n_rw:pos + n_rw + n_ml]))
        pos += n_rw + n_ml
    rw_refs = [r for rw_part, _ in groups for r in rw_part]
    ml_refs = [r for _, ml_part in groups for r in ml_part]
    step = pl.program_id(0)
    solve_chunk = jnp.maximum(step - 1, 0) % n_chunks
    ml = _mlstm_stages(*ml_refs, chunk=chunk, first=solve_chunk == 0,
                       last=(solve_chunk == n_chunks - 1) & (step > 0))
    next(ml)
    _rwkv_body(*rw_refs, chunk=chunk, n_chunks=n_chunks, n_items=n_items, companion=ml)


def _recurrent(p_rw, shift_prev, h0_bd, qk, p_vo, c0, n0, m0, w, chunk):
    batch, t_len, _ = p_rw.shape
    n_chunks = t_len // chunk
    bg = BATCH_GROUP
    hd = MLSTM_HEAD_DIM
    n_items = (batch // bg) * n_chunks
    prep = lambda i: jnp.minimum(i, n_items - 1)
    solve = lambda i: jnp.maximum(i - 1, 0)
    prep_seq = lambda i: (prep(i) // n_chunks, prep(i) % n_chunks, 0)
    solve_seq = lambda i: (solve(i) // n_chunks, solve(i) % n_chunks, 0)
    vec = lambda n: pl.BlockSpec((1, n), lambda i: (0, 0))
    group = lambda *dims: pl.BlockSpec((bg,) + dims, lambda i: (solve(i) // n_chunks,) + (0,) * len(dims))
    rw_state = group(RWKV_PAIRS, LANES, LANES)
    ml_states = [group(MLSTM_HEADS, hd, hd), group(MLSTM_HEADS, hd), group(1, MLSTM_HEADS)]
    wide = lambda rows, dtype: pltpu.VMEM((bg, rows, D_RWKV), dtype)
    kern = functools.partial(_recurrent_kernel, chunk=chunk, n_chunks=n_chunks, n_items=n_items)
    return pl.pallas_call(
        kern,
        grid=(n_items + 1,),
        in_specs=[
            pl.BlockSpec((bg, chunk, RWKV_PROJ), prep_seq),
            pl.BlockSpec((bg, 1, RWKV_PROJ), lambda i: (prep(i) // n_chunks, 0, 0)),
            rw_state,
            vec(RWKV_PROJ), vec(D_RWKV), vec(D_RWKV), vec(D_RWKV), vec(D_RWKV), vec(D_RWKV), vec(D_RWKV),
            vec(D_RWKV),
            pl.BlockSpec((LORA_ALL, 3 * D_RWKV), lambda i: (0, 0)),
            pl.BlockSpec((bg, chunk, 2 * D_MLSTM), solve_seq),
            pl.BlockSpec((bg, chunk, VO_COLS), solve_seq),
            *ml_states,
            vec(LANES), vec(D_MLSTM), vec(D_MLSTM),
        ],
        out_specs=[
            pl.BlockSpec((bg, chunk, D_RWKV), solve_seq), rw_state,
            pl.BlockSpec((bg, chunk, D_MLSTM), solve_seq), *ml_states,
        ],
        out_shape=[
            jax.ShapeDtypeStruct((batch, t_len, D_RWKV), BF16),
            jax.ShapeDtypeStruct((batch, RWKV_PAIRS, LANES, LANES), F32),
            jax.ShapeDtypeStruct((batch, t_len, D_MLSTM), BF16),
            jax.ShapeDtypeStruct((batch, MLSTM_HEADS, hd, hd), F32),
            jax.ShapeDtypeStruct((batch, MLSTM_HEADS, hd), F32),
            jax.ShapeDtypeStruct((batch, 1, MLSTM_HEADS), F32),
        ],
        scratch_shapes=[
            pltpu.VMEM((bg, RWKV_PAIRS, LANES, LANES), F32),
            pltpu.VMEM((bg, 8, RWKV_PROJ), F32),
            wide(2 * chunk, BF16), wide(2 * chunk, BF16), wide(2 * chunk, BF16), wide(chunk, BF16),
            wide(chunk, F32), wide(chunk, F32), wide(1, F32),
            pltpu.VMEM((bg, MLSTM_HEADS, hd, hd), F32),
            pltpu.VMEM((bg, MLSTM_HEADS, hd), F32),
            pltpu.VMEM((bg, 1, MLSTM_HEADS), F32),
        ],
        compiler_params=_params(("arbitrary",)),
        name="rwkv7_mlstm_recurrent",
    )(p_rw, shift_prev, h0_bd, w["mu"], w["w0"], w["a0"], w["k_k"], w["k_a"], w["r_k"], w["rw_ln_w"],
      w["rw_ln_b"], w["w_lora"], qk, p_vo, c0, n0, m0, w["b_if"], w["ml_ln_w"], w["ml_ln_b"])


def _tail_kernel(x_ref, ya_ref, hb_ref, mq_ref, gate_ref, mk_ref, mv_ref, wpa_ref, wpb_ref, wpc_ref, wout_ref,
                 lnw_ref, lnb_ref, o_ref, *, rows):
    n_seg = mk_ref.shape[0]
    d = D_MODEL
    head = lambda h: slice(h * MEM_HEAD_DIM, (h + 1) * MEM_HEAD_DIM)
    units = [(b, h) for b in range(n_seg) for h in range(MEM_HEADS)]
    idx = range(len(units))
    mk = [mk_ref[b].astype(BF16) for b in range(n_seg)]
    mv = [mv_ref[b].astype(BF16) for b in range(n_seg)]
    s = [_dot_nt(mq_ref[b * rows:(b + 1) * rows, head(h)], mk[b][:, head(h)]) * (MEM_HEAD_DIM ** -0.5)
         for b, h in units]
    y_a = _sigmoid(gate_ref[:, 0:d]) * _dot(ya_ref[...], wpa_ref[...])
    top = [jnp.max(s[u], axis=-1, keepdims=True) for u in idx]
    e = [jnp.exp(s[u] - top[u]) for u in idx]
    y_b = _sigmoid(gate_ref[:, d:2 * d]) * _dot(hb_ref[...], wpb_ref[...])
    den = [jnp.sum(e[u], axis=-1, keepdims=True) for u in idx]
    prob = [(e[u] / den[u]).astype(BF16) for u in idx]
    out = [_dot(prob[u], mv[b][:, head(h)]).astype(BF16) for u, (b, h) in enumerate(units)]
    attn = [jnp.concatenate(out[b * MEM_HEADS:(b + 1) * MEM_HEADS], axis=-1) for b in range(n_seg)]
    attn = attn[0] if n_seg == 1 else jnp.concatenate(attn, axis=0)
    y_c = _sigmoid(gate_ref[:, 2 * d:3 * d]) * _dot(attn, wpc_ref[...])
    mixed = y_a + y_b + y_c
    u = _bdot(mixed, wout_ref[...])
    o_ref[...] = _layer_norm(DN_ALPHA * x_ref[...] + u, lnw_ref[...], lnb_ref[...])


def _tail(x, ya, hb, p_mq, p_gate, mem_k, mem_v, w, rows, batches):
    n = x.shape[0]
    tm = rows * batches
    tiles_per_batch = (n // mem_k.shape[0]) // rows if batches == 1 else 1
    row = lambda i: (i, 0)
    mem = pl.BlockSpec((batches, N_MEM, D_MEM), lambda i: (i // tiles_per_batch, 0, 0))
    tok = lambda cols: pl.BlockSpec((tm, cols), row)
    return pl.pallas_call(
        functools.partial(_tail_kernel, rows=rows),
        grid=(n // tm,),
        in_specs=[
            tok(D_MODEL), tok(D_RWKV), tok(D_MLSTM), tok(D_MEM), tok(3 * D_MODEL), mem, mem,
            _const_spec((D_RWKV, D_MODEL)), _const_spec((D_MLSTM, D_MODEL)), _const_spec((D_MEM, D_MODEL)),
            _const_spec((D_MODEL, D_MODEL)), _const_spec((1, D_MODEL)), _const_spec((1, D_MODEL)),
        ],
        out_specs=tok(D_MODEL),
        out_shape=jax.ShapeDtypeStruct((n, D_MODEL), F32),
        compiler_params=_params(("arbitrary",)),
        name="attn_merge_ln",
    )(x, ya, hb, p_mq, p_gate, mem_k, mem_v, w["w_pa"], w["w_pb"], w["w_pc"], w["w_out"], w["ln1_w"],
      w["ln1_b"])


def _expert_copy(src_hbm, dst, sem, e):
    return pltpu.make_async_copy(src_hbm.at[e], dst.at[:, pl.ds(e * EXPERT_HIDDEN, EXPERT_HIDDEN)], sem)


def _moe_kernel(x_ref, wr_ref, rb_ref, w1_hbm, w3_hbm, w2_ref, ws1_ref, ws3_ref, ws2_ref, lnw_ref, lnb_ref, o_ref,
                w1_ref, w3_ref, sems):
    @pl.when(pl.program_id(0) == 0)
    def _():
        copies = [_expert_copy(src, dst, sems.at[i], e)
                  for i, (src, dst) in enumerate(((w1_hbm, w1_ref), (w3_hbm, w3_ref))) for e in range(N_EXPERTS)]
        for cp in copies:
            cp.start()
        for cp in copies:
            cp.wait()

    x = x_ref[...]
    xb = x.astype(BF16)
    group = 8
    width = group * EXPERT_HIDDEN
    n_slabs = N_EXPERTS // group

    def up_project(s):
        cols = slice(s * width, (s + 1) * width)
        return _dot(xb, w1_ref[:, cols]), _dot(xb, w3_ref[:, cols])

    logits = _dot(xb, wr_ref[...])
    ahead = up_project(0)
    scores = _sigmoid(logits)
    lane = lax.broadcasted_iota(jnp.int32, scores.shape, 1)
    work = jnp.where(lane < N_EXPERTS, scores + rb_ref[...], -jnp.inf)
    chosen = jnp.zeros(scores.shape, dtype=jnp.bool_)
    for _ in range(TOP_K):
        best = jnp.max(work, axis=-1, keepdims=True)
        first = jnp.min(jnp.where(work == best, lane, LANES), axis=-1, keepdims=True)
        pick = lane == first
        chosen = chosen | pick
        work = jnp.where(pick, -jnp.inf, work)
    sel = jnp.where(chosen, scores, 0.0)
    gates = sel / jnp.sum(sel, axis=-1, keepdims=True) * ROUTED_SCALE

    acc = _dot((_silu(_dot(xb, ws1_ref[...])) * _dot(xb, ws3_ref[...])).astype(BF16), ws2_ref[...])
    for s in range(n_slabs):
        cols = slice(s * width, (s + 1) * width)
        h1, h3 = ahead
        if s + 1 < n_slabs:
            ahead = up_project(s + 1)
        hidden = _silu(h1) * h3
        gated = jnp.concatenate(
            [hidden[:, e * EXPERT_HIDDEN:(e + 1) * EXPERT_HIDDEN] * gates[:, s * group + e:s * group + e + 1]
             for e in range(group)], axis=-1)
        acc = acc + _dot(gated.astype(BF16), w2_ref[cols, :])
    o_ref[...] = _layer_norm(DN_ALPHA * x + acc, lnw_ref[...], lnb_ref[...])


def _moe(x, w, tm):
    n = x.shape[0]
    row = lambda i: (i, 0)
    hid = N_EXPERTS * EXPERT_HIDDEN
    return pl.pallas_call(
        _moe_kernel,
        grid=(n // tm,),
        in_specs=[
            pl.BlockSpec((tm, D_MODEL), row),
            _const_spec((D_MODEL, LANES)), _const_spec((1, LANES)),
            pl.BlockSpec(memory_space=pl.ANY), pl.BlockSpec(memory_space=pl.ANY), _const_spec((hid, D_MODEL)),
            _const_spec((D_MODEL, EXPERT_HIDDEN)), _const_spec((D_MODEL, EXPERT_HIDDEN)),
            _const_spec((EXPERT_HIDDEN, D_MODEL)),
            _const_spec((1, D_MODEL)), _const_spec((1, D_MODEL)),
        ],
        out_specs=pl.BlockSpec((tm, D_MODEL), row),
        out_shape=jax.ShapeDtypeStruct((n, D_MODEL), F32),
        scratch_shapes=[pltpu.VMEM((D_MODEL, hid), BF16), pltpu.VMEM((D_MODEL, hid), BF16),
                        pltpu.SemaphoreType.DMA((2,))],
        compiler_params=_params(("arbitrary",)),
        name="moe_ln",
    )(x, w["w_router"], w["router_bias"], w["w_e1"], w["w_e3"], w["w_e2"], w["w_s1"], w["w_s3"], w["w_s2"],
      w["ln2_w"], w["ln2_b"])


def _prep_weights(w_in, b_in, rwkv_mu, rwkv_w0, rwkv_w2, rwkv_a0, rwkv_a2, rwkv_g2, rwkv_k_k, rwkv_k_a,
                  rwkv_r_k, rwkv_ln_w, rwkv_ln_b, mlstm_conv_w, mlstm_conv_b, mlstm_b_i, mlstm_b_f,
                  mlstm_ln_w, mlstm_ln_b, w_mem_kv, w_pa, w_pb, w_pc, w_out, ln1_w, ln1_b,
                  w_router, router_bias, w_e1, w_e3, w_e2, w_s1, w_s3, w_s2, ln2_w, ln2_b):
    off_ml = RWKV_PROJ
    off_if = off_ml + 4 * D_MLSTM
    off_mq = off_if + 2 * MLSTM_HEADS
    pad_if = LANES - 2 * MLSTM_HEADS
    row = lambda t: t.reshape(1, -1)
    hid = N_EXPERTS * EXPERT_HIDDEN
    w_lora = jnp.zeros((LORA_ALL, 3 * D_RWKV), F32)
    w_lora = w_lora.at[0:W_LORA, 0:D_RWKV].set(rwkv_w2)
    w_lora = w_lora.at[W_LORA:W_LORA + A_LORA, D_RWKV:2 * D_RWKV].set(rwkv_a2)
    w_lora = w_lora.at[W_LORA + A_LORA:, 2 * D_RWKV:].set(rwkv_g2)
    w_tail = lax.optimization_barrier(w_in[:, off_mq:])
    return dict(
        w_rw=w_in[:, :off_ml].astype(BF16), b_rw=row(b_in[:off_ml]),
        w_qk=w_in[:, off_ml:off_ml + 2 * D_MLSTM].astype(BF16), b_qk=row(b_in[off_ml:off_ml + 2 * D_MLSTM]),
        w_vo=jnp.pad(w_in[:, off_ml + 2 * D_MLSTM:off_mq], ((0, 0), (0, pad_if))).astype(BF16),
        b_vo=row(jnp.pad(b_in[off_ml + 2 * D_MLSTM:off_mq], (0, pad_if))),
        w_mq=w_tail[:, :D_MEM].astype(BF16), b_mq=row(b_in[off_mq:off_mq + D_MEM]),
        w_gate=w_tail[:, D_MEM:].astype(BF16), b_gate=row(b_in[off_mq + D_MEM:]),
        w_mem_k=w_mem_kv[:, :D_MEM].astype(BF16), w_mem_v=w_mem_kv[:, D_MEM:].astype(BF16),
        mu=row(rwkv_mu), w0=row(rwkv_w0), a0=row(rwkv_a0), k_k=row(rwkv_k_k), k_a=row(rwkv_k_a),
        r_k=row(rwkv_r_k), rw_ln_w=row(rwkv_ln_w), rw_ln_b=row(rwkv_ln_b), w_lora=w_lora.astype(BF16),
        conv_w=mlstm_conv_w, conv_b=row(mlstm_conv_b),
        b_if=row(jnp.pad(jnp.concatenate([mlstm_b_i, mlstm_b_f]), (0, pad_if))),
        ml_ln_w=row(mlstm_ln_w), ml_ln_b=row(mlstm_ln_b),
        w_pa=w_pa.astype(BF16), w_pb=w_pb.astype(BF16), w_pc=w_pc.astype(BF16), w_out=w_out.astype(BF16),
        ln1_w=row(ln1_w), ln1_b=row(ln1_b),
        w_router=jnp.pad(w_router, ((0, 0), (0, LANES - N_EXPERTS))).astype(BF16),
        router_bias=row(jnp.pad(router_bias, (0, LANES - N_EXPERTS))),
        w_e1=w_e1.astype(BF16), w_e3=w_e3.astype(BF16),
        w_e2=w_e2.reshape(hid, D_MODEL).astype(BF16),
        w_s1=w_s1.astype(BF16), w_s3=w_s3.astype(BF16), w_s2=w_s2.astype(BF16),
        ln2_w=row(ln2_w), ln2_b=row(ln2_b),
    )


def _pair_state(wkv):
    bsz = wkv.shape[0]
    s = wkv.reshape(bsz, RWKV_PAIRS, 2, RWKV_HEAD_DIM, RWKV_HEAD_DIM)
    z = jnp.zeros_like(s[:, :, 0])
    top = jnp.concatenate([s[:, :, 0], z], axis=-1)
    bot = jnp.concatenate([z, s[:, :, 1]], axis=-1)
    return jnp.concatenate([top, bot], axis=-2)


def _unpair_state(h_bd):
    bsz = h_bd.shape[0]
    hd = RWKV_HEAD_DIM
    heads = jnp.stack([h_bd[:, :, :hd, :hd], h_bd[:, :, hd:, hd:]], axis=2)
    return heads.reshape(bsz, RWKV_HEADS, hd, hd)


def _trunk(x3, mem_k, mem_v, shift_prev, wkv0, conv_prev, c0, n0, m0, w, chunk, tm_proj, tail_rows, tail_batches,
           tm_moe):
    bsz, t_len, _ = x3.shape
    n = bsz * t_len
    x = x3.reshape(n, D_MODEL)
    p_rw = _matmul_bias(x, w["w_rw"], w["b_rw"], tm_proj, RWKV_PROJ)
    qk, qk_tail, p_vo = _mlstm_proj(x, w["w_qk"], w["b_qk"], w["conv_w"], w["conv_b"], conv_prev, w["w_vo"],
                                    w["b_vo"], tm_proj)
    p_mq, p_gate = _matmul_bias2(x, w["w_mq"], w["b_mq"], w["w_gate"], w["b_gate"], tm_proj, BF16, F32)

    p_rw3 = p_rw.reshape(bsz, t_len, RWKV_PROJ)
    ya, h_bd, hb, c_t, n_t, m_t = _recurrent(
        p_rw3, shift_prev, _pair_state(wkv0), qk.reshape(bsz, t_len, 2 * D_MLSTM), p_vo.reshape(bsz, t_len, VO_COLS),
        c0, n0, m0.reshape(bsz, 1, MLSTM_HEADS), w, chunk)
    x1 = _tail(x, ya.reshape(n, D_RWKV), hb.reshape(n, D_MLSTM), p_mq, p_gate, mem_k, mem_v, w, tail_rows,
               tail_batches)
    y = _moe(x1, w, tm_moe)

    new_shift = p_rw3[:, t_len - 1:, :]
    new_conv = qk_tail[:, 8 - (CONV_W - 1):, :]
    states = (new_shift, _unpair_state(h_bd), new_conv, c_t, n_t, m_t.reshape(bsz, MLSTM_HEADS))
    return y.reshape(bsz, t_len, D_MODEL), states


def kernel(x_prompt, x_sample, state_rwkv_shift, state_rwkv_wkv, state_mlstm_conv, state_mlstm_c, state_mlstm_n, state_mlstm_m, cache_mem_k, cache_mem_v, mem_prompt, w_in, b_in, rwkv_mu, rwkv_w0, rwkv_w2, rwkv_a0, rwkv_a2, rwkv_g2, rwkv_k_k, rwkv_k_a, rwkv_r_k, rwkv_ln_w, rwkv_ln_b, mlstm_conv_w, mlstm_conv_b, mlstm_b_i, mlstm_b_f, mlstm_ln_w, mlstm_ln_b, w_mem_kv, w_pa, w_pb, w_pc, w_out, ln1_w, ln1_b, w_router, router_bias, w_e1, w_e3, w_e2, w_s1, w_s3, w_s2, ln2_w, ln2_b):
    weights = (w_in, b_in, rwkv_mu, rwkv_w0, rwkv_w2, rwkv_a0, rwkv_a2, rwkv_g2, rwkv_k_k, rwkv_k_a, rwkv_r_k,
               rwkv_ln_w, rwkv_ln_b, mlstm_conv_w, mlstm_conv_b, mlstm_b_i, mlstm_b_f, mlstm_ln_w, mlstm_ln_b,
               w_mem_kv, w_pa, w_pb, w_pc, w_out, ln1_w, ln1_b, w_router, router_bias, w_e1, w_e3, w_e2,
               w_s1, w_s3, w_s2, ln2_w, ln2_b)
    w = _prep_weights(*(t[0] for t in weights))
    bp, t_p, _ = x_prompt.shape
    bs, t_s, _ = x_sample.shape
    hd = MLSTM_HEAD_DIM

    mem_flat = mem_prompt.reshape(bp * N_MEM, D_MODEL)
    zero_bias = jnp.zeros((1, D_MEM), F32)
    mk = _matmul_bias(mem_flat, w["w_mem_k"], zero_bias, bp * N_MEM, D_MEM).reshape(bp, N_MEM, D_MEM)
    mv = _matmul_bias(mem_flat, w["w_mem_v"], zero_bias, bp * N_MEM, D_MEM).reshape(bp, N_MEM, D_MEM)

    y_p, st_p = _trunk(
        x_prompt, mk, mv,
        jnp.zeros((bp, 1, RWKV_PROJ), F32), jnp.zeros((bp, RWKV_HEADS, RWKV_HEAD_DIM, RWKV_HEAD_DIM), F32),
        jnp.zeros((bp, CONV_W - 1, 2 * D_MLSTM), F32), jnp.zeros((bp, MLSTM_HEADS, hd, hd), F32),
        jnp.zeros((bp, MLSTM_HEADS, hd), F32), jnp.zeros((bp, MLSTM_HEADS), F32),
        w, chunk=min(MLSTM_CHUNK, t_p), tm_proj=1024, tail_rows=512, tail_batches=1, tm_moe=512)
    y_s, st_s = _trunk(
        x_sample, cache_mem_k[0].astype(BF16).reshape(bs, N_MEM, D_MEM),
        cache_mem_v[0].astype(BF16).reshape(bs, N_MEM, D_MEM),
        state_rwkv_shift[0], state_rwkv_wkv[0], state_mlstm_conv[0], state_mlstm_c[0], state_mlstm_n[0],
        state_mlstm_m[0],
        w, chunk=min(MLSTM_CHUNK, t_s), tm_proj=bs * t_s, tail_rows=t_s, tail_batches=4, tm_moe=bs * t_s)

    lead = lambda t: t[None]
    mem_shape = (1, bp, N_MEM, MEM_HEADS, MEM_HEAD_DIM)
    return (y_p, y_s, *(lead(t) for t in st_p), mk.reshape(mem_shape), mv.reshape(mem_shape),
            *(lead(t) for t in st_s))
```

```python
import functools
import itertools

import jax
import jax.numpy as jnp
from jax import lax
from jax.experimental import pallas as pl
from jax.experimental.pallas import tpu as pltpu

F32 = jnp.float32
BF16 = jnp.bfloat16

D_MODEL = 1024
DEPTH = 1
D_RWKV = 1024
RWKV_HEAD_DIM = 64
RWKV_HEADS = 16
W_LORA = 64
A_LORA = 64
G_LORA = 128
LORA_ALL = W_LORA + A_LORA + G_LORA
RWKV_PROJ = 3 * D_RWKV + LORA_ALL
RWKV_GN_EPS = 64e-5
DECAY_SCALE_LOG2 = 0.6065306597126334 * 1.4426950408889634
D_MLSTM = 1024
MLSTM_HEADS = 4
MLSTM_HEAD_DIM = 256
CONV_W = 4
MLSTM_CHUNK = 64
N_MEM = 256
MEM_HEADS = 4
MEM_HEAD_DIM = 256
D_MEM = 1024
N_EXPERTS = 32
TOP_K = 4
EXPERT_HIDDEN = 128
ROUTED_SCALE = 2.5
DN_ALPHA = (2 * DEPTH) ** 0.25
LN_EPS = 1e-5

LANES = 128
RWKV_PAIRS = D_RWKV // LANES
VO_COLS = 2 * D_MLSTM + LANES
VMEM_LIMIT = 56 * 1024 * 1024
BATCH_GROUP = 2


def _dot(a, b):
    return jnp.dot(a, b, preferred_element_type=F32)


def _dot_nt(a, b):
    return lax.dot_general(a, b, (((1,), (1,)), ((), ())), preferred_element_type=F32)


def _dot_tn(a, b):
    return lax.dot_general(a, b, (((0,), (0,)), ((), ())), preferred_element_type=F32)


def _split(a):
    hi = a.astype(BF16)
    return hi, (a - hi.astype(F32)).astype(BF16)


def _split3(a):
    hi = a.astype(BF16)
    rest = a - hi.astype(F32)
    mid = rest.astype(BF16)
    return hi, mid, (rest - mid.astype(F32)).astype(BF16)


def _bdot(a, b):
    return _dot(a.astype(BF16), b.astype(BF16))


def _sigmoid(x):
    return 0.5 * jnp.tanh(0.5 * x) + 0.5


def _softplus(x):
    return jnp.maximum(x, 0.0) + jnp.log(1.0 + jnp.exp(-jnp.abs(x)))


def _silu(x):
    return x * _sigmoid(x)


def _layer_norm(x, w, b):
    mu = jnp.mean(x, axis=-1, keepdims=True)
    d = x - mu
    var = jnp.mean(d * d, axis=-1, keepdims=True)
    return d * lax.rsqrt(var + LN_EPS) * w + b


def _params(semantics):
    return pltpu.CompilerParams(dimension_semantics=semantics, vmem_limit_bytes=VMEM_LIMIT)


def _const_spec(shape):
    zeros = (0,) * len(shape)
    return pl.BlockSpec(shape, lambda *_: zeros, pipeline_mode=pl.Buffered(1))


def _mm_kernel(x_ref, w_ref, b_ref, o_ref):
    o_ref[...] = (_dot(x_ref[...].astype(BF16), w_ref[...]) + b_ref[...]).astype(o_ref.dtype)


def _matmul_bias(x, w, b, tm, tn, out_dtype=F32):
    n, k = x.shape
    nc = w.shape[1]
    resident = dict(pipeline_mode=pl.Buffered(1)) if nc == tn else {}
    return pl.pallas_call(
        _mm_kernel,
        grid=(nc // tn, n // tm),
        in_specs=[
            pl.BlockSpec((tm, k), lambda j, i: (i, 0)),
            pl.BlockSpec((k, tn), lambda j, i: (0, j), **resident),
            pl.BlockSpec((1, tn), lambda j, i: (0, j), **resident),
        ],
        out_specs=pl.BlockSpec((tm, tn), lambda j, i: (i, j)),
        out_shape=jax.ShapeDtypeStruct((n, nc), out_dtype),
        compiler_params=_params(("arbitrary", "arbitrary")),
        name="matmul_bias",
    )(x, w, b)


def _mm2_kernel(x_ref, w1_ref, b1_ref, w2_ref, b2_ref, o1_ref, o2_ref):
    xb = x_ref[...].astype(BF16)
    o1_ref[...] = (_dot(xb, w1_ref[...]) + b1_ref[...]).astype(o1_ref.dtype)
    o2_ref[...] = (_dot(xb, w2_ref[...]) + b2_ref[...]).astype(o2_ref.dtype)


def _matmul_bias2(x, w1, b1, w2, b2, tm, dtype1, dtype2):
    n, k = x.shape
    n1, n2 = w1.shape[1], w2.shape[1]
    row = lambda i: (i, 0)
    return pl.pallas_call(
        _mm2_kernel,
        grid=(n // tm,),
        in_specs=[pl.BlockSpec((tm, k), row), _const_spec((k, n1)), _const_spec((1, n1)), _const_spec((k, n2)),
                  _const_spec((1, n2))],
        out_specs=[pl.BlockSpec((tm, n1), row), pl.BlockSpec((tm, n2), row)],
        out_shape=[jax.ShapeDtypeStruct((n, n1), dtype1), jax.ShapeDtypeStruct((n, n2), dtype2)],
        compiler_params=_params(("arbitrary",)),
        name="matmul_bias_pair",
    )(x, w1, b1, w2, b2)


def _mlstm_proj_kernel(x_ref, w_ref, b_ref, cw_ref, cb_ref, prev_ref, w2_ref, b2_ref, o_ref, tail_ref, o2_ref, carry,
                       *, rows, tiles_per_batch):
    row_tile = pl.program_id(0)
    n_seg = prev_ref.shape[0]
    tm, tn = o_ref.shape
    n2 = o2_ref.shape[1]
    pad = jnp.zeros((8 - (CONV_W - 1), LANES), F32)
    row8 = lax.broadcasted_iota(jnp.int32, (8, LANES), 0)

    if n_seg == 1:
        @pl.when(row_tile % tiles_per_batch == 0)
        def _():
            carry[0:8 - (CONV_W - 1), :] = jnp.zeros((8 - (CONV_W - 1), tn), F32)
            carry[8 - (CONV_W - 1):8, :] = prev_ref[0]

    xb = x_ref[...].astype(BF16)
    n_parts = 4
    q_step = tn // n_parts
    for part in range(n_parts):
        cs = slice(part * q_step, (part + 1) * q_step)
        o_ref[:, cs] = _dot(xb, w_ref[:, cs]) + b_ref[:, cs]
    for seg in range(n_seg):
        tail_ref[seg] = o_ref[(seg + 1) * rows - 8:(seg + 1) * rows, :]
    strip = min(rows, 64)

    def conv_strips(r_lo, r_hi):
        for r0 in reversed(range(r_lo, r_hi, strip)):
            for c0 in range(0, tn, LANES):
                cols = slice(c0, c0 + LANES)
                half_scale = 0.5 * MLSTM_HEAD_DIM ** -0.5 if c0 >= D_MLSTM else 0.5
                if n_seg > 1 and r0 % rows == 0:
                    head = jnp.concatenate([pad, prev_ref[r0 // rows, :, cols]], axis=0)
                elif r0 == 0:
                    head = carry[:, cols]
                else:
                    head = o_ref[r0 - 8:r0, cols]
                x = o_ref[r0:r0 + strip, cols]
                blocks = [head] + [x[r:r + 8, :] for r in range(0, strip, 8)]
                conv = cb_ref[:, cols] + cw_ref[CONV_W - 1:CONV_W, cols] * x
                for j in range(CONV_W - 1):
                    lag = CONV_W - 1 - j
                    turned = [pltpu.roll(blk, lag, 0) for blk in blocks]
                    shifted = jnp.concatenate([jnp.where(row8 < lag, turned[i], turned[i + 1])
                                               for i in range(len(blocks) - 1)], axis=0)
                    conv = conv + cw_ref[j:j + 1, cols] * shifted
                o_ref[r0:r0 + strip, cols] = conv * (half_scale * jnp.tanh(0.5 * conv) + half_scale)

    col_step = -(-n2 // (n_parts * LANES)) * LANES
    row_step = tm // n_parts
    for part in reversed(range(n_parts)):
        cs = slice(part * col_step, min((part + 1) * col_step, n2))
        o2_ref[:, cs] = _dot(xb, w2_ref[:, cs]) + b2_ref[:, cs]
        conv_strips(part * row_step, (part + 1) * row_step)

    if n_seg == 1:
        carry[...] = tail_ref[0]


def _mlstm_proj(x, w, b, conv_w, conv_b, conv_prev, w2, b2, tm):
    n, k = x.shape
    batch = conv_prev.shape[0]
    t_len = n // batch
    rows = min(tm, t_len)
    n_seg = tm // rows
    tiles_per_batch = t_len // rows
    nc, n2 = w.shape[1], w2.shape[1]
    seg = lambda i: (i // tiles_per_batch if n_seg == 1 else i, 0, 0)
    return pl.pallas_call(
        functools.partial(_mlstm_proj_kernel, rows=rows, tiles_per_batch=tiles_per_batch),
        grid=(n // tm,),
        in_specs=[
            pl.BlockSpec((tm, k), lambda i: (i, 0)),
            _const_spec((k, nc)), _const_spec((1, nc)), _const_spec((CONV_W, nc)), _const_spec((1, nc)),
            pl.BlockSpec((n_seg, CONV_W - 1, nc), seg),
            _const_spec((k, n2)), _const_spec((1, n2)),
        ],
        out_specs=[
            pl.BlockSpec((tm, nc), lambda i: (i, 0)),
            pl.BlockSpec((n_seg, 8, nc), seg),
            pl.BlockSpec((tm, n2), lambda i: (i, 0)),
        ],
        out_shape=[jax.ShapeDtypeStruct((n, nc), F32), jax.ShapeDtypeStruct((batch, 8, nc), F32),
                   jax.ShapeDtypeStruct((n, n2), F32)],
        scratch_shapes=[pltpu.VMEM((8, nc), F32)],
        compiler_params=_params(("arbitrary",)),
        name="mlstm_proj_conv",
    )(x, w, b, conv_w, conv_b, conv_prev, w2, b2)


def _rwkv_body(p_ref, prev_ref, h0_ref, mu_ref, w0_ref, a0_ref, kk_ref, ka_ref, rk_ref, lnw_ref, lnb_ref,
               wl_ref, y_ref, hout_ref, h_scr, xbuf, ar_scr, bk_scr, bkh_scr, v_scr, bonus_scr, g_scr, gamma_scr,
               *, chunk, n_chunks, n_items, companion):
    step = pl.program_id(0)
    L = chunk
    half = 2 * L
    prep_chunk = jnp.minimum(step, n_items - 1) % n_chunks
    solve_chunk = jnp.maximum(step - 1, 0) % n_chunks

    @pl.when(step == 0)
    def _():
        for ref in (ar_scr, bk_scr, bkh_scr, v_scr, bonus_scr, g_scr):
            ref[...] = jnp.zeros(ref.shape, ref.dtype)
        gamma_scr[...] = jnp.ones(gamma_scr.shape, F32)

    @pl.when(solve_chunk == 0)
    def _():
        h_scr[...] = h0_ref[...]

    @pl.when(prep_chunk == 0)
    def _():
        for bi in range(BATCH_GROUP):
            xbuf[bi, 0:1, :] = prev_ref[bi]

    ti = lax.broadcasted_iota(jnp.int32, (L, L), 0)
    tj = lax.broadcasted_iota(jnp.int32, (L, L), 1)
    ltri = (tj <= ti).astype(BF16)
    row_m = lax.broadcasted_iota(jnp.int32, (L, half), 0)
    col_m = lax.broadcasted_iota(jnp.int32, (L, half), 1)
    col_in = jnp.where(col_m >= L, col_m - L, col_m)
    strict = col_in < row_m
    incl = col_in <= row_m
    eye = (col_in == row_m).astype(F32)
    gi = lax.broadcasted_iota(jnp.int32, (LANES, LANES), 0)
    gj = lax.broadcasted_iota(jnp.int32, (LANES, LANES), 1)
    same_head = (gi >= RWKV_HEAD_DIM) == (gj >= RWKV_HEAD_DIM)
    head_ones = same_head.astype(BF16)
    first_nat = lax.broadcasted_iota(jnp.int32, (L, LANES), 1) < RWKV_HEAD_DIM
    first_sbs = col_m < L
    lane_l = lax.broadcasted_iota(jnp.int32, (L, LORA_ALL), 1)

    def diag(x, first):
        zero = jnp.zeros_like(x)
        return jnp.concatenate([jnp.where(first, x, zero), jnp.where(first, zero, x)], axis=0)

    def seg_sum(x):
        return _dot(x.astype(BF16), head_ones)

    units = [(bi, p) for bi in range(BATCH_GROUP) for p in range(RWKV_PAIRS)]
    idx = range(len(units))
    lane_of = lambda p: slice(p * LANES, (p + 1) * LANES)

    ar = [ar_scr[bi, :, lane_of(p)] for bi, p in units]
    bk = [bk_scr[bi, :, lane_of(p)] for bi, p in units]
    bkh = [bkh_scr[bi, :, lane_of(p)] for bi, p in units]
    v_b = [v_scr[bi, :, lane_of(p)] for bi, p in units]
    bonus = [bonus_scr[bi, :, lane_of(p)] for bi, p in units]
    gate = [g_scr[bi, :, lane_of(p)] for bi, p in units]
    gamma = [gamma_scr[bi, :, lane_of(p)] for bi, p in units]

    first_row = lax.broadcasted_iota(jnp.int32, (L, RWKV_PROJ), 0) == 0

    def prepare_row(bi):
        cur = p_ref[bi]
        prev = jnp.where(first_row, xbuf[bi, 0:1, :], pltpu.roll(cur, 1, 0))
        xbuf[bi, 0:1, :] = cur[L - 1:L, :]
        xr = cur + (prev - cur) * mu_ref[...]
        r_all = xr[:, 0:D_RWKV]
        k_all = xr[:, D_RWKV:2 * D_RWKV]
        v_all = xr[:, 2 * D_RWKV:3 * D_RWKV]
        slab = xr[:, 3 * D_RWKV:]
        act = jnp.where(lane_l < W_LORA, jnp.tanh(slab),
                        jnp.where(lane_l < W_LORA + A_LORA, slab, _sigmoid(slab)))
        yield
        lora = _bdot(act, wl_ref[...])
        yield
        lw = -DECAY_SCALE_LOG2 * _sigmoid(w0_ref[...] + lora[:, 0:D_RWKV])
        a_sig = _sigmoid(a0_ref[...] + lora[:, D_RWKV:2 * D_RWKV])
        lw_hi, lw_lo = _split(lw)
        yield
        kk0 = k_all * kk_ref[...]
        k2 = k_all * (1.0 + (a_sig - 1.0) * ka_ref[...])
        sums = jnp.concatenate([kk0 * kk0, r_all * k2 * rk_ref[...]], axis=0).astype(BF16)
        yield
        cum = _dot(ltri, lw_hi) + _dot(ltri, lw_lo)
        sums = jnp.concatenate([_dot(sums[:, lane_of(p)], head_ones) for p in range(RWKV_PAIRS)], axis=1)
        yield
        kk = kk0 * lax.rsqrt(jnp.maximum(sums[0:L], 1e-24))
        b_vec = kk * a_sig
        c_last = cum[L - 1:L, :]
        e_neg = jnp.exp2(-cum)
        e_end = jnp.exp2(c_last - cum)
        yield
        ar_scr[bi] = jnp.concatenate([-kk * jnp.exp2(cum - lw), r_all * jnp.exp2(cum)], axis=0).astype(BF16)
        bk_scr[bi] = jnp.concatenate([b_vec * e_neg, k2 * e_neg], axis=0).astype(BF16)
        yield
        bkh_scr[bi] = jnp.concatenate([b_vec * e_end, k2 * e_end], axis=0).astype(BF16)
        v_scr[bi] = v_all.astype(BF16)
        bonus_scr[bi] = sums[L:] * v_all
        g_scr[bi] = lora[:, 2 * D_RWKV:]
        gamma_scr[bi] = jnp.exp2(c_last)

    segments = itertools.chain(*(prepare_row(bi) for bi in range(BATCH_GROUP)))

    def prepare_some(count):
        for _ in range(count):
            next(segments, None)
            next(companion, None)

    n_sq = L.bit_length() - 1
    per_stage = 2
    bk_d = [jnp.concatenate([diag(bk[u][0:L], first_nat), diag(bk[u][L:], first_nat)], axis=0) for u in idx]
    v_d = [diag(v_b[u], first_nat) for u in idx]
    h_bd = [h_scr[bi, p] for bi, p in units]
    sc = [_dot_nt(ar[u], bk_d[u]) for u in idx]
    prepare_some(per_stage)
    arh = [_dot_nt(ar[u], h_bd[u].astype(BF16)) for u in idx]
    m_ab = [jnp.where(strict, sc[u][0:L, 0:half], 0.0) for u in idx]
    m_ak = [jnp.where(strict, sc[u][0:L, half:], 0.0).astype(BF16) for u in idx]
    n_rbk = [jnp.concatenate([jnp.where(incl, sc[u][L:, 0:half], 0.0),
                              jnp.where(incl, sc[u][L:, half:], 0.0)], axis=1).astype(BF16) for u in idx]
    prepare_some(per_stage)
    w0 = [arh[u][0:L] + _dot(m_ak[u], v_d[u]) for u in idx]
    inv = [eye + m_ab[u] for u in idx]
    m_b = [m_ab[u].astype(BF16) for u in idx]
    pw = [_dot(m_b[u], diag(m_b[u], first_sbs)) for u in idx]
    prepare_some(per_stage)
    for i in range(n_sq - 1):
        pw_d = [diag(pw[u].astype(BF16), first_sbs) for u in idx]
        if i + 1 < n_sq - 1:
            z = [_dot(jnp.concatenate([inv[u], pw[u]], axis=0).astype(BF16), pw_d[u]) for u in idx]
            inv = [inv[u] + z[u][0:L] for u in idx]
            pw = [z[u][L:] for u in idx]
        else:
            inv = [inv[u] + _dot(inv[u].astype(BF16), pw_d[u]) for u in idx]
        prepare_some(per_stage)
    c_b = [_dot(inv[u].astype(BF16), diag(w0[u].astype(BF16), first_nat)).astype(BF16) for u in idx]
    cv = [jnp.concatenate([c_b[u], v_b[u]], axis=0) for u in idx]
    y = [arh[u][L:] + _dot(n_rbk[u], jnp.concatenate([diag(c_b[u], first_nat), v_d[u]], axis=0)) for u in idx]
    for u, (bi, p) in enumerate(units):
        h_scr[bi, p] = h_bd[u] * gamma[u] + jnp.where(same_head, _dot_tn(cv[u], bkh[u]), 0.0)
    mean = [seg_sum(y[u]) * (1.0 / RWKV_HEAD_DIM) for u in idx]
    dev = [y[u] - mean[u] for u in idx]
    var = [seg_sum(dev[u] * dev[u]) * (1.0 / RWKV_HEAD_DIM) for u in idx]
    for u, (bi, p) in enumerate(units):
        s = lane_of(p)
        yn = dev[u] * lax.rsqrt(var[u] + RWKV_GN_EPS) * lnw_ref[:, s] + lnb_ref[:, s]
        y_ref[bi, :, s] = ((yn + bonus[u]) * gate[u]).astype(y_ref.dtype)
    prepare_some(8 * BATCH_GROUP)
    for _ in companion:
        pass

    @pl.when((solve_chunk == n_chunks - 1) & (step > 0))
    def _():
        hout_ref[...] = h_scr[...]


def _mlstm_stages(qk_ref, p_ref, c0_ref, n0_ref, m0_ref, bif_ref, lnw_ref, lnb_ref,
                  h_ref, cout_ref, nout_ref, mout_ref, c_scr, n_scr, m_scr, *, chunk, first, last):
    L = chunk
    d = D_MLSTM
    nh = MLSTM_HEADS

    @pl.when(first)
    def _():
        c_scr[...] = c0_ref[...]
        n_scr[...] = n0_ref[...]
        m_scr[...] = m0_ref[...]

    yield

    ti =lax.broadcasted_iota(jnp.int32, (L, L), 0)
    tj = lax.broadcasted_iota(jnp.int32, (L, L), 1)
    causal = tj <= ti
    ltri = causal.astype(BF16)
    utri = (ti <= tj).astype(BF16)
    sel = (lax.broadcasted_iota(jnp.int32, (8, LANES), 0)
           == lax.broadcasted_iota(jnp.int32, (8, LANES), 1)).astype(BF16)
    lane_g = lax.broadcasted_iota(jnp.int32, (L, LANES), 1)
    is_f = (lane_g >= nh) & (lane_g < 2 * nh)

    rows = []
    for bi in range(BATCH_GROUP):
        gates = p_ref[bi, :, 2 * d:] + bif_ref[...]
        glog = jnp.where(is_f, -_softplus(-gates), gates)
        parts = _split3(glog)
        b_cols = sum(_dot(ltri, t) for t in parts)
        g_rows = sum(_dot_nt(sel, t) for t in parts)
        b_rows = sum(_dot(t, utri) for t in _split3(g_rows))
        rows.append(dict(q=qk_ref[bi, :, 0:d], k=qk_ref[bi, :, d:2 * d], v=p_ref[bi, :, 0:d],
                         o=_sigmoid(p_ref[bi, :, d:2 * d]),
                         glog=glog, b_cols=b_cols, g_rows=g_rows, b_rows=b_rows))
        yield

    units = [(bi, h) for bi in range(BATCH_GROUP) for h in range(nh)]
    idx = range(len(units))
    head = lambda h: slice(h * MLSTM_HEAD_DIM, (h + 1) * MLSTM_HEAD_DIM)
    q = [rows[bi]["q"][:, head(h)] for bi, h in units]
    k = [rows[bi]["k"][:, head(h)] for bi, h in units]
    v = [rows[bi]["v"][:, head(h)] for bi, h in units]
    q_b = [t.astype(BF16) for t in q]
    k_b = [t.astype(BF16) for t in k]
    c_st = [c_scr[bi, h] for bi, h in units]
    n_st = [n_scr[bi, h:h + 1, :] for bi, h in units]
    m_prev = [m_scr[bi, :, h:h + 1] for bi, h in units]
    qk_t = [_dot_nt(q_b[u], k_b[u]) for u in idx]
    yield
    qc = [_dot_nt(q_b[u], c_st[u].astype(BF16)) for u in idx]
    yield
    s, w_inter, m_t, wk, carry, m_next = [], [], [], [], [], []
    for u, (bi, h) in enumerate(units):
        t = rows[bi]
        ig_row = t["g_rows"][h:h + 1, :]
        b_row = t["b_rows"][nh + h:nh + h + 1, :]
        ig_col = t["glog"][:, h:h + 1]
        b_col = t["b_cols"][:, nh + h:nh + h + 1]
        log_w = jnp.where(causal, b_col - b_row + ig_row, -jnp.inf)
        inter = b_col + m_prev[u]
        m_now = jnp.maximum(inter, jnp.max(log_w, axis=-1, keepdims=True))
        s.append(qk_t[u] * jnp.exp(log_w - m_now))
        w_inter.append(jnp.exp(inter - m_now))
        m_t.append(m_now)
        b_last = b_col[L - 1:L, :]
        g_col = b_last - b_col + ig_col
        m_new = jnp.maximum(b_last + m_prev[u], jnp.max(g_col, axis=0, keepdims=True))
        carry.append(jnp.exp(b_last + m_prev[u] - m_new))
        wk.append(jnp.exp(g_col - m_new))
        m_next.append(m_new)
        if u % 2 == 1:
            yield
    sv = [_dot(s[u].astype(BF16), v[u].astype(BF16)) for u in idx]
    yield
    upd = [_dot_tn((wk[u] * v[u]).astype(BF16), k_b[u]) for u in idx]
    yield
    qn =[jnp.sum(q[u] * n_st[u], axis=-1, keepdims=True) for u in idx]
    s_sum = [jnp.sum(s[u], axis=-1, keepdims=True) for u in idx]
    den = [jnp.maximum(jnp.abs(w_inter[u] * qn[u] + s_sum[u]), jnp.exp(-m_t[u])) for u in idx]
    yield
    hb = [rows[bi]["o"][:, head(h)] * ((w_inter[u] * qc[u] + sv[u]) / den[u]) for u, (bi, h) in enumerate(units)]
    yield
    mu = [jnp.mean(hb[u], axis=-1, keepdims=True) for u in idx]
    dev = [hb[u] - mu[u] for u in idx]
    var = [jnp.mean(dev[u] * dev[u], axis=-1, keepdims=True) for u in idx]
    yield
    n_new = [carry[u] * n_st[u] + jnp.sum(wk[u] * k[u], axis=0, keepdims=True) for u in idx]
    for u, (bi, h) in enumerate(units):
        sl = head(h)
        h_ref[bi, :, sl] = (dev[u] * lax.rsqrt(var[u] + LN_EPS) * lnw_ref[:, sl] + lnb_ref[:, sl]).astype(h_ref.dtype)
        c_scr[bi, h] = carry[u] * c_st[u] + upd[u]
        n_scr[bi, h:h + 1, :] = n_new[u]
        m_scr[bi, :, h:h + 1] = m_next[u]
        if u % 4 == 3:
            yield

    @pl.when(last)
    def _():
        cout_ref[...] = c_scr[...]
        nout_ref[...] = n_scr[...]
        mout_ref[...] = m_scr[...]


RW_REFS = (12, 2, 9)
ML_REFS = (8, 4, 3)


def _recurrent_kernel(*refs, chunk, n_chunks, n_items):
    groups, pos = [], 0
    for n_rw, n_ml in zip(RW_REFS, ML_REFS):
        groups.append((refs[pos:pos + n_rw], refs[pos + n_rw:pos + n_rw + n_ml]))
        pos += n_rw + n_ml
    rw_refs = [r for rw_part, _ in groups for r in rw_part]
    ml_refs = [r for _, ml_part in groups for r in ml_part]
    step = pl.program_id(0)
    solve_chunk = jnp.maximum(step - 1, 0) % n_chunks
    ml = _mlstm_stages(*ml_refs, chunk=chunk, first=solve_chunk == 0,
                       last=(solve_chunk == n_chunks - 1) & (step > 0))
    next(ml)
    _rwkv_body(*rw_refs, chunk=chunk, n_chunks=n_chunks, n_items=n_items, companion=ml)


def _recurrent(p_rw, shift_prev, h0_bd, qk, p_vo, c0, n0, m0, w, chunk):
    batch, t_len, _ = p_rw.shape
    n_chunks = t_len // chunk
    bg = BATCH_GROUP
    hd = MLSTM_HEAD_DIM
    n_items = (batch // bg) * n_chunks
    prep = lambda i: jnp.minimum(i, n_items - 1)
    solve = lambda i: jnp.maximum(i - 1, 0)
    prep_seq = lambda i: (prep(i) // n_chunks, prep(i) % n_chunks, 0)
    solve_seq = lambda i: (solve(i) // n_chunks, solve(i) % n_chunks, 0)
    vec = lambda n: pl.BlockSpec((1, n), lambda i: (0, 0))
    group = lambda *dims: pl.BlockSpec((bg,) + dims, lambda i: (solve(i) // n_chunks,) + (0,) * len(dims))
    rw_state = group(RWKV_PAIRS, LANES, LANES)
    ml_states = [group(MLSTM_HEADS, hd, hd), group(MLSTM_HEADS, hd), group(1, MLSTM_HEADS)]
    wide = lambda rows, dtype: pltpu.VMEM((bg, rows, D_RWKV), dtype)
    kern = functools.partial(_recurrent_kernel, chunk=chunk, n_chunks=n_chunks, n_items=n_items)
    return pl.pallas_call(
        kern,
        grid=(n_items + 1,),
        in_specs=[
            pl.BlockSpec((bg, chunk, RWKV_PROJ), prep_seq),
            pl.BlockSpec((bg, 1, RWKV_PROJ), lambda i: (prep(i) // n_chunks, 0, 0)),
            rw_state,
            vec(RWKV_PROJ), vec(D_RWKV), vec(D_RWKV), vec(D_RWKV), vec(D_RWKV), vec(D_RWKV), vec(D_RWKV),
            vec(D_RWKV),
            pl.BlockSpec((LORA_ALL, 3 * D_RWKV), lambda i: (0, 0)),
            pl.BlockSpec((bg, chunk, 2 * D_MLSTM), solve_seq),
            pl.BlockSpec((bg, chunk, VO_COLS), solve_seq),
            *ml_states,
            vec(LANES), vec(D_MLSTM), vec(D_MLSTM),
        ],
        out_specs=[
            pl.BlockSpec((bg, chunk, D_RWKV), solve_seq), rw_state,
            pl.BlockSpec((bg, chunk, D_MLSTM), solve_seq), *ml_states,
        ],
        out_shape=[
            jax.ShapeDtypeStruct((batch, t_len, D_RWKV), BF16),
            jax.ShapeDtypeStruct((batch, RWKV_PAIRS, LANES, LANES), F32),
            jax.ShapeDtypeStruct((batch, t_len, D_MLSTM), BF16),
            jax.ShapeDtypeStruct((batch, MLSTM_HEADS, hd, hd), F32),
            jax.ShapeDtypeStruct((batch, MLSTM_HEADS, hd), F32),
            jax.ShapeDtypeStruct((batch, 1, MLSTM_HEADS), F32),
        ],
        scratch_shapes=[
            pltpu.VMEM((bg, RWKV_PAIRS, LANES, LANES), F32),
            pltpu.VMEM((bg, 8, RWKV_PROJ), F32),
            wide(2 * chunk, BF16), wide(2 * chunk, BF16), wide(2 * chunk, BF16), wide(chunk, BF16),
            wide(chunk, F32), wide(chunk, F32), wide(1, F32),
            pltpu.VMEM((bg, MLSTM_HEADS, hd, hd), F32),
            pltpu.VMEM((bg, MLSTM_HEADS, hd), F32),
            pltpu.VMEM((bg, 1, MLSTM_HEADS), F32),
        ],
        compiler_params=_params(("arbitrary",)),
        name="rwkv7_mlstm_recurrent",
    )(p_rw, shift_prev, h0_bd, w["mu"], w["w0"], w["a0"], w["k_k"], w["k_a"], w["r_k"], w["rw_ln_w"],
      w["rw_ln_b"], w["w_lora"], qk, p_vo, c0, n0, m0, w["b_if"], w["ml_ln_w"], w["ml_ln_b"])


def _tail_kernel(x_ref, ya_ref, hb_ref, mq_ref, gate_ref, mk_ref, mv_ref, wpa_ref, wpb_ref, wpc_ref, wout_ref,
                 lnw_ref, lnb_ref, o_ref, *, rows):
    n_seg = mk_ref.shape[0]
    d = D_MODEL
    head = lambda h: slice(h * MEM_HEAD_DIM, (h + 1) * MEM_HEAD_DIM)
    units = [(b, h) for b in range(n_seg) for h in range(MEM_HEADS)]
    idx = range(len(units))
    mk = [mk_ref[b].astype(BF16) for b in range(n_seg)]
    mv = [mv_ref[b].astype(BF16) for b in range(n_seg)]
    s = [_dot_nt(mq_ref[b * rows:(b + 1) * rows, head(h)], mk[b][:, head(h)]) * (MEM_HEAD_DIM ** -0.5)
         for b, h in units]
    y_a = _sigmoid(gate_ref[:, 0:d]) * _dot(ya_ref[...], wpa_ref[...])
    top = [jnp.max(s[u], axis=-1, keepdims=True) for u in idx]
    e = [jnp.exp(s[u] - top[u]) for u in idx]
    y_b = _sigmoid(gate_ref[:, d:2 * d]) * _dot(hb_ref[...], wpb_ref[...])
    den = [jnp.sum(e[u], axis=-1, keepdims=True) for u in idx]
    prob = [(e[u] / den[u]).astype(BF16) for u in idx]
    out = [_dot(prob[u], mv[b][:, head(h)]).astype(BF16) for u, (b, h) in enumerate(units)]
    attn = [jnp.concatenate(out[b * MEM_HEADS:(b + 1) * MEM_HEADS], axis=-1) for b in range(n_seg)]
    attn = attn[0] if n_seg == 1 else jnp.concatenate(attn, axis=0)
    y_c = _sigmoid(gate_ref[:, 2 * d:3 * d]) * _dot(attn, wpc_ref[...])
    mixed = y_a + y_b + y_c
    u = _bdot(mixed, wout_ref[...])
    o_ref[...] = _layer_norm(DN_ALPHA * x_ref[...] + u, lnw_ref[...], lnb_ref[...])


def _tail(x, ya, hb, p_mq, p_gate, mem_k, mem_v, w, rows, batches):
    n = x.shape[0]
    tm = rows * batches
    tiles_per_batch = (n // mem_k.shape[0]) // rows if batches == 1 else 1
    row = lambda i: (i, 0)
    mem = pl.BlockSpec((batches, N_MEM, D_MEM), lambda i: (i // tiles_per_batch, 0, 0))
    tok = lambda cols: pl.BlockSpec((tm, cols), row)
    return pl.pallas_call(
        functools.partial(_tail_kernel, rows=rows),
        grid=(n // tm,),
        in_specs=[
            tok(D_MODEL), tok(D_RWKV), tok(D_MLSTM), tok(D_MEM), tok(3 * D_MODEL), mem, mem,
            _const_spec((D_RWKV, D_MODEL)), _const_spec((D_MLSTM, D_MODEL)), _const_spec((D_MEM, D_MODEL)),
            _const_spec((D_MODEL, D_MODEL)), _const_spec((1, D_MODEL)), _const_spec((1, D_MODEL)),
        ],
        out_specs=tok(D_MODEL),
        out_shape=jax.ShapeDtypeStruct((n, D_MODEL), F32),
        compiler_params=_params(("arbitrary",)),
        name="attn_merge_ln",
    )(x, ya, hb, p_mq, p_gate, mem_k, mem_v, w["w_pa"], w["w_pb"], w["w_pc"], w["w_out"], w["ln1_w"],
      w["ln1_b"])


def _expert_copy(src_hbm, dst, sem, e):
    return pltpu.make_async_copy(src_hbm.at[e], dst.at[:, pl.ds(e * EXPERT_HIDDEN, EXPERT_HIDDEN)], sem)


def _moe_kernel(x_ref, wr_ref, rb_ref, w1_hbm, w3_hbm, w2_ref, ws1_ref, ws3_ref, ws2_ref, lnw_ref, lnb_ref, o_ref,
                w1_ref, w3_ref, sems):
    @pl.when(pl.program_id(0) == 0)
    def _():
        copies = [_expert_copy(src, dst, sems.at[i], e)
                  for i, (src, dst) in enumerate(((w1_hbm, w1_ref), (w3_hbm, w3_ref))) for e in range(N_EXPERTS)]
        for cp in copies:
            cp.start()
        for cp in copies:
            cp.wait()

    x = x_ref[...]
    xb = x.astype(BF16)
    group = 8
    width = group * EXPERT_HIDDEN
    n_slabs = N_EXPERTS // group

    def up_project(s):
        cols = slice(s * width, (s + 1) * width)
        return _dot(xb, w1_ref[:, cols]), _dot(xb, w3_ref[:, cols])

    logits = _dot(xb, wr_ref[...])
    ahead = up_project(0)
    scores = _sigmoid(logits)
    lane = lax.broadcasted_iota(jnp.int32, scores.shape, 1)
    work = jnp.where(lane < N_EXPERTS, scores + rb_ref[...], -jnp.inf)
    chosen = jnp.zeros(scores.shape, dtype=jnp.bool_)
    for _ in range(TOP_K):
        best = jnp.max(work, axis=-1, keepdims=True)
        first = jnp.min(jnp.where(work == best, lane, LANES), axis=-1, keepdims=True)
        pick = lane == first
        chosen = chosen | pick
        work = jnp.where(pick, -jnp.inf, work)
    sel = jnp.where(chosen, scores, 0.0)
    gates = sel / jnp.sum(sel, axis=-1, keepdims=True) * ROUTED_SCALE

    acc = _dot((_silu(_dot(xb, ws1_ref[...])) * _dot(xb, ws3_ref[...])).astype(BF16), ws2_ref[...])
    for s in range(n_slabs):
        cols = slice(s * width, (s + 1) * width)
        h1, h3 = ahead
        if s + 1 < n_slabs:
            ahead = up_project(s + 1)
        hidden = _silu(h1) * h3
        gated = jnp.concatenate(
            [hidden[:, e * EXPERT_HIDDEN:(e + 1) * EXPERT_HIDDEN] * gates[:, s * group + e:s * group + e + 1]
             for e in range(group)], axis=-1)
        acc = acc + _dot(gated.astype(BF16), w2_ref[cols, :])
    o_ref[...] = _layer_norm(DN_ALPHA * x + acc, lnw_ref[...], lnb_ref[...])


def _moe(x, w, tm):
    n = x.shape[0]
    row = lambda i: (i, 0)
    hid = N_EXPERTS * EXPERT_HIDDEN
    return pl.pallas_call(
        _moe_kernel,
        grid=(n // tm,),
        in_specs=[
            pl.BlockSpec((tm, D_MODEL), row),
            _const_spec((D_MODEL, LANES)), _const_spec((1, LANES)),
            pl.BlockSpec(memory_space=pl.ANY), pl.BlockSpec(memory_space=pl.ANY), _const_spec((hid, D_MODEL)),
            _const_spec((D_MODEL, EXPERT_HIDDEN)), _const_spec((D_MODEL, EXPERT_HIDDEN)),
            _const_spec((EXPERT_HIDDEN, D_MODEL)),
            _const_spec((1, D_MODEL)), _const_spec((1, D_MODEL)),
        ],
        out_specs=pl.BlockSpec((tm, D_MODEL), row),
        out_shape=jax.ShapeDtypeStruct((n, D_MODEL), F32),
        scratch_shapes=[pltpu.VMEM((D_MODEL, hid), BF16), pltpu.VMEM((D_MODEL, hid), BF16),
                        pltpu.SemaphoreType.DMA((2,))],
        compiler_params=_params(("arbitrary",)),
        name="moe_ln",
    )(x, w["w_router"], w["router_bias"], w["w_e1"], w["w_e3"], w["w_e2"], w["w_s1"], w["w_s3"], w["w_s2"],
      w["ln2_w"], w["ln2_b"])


W_IN_GROUPS = (
    (0, RWKV_PROJ, RWKV_PROJ),
    (RWKV_PROJ, 2 * D_MLSTM, 2 * D_MLSTM),
    (RWKV_PROJ + 2 * D_MLSTM, 2 * D_MLSTM + 2 * MLSTM_HEADS, VO_COLS),
    (RWKV_PROJ + 4 * D_MLSTM + 2 * MLSTM_HEADS, D_MEM, D_MEM),
    (RWKV_PROJ + 4 * D_MLSTM + 2 * MLSTM_HEADS + D_MEM, 3 * D_MODEL, 3 * D_MODEL),
)


def _split_w_in_kernel(w_ref, *out_refs):
    for (start, cols, padded), o_ref in zip(W_IN_GROUPS, out_refs):
        part = w_ref[:, start:start + cols].astype(BF16)
        if padded > cols:
            part = jnp.concatenate([part, jnp.zeros((part.shape[0], padded - cols), BF16)], axis=1)
        o_ref[...] = part


def _split_w_in(w_in):
    k, total = w_in.shape
    rows = 128
    return pl.pallas_call(
        _split_w_in_kernel,
        grid=(k // rows,),
        in_specs=[pl.BlockSpec((rows, total), lambda i: (i, 0))],
        out_specs=[pl.BlockSpec((rows, padded), lambda i: (i, 0)) for _, _, padded in W_IN_GROUPS],
        out_shape=[jax.ShapeDtypeStruct((k, padded), BF16) for _, _, padded in W_IN_GROUPS],
        compiler_params=_params(("arbitrary",)),
        name="split_w_in",
    )(w_in)


def _prep_weights(w_in, b_in, rwkv_mu, rwkv_w0, rwkv_w2, rwkv_a0, rwkv_a2, rwkv_g2, rwkv_k_k, rwkv_k_a,
                  rwkv_r_k, rwkv_ln_w, rwkv_ln_b, mlstm_conv_w, mlstm_conv_b, mlstm_b_i, mlstm_b_f,
                  mlstm_ln_w, mlstm_ln_b, w_mem_kv, w_pa, w_pb, w_pc, w_out, ln1_w, ln1_b,
                  w_router, router_bias, w_e1, w_e3, w_e2, w_s1, w_s3, w_s2, ln2_w, ln2_b):
    off_ml = RWKV_PROJ
    off_if = off_ml + 4 * D_MLSTM
    off_mq = off_if + 2 * MLSTM_HEADS
    pad_if = LANES - 2 * MLSTM_HEADS
    row = lambda t: t.reshape(1, -1)
    hid = N_EXPERTS * EXPERT_HIDDEN
    w_lora = jnp.zeros((LORA_ALL, 3 * D_RWKV), F32)
    w_lora = w_lora.at[0:W_LORA, 0:D_RWKV].set(rwkv_w2)
    w_lora = w_lora.at[W_LORA:W_LORA + A_LORA, D_RWKV:2 * D_RWKV].set(rwkv_a2)
    w_lora = w_lora.at[W_LORA + A_LORA:, 2 * D_RWKV:].set(rwkv_g2)
    w_rw, w_qk, w_vo, w_mq, w_gate = _split_w_in(w_in)
    return dict(
        w_rw=w_rw, b_rw=row(b_in[:off_ml]),
        w_qk=w_qk, b_qk=row(b_in[off_ml:off_ml + 2 * D_MLSTM]),
        w_vo=w_vo, b_vo=row(jnp.pad(b_in[off_ml + 2 * D_MLSTM:off_mq], (0, pad_if))),
        w_mq=w_mq, b_mq=row(b_in[off_mq:off_mq + D_MEM]),
        w_gate=w_gate, b_gate=row(b_in[off_mq + D_MEM:]),
        w_mem_k=w_mem_kv[:, :D_MEM].astype(BF16), w_mem_v=w_mem_kv[:, D_MEM:].astype(BF16),
        mu=row(rwkv_mu), w0=row(rwkv_w0), a0=row(rwkv_a0), k_k=row(rwkv_k_k), k_a=row(rwkv_k_a),
        r_k=row(rwkv_r_k), rw_ln_w=row(rwkv_ln_w), rw_ln_b=row(rwkv_ln_b), w_lora=w_lora.astype(BF16),
        conv_w=mlstm_conv_w, conv_b=row(mlstm_conv_b),
        b_if=row(jnp.pad(jnp.concatenate([mlstm_b_i, mlstm_b_f]), (0, pad_if))),
        ml_ln_w=row(mlstm_ln_w), ml_ln_b=row(mlstm_ln_b),
        w_pa=w_pa.astype(BF16), w_pb=w_pb.astype(BF16), w_pc=w_pc.astype(BF16), w_out=w_out.astype(BF16),
        ln1_w=row(ln1_w), ln1_b=row(ln1_b),
        w_router=jnp.pad(w_router, ((0, 0), (0, LANES - N_EXPERTS))).astype(BF16),
        router_bias=row(jnp.pad(router_bias, (0, LANES - N_EXPERTS))),
        w_e1=w_e1.astype(BF16), w_e3=w_e3.astype(BF16),
        w_e2=w_e2.reshape(hid, D_MODEL).astype(BF16),
        w_s1=w_s1.astype(BF16), w_s3=w_s3.astype(BF16), w_s2=w_s2.astype(BF16),
        ln2_w=row(ln2_w), ln2_b=row(ln2_b),
    )


def _pair_state(wkv):
    bsz = wkv.shape[0]
    s = wkv.reshape(bsz, RWKV_PAIRS, 2, RWKV_HEAD_DIM, RWKV_HEAD_DIM)
    z = jnp.zeros_like(s[:, :, 0])
    top = jnp.concatenate([s[:, :, 0], z], axis=-1)
    bot = jnp.concatenate([z, s[:, :, 1]], axis=-1)
    return jnp.concatenate([top, bot], axis=-2)


def _unpair_state(h_bd):
    bsz = h_bd.shape[0]
    hd = RWKV_HEAD_DIM
    heads = jnp.stack([h_bd[:, :, :hd, :hd], h_bd[:, :, hd:, hd:]], axis=2)
    return heads.reshape(bsz, RWKV_HEADS, hd, hd)


def _trunk(x3, mem_k, mem_v, shift_prev, wkv0, conv_prev, c0, n0, m0, w, chunk, tm_proj, tail_rows, tail_batches,
           tm_moe):
    bsz, t_len, _ = x3.shape
    n = bsz * t_len
    x = x3.reshape(n, D_MODEL)
    p_rw = _matmul_bias(x, w["w_rw"], w["b_rw"], tm_proj, RWKV_PROJ)
    qk, qk_tail, p_vo = _mlstm_proj(x, w["w_qk"], w["b_qk"], w["conv_w"], w["conv_b"], conv_prev, w["w_vo"],
                                    w["b_vo"], tm_proj)
    p_mq, p_gate = _matmul_bias2(x, w["w_mq"], w["b_mq"], w["w_gate"], w["b_gate"], tm_proj, BF16, F32)

    p_rw3 = p_rw.reshape(bsz, t_len, RWKV_PROJ)
    ya, h_bd, hb, c_t, n_t, m_t = _recurrent(
        p_rw3, shift_prev, _pair_state(wkv0), qk.reshape(bsz, t_len, 2 * D_MLSTM), p_vo.reshape(bsz, t_len, VO_COLS),
        c0, n0, m0.reshape(bsz, 1, MLSTM_HEADS), w, chunk)
    x1 = _tail(x, ya.reshape(n, D_RWKV), hb.reshape(n, D_MLSTM), p_mq, p_gate, mem_k, mem_v, w, tail_rows,
               tail_batches)
    y = _moe(x1, w, tm_moe)

    new_shift = p_rw3[:, t_len - 1:, :]
    new_conv = qk_tail[:, 8 - (CONV_W - 1):, :]
    states = (new_shift, _unpair_state(h_bd), new_conv, c_t, n_t, m_t.reshape(bsz, MLSTM_HEADS))
    return y.reshape(bsz, t_len, D_MODEL), states


def kernel(x_prompt, x_sample, state_rwkv_shift, state_rwkv_wkv, state_mlstm_conv, state_mlstm_c, state_mlstm_n, state_mlstm_m, cache_mem_k, cache_mem_v, mem_prompt, w_in, b_in, rwkv_mu, rwkv_w0, rwkv_w2, rwkv_a0, rwkv_a2, rwkv_g2, rwkv_k_k, rwkv_k_a, rwkv_r_k, rwkv_ln_w, rwkv_ln_b, mlstm_conv_w, mlstm_conv_b, mlstm_b_i, mlstm_b_f, mlstm_ln_w, mlstm_ln_b, w_mem_kv, w_pa, w_pb, w_pc, w_out, ln1_w, ln1_b, w_router, router_bias, w_e1, w_e3, w_e2, w_s1, w_s3, w_s2, ln2_w, ln2_b):
    weights = (w_in, b_in, rwkv_mu, rwkv_w0, rwkv_w2, rwkv_a0, rwkv_a2, rwkv_g2, rwkv_k_k, rwkv_k_a, rwkv_r_k,
               rwkv_ln_w, rwkv_ln_b, mlstm_conv_w, mlstm_conv_b, mlstm_b_i, mlstm_b_f, mlstm_ln_w, mlstm_ln_b,
               w_mem_kv, w_pa, w_pb, w_pc, w_out, ln1_w, ln1_b, w_router, router_bias, w_e1, w_e3, w_e2,
               w_s1, w_s3, w_s2, ln2_w, ln2_b)
    w = _prep_weights(*(t[0] for t in weights))
    bp, t_p, _ = x_prompt.shape
    bs, t_s, _ = x_sample.shape
    hd = MLSTM_HEAD_DIM

    mem_flat = mem_prompt.reshape(bp * N_MEM, D_MODEL)
    zero_bias = jnp.zeros((1, D_MEM), F32)
    mk = _matmul_bias(mem_flat, w["w_mem_k"], zero_bias, bp * N_MEM, D_MEM).reshape(bp, N_MEM, D_MEM)
    mv = _matmul_bias(mem_flat, w["w_mem_v"], zero_bias, bp * N_MEM, D_MEM).reshape(bp, N_MEM, D_MEM)

    y_p, st_p = _trunk(
        x_prompt, mk, mv,
        jnp.zeros((bp, 1, RWKV_PROJ), F32), jnp.zeros((bp, RWKV_HEADS, RWKV_HEAD_DIM, RWKV_HEAD_DIM), F32),
        jnp.zeros((bp, CONV_W - 1, 2 * D_MLSTM), F32), jnp.zeros((bp, MLSTM_HEADS, hd, hd), F32),
        jnp.zeros((bp, MLSTM_HEADS, hd), F32), jnp.zeros((bp, MLSTM_HEADS), F32),
        w, chunk=min(MLSTM_CHUNK, t_p), tm_proj=1024, tail_rows=512, tail_batches=1, tm_moe=512)
    y_s, st_s = _trunk(
        x_sample, cache_mem_k[0].reshape(bs, N_MEM, D_MEM), cache_mem_v[0].reshape(bs, N_MEM, D_MEM),
        state_rwkv_shift[0], state_rwkv_wkv[0], state_mlstm_conv[0], state_mlstm_c[0], state_mlstm_n[0],
        state_mlstm_m[0],
        w, chunk=min(MLSTM_CHUNK, t_s), tm_proj=bs * t_s, tail_rows=t_s, tail_batches=4, tm_moe=bs * t_s)

    lead = lambda t: t[None]
    mem_shape = (1, bp, N_MEM, MEM_HEADS, MEM_HEAD_DIM)
    return (y_p, y_s, *(lead(t) for t in st_p), mk.reshape(mem_shape), mv.reshape(mem_shape),
            *(lead(t) for t in st_s))
```

```python
import functools
import itertools

import jax
import jax.numpy as jnp
from jax import lax
from jax.experimental import pallas as pl
from jax.experimental.pallas import tpu as pltpu

F32 = jnp.float32
BF16 = jnp.bfloat16

D_MODEL = 1024
DEPTH = 1
D_RWKV = 1024
RWKV_HEAD_DIM = 64
RWKV_HEADS = 16
W_LORA = 64
A_LORA = 64
G_LORA = 128
LORA_ALL = W_LORA + A_LORA + G_LORA
RWKV_PROJ = 3 * D_RWKV + LORA_ALL
RWKV_GN_EPS = 64e-5
DECAY_SCALE_LOG2 = 0.6065306597126334 * 1.4426950408889634
D_MLSTM = 1024
MLSTM_HEADS = 4
MLSTM_HEAD_DIM = 256
CONV_W = 4
MLSTM_CHUNK = 64
N_MEM = 256
MEM_HEADS = 4
MEM_HEAD_DIM = 256
D_MEM = 1024
N_EXPERTS = 32
TOP_K = 4
EXPERT_HIDDEN = 128
ROUTED_SCALE = 2.5
DN_ALPHA = (2 * DEPTH) ** 0.25
LN_EPS = 1e-5

LANES = 128
RWKV_PAIRS = D_RWKV // LANES
VO_COLS = 2 * D_MLSTM + LANES
VMEM_LIMIT = 56 * 1024 * 1024
BATCH_GROUP = 2


def _dot(a, b):
    return jnp.dot(a, b, preferred_element_type=F32)


def _dot_nt(a, b):
    return lax.dot_general(a, b, (((1,), (1,)), ((), ())), preferred_element_type=F32)


def _dot_tn(a, b):
    return lax.dot_general(a, b, (((0,), (0,)), ((), ())), preferred_element_type=F32)


def _split(a):
    hi = a.astype(BF16)
    return hi, (a - hi.astype(F32)).astype(BF16)


def _split3(a):
    hi = a.astype(BF16)
    rest = a - hi.astype(F32)
    mid = rest.astype(BF16)
    return hi, mid, (rest - mid.astype(F32)).astype(BF16)


def _bdot(a, b):
    return _dot(a.astype(BF16), b.astype(BF16))


def _sigmoid(x):
    return 0.5 * jnp.tanh(0.5 * x) + 0.5


def _softplus(x):
    return jnp.maximum(x, 0.0) + jnp.log(1.0 + jnp.exp(-jnp.abs(x)))


def _silu(x):
    return x * _sigmoid(x)


def _layer_norm(x, w, b):
    mu = jnp.mean(x, axis=-1, keepdims=True)
    d = x - mu
    var = jnp.mean(d * d, axis=-1, keepdims=True)
    return d * lax.rsqrt(var + LN_EPS) * w + b


def _params(semantics):
    return pltpu.CompilerParams(dimension_semantics=semantics, vmem_limit_bytes=VMEM_LIMIT)


def _const_spec(shape):
    zeros = (0,) * len(shape)
    return pl.BlockSpec(shape, lambda *_: zeros, pipeline_mode=pl.Buffered(1))


def _mm_kernel(x_ref, w_ref, b_ref, o_ref):
    o_ref[...] = (_dot(x_ref[...].astype(BF16), w_ref[...]) + b_ref[...]).astype(o_ref.dtype)


def _matmul_bias(x, w, b, tm, tn, out_dtype=F32):
    n, k = x.shape
    nc = w.shape[1]
    resident = dict(pipeline_mode=pl.Buffered(1)) if nc == tn else {}
    return pl.pallas_call(
        _mm_kernel,
        grid=(nc // tn, n // tm),
        in_specs=[
            pl.BlockSpec((tm, k), lambda j, i: (i, 0)),
            pl.BlockSpec((k, tn), lambda j, i: (0, j), **resident),
            pl.BlockSpec((1, tn), lambda j, i: (0, j), **resident),
        ],
        out_specs=pl.BlockSpec((tm, tn), lambda j, i: (i, j)),
        out_shape=jax.ShapeDtypeStruct((n, nc), out_dtype),
        compiler_params=_params(("arbitrary", "arbitrary")),
        name="matmul_bias",
    )(x, w, b)


def _mm2_kernel(x_ref, w1_ref, b1_ref, w2_ref, b2_ref, o1_ref, o2_ref):
    xb = x_ref[...].astype(BF16)
    o1_ref[...] = (_dot(xb, w1_ref[...]) + b1_ref[...]).astype(o1_ref.dtype)
    o2_ref[...] = (_dot(xb, w2_ref[...]) + b2_ref[...]).astype(o2_ref.dtype)


def _matmul_bias2(x, w1, b1, w2, b2, tm, dtype1, dtype2):
    n, k = x.shape
    n1, n2 = w1.shape[1], w2.shape[1]
    row = lambda i: (i, 0)
    return pl.pallas_call(
        _mm2_kernel,
        grid=(n // tm,),
        in_specs=[pl.BlockSpec((tm, k), row), _const_spec((k, n1)), _const_spec((1, n1)), _const_spec((k, n2)),
                  _const_spec((1, n2))],
        out_specs=[pl.BlockSpec((tm, n1), row), pl.BlockSpec((tm, n2), row)],
        out_shape=[jax.ShapeDtypeStruct((n, n1), dtype1), jax.ShapeDtypeStruct((n, n2), dtype2)],
        compiler_params=_params(("arbitrary",)),
        name="matmul_bias_pair",
    )(x, w1, b1, w2, b2)


def _mlstm_proj_kernel(x_ref, w_ref, b_ref, cw_ref, cb_ref, prev_ref, w2_ref, b2_ref, o_ref, tail_ref, o2_ref, carry,
                       *, rows, tiles_per_batch):
    row_tile = pl.program_id(0)
    n_seg = prev_ref.shape[0]
    tm, tn = o_ref.shape
    n2 = o2_ref.shape[1]
    pad = jnp.zeros((8 - (CONV_W - 1), LANES), F32)
    row8 = lax.broadcasted_iota(jnp.int32, (8, LANES), 0)

    if n_seg == 1:
        @pl.when(row_tile % tiles_per_batch == 0)
        def _():
            carry[0:8 - (CONV_W - 1), :] = jnp.zeros((8 - (CONV_W - 1), tn), F32)
            carry[8 - (CONV_W - 1):8, :] = prev_ref[0]

    xb = x_ref[...].astype(BF16)
    n_parts = 4
    q_step = tn // n_parts
    for part in range(n_parts):
        cs = slice(part * q_step, (part + 1) * q_step)
        o_ref[:, cs] = _dot(xb, w_ref[:, cs]) + b_ref[:, cs]
    for seg in range(n_seg):
        tail_ref[seg] = o_ref[(seg + 1) * rows - 8:(seg + 1) * rows, :]
    strip = min(rows, 64)

    def conv_strips(r_lo, r_hi):
        for r0 in reversed(range(r_lo, r_hi, strip)):
            for c0 in range(0, tn, LANES):
                cols = slice(c0, c0 + LANES)
                half_scale = 0.5 * MLSTM_HEAD_DIM ** -0.5 if c0 >= D_MLSTM else 0.5
                if n_seg > 1 and r0 % rows == 0:
                    head = jnp.concatenate([pad, prev_ref[r0 // rows, :, cols]], axis=0)
                elif r0 == 0:
                    head = carry[:, cols]
                else:
                    head = o_ref[r0 - 8:r0, cols]
                x = o_ref[r0:r0 + strip, cols]
                blocks = [head] + [x[r:r + 8, :] for r in range(0, strip, 8)]
                conv = cb_ref[:, cols] + cw_ref[CONV_W - 1:CONV_W, cols] * x
                for j in range(CONV_W - 1):
                    lag = CONV_W - 1 - j
                    turned = [pltpu.roll(blk, lag, 0) for blk in blocks]
                    shifted = jnp.concatenate([jnp.where(row8 < lag, turned[i], turned[i + 1])
                                               for i in range(len(blocks) - 1)], axis=0)
                    conv = conv + cw_ref[j:j + 1, cols] * shifted
                o_ref[r0:r0 + strip, cols] = conv * (half_scale * jnp.tanh(0.5 * conv) + half_scale)

    col_step = -(-n2 // (n_parts * LANES)) * LANES
    row_step = tm // n_parts
    for part in reversed(range(n_parts)):
        cs = slice(part * col_step, min((part + 1) * col_step, n2))
        o2_ref[:, cs] = _dot(xb, w2_ref[:, cs]) + b2_ref[:, cs]
        conv_strips(part * row_step, (part + 1) * row_step)

    if n_seg == 1:
        carry[...] = tail_ref[0]


def _mlstm_proj(x, w, b, conv_w, conv_b, conv_prev, w2, b2, tm):
    n, k = x.shape
    batch = conv_prev.shape[0]
    t_len = n // batch
    rows = min(tm, t_len)
    n_seg = tm // rows
    tiles_per_batch = t_len // rows
    nc, n2 = w.shape[1], w2.shape[1]
    seg = lambda i: (i // tiles_per_batch if n_seg == 1 else i, 0, 0)
    return pl.pallas_call(
        functools.partial(_mlstm_proj_kernel, rows=rows, tiles_per_batch=tiles_per_batch),
        grid=(n // tm,),
        in_specs=[
            pl.BlockSpec((tm, k), lambda i: (i, 0)),
            _const_spec((k, nc)), _const_spec((1, nc)), _const_spec((CONV_W, nc)), _const_spec((1, nc)),
            pl.BlockSpec((n_seg, CONV_W - 1, nc), seg),
            _const_spec((k, n2)), _const_spec((1, n2)),
        ],
        out_specs=[
            pl.BlockSpec((tm, nc), lambda i: (i, 0)),
            pl.BlockSpec((n_seg, 8, nc), seg),
            pl.BlockSpec((tm, n2), lambda i: (i, 0)),
        ],
        out_shape=[jax.ShapeDtypeStruct((n, nc), F32), jax.ShapeDtypeStruct((batch, 8, nc), F32),
                   jax.ShapeDtypeStruct((n, n2), F32)],
        scratch_shapes=[pltpu.VMEM((8, nc), F32)],
        compiler_params=_params(("arbitrary",)),
        name="mlstm_proj_conv",
    )(x, w, b, conv_w, conv_b, conv_prev, w2, b2)


def _rwkv_body(p_ref, prev_ref, h0_ref, mu_ref, w0_ref, a0_ref, kk_ref, ka_ref, rk_ref, lnw_ref, lnb_ref,
               wl_ref, y_ref, hout_ref, h_scr, xbuf, ar_scr, bk_scr, bkh_scr, v_scr, bonus_scr, g_scr, gamma_scr,
               *, chunk, n_chunks, n_items, companion):
    step = pl.program_id(0)
    L = chunk
    half = 2 * L
    prep_chunk = jnp.minimum(step, n_items - 1) % n_chunks
    solve_chunk = jnp.maximum(step - 1, 0) % n_chunks

    @pl.when(step == 0)
    def _():
        for ref in (ar_scr, bk_scr, bkh_scr, v_scr, bonus_scr, g_scr):
            ref[...] = jnp.zeros(ref.shape, ref.dtype)
        gamma_scr[...] = jnp.ones(gamma_scr.shape, F32)

    @pl.when(solve_chunk == 0)
    def _():
        h_scr[...] = h0_ref[...]

    @pl.when(prep_chunk == 0)
    def _():
        for bi in range(BATCH_GROUP):
            xbuf[bi, 0:1, :] = prev_ref[bi]

    ti = lax.broadcasted_iota(jnp.int32, (L, L), 0)
    tj = lax.broadcasted_iota(jnp.int32, (L, L), 1)
    ltri = (tj <= ti).astype(BF16)
    row_m = lax.broadcasted_iota(jnp.int32, (L, half), 0)
    col_m = lax.broadcasted_iota(jnp.int32, (L, half), 1)
    col_in = jnp.where(col_m >= L, col_m - L, col_m)
    strict = col_in < row_m
    incl = col_in <= row_m
    eye = (col_in == row_m).astype(F32)
    gi = lax.broadcasted_iota(jnp.int32, (LANES, LANES), 0)
    gj = lax.broadcasted_iota(jnp.int32, (LANES, LANES), 1)
    same_head = (gi >= RWKV_HEAD_DIM) == (gj >= RWKV_HEAD_DIM)
    head_ones = same_head.astype(BF16)
    first_nat = lax.broadcasted_iota(jnp.int32, (L, LANES), 1) < RWKV_HEAD_DIM
    first_sbs = col_m < L
    lane_l = lax.broadcasted_iota(jnp.int32, (L, LORA_ALL), 1)

    def diag(x, first):
        zero = jnp.zeros_like(x)
        return jnp.concatenate([jnp.where(first, x, zero), jnp.where(first, zero, x)], axis=0)

    def seg_sum(x):
        return _dot(x.astype(BF16), head_ones)

    units = [(bi, p) for bi in range(BATCH_GROUP) for p in range(RWKV_PAIRS)]
    idx = range(len(units))
    lane_of = lambda p: slice(p * LANES, (p + 1) * LANES)

    ar = [ar_scr[bi, :, lane_of(p)] for bi, p in units]
    bk = [bk_scr[bi, :, lane_of(p)] for bi, p in units]
    bkh = [bkh_scr[bi, :, lane_of(p)] for bi, p in units]
    v_b = [v_scr[bi, :, lane_of(p)] for bi, p in units]
    bonus = [bonus_scr[bi, :, lane_of(p)] for bi, p in units]
    gate = [g_scr[bi, :, lane_of(p)] for bi, p in units]
    gamma = [gamma_scr[bi, :, lane_of(p)] for bi, p in units]

    first_row = lax.broadcasted_iota(jnp.int32, (L, RWKV_PROJ), 0) == 0

    def prepare_row(bi):
        cur = p_ref[bi]
        prev = jnp.where(first_row, xbuf[bi, 0:1, :], pltpu.roll(cur, 1, 0))
        xbuf[bi, 0:1, :] = cur[L - 1:L, :]
        xr = cur + (prev - cur) * mu_ref[...]
        r_all = xr[:, 0:D_RWKV]
        k_all = xr[:, D_RWKV:2 * D_RWKV]
        v_all = xr[:, 2 * D_RWKV:3 * D_RWKV]
        slab = xr[:, 3 * D_RWKV:]
        act = jnp.where(lane_l < W_LORA, jnp.tanh(slab),
                        jnp.where(lane_l < W_LORA + A_LORA, slab, _sigmoid(slab)))
        yield
        lora = _bdot(act, wl_ref[...])
        yield
        lw = -DECAY_SCALE_LOG2 * _sigmoid(w0_ref[...] + lora[:, 0:D_RWKV])
        a_sig = _sigmoid(a0_ref[...] + lora[:, D_RWKV:2 * D_RWKV])
        lw_hi, lw_lo = _split(lw)
        yield
        kk0 = k_all * kk_ref[...]
        k2 = k_all * (1.0 + (a_sig - 1.0) * ka_ref[...])
        sums = jnp.concatenate([kk0 * kk0, r_all * k2 * rk_ref[...]], axis=0).astype(BF16)
        yield
        cum = _dot(ltri, lw_hi) + _dot(ltri, lw_lo)
        sums = jnp.concatenate([_dot(sums[:, lane_of(p)], head_ones) for p in range(RWKV_PAIRS)], axis=1)
        yield
        kk = kk0 * lax.rsqrt(jnp.maximum(sums[0:L], 1e-24))
        b_vec = kk * a_sig
        c_last = cum[L - 1:L, :]
        e_neg = jnp.exp2(-cum)
        e_end = jnp.exp2(c_last - cum)
        yield
        ar_scr[bi] = jnp.concatenate([-kk * jnp.exp2(cum - lw), r_all * jnp.exp2(cum)], axis=0).astype(BF16)
        bk_scr[bi] = jnp.concatenate([b_vec * e_neg, k2 * e_neg], axis=0).astype(BF16)
        yield
        bkh_scr[bi] = jnp.concatenate([b_vec * e_end, k2 * e_end], axis=0).astype(BF16)
        v_scr[bi] = v_all.astype(BF16)
        bonus_scr[bi] = sums[L:] * v_all
        g_scr[bi] = lora[:, 2 * D_RWKV:]
        gamma_scr[bi] = jnp.exp2(c_last)

    segments = itertools.chain(*(prepare_row(bi) for bi in range(BATCH_GROUP)))

    def prepare_some(count):
        for _ in range(count):
            next(segments, None)
            next(companion, None)

    n_sq = L.bit_length() - 1
    per_stage = 1
    bk_d = [jnp.concatenate([diag(bk[u][0:L], first_nat), diag(bk[u][L:], first_nat)], axis=0) for u in idx]
    v_d = [diag(v_b[u], first_nat) for u in idx]
    h_bd = [h_scr[bi, p] for bi, p in units]
    sc = [_dot_nt(ar[u], bk_d[u]) for u in idx]
    prepare_some(per_stage)
    arh = [_dot_nt(ar[u], h_bd[u].astype(BF16)) for u in idx]
    m_ab = [jnp.where(strict, sc[u][0:L, 0:half], 0.0) for u in idx]
    m_ak = [jnp.where(strict, sc[u][0:L, half:], 0.0).astype(BF16) for u in idx]
    n_rbk = [jnp.concatenate([jnp.where(incl, sc[u][L:, 0:half], 0.0),
                              jnp.where(incl, sc[u][L:, half:], 0.0)], axis=1).astype(BF16) for u in idx]
    prepare_some(per_stage)
    w0 = [arh[u][0:L] + _dot(m_ak[u], v_d[u]) for u in idx]
    inv = [eye + m_ab[u] for u in idx]
    m_b = [m_ab[u].astype(BF16) for u in idx]
    pw = [_dot(m_b[u], diag(m_b[u], first_sbs)) for u in idx]
    prepare_some(per_stage)
    for i in range(n_sq - 1):
        pw_d = [diag(pw[u].astype(BF16), first_sbs) for u in idx]
        if i + 1 < n_sq - 1:
            z = [_dot(jnp.concatenate([inv[u], pw[u]], axis=0).astype(BF16), pw_d[u]) for u in idx]
            inv = [inv[u] + z[u][0:L] for u in idx]
            pw = [z[u][L:] for u in idx]
        else:
            inv = [inv[u] + _dot(inv[u].astype(BF16), pw_d[u]) for u in idx]
        prepare_some(per_stage)
    c_b = [_dot(inv[u].astype(BF16), diag(w0[u].astype(BF16), first_nat)).astype(BF16) for u in idx]
    prepare_some(per_stage)
    cv = [jnp.concatenate([c_b[u], v_b[u]], axis=0) for u in idx]
    y = [arh[u][L:] + _dot(n_rbk[u], jnp.concatenate([diag(c_b[u], first_nat), v_d[u]], axis=0)) for u in idx]
    for u, (bi, p) in enumerate(units):
        h_scr[bi, p] = h_bd[u] * gamma[u] + jnp.where(same_head, _dot_tn(cv[u], bkh[u]), 0.0)
    prepare_some(per_stage)
    mean = [seg_sum(y[u]) * (1.0 / RWKV_HEAD_DIM) for u in idx]
    prepare_some(per_stage)
    dev = [y[u] - mean[u] for u in idx]
    var = [seg_sum(dev[u] * dev[u]) * (1.0 / RWKV_HEAD_DIM) for u in idx]
    prepare_some(per_stage)
    for u, (bi, p) in enumerate(units):
        s = lane_of(p)
        yn = dev[u] * lax.rsqrt(var[u] + RWKV_GN_EPS) * lnw_ref[:, s] + lnb_ref[:, s]
        y_ref[bi, :, s] = ((yn + bonus[u]) * gate[u]).astype(y_ref.dtype)
    prepare_some(8 * BATCH_GROUP)
    for _ in companion:
        pass

    @pl.when((solve_chunk == n_chunks - 1) & (step > 0))
    def _():
        hout_ref[...] = h_scr[...]


def _mlstm_stages(qk_ref, p_ref, c0_ref, n0_ref, m0_ref, bif_ref, lnw_ref, lnb_ref,
                  h_ref, cout_ref, nout_ref, mout_ref, c_scr, n_scr, m_scr, *, chunk, first, last):
    L = chunk
    d = D_MLSTM
    nh = MLSTM_HEADS

    @pl.when(first)
    def _():
        c_scr[...] = c0_ref[...]
        n_scr[...] = n0_ref[...]
        m_scr[...] = m0_ref[...]

    yield

    ti =lax.broadcasted_iota(jnp.int32, (L, L), 0)
    tj = lax.broadcasted_iota(jnp.int32, (L, L), 1)
    causal = tj <= ti
    ltri = causal.astype(BF16)
    utri = (ti <= tj).astype(BF16)
    sel = (lax.broadcasted_iota(jnp.int32, (8, LANES), 0)
           == lax.broadcasted_iota(jnp.int32, (8, LANES), 1)).astype(BF16)
    lane_g = lax.broadcasted_iota(jnp.int32, (L, LANES), 1)
    is_f = (lane_g >= nh) & (lane_g < 2 * nh)

    rows = []
    for bi in range(BATCH_GROUP):
        gates = p_ref[bi, :, 2 * d:] + bif_ref[...]
        glog = jnp.where(is_f, -_softplus(-gates), gates)
        parts = _split3(glog)
        b_cols = sum(_dot(ltri, t) for t in parts)
        g_rows = sum(_dot_nt(sel, t) for t in parts)
        b_rows = sum(_dot(t, utri) for t in _split3(g_rows))
        rows.append(dict(q=qk_ref[bi, :, 0:d], k=qk_ref[bi, :, d:2 * d], v=p_ref[bi, :, 0:d],
                         o=_sigmoid(p_ref[bi, :, d:2 * d]),
                         glog=glog, b_cols=b_cols, g_rows=g_rows, b_rows=b_rows))
        yield

    units = [(bi, h) for bi in range(BATCH_GROUP) for h in range(nh)]
    idx = range(len(units))
    head = lambda h: slice(h * MLSTM_HEAD_DIM, (h + 1) * MLSTM_HEAD_DIM)
    q = [rows[bi]["q"][:, head(h)] for bi, h in units]
    k = [rows[bi]["k"][:, head(h)] for bi, h in units]
    v = [rows[bi]["v"][:, head(h)] for bi, h in units]
    q_b = [t.astype(BF16) for t in q]
    k_b = [t.astype(BF16) for t in k]
    c_st = [c_scr[bi, h] for bi, h in units]
    n_st = [n_scr[bi, h:h + 1, :] for bi, h in units]
    m_prev = [m_scr[bi, :, h:h + 1] for bi, h in units]
    qk_t = [_dot_nt(q_b[u], k_b[u]) for u in idx]
    yield
    qc = [_dot_nt(q_b[u], c_st[u].astype(BF16)) for u in idx]
    yield
    s, w_inter, m_t, wk, carry, m_next = [], [], [], [], [], []
    for u, (bi, h) in enumerate(units):
        t = rows[bi]
        ig_row = t["g_rows"][h:h + 1, :]
        b_row = t["b_rows"][nh + h:nh + h + 1, :]
        ig_col = t["glog"][:, h:h + 1]
        b_col = t["b_cols"][:, nh + h:nh + h + 1]
        log_w = jnp.where(causal, b_col - b_row + ig_row, -jnp.inf)
        inter = b_col + m_prev[u]
        m_now = jnp.maximum(inter, jnp.max(log_w, axis=-1, keepdims=True))
        s.append(qk_t[u] * jnp.exp(log_w - m_now))
        w_inter.append(jnp.exp(inter - m_now))
        m_t.append(m_now)
        b_last = b_col[L - 1:L, :]
        g_col = b_last - b_col + ig_col
        m_new = jnp.maximum(b_last + m_prev[u], jnp.max(g_col, axis=0, keepdims=True))
        carry.append(jnp.exp(b_last + m_prev[u] - m_new))
        wk.append(jnp.exp(g_col - m_new))
        m_next.append(m_new)
        if u % 2 == 1:
            yield
    sv = [_dot(s[u].astype(BF16), v[u].astype(BF16)) for u in idx]
    yield
    upd = [_dot_tn((wk[u] * v[u]).astype(BF16), k_b[u]) for u in idx]
    yield
    qn =[jnp.sum(q[u] * n_st[u], axis=-1, keepdims=True) for u in idx]
    s_sum = [jnp.sum(s[u], axis=-1, keepdims=True) for u in idx]
    den = [jnp.maximum(jnp.abs(w_inter[u] * qn[u] + s_sum[u]), jnp.exp(-m_t[u])) for u in idx]
    yield
    hb = [rows[bi]["o"][:, head(h)] * ((w_inter[u] * qc[u] + sv[u]) / den[u]) for u, (bi, h) in enumerate(units)]
    yield
    mu = [jnp.mean(hb[u], axis=-1, keepdims=True) for u in idx]
    dev = [hb[u] - mu[u] for u in idx]
    var = [jnp.mean(dev[u] * dev[u], axis=-1, keepdims=True) for u in idx]
    yield
    n_new = [carry[u] * n_st[u] + jnp.sum(wk[u] * k[u], axis=0, keepdims=True) for u in idx]
    for u, (bi, h) in enumerate(units):
        sl = head(h)
        h_ref[bi, :, sl] = (dev[u] * lax.rsqrt(var[u] + LN_EPS) * lnw_ref[:, sl] + lnb_ref[:, sl]).astype(h_ref.dtype)
        c_scr[bi, h] = carry[u] * c_st[u] + upd[u]
        n_scr[bi, h:h + 1, :] = n_new[u]
        m_scr[bi, :, h:h + 1] = m_next[u]
        if u % 4 == 3:
            yield

    @pl.when(last)
    def _():
        cout_ref[...] = c_scr[...]
        nout_ref[...] = n_scr[...]
        mout_ref[...] = m_scr[...]


RW_REFS = (12, 2, 9)
ML_REFS = (8, 4, 3)


def _recurrent_kernel(*refs, chunk, n_chunks, n_items):
    groups, pos = [], 0
    for n_rw, n_ml in zip(RW_REFS, ML_REFS):
        groups.append((refs[pos:pos + n_rw], refs[pos + n_rw:pos + n_rw + n_ml]))
        pos += n_rw + n_ml
    rw_refs = [r for rw_part, _ in groups for r in rw_part]
    ml_refs = [r for _, ml_part in groups for r in ml_part]
    step = pl.program_id(0)
    solve_chunk = jnp.maximum(step - 1, 0) % n_chunks
    ml = _mlstm_stages(*ml_refs, chunk=chunk, first=solve_chunk == 0,
                       last=(solve_chunk == n_chunks - 1) & (step > 0))
    next(ml)
    _rwkv_body(*rw_refs, chunk=chunk, n_chunks=n_chunks, n_items=n_items, companion=ml)


def _recurrent(p_rw, shift_prev, h0_bd, qk, p_vo, c0, n0, m0, w, chunk):
    batch, t_len, _ = p_rw.shape
    n_chunks = t_len // chunk
    bg = BATCH_GROUP
    hd = MLSTM_HEAD_DIM
    n_items = (batch // bg) * n_chunks
    prep = lambda i: jnp.minimum(i, n_items - 1)
    solve = lambda i: jnp.maximum(i - 1, 0)
    prep_seq = lambda i: (prep(i) // n_chunks, prep(i) % n_chunks, 0)
    solve_seq = lambda i: (solve(i) // n_chunks, solve(i) % n_chunks, 0)
    vec = lambda n: pl.BlockSpec((1, n), lambda i: (0, 0))
    group = lambda *dims: pl.BlockSpec((bg,) + dims, lambda i: (solve(i) // n_chunks,) + (0,) * len(dims))
    rw_state = group(RWKV_PAIRS, LANES, LANES)
    ml_states = [group(MLSTM_HEADS, hd, hd), group(MLSTM_HEADS, hd), group(1, MLSTM_HEADS)]
    wide = lambda rows, dtype: pltpu.VMEM((bg, rows, D_RWKV), dtype)
    kern = functools.partial(_recurrent_kernel, chunk=chunk, n_chunks=n_chunks, n_items=n_items)
    return pl.pallas_call(
        kern,
        grid=(n_items + 1,),
        in_specs=[
            pl.BlockSpec((bg, chunk, RWKV_PROJ), prep_seq),
            pl.BlockSpec((bg, 1, RWKV_PROJ), lambda i: (prep(i) // n_chunks, 0, 0)),
            rw_state,
            vec(RWKV_PROJ), vec(D_RWKV), vec(D_RWKV), vec(D_RWKV), vec(D_RWKV), vec(D_RWKV), vec(D_RWKV),
            vec(D_RWKV),
            pl.BlockSpec((LORA_ALL, 3 * D_RWKV), lambda i: (0, 0)),
            pl.BlockSpec((bg, chunk, 2 * D_MLSTM), solve_seq),
            pl.BlockSpec((bg, chunk, VO_COLS), solve_seq),
            *ml_states,
            vec(LANES), vec(D_MLSTM), vec(D_MLSTM),
        ],
        out_specs=[
            pl.BlockSpec((bg, chunk, D_RWKV), solve_seq), rw_state,
            pl.BlockSpec((bg, chunk, D_MLSTM), solve_seq), *ml_states,
        ],
        out_shape=[
            jax.ShapeDtypeStruct((batch, t_len, D_RWKV), BF16),
            jax.ShapeDtypeStruct((batch, RWKV_PAIRS, LANES, LANES), F32),
            jax.ShapeDtypeStruct((batch, t_len, D_MLSTM), BF16),
            jax.ShapeDtypeStruct((batch, MLSTM_HEADS, hd, hd), F32),
            jax.ShapeDtypeStruct((batch, MLSTM_HEADS, hd), F32),
            jax.ShapeDtypeStruct((batch, 1, MLSTM_HEADS), F32),
        ],
        scratch_shapes=[
            pltpu.VMEM((bg, RWKV_PAIRS, LANES, LANES), F32),
            pltpu.VMEM((bg, 8, RWKV_PROJ), F32),
            wide(2 * chunk, BF16), wide(2 * chunk, BF16), wide(2 * chunk, BF16), wide(chunk, BF16),
            wide(chunk, F32), wide(chunk, F32), wide(1, F32),
            pltpu.VMEM((bg, MLSTM_HEADS, hd, hd), F32),
            pltpu.VMEM((bg, MLSTM_HEADS, hd), F32),
            pltpu.VMEM((bg, 1, MLSTM_HEADS), F32),
        ],
        compiler_params=_params(("arbitrary",)),
        name="rwkv7_mlstm_recurrent",
    )(p_rw, shift_prev, h0_bd, w["mu"], w["w0"], w["a0"], w["k_k"], w["k_a"], w["r_k"], w["rw_ln_w"],
      w["rw_ln_b"], w["w_lora"], qk, p_vo, c0, n0, m0, w["b_if"], w["ml_ln_w"], w["ml_ln_b"])


def _tail_kernel(x_ref, ya_ref, hb_ref, mq_ref, gate_ref, mk_ref, mv_ref, wpa_ref, wpb_ref, wpc_ref, wout_ref,
                 lnw_ref, lnb_ref, o_ref, *, rows):
    n_seg = mk_ref.shape[0]
    d = D_MODEL
    head = lambda h: slice(h * MEM_HEAD_DIM, (h + 1) * MEM_HEAD_DIM)
    units = [(b, h) for b in range(n_seg) for h in range(MEM_HEADS)]
    idx = range(len(units))
    mk = [mk_ref[b].astype(BF16) for b in range(n_seg)]
    mv = [mv_ref[b].astype(BF16) for b in range(n_seg)]
    s = [_dot_nt(mq_ref[b * rows:(b + 1) * rows, head(h)], mk[b][:, head(h)]) * (MEM_HEAD_DIM ** -0.5)
         for b, h in units]
    y_a = _sigmoid(gate_ref[:, 0:d]) * _dot(ya_ref[...], wpa_ref[...])
    top = [jnp.max(s[u], axis=-1, keepdims=True) for u in idx]
    e = [jnp.exp(s[u] - top[u]) for u in idx]
    y_b = _sigmoid(gate_ref[:, d:2 * d]) * _dot(hb_ref[...], wpb_ref[...])
    den = [jnp.sum(e[u], axis=-1, keepdims=True) for u in idx]
    prob = [(e[u] / den[u]).astype(BF16) for u in idx]
    out = [_dot(prob[u], mv[b][:, head(h)]).astype(BF16) for u, (b, h) in enumerate(units)]
    attn = [jnp.concatenate(out[b * MEM_HEADS:(b + 1) * MEM_HEADS], axis=-1) for b in range(n_seg)]
    attn = attn[0] if n_seg == 1 else jnp.concatenate(attn, axis=0)
    y_c = _sigmoid(gate_ref[:, 2 * d:3 * d]) * _dot(attn, wpc_ref[...])
    mixed = y_a + y_b + y_c
    u = _bdot(mixed, wout_ref[...])
    o_ref[...] = _layer_norm(DN_ALPHA * x_ref[...] + u, lnw_ref[...], lnb_ref[...])


def _tail(x, ya, hb, p_mq, p_gate, mem_k, mem_v, w, rows, batches):
    n = x.shape[0]
    tm = rows * batches
    tiles_per_batch = (n // mem_k.shape[0]) // rows if batches == 1 else 1
    row = lambda i: (i, 0)
    mem = pl.BlockSpec((batches, N_MEM, D_MEM), lambda i: (i // tiles_per_batch, 0, 0))
    tok = lambda cols: pl.BlockSpec((tm, cols), row)
    return pl.pallas_call(
        functools.partial(_tail_kernel, rows=rows),
        grid=(n // tm,),
        in_specs=[
            tok(D_MODEL), tok(D_RWKV), tok(D_MLSTM), tok(D_MEM), tok(3 * D_MODEL), mem, mem,
            _const_spec((D_RWKV, D_MODEL)), _const_spec((D_MLSTM, D_MODEL)), _const_spec((D_MEM, D_MODEL)),
            _const_spec((D_MODEL, D_MODEL)), _const_spec((1, D_MODEL)), _const_spec((1, D_MODEL)),
        ],
        out_specs=tok(D_MODEL),
        out_shape=jax.ShapeDtypeStruct((n, D_MODEL), F32),
        compiler_params=_params(("arbitrary",)),
        name="attn_merge_ln",
    )(x, ya, hb, p_mq, p_gate, mem_k, mem_v, w["w_pa"], w["w_pb"], w["w_pc"], w["w_out"], w["ln1_w"],
      w["ln1_b"])


def _expert_copy(src_hbm, dst, sem, e):
    return pltpu.make_async_copy(src_hbm.at[e], dst.at[:, pl.ds(e * EXPERT_HIDDEN, EXPERT_HIDDEN)], sem)


def _moe_kernel(x_ref, wr_ref, rb_ref, w1_hbm, w3_hbm, w2_ref, ws1_ref, ws3_ref, ws2_ref, lnw_ref, lnb_ref, o_ref,
                w1_ref, w3_ref, sems):
    @pl.when(pl.program_id(0) == 0)
    def _():
        copies = [_expert_copy(src, dst, sems.at[i], e)
                  for i, (src, dst) in enumerate(((w1_hbm, w1_ref), (w3_hbm, w3_ref))) for e in range(N_EXPERTS)]
        for cp in copies:
            cp.start()
        for cp in copies:
            cp.wait()

    x = x_ref[...]
    xb = x.astype(BF16)
    group = 8
    width = group * EXPERT_HIDDEN
    n_slabs = N_EXPERTS // group

    def up_project(s):
        cols = slice(s * width, (s + 1) * width)
        return _dot(xb, w1_ref[:, cols]), _dot(xb, w3_ref[:, cols])

    logits = _dot(xb, wr_ref[...])
    ahead = up_project(0)
    scores = _sigmoid(logits)
    lane = lax.broadcasted_iota(jnp.int32, scores.shape, 1)
    work = jnp.where(lane < N_EXPERTS, scores + rb_ref[...], -jnp.inf)
    chosen = jnp.zeros(scores.shape, dtype=jnp.bool_)
    for _ in range(TOP_K):
        best = jnp.max(work, axis=-1, keepdims=True)
        first = jnp.min(jnp.where(work == best, lane, LANES), axis=-1, keepdims=True)
        pick = lane == first
        chosen = chosen | pick
        work = jnp.where(pick, -jnp.inf, work)
    sel = jnp.where(chosen, scores, 0.0)
    gates = sel / jnp.sum(sel, axis=-1, keepdims=True) * ROUTED_SCALE

    acc = _dot((_silu(_dot(xb, ws1_ref[...])) * _dot(xb, ws3_ref[...])).astype(BF16), ws2_ref[...])
    for s in range(n_slabs):
        cols = slice(s * width, (s + 1) * width)
        h1, h3 = ahead
        if s + 1 < n_slabs:
            ahead = up_project(s + 1)
        hidden = _silu(h1) * h3
        gated = jnp.concatenate(
            [hidden[:, e * EXPERT_HIDDEN:(e + 1) * EXPERT_HIDDEN] * gates[:, s * group + e:s * group + e + 1]
             for e in range(group)], axis=-1)
        acc = acc + _dot(gated.astype(BF16), w2_ref[cols, :])
    o_ref[...] = _layer_norm(DN_ALPHA * x + acc, lnw_ref[...], lnb_ref[...])


def _moe(x, w, tm):
    n = x.shape[0]
    row = lambda i: (i, 0)
    hid = N_EXPERTS * EXPERT_HIDDEN
    return pl.pallas_call(
        _moe_kernel,
        grid=(n // tm,),
        in_specs=[
            pl.BlockSpec((tm, D_MODEL), row),
            _const_spec((D_MODEL, LANES)), _const_spec((1, LANES)),
            pl.BlockSpec(memory_space=pl.ANY), pl.BlockSpec(memory_space=pl.ANY), _const_spec((hid, D_MODEL)),
            _const_spec((D_MODEL, EXPERT_HIDDEN)), _const_spec((D_MODEL, EXPERT_HIDDEN)),
            _const_spec((EXPERT_HIDDEN, D_MODEL)),
            _const_spec((1, D_MODEL)), _const_spec((1, D_MODEL)),
        ],
        out_specs=pl.BlockSpec((tm, D_MODEL), row),
        out_shape=jax.ShapeDtypeStruct((n, D_MODEL), F32),
        scratch_shapes=[pltpu.VMEM((D_MODEL, hid), BF16), pltpu.VMEM((D_MODEL, hid), BF16),
                        pltpu.SemaphoreType.DMA((2,))],
        compiler_params=_params(("arbitrary",)),
        name="moe_ln",
    )(x, w["w_router"], w["router_bias"], w["w_e1"], w["w_e3"], w["w_e2"], w["w_s1"], w["w_s3"], w["w_s2"],
      w["ln2_w"], w["ln2_b"])


def _prep_weights(w_in, b_in, rwkv_mu, rwkv_w0, rwkv_w2, rwkv_a0, rwkv_a2, rwkv_g2, rwkv_k_k, rwkv_k_a,
                  rwkv_r_k, rwkv_ln_w, rwkv_ln_b, mlstm_conv_w, mlstm_conv_b, mlstm_b_i, mlstm_b_f,
                  mlstm_ln_w, mlstm_ln_b, w_mem_kv, w_pa, w_pb, w_pc, w_out, ln1_w, ln1_b,
                  w_router, router_bias, w_e1, w_e3, w_e2, w_s1, w_s3, w_s2, ln2_w, ln2_b):
    off_ml = RWKV_PROJ
    off_if = off_ml + 4 * D_MLSTM
    off_mq = off_if + 2 * MLSTM_HEADS
    pad_if = LANES - 2 * MLSTM_HEADS
    row = lambda t: t.reshape(1, -1)
    hid = N_EXPERTS * EXPERT_HIDDEN
    w_lora = jnp.zeros((LORA_ALL, 3 * D_RWKV), F32)
    w_lora = w_lora.at[0:W_LORA, 0:D_RWKV].set(rwkv_w2)
    w_lora = w_lora.at[W_LORA:W_LORA + A_LORA, D_RWKV:2 * D_RWKV].set(rwkv_a2)
    w_lora = w_lora.at[W_LORA + A_LORA:, 2 * D_RWKV:].set(rwkv_g2)
    return dict(
        w_rw=w_in[:, :off_ml].astype(BF16), b_rw=row(b_in[:off_ml]),
        w_qk=w_in[:, off_ml:off_ml + 2 * D_MLSTM].astype(BF16), b_qk=row(b_in[off_ml:off_ml + 2 * D_MLSTM]),
        w_vo=jnp.pad(w_in[:, off_ml + 2 * D_MLSTM:off_mq], ((0, 0), (0, pad_if))).astype(BF16),
        b_vo=row(jnp.pad(b_in[off_ml + 2 * D_MLSTM:off_mq], (0, pad_if))),
        w_mq=w_in[:, off_mq:off_mq + D_MEM].astype(BF16), b_mq=row(b_in[off_mq:off_mq + D_MEM]),
        w_gate=w_in[:, off_mq + D_MEM:].astype(BF16), b_gate=row(b_in[off_mq + D_MEM:]),
        w_mem_k=w_mem_kv[:, :D_MEM].astype(BF16), w_mem_v=w_mem_kv[:, D_MEM:].astype(BF16),
        mu=row(rwkv_mu), w0=row(rwkv_w0), a0=row(rwkv_a0), k_k=row(rwkv_k_k), k_a=row(rwkv_k_a),
        r_k=row(rwkv_r_k), rw_ln_w=row(rwkv_ln_w), rw_ln_b=row(rwkv_ln_b), w_lora=w_lora.astype(BF16),
        conv_w=mlstm_conv_w, conv_b=row(mlstm_conv_b),
        b_if=row(jnp.pad(jnp.concatenate([mlstm_b_i, mlstm_b_f]), (0, pad_if))),
        ml_ln_w=row(mlstm_ln_w), ml_ln_b=row(mlstm_ln_b),
        w_pa=w_pa.astype(BF16), w_pb=w_pb.astype(BF16), w_pc=w_pc.astype(BF16), w_out=w_out.astype(BF16),
        ln1_w=row(ln1_w), ln1_b=row(ln1_b),
        w_router=jnp.pad(w_router, ((0, 0), (0, LANES - N_EXPERTS))).astype(BF16),
        router_bias=row(jnp.pad(router_bias, (0, LANES - N_EXPERTS))),
        w_e1=w_e1.astype(BF16), w_e3=w_e3.astype(BF16),
        w_e2=w_e2.reshape(hid, D_MODEL).astype(BF16),
        w_s1=w_s1.astype(BF16), w_s3=w_s3.astype(BF16), w_s2=w_s2.astype(BF16),
        ln2_w=row(ln2_w), ln2_b=row(ln2_b),
    )


def _pair_state(wkv):
    bsz = wkv.shape[0]
    s = wkv.reshape(bsz, RWKV_PAIRS, 2, RWKV_HEAD_DIM, RWKV_HEAD_DIM)
    z = jnp.zeros_like(s[:, :, 0])
    top = jnp.concatenate([s[:, :, 0], z], axis=-1)
    bot = jnp.concatenate([z, s[:, :, 1]], axis=-1)
    return jnp.concatenate([top, bot], axis=-2)


def _unpair_state(h_bd):
    bsz = h_bd.shape[0]
    hd = RWKV_HEAD_DIM
    heads = jnp.stack([h_bd[:, :, :hd, :hd], h_bd[:, :, hd:, hd:]], axis=2)
    return heads.reshape(bsz, RWKV_HEADS, hd, hd)


def _trunk(x3, mem_k, mem_v, shift_prev, wkv0, conv_prev, c0, n0, m0, w, chunk, tm_proj, tail_rows, tail_batches,
           tm_moe):
    bsz, t_len, _ = x3.shape
    n = bsz * t_len
    x = x3.reshape(n, D_MODEL)
    p_rw = _matmul_bias(x, w["w_rw"], w["b_rw"], tm_proj, RWKV_PROJ)
    qk, qk_tail, p_vo = _mlstm_proj(x, w["w_qk"], w["b_qk"], w["conv_w"], w["conv_b"], conv_prev, w["w_vo"],
                                    w["b_vo"], tm_proj)
    p_mq, p_gate = _matmul_bias2(x, w["w_mq"], w["b_mq"], w["w_gate"], w["b_gate"], tm_proj, BF16, F32)

    p_rw3 = p_rw.reshape(bsz, t_len, RWKV_PROJ)
    ya, h_bd, hb, c_t, n_t, m_t = _recurrent(
        p_rw3, shift_prev, _pair_state(wkv0), qk.reshape(bsz, t_len, 2 * D_MLSTM), p_vo.reshape(bsz, t_len, VO_COLS),
        c0, n0, m0.reshape(bsz, 1, MLSTM_HEADS), w, chunk)
    x1 = _tail(x, ya.reshape(n, D_RWKV), hb.reshape(n, D_MLSTM), p_mq, p_gate, mem_k, mem_v, w, tail_rows,
               tail_batches)
    y = _moe(x1, w, tm_moe)

    new_shift = p_rw3[:, t_len - 1:, :]
    new_conv = qk_tail[:, 8 - (CONV_W - 1):, :]
    states = (new_shift, _unpair_state(h_bd), new_conv, c_t, n_t, m_t.reshape(bsz, MLSTM_HEADS))
    return y.reshape(bsz, t_len, D_MODEL), states


def kernel(x_prompt, x_sample, state_rwkv_shift, state_rwkv_wkv, state_mlstm_conv, state_mlstm_c, state_mlstm_n, state_mlstm_m, cache_mem_k, cache_mem_v, mem_prompt, w_in, b_in, rwkv_mu, rwkv_w0, rwkv_w2, rwkv_a0, rwkv_a2, rwkv_g2, rwkv_k_k, rwkv_k_a, rwkv_r_k, rwkv_ln_w, rwkv_ln_b, mlstm_conv_w, mlstm_conv_b, mlstm_b_i, mlstm_b_f, mlstm_ln_w, mlstm_ln_b, w_mem_kv, w_pa, w_pb, w_pc, w_out, ln1_w, ln1_b, w_router, router_bias, w_e1, w_e3, w_e2, w_s1, w_s3, w_s2, ln2_w, ln2_b):
    weights = (w_in, b_in, rwkv_mu, rwkv_w0, rwkv_w2, rwkv_a0, rwkv_a2, rwkv_g2, rwkv_k_k, rwkv_k_a, rwkv_r_k,
               rwkv_ln_w, rwkv_ln_b, mlstm_conv_w, mlstm_conv_b, mlstm_b_i, mlstm_b_f, mlstm_ln_w, mlstm_ln_b,
               w_mem_kv, w_pa, w_pb, w_pc, w_out, ln1_w, ln1_b, w_router, router_bias, w_e1, w_e3, w_e2,
               w_s1, w_s3, w_s2, ln2_w, ln2_b)
    w = _prep_weights(*(t[0] for t in weights))
    bp, t_p, _ = x_prompt.shape
    bs, t_s, _ = x_sample.shape
    hd = MLSTM_HEAD_DIM

    mem_flat = mem_prompt.reshape(bp * N_MEM, D_MODEL)
    zero_bias = jnp.zeros((1, D_MEM), F32)
    mk = _matmul_bias(mem_flat, w["w_mem_k"], zero_bias, bp * N_MEM, D_MEM).reshape(bp, N_MEM, D_MEM)
    mv = _matmul_bias(mem_flat, w["w_mem_v"], zero_bias, bp * N_MEM, D_MEM).reshape(bp, N_MEM, D_MEM)

    y_p, st_p = _trunk(
        x_prompt, mk, mv,
        jnp.zeros((bp, 1, RWKV_PROJ), F32), jnp.zeros((bp, RWKV_HEADS, RWKV_HEAD_DIM, RWKV_HEAD_DIM), F32),
        jnp.zeros((bp, CONV_W - 1, 2 * D_MLSTM), F32), jnp.zeros((bp, MLSTM_HEADS, hd, hd), F32),
        jnp.zeros((bp, MLSTM_HEADS, hd), F32), jnp.zeros((bp, MLSTM_HEADS), F32),
        w, chunk=min(MLSTM_CHUNK, t_p), tm_proj=1024, tail_rows=512, tail_batches=1, tm_moe=512)
    y_s, st_s = _trunk(
        x_sample, cache_mem_k[0].reshape(bs, N_MEM, D_MEM), cache_mem_v[0].reshape(bs, N_MEM, D_MEM),
        state_rwkv_shift[0], state_rwkv_wkv[0], state_mlstm_conv[0], state_mlstm_c[0], state_mlstm_n[0],
        state_mlstm_m[0],
        w, chunk=min(MLSTM_CHUNK, t_s), tm_proj=bs * t_s, tail_rows=t_s, tail_batches=4, tm_moe=bs * t_s)

    lead = lambda t: t[None]
    mem_shape = (1, bp, N_MEM, MEM_HEADS, MEM_HEAD_DIM)
    return (y_p, y_s, *(lead(t) for t in st_p), mk.reshape(mem_shape), mv.reshape(mem_shape),
            *(lead(t) for t in st_s))
```

```python
import functools
import itertools

import jax
import jax.numpy as jnp
from jax import lax
from jax.experimental import pallas as pl
from jax.experimental.pallas import tpu as pltpu

F32 = jnp.float32
BF16 = jnp.bfloat16

D_MODEL = 1024
DEPTH = 1
D_RWKV = 1024
RWKV_HEAD_DIM = 64
RWKV_HEADS = 16
W_LORA = 64
A_LORA = 64
G_LORA = 128
LORA_ALL = W_LORA + A_LORA + G_LORA
RWKV_PROJ = 3 * D_RWKV + LORA_ALL
RWKV_GN_EPS = 64e-5
DECAY_SCALE_LOG2 = 0.6065306597126334 * 1.4426950408889634
D_MLSTM = 1024
MLSTM_HEADS = 4
MLSTM_HEAD_DIM = 256
CONV_W = 4
MLSTM_CHUNK = 64
N_MEM = 256
MEM_HEADS = 4
MEM_HEAD_DIM = 256
D_MEM = 1024
N_EXPERTS = 32
TOP_K = 4
EXPERT_HIDDEN = 128
ROUTED_SCALE = 2.5
DN_ALPHA = (2 * DEPTH) ** 0.25
LN_EPS = 1e-5

LANES = 128
RWKV_PAIRS = D_RWKV // LANES
VO_COLS = 2 * D_MLSTM + LANES
VMEM_LIMIT = 56 * 1024 * 1024
BATCH_GROUP = 2


def _dot(a, b):
    return jnp.dot(a, b, preferred_element_type=F32)


def _dot_nt(a, b):
    return lax.dot_general(a, b, (((1,), (1,)), ((), ())), preferred_element_type=F32)


def _dot_tn(a, b):
    return lax.dot_general(a, b, (((0,), (0,)), ((), ())), preferred_element_type=F32)


def _split(a):
    hi = a.astype(BF16)
    return hi, (a - hi.astype(F32)).astype(BF16)


def _split3(a):
    hi = a.astype(BF16)
    rest = a - hi.astype(F32)
    mid = rest.astype(BF16)
    return hi, mid, (rest - mid.astype(F32)).astype(BF16)


def _bdot(a, b):
    return _dot(a.astype(BF16), b.astype(BF16))


def _sigmoid(x):
    return 0.5 * jnp.tanh(0.5 * x) + 0.5


def _softplus(x):
    return jnp.maximum(x, 0.0) + jnp.log(1.0 + jnp.exp(-jnp.abs(x)))


def _silu(x):
    return x * _sigmoid(x)


def _layer_norm(x, w, b):
    mu = jnp.mean(x, axis=-1, keepdims=True)
    d = x - mu
    var = jnp.mean(d * d, axis=-1, keepdims=True)
    return d * lax.rsqrt(var + LN_EPS) * w + b


def _params(semantics):
    return pltpu.CompilerParams(dimension_semantics=semantics, vmem_limit_bytes=VMEM_LIMIT)


def _const_spec(shape):
    zeros = (0,) * len(shape)
    return pl.BlockSpec(shape, lambda *_: zeros, pipeline_mode=pl.Buffered(1))


def _mm_kernel(x_ref, w_ref, b_ref, o_ref):
    o_ref[...] = (_dot(x_ref[...].astype(BF16), w_ref[...]) + b_ref[...]).astype(o_ref.dtype)


def _matmul_bias(x, w, b, tm, tn, out_dtype=F32):
    n, k = x.shape
    nc = w.shape[1]
    resident = dict(pipeline_mode=pl.Buffered(1)) if nc == tn else {}
    return pl.pallas_call(
        _mm_kernel,
        grid=(nc // tn, n // tm),
        in_specs=[
            pl.BlockSpec((tm, k), lambda j, i: (i, 0)),
            pl.BlockSpec((k, tn), lambda j, i: (0, j), **resident),
            pl.BlockSpec((1, tn), lambda j, i: (0, j), **resident),
        ],
        out_specs=pl.BlockSpec((tm, tn), lambda j, i: (i, j)),
        out_shape=jax.ShapeDtypeStruct((n, nc), out_dtype),
        compiler_params=_params(("arbitrary", "arbitrary")),
        name="matmul_bias",
    )(x, w, b)


def _mm2_kernel(x_ref, w1_ref, b1_ref, w2_ref, b2_ref, o1_ref, o2_ref):
    xb = x_ref[...].astype(BF16)
    o1_ref[...] = (_dot(xb, w1_ref[...]) + b1_ref[...]).astype(o1_ref.dtype)
    o2_ref[...] = (_dot(xb, w2_ref[...]) + b2_ref[...]).astype(o2_ref.dtype)


def _matmul_bias2(x, w1, b1, w2, b2, tm, dtype1, dtype2):
    n, k = x.shape
    n1, n2 = w1.shape[1], w2.shape[1]
    row = lambda i: (i, 0)
    return pl.pallas_call(
        _mm2_kernel,
        grid=(n // tm,),
        in_specs=[pl.BlockSpec((tm, k), row), _const_spec((k, n1)), _const_spec((1, n1)), _const_spec((k, n2)),
                  _const_spec((1, n2))],
        out_specs=[pl.BlockSpec((tm, n1), row), pl.BlockSpec((tm, n2), row)],
        out_shape=[jax.ShapeDtypeStruct((n, n1), dtype1), jax.ShapeDtypeStruct((n, n2), dtype2)],
        compiler_params=_params(("arbitrary",)),
        name="matmul_bias_pair",
    )(x, w1, b1, w2, b2)


def _mlstm_proj_kernel(x_ref, w_ref, b_ref, cw_ref, cb_ref, prev_ref, w2_ref, b2_ref, o_ref, tail_ref, o2_ref, carry,
                       *, rows, tiles_per_batch):
    row_tile = pl.program_id(0)
    n_seg = prev_ref.shape[0]
    tm, tn = o_ref.shape
    n2 = o2_ref.shape[1]
    pad = jnp.zeros((8 - (CONV_W - 1), LANES), F32)
    row8 = lax.broadcasted_iota(jnp.int32, (8, LANES), 0)

    if n_seg == 1:
        @pl.when(row_tile % tiles_per_batch == 0)
        def _():
            carry[0:8 - (CONV_W - 1), :] = jnp.zeros((8 - (CONV_W - 1), tn), F32)
            carry[8 - (CONV_W - 1):8, :] = prev_ref[0]

    xb = x_ref[...].astype(BF16)
    n_parts = 4
    q_step = tn // n_parts
    for part in range(n_parts):
        cs = slice(part * q_step, (part + 1) * q_step)
        o_ref[:, cs] = _dot(xb, w_ref[:, cs]) + b_ref[:, cs]
    for seg in range(n_seg):
        tail_ref[seg] = o_ref[(seg + 1) * rows - 8:(seg + 1) * rows, :]
    strip = min(rows, 64)

    def conv_strips(r_lo, r_hi):
        for r0 in reversed(range(r_lo, r_hi, strip)):
            for c0 in range(0, tn, LANES):
                cols = slice(c0, c0 + LANES)
                half_scale = 0.5 * MLSTM_HEAD_DIM ** -0.5 if c0 >= D_MLSTM else 0.5
                if n_seg > 1 and r0 % rows == 0:
                    head = jnp.concatenate([pad, prev_ref[r0 // rows, :, cols]], axis=0)
                elif r0 == 0:
                    head = carry[:, cols]
                else:
                    head = o_ref[r0 - 8:r0, cols]
                x = o_ref[r0:r0 + strip, cols]
                blocks = [head] + [x[r:r + 8, :] for r in range(0, strip, 8)]
                conv = cb_ref[:, cols] + cw_ref[CONV_W - 1:CONV_W, cols] * x
                for j in range(CONV_W - 1):
                    lag = CONV_W - 1 - j
                    turned = [pltpu.roll(blk, lag, 0) for blk in blocks]
                    shifted = jnp.concatenate([jnp.where(row8 < lag, turned[i], turned[i + 1])
                                               for i in range(len(blocks) - 1)], axis=0)
                    conv = conv + cw_ref[j:j + 1, cols] * shifted
                o_ref[r0:r0 + strip, cols] = conv * (half_scale * jnp.tanh(0.5 * conv) + half_scale)

    col_step = -(-n2 // (n_parts * LANES)) * LANES
    row_step = tm // n_parts
    for part in reversed(range(n_parts)):
        cs = slice(part * col_step, min((part + 1) * col_step, n2))
        o2_ref[:, cs] = _dot(xb, w2_ref[:, cs]) + b2_ref[:, cs]
        conv_strips(part * row_step, (part + 1) * row_step)

    if n_seg == 1:
        carry[...] = tail_ref[0]


def _mlstm_proj(x, w, b, conv_w, conv_b, conv_prev, w2, b2, tm):
    n, k = x.shape
    batch = conv_prev.shape[0]
    t_len = n // batch
    rows = min(tm, t_len)
    n_seg = tm // rows
    tiles_per_batch = t_len // rows
    nc, n2 = w.shape[1], w2.shape[1]
    seg = lambda i: (i // tiles_per_batch if n_seg == 1 else i, 0, 0)
    return pl.pallas_call(
        functools.partial(_mlstm_proj_kernel, rows=rows, tiles_per_batch=tiles_per_batch),
        grid=(n // tm,),
        in_specs=[
            pl.BlockSpec((tm, k), lambda i: (i, 0)),
            _const_spec((k, nc)), _const_spec((1, nc)), _const_spec((CONV_W, nc)), _const_spec((1, nc)),
            pl.BlockSpec((n_seg, CONV_W - 1, nc), seg),
            _const_spec((k, n2)), _const_spec((1, n2)),
        ],
        out_specs=[
            pl.BlockSpec((tm, nc), lambda i: (i, 0)),
            pl.BlockSpec((n_seg, 8, nc), seg),
            pl.BlockSpec((tm, n2), lambda i: (i, 0)),
        ],
        out_shape=[jax.ShapeDtypeStruct((n, nc), F32), jax.ShapeDtypeStruct((batch, 8, nc), F32),
                   jax.ShapeDtypeStruct((n, n2), F32)],
        scratch_shapes=[pltpu.VMEM((8, nc), F32)],
        compiler_params=_params(("arbitrary",)),
        name="mlstm_proj_conv",
    )(x, w, b, conv_w, conv_b, conv_prev, w2, b2)


def _rwkv_body(p_ref, prev_ref, h0_ref, mu_ref, w0_ref, a0_ref, kk_ref, ka_ref, rk_ref, lnw_ref, lnb_ref,
               wl_ref, y_ref, hout_ref, h_scr, xbuf, ar_scr, bk_scr, bkh_scr, v_scr, bonus_scr, g_scr, gamma_scr,
               *, chunk, n_chunks, n_items, companion):
    step = pl.program_id(0)
    L = chunk
    half = 2 * L
    prep_chunk = jnp.minimum(step, n_items - 1) % n_chunks
    solve_chunk = jnp.maximum(step - 1, 0) % n_chunks

    @pl.when(step == 0)
    def _():
        for ref in (ar_scr, bk_scr, bkh_scr, v_scr, bonus_scr, g_scr):
            ref[...] = jnp.zeros(ref.shape, ref.dtype)
        gamma_scr[...] = jnp.ones(gamma_scr.shape, F32)

    @pl.when(solve_chunk == 0)
    def _():
        h_scr[...] = h0_ref[...]

    @pl.when(prep_chunk == 0)
    def _():
        for bi in range(BATCH_GROUP):
            xbuf[bi, 0:1, :] = prev_ref[bi]

    ti = lax.broadcasted_iota(jnp.int32, (L, L), 0)
    tj = lax.broadcasted_iota(jnp.int32, (L, L), 1)
    ltri = (tj <= ti).astype(BF16)
    row_m = lax.broadcasted_iota(jnp.int32, (L, half), 0)
    col_m = lax.broadcasted_iota(jnp.int32, (L, half), 1)
    col_in = jnp.where(col_m >= L, col_m - L, col_m)
    strict = col_in < row_m
    incl = col_in <= row_m
    eye = (col_in == row_m).astype(F32)
    gi = lax.broadcasted_iota(jnp.int32, (LANES, LANES), 0)
    gj = lax.broadcasted_iota(jnp.int32, (LANES, LANES), 1)
    same_head = (gi >= RWKV_HEAD_DIM) == (gj >= RWKV_HEAD_DIM)
    head_ones = same_head.astype(BF16)
    first_nat = lax.broadcasted_iota(jnp.int32, (L, LANES), 1) < RWKV_HEAD_DIM
    first_sbs = col_m < L
    lane_l = lax.broadcasted_iota(jnp.int32, (L, LORA_ALL), 1)

    def diag(x, first):
        zero = jnp.zeros_like(x)
        return jnp.concatenate([jnp.where(first, x, zero), jnp.where(first, zero, x)], axis=0)

    def seg_sum(x):
        return _dot(x.astype(BF16), head_ones)

    units = [(bi, p) for bi in range(BATCH_GROUP) for p in range(RWKV_PAIRS)]
    idx = range(len(units))
    lane_of = lambda p: slice(p * LANES, (p + 1) * LANES)

    ar = [ar_scr[bi, :, lane_of(p)] for bi, p in units]
    bk = [bk_scr[bi, :, lane_of(p)] for bi, p in units]
    bkh = [bkh_scr[bi, :, lane_of(p)] for bi, p in units]
    v_b = [v_scr[bi, :, lane_of(p)] for bi, p in units]
    bonus = [bonus_scr[bi, :, lane_of(p)] for bi, p in units]
    gate = [g_scr[bi, :, lane_of(p)] for bi, p in units]
    gamma = [gamma_scr[bi, :, lane_of(p)] for bi, p in units]

    first_row = lax.broadcasted_iota(jnp.int32, (L, RWKV_PROJ), 0) == 0

    def prepare_row(bi):
        cur = p_ref[bi]
        prev = jnp.where(first_row, xbuf[bi, 0:1, :], pltpu.roll(cur, 1, 0))
        xbuf[bi, 0:1, :] = cur[L - 1:L, :]
        xr = cur + (prev - cur) * mu_ref[...]
        r_all = xr[:, 0:D_RWKV]
        k_all = xr[:, D_RWKV:2 * D_RWKV]
        v_all = xr[:, 2 * D_RWKV:3 * D_RWKV]
        slab = xr[:, 3 * D_RWKV:]
        act = jnp.where(lane_l < W_LORA, jnp.tanh(slab),
                        jnp.where(lane_l < W_LORA + A_LORA, slab, _sigmoid(slab)))
        yield
        lora = _bdot(act, wl_ref[...])
        yield
        lw = -DECAY_SCALE_LOG2 * _sigmoid(w0_ref[...] + lora[:, 0:D_RWKV])
        a_sig = _sigmoid(a0_ref[...] + lora[:, D_RWKV:2 * D_RWKV])
        lw_hi, lw_lo = _split(lw)
        yield
        kk0 = k_all * kk_ref[...]
        k2 = k_all * (1.0 + (a_sig - 1.0) * ka_ref[...])
        sums = jnp.concatenate([kk0 * kk0, r_all * k2 * rk_ref[...]], axis=0).astype(BF16)
        yield
        cum = _dot(ltri, lw_hi) + _dot(ltri, lw_lo)
        sums = jnp.concatenate([_dot(sums[:, lane_of(p)], head_ones) for p in range(RWKV_PAIRS)], axis=1)
        yield
        kk = kk0 * lax.rsqrt(jnp.maximum(sums[0:L], 1e-24))
        b_vec = kk * a_sig
        c_last = cum[L - 1:L, :]
        e_neg = jnp.exp2(-cum)
        e_end = jnp.exp2(c_last - cum)
        yield
        ar_scr[bi] = jnp.concatenate([-kk * jnp.exp2(cum - lw), r_all * jnp.exp2(cum)], axis=0).astype(BF16)
        bk_scr[bi] = jnp.concatenate([b_vec * e_neg, k2 * e_neg], axis=0).astype(BF16)
        yield
        bkh_scr[bi] = jnp.concatenate([b_vec * e_end, k2 * e_end], axis=0).astype(BF16)
        v_scr[bi] = v_all.astype(BF16)
        bonus_scr[bi] = sums[L:] * v_all
        g_scr[bi] = lora[:, 2 * D_RWKV:]
        gamma_scr[bi] = jnp.exp2(c_last)

    segments = itertools.chain(*(prepare_row(bi) for bi in range(BATCH_GROUP)))

    def prepare_some(count):
        for _ in range(count):
            next(segments, None)
            next(companion, None)

    n_sq = L.bit_length() - 1
    per_stage = 1
    bk_d = [jnp.concatenate([diag(bk[u][0:L], first_nat), diag(bk[u][L:], first_nat)], axis=0) for u in idx]
    v_d = [diag(v_b[u], first_nat) for u in idx]
    h_bd = [h_scr[bi, p] for bi, p in units]
    sc = [_dot_nt(ar[u], bk_d[u]) for u in idx]
    prepare_some(per_stage)
    arh = [_dot_nt(ar[u], h_bd[u].astype(BF16)) for u in idx]
    m_ab = [jnp.where(strict, sc[u][0:L, 0:half], 0.0) for u in idx]
    m_ak = [jnp.where(strict, sc[u][0:L, half:], 0.0).astype(BF16) for u in idx]
    n_rbk = [jnp.concatenate([jnp.where(incl, sc[u][L:, 0:half], 0.0),
                              jnp.where(incl, sc[u][L:, half:], 0.0)], axis=1).astype(BF16) for u in idx]
    prepare_some(per_stage)
    w0 = [arh[u][0:L] + _dot(m_ak[u], v_d[u]) for u in idx]
    inv = [eye + m_ab[u] for u in idx]
    m_b = [m_ab[u].astype(BF16) for u in idx]
    pw = [_dot(m_b[u], diag(m_b[u], first_sbs)) for u in idx]
    prepare_some(per_stage)
    for i in range(n_sq - 1):
        pw_d = [diag(pw[u].astype(BF16), first_sbs) for u in idx]
        if i + 1 < n_sq - 1:
            z = [_dot(jnp.concatenate([inv[u], pw[u]], axis=0).astype(BF16), pw_d[u]) for u in idx]
            inv = [inv[u] + z[u][0:L] for u in idx]
            pw = [z[u][L:] for u in idx]
        else:
            inv = [inv[u] + _dot(inv[u].astype(BF16), pw_d[u]) for u in idx]
        prepare_some(per_stage)
    c_b = [_dot(inv[u].astype(BF16), diag(w0[u].astype(BF16), first_nat)).astype(BF16) for u in idx]
    prepare_some(per_stage)
    cv = [jnp.concatenate([c_b[u], v_b[u]], axis=0) for u in idx]
    y = [arh[u][L:] + _dot(n_rbk[u], jnp.concatenate([diag(c_b[u], first_nat), v_d[u]], axis=0)) for u in idx]
    for u, (bi, p) in enumerate(units):
        h_scr[bi, p] = h_bd[u] * gamma[u] + jnp.where(same_head, _dot_tn(cv[u], bkh[u]), 0.0)
    prepare_some(per_stage)
    mean = [seg_sum(y[u]) * (1.0 / RWKV_HEAD_DIM) for u in idx]
    prepare_some(per_stage)
    dev = [y[u] - mean[u] for u in idx]
    var = [seg_sum(dev[u] * dev[u]) * (1.0 / RWKV_HEAD_DIM) for u in idx]
    prepare_some(per_stage)
    for u, (bi, p) in enumerate(units):
        s = lane_of(p)
        yn = dev[u] * lax.rsqrt(var[u] + RWKV_GN_EPS) * lnw_ref[:, s] + lnb_ref[:, s]
        y_ref[bi, :, s] = ((yn + bonus[u]) * gate[u]).astype(y_ref.dtype)
    prepare_some(8 * BATCH_GROUP)
    for _ in companion:
        pass

    @pl.when((solve_chunk == n_chunks - 1) & (step > 0))
    def _():
        hout_ref[...] = h_scr[...]


def _mlstm_stages(qk_ref, p_ref, c0_ref, n0_ref, m0_ref, bif_ref, lnw_ref, lnb_ref,
                  h_ref, cout_ref, nout_ref, mout_ref, c_scr, n_scr, m_scr, *, chunk, first, last):
    L = chunk
    d = D_MLSTM
    nh = MLSTM_HEADS

    @pl.when(first)
    def _():
        c_scr[...] = c0_ref[...]
        n_scr[...] = n0_ref[...]
        m_scr[...] = m0_ref[...]

    yield

    ti =lax.broadcasted_iota(jnp.int32, (L, L), 0)
    tj = lax.broadcasted_iota(jnp.int32, (L, L), 1)
    causal = tj <= ti
    ltri = causal.astype(BF16)
    utri = (ti <= tj).astype(BF16)
    sel = (lax.broadcasted_iota(jnp.int32, (8, LANES), 0)
           == lax.broadcasted_iota(jnp.int32, (8, LANES), 1)).astype(BF16)
    lane_g = lax.broadcasted_iota(jnp.int32, (L, LANES), 1)
    is_f = (lane_g >= nh) & (lane_g < 2 * nh)

    rows = []
    for bi in range(BATCH_GROUP):
        gates = p_ref[bi, :, 2 * d:] + bif_ref[...]
        glog = jnp.where(is_f, -_softplus(-gates), gates)
        parts = _split3(glog)
        b_cols = sum(_dot(ltri, t) for t in parts)
        g_rows = sum(_dot_nt(sel, t) for t in parts)
        b_rows = sum(_dot(t, utri) for t in _split3(g_rows))
        rows.append(dict(q=qk_ref[bi, :, 0:d], k=qk_ref[bi, :, d:2 * d], v=p_ref[bi, :, 0:d],
                         o=_sigmoid(p_ref[bi, :, d:2 * d]),
                         glog=glog, b_cols=b_cols, g_rows=g_rows, b_rows=b_rows))
        yield

    units = [(bi, h) for bi in range(BATCH_GROUP) for h in range(nh)]
    idx = range(len(units))
    head = lambda h: slice(h * MLSTM_HEAD_DIM, (h + 1) * MLSTM_HEAD_DIM)
    q = [rows[bi]["q"][:, head(h)] for bi, h in units]
    k = [rows[bi]["k"][:, head(h)] for bi, h in units]
    v = [rows[bi]["v"][:, head(h)] for bi, h in units]
    q_b = [t.astype(BF16) for t in q]
    k_b = [t.astype(BF16) for t in k]
    halves = [slice(0, MLSTM_HEAD_DIM // 2), slice(MLSTM_HEAD_DIM // 2, MLSTM_HEAD_DIM)]
    c_st = [[c_scr[bi, h, hs, :] for hs in halves] for bi, h in units]
    n_st = [n_scr[bi, h:h + 1, :] for bi, h in units]
    m_prev = [m_scr[bi, :, h:h + 1] for bi, h in units]
    qk_t = [_dot_nt(q_b[u], k_b[u]) for u in idx]
    yield
    qc = [jnp.concatenate([_dot_nt(q_b[u], part.astype(BF16)) for part in c_st[u]], axis=1) for u in idx]
    yield
    s, w_inter, m_t, wk, carry, m_next = [], [], [], [], [], []
    for u, (bi, h) in enumerate(units):
        t = rows[bi]
        ig_row = t["g_rows"][h:h + 1, :]
        b_row = t["b_rows"][nh + h:nh + h + 1, :]
        ig_col = t["glog"][:, h:h + 1]
        b_col = t["b_cols"][:, nh + h:nh + h + 1]
        log_w = jnp.where(causal, b_col - b_row + ig_row, -jnp.inf)
        inter = b_col + m_prev[u]
        m_now = jnp.maximum(inter, jnp.max(log_w, axis=-1, keepdims=True))
        s.append(qk_t[u] * jnp.exp(log_w - m_now))
        w_inter.append(jnp.exp(inter - m_now))
        m_t.append(m_now)
        b_last = b_col[L - 1:L, :]
        g_col = b_last - b_col + ig_col
        m_new = jnp.maximum(b_last + m_prev[u], jnp.max(g_col, axis=0, keepdims=True))
        carry.append(jnp.exp(b_last + m_prev[u] - m_new))
        wk.append(jnp.exp(g_col - m_new))
        m_next.append(m_new)
        if u % 2 == 1:
            yield
    sv = [_dot(s[u].astype(BF16), v[u].astype(BF16)) for u in idx]
    yield
    upd = [[_dot_tn((wk[u] * v[u][:, hs]).astype(BF16), k_b[u]) for hs in halves] for u in idx]
    yield
    qn =[jnp.sum(q[u] * n_st[u], axis=-1, keepdims=True) for u in idx]
    s_sum = [jnp.sum(s[u], axis=-1, keepdims=True) for u in idx]
    den = [jnp.maximum(jnp.abs(w_inter[u] * qn[u] + s_sum[u]), jnp.exp(-m_t[u])) for u in idx]
    yield
    hb = [rows[bi]["o"][:, head(h)] * ((w_inter[u] * qc[u] + sv[u]) / den[u]) for u, (bi, h) in enumerate(units)]
    yield
    mu = [jnp.mean(hb[u], axis=-1, keepdims=True) for u in idx]
    dev = [hb[u] - mu[u] for u in idx]
    var = [jnp.mean(dev[u] * dev[u], axis=-1, keepdims=True) for u in idx]
    yield
    n_new = [carry[u] * n_st[u] + jnp.sum(wk[u] * k[u], axis=0, keepdims=True) for u in idx]
    for u, (bi, h) in enumerate(units):
        sl = head(h)
        h_ref[bi, :, sl] = (dev[u] * lax.rsqrt(var[u] + LN_EPS) * lnw_ref[:, sl] + lnb_ref[:, sl]).astype(h_ref.dtype)
        for j, hs in enumerate(halves):
            c_scr[bi, h, hs, :] = carry[u] * c_st[u][j] + upd[u][j]
        n_scr[bi, h:h + 1, :] = n_new[u]
        m_scr[bi, :, h:h + 1] = m_next[u]
        if u % 4 == 3:
            yield

    @pl.when(last)
    def _():
        cout_ref[...] = c_scr[...]
        nout_ref[...] = n_scr[...]
        mout_ref[...] = m_scr[...]


RW_REFS = (12, 2, 9)
ML_REFS = (8, 4, 3)


def _recurrent_kernel(*refs, chunk, n_chunks, n_items):
    groups, pos = [], 0
    for n_rw, n_ml in zip(RW_REFS, ML_REFS):
        groups.append((refs[pos:pos + n_rw], refs[pos + n_rw:pos + n_rw + n_ml]))
        pos += n_rw + n_ml
    rw_refs = [r for rw_part, _ in groups for r in rw_part]
    ml_refs = [r for _, ml_part in groups for r in ml_part]
    step = pl.program_id(0)
    solve_chunk = jnp.maximum(step - 1, 0) % n_chunks
    ml = _mlstm_stages(*ml_refs, chunk=chunk, first=solve_chunk == 0,
                       last=(solve_chunk == n_chunks - 1) & (step > 0))
    next(ml)
    _rwkv_body(*rw_refs, chunk=chunk, n_chunks=n_chunks, n_items=n_items, companion=ml)


def _recurrent(p_rw, shift_prev, h0_bd, qk, p_vo, c0, n0, m0, w, chunk):
    batch, t_len, _ = p_rw.shape
    n_chunks = t_len // chunk
    bg = BATCH_GROUP
    hd = MLSTM_HEAD_DIM
    n_items = (batch // bg) * n_chunks
    prep = lambda i: jnp.minimum(i, n_items - 1)
    solve = lambda i: jnp.maximum(i - 1, 0)
    prep_seq = lambda i: (prep(i) // n_chunks, prep(i) % n_chunks, 0)
    solve_seq = lambda i: (solve(i) // n_chunks, solve(i) % n_chunks, 0)
    vec = lambda n: pl.BlockSpec((1, n), lambda i: (0, 0))
    group = lambda *dims: pl.BlockSpec((bg,) + dims, lambda i: (solve(i) // n_chunks,) + (0,) * len(dims))
    rw_state = group(RWKV_PAIRS, LANES, LANES)
    ml_states = [group(MLSTM_HEADS, hd, hd), group(MLSTM_HEADS, hd), group(1, MLSTM_HEADS)]
    wide = lambda rows, dtype: pltpu.VMEM((bg, rows, D_RWKV), dtype)
    kern = functools.partial(_recurrent_kernel, chunk=chunk, n_chunks=n_chunks, n_items=n_items)
    return pl.pallas_call(
        kern,
        grid=(n_items + 1,),
        in_specs=[
            pl.BlockSpec((bg, chunk, RWKV_PROJ), prep_seq),
            pl.BlockSpec((bg, 1, RWKV_PROJ), lambda i: (prep(i) // n_chunks, 0, 0)),
            rw_state,
            vec(RWKV_PROJ), vec(D_RWKV), vec(D_RWKV), vec(D_RWKV), vec(D_RWKV), vec(D_RWKV), vec(D_RWKV),
            vec(D_RWKV),
            pl.BlockSpec((LORA_ALL, 3 * D_RWKV), lambda i: (0, 0)),
            pl.BlockSpec((bg, chunk, 2 * D_MLSTM), solve_seq),
            pl.BlockSpec((bg, chunk, VO_COLS), solve_seq),
            *ml_states,
            vec(LANES), vec(D_MLSTM), vec(D_MLSTM),
        ],
        out_specs=[
            pl.BlockSpec((bg, chunk, D_RWKV), solve_seq), rw_state,
            pl.BlockSpec((bg, chunk, D_MLSTM), solve_seq), *ml_states,
        ],
        out_shape=[
            jax.ShapeDtypeStruct((batch, t_len, D_RWKV), BF16),
            jax.ShapeDtypeStruct((batch, RWKV_PAIRS, LANES, LANES), F32),
            jax.ShapeDtypeStruct((batch, t_len, D_MLSTM), BF16),
            jax.ShapeDtypeStruct((batch, MLSTM_HEADS, hd, hd), F32),
            jax.ShapeDtypeStruct((batch, MLSTM_HEADS, hd), F32),
            jax.ShapeDtypeStruct((batch, 1, MLSTM_HEADS), F32),
        ],
        scratch_shapes=[
            pltpu.VMEM((bg, RWKV_PAIRS, LANES, LANES), F32),
            pltpu.VMEM((bg, 8, RWKV_PROJ), F32),
            wide(2 * chunk, BF16), wide(2 * chunk, BF16), wide(2 * chunk, BF16), wide(chunk, BF16),
            wide(chunk, F32), wide(chunk, F32), wide(1, F32),
            pltpu.VMEM((bg, MLSTM_HEADS, hd, hd), F32),
            pltpu.VMEM((bg, MLSTM_HEADS, hd), F32),
            pltpu.VMEM((bg, 1, MLSTM_HEADS), F32),
        ],
        compiler_params=_params(("arbitrary",)),
        name="rwkv7_mlstm_recurrent",
    )(p_rw, shift_prev, h0_bd, w["mu"], w["w0"], w["a0"], w["k_k"], w["k_a"], w["r_k"], w["rw_ln_w"],
      w["rw_ln_b"], w["w_lora"], qk, p_vo, c0, n0, m0, w["b_if"], w["ml_ln_w"], w["ml_ln_b"])


def _tail_kernel(x_ref, ya_ref, hb_ref, mq_ref, gate_ref, mk_ref, mv_ref, wpa_ref, wpb_ref, wpc_ref, wout_ref,
                 lnw_ref, lnb_ref, o_ref, *, rows):
    n_seg = mk_ref.shape[0]
    d = D_MODEL
    head = lambda h: slice(h * MEM_HEAD_DIM, (h + 1) * MEM_HEAD_DIM)
    units = [(b, h) for b in range(n_seg) for h in range(MEM_HEADS)]
    idx = range(len(units))
    mk = [mk_ref[b].astype(BF16) for b in range(n_seg)]
    mv = [mv_ref[b].astype(BF16) for b in range(n_seg)]
    s = [_dot_nt(mq_ref[b * rows:(b + 1) * rows, head(h)], mk[b][:, head(h)]) * (MEM_HEAD_DIM ** -0.5)
         for b, h in units]
    y_a = _sigmoid(gate_ref[:, 0:d]) * _dot(ya_ref[...], wpa_ref[...])
    top = [jnp.max(s[u], axis=-1, keepdims=True) for u in idx]
    e = [jnp.exp(s[u] - top[u]) for u in idx]
    y_b = _sigmoid(gate_ref[:, d:2 * d]) * _dot(hb_ref[...], wpb_ref[...])
    den = [jnp.sum(e[u], axis=-1, keepdims=True) for u in idx]
    prob = [(e[u] / den[u]).astype(BF16) for u in idx]
    out = [_dot(prob[u], mv[b][:, head(h)]).astype(BF16) for u, (b, h) in enumerate(units)]
    attn = [jnp.concatenate(out[b * MEM_HEADS:(b + 1) * MEM_HEADS], axis=-1) for b in range(n_seg)]
    attn = attn[0] if n_seg == 1 else jnp.concatenate(attn, axis=0)
    y_c = _sigmoid(gate_ref[:, 2 * d:3 * d]) * _dot(attn, wpc_ref[...])
    mixed = y_a + y_b + y_c
    u = _bdot(mixed, wout_ref[...])
    o_ref[...] = _layer_norm(DN_ALPHA * x_ref[...] + u, lnw_ref[...], lnb_ref[...])


def _tail(x, ya, hb, p_mq, p_gate, mem_k, mem_v, w, rows, batches):
    n = x.shape[0]
    tm = rows * batches
    tiles_per_batch = (n // mem_k.shape[0]) // rows if batches == 1 else 1
    row = lambda i: (i, 0)
    mem = pl.BlockSpec((batches, N_MEM, D_MEM), lambda i: (i // tiles_per_batch, 0, 0))
    tok = lambda cols: pl.BlockSpec((tm, cols), row)
    return pl.pallas_call(
        functools.partial(_tail_kernel, rows=rows),
        grid=(n // tm,),
        in_specs=[
            tok(D_MODEL), tok(D_RWKV), tok(D_MLSTM), tok(D_MEM), tok(3 * D_MODEL), mem, mem,
            _const_spec((D_RWKV, D_MODEL)), _const_spec((D_MLSTM, D_MODEL)), _const_spec((D_MEM, D_MODEL)),
            _const_spec((D_MODEL, D_MODEL)), _const_spec((1, D_MODEL)), _const_spec((1, D_MODEL)),
        ],
        out_specs=tok(D_MODEL),
        out_shape=jax.ShapeDtypeStruct((n, D_MODEL), F32),
        compiler_params=_params(("arbitrary",)),
        name="attn_merge_ln",
    )(x, ya, hb, p_mq, p_gate, mem_k, mem_v, w["w_pa"], w["w_pb"], w["w_pc"], w["w_out"], w["ln1_w"],
      w["ln1_b"])


def _expert_copy(src_hbm, dst, sem, e):
    return pltpu.make_async_copy(src_hbm.at[e], dst.at[:, pl.ds(e * EXPERT_HIDDEN, EXPERT_HIDDEN)], sem)


def _moe_kernel(x_ref, wr_ref, rb_ref, w1_hbm, w3_hbm, w2_ref, ws1_ref, ws3_ref, ws2_ref, lnw_ref, lnb_ref, o_ref,
                w1_ref, w3_ref, sems):
    @pl.when(pl.program_id(0) == 0)
    def _():
        copies = [_expert_copy(src, dst, sems.at[i], e)
                  for i, (src, dst) in enumerate(((w1_hbm, w1_ref), (w3_hbm, w3_ref))) for e in range(N_EXPERTS)]
        for cp in copies:
            cp.start()
        for cp in copies:
            cp.wait()

    x = x_ref[...]
    xb = x.astype(BF16)
    group = 8
    width = group * EXPERT_HIDDEN
    n_slabs = N_EXPERTS // group

    def up_project(s):
        cols = slice(s * width, (s + 1) * width)
        return _dot(xb, w1_ref[:, cols]), _dot(xb, w3_ref[:, cols])

    logits = _dot(xb, wr_ref[...])
    ahead = up_project(0)
    scores = _sigmoid(logits)
    lane = lax.broadcasted_iota(jnp.int32, scores.shape, 1)
    work = jnp.where(lane < N_EXPERTS, scores + rb_ref[...], -jnp.inf)
    chosen = jnp.zeros(scores.shape, dtype=jnp.bool_)
    for _ in range(TOP_K):
        best = jnp.max(work, axis=-1, keepdims=True)
        first = jnp.min(jnp.where(work == best, lane, LANES), axis=-1, keepdims=True)
        pick = lane == first
        chosen = chosen | pick
        work = jnp.where(pick, -jnp.inf, work)
    sel = jnp.where(chosen, scores, 0.0)
    gates = sel / jnp.sum(sel, axis=-1, keepdims=True) * ROUTED_SCALE

    acc = _dot((_silu(_dot(xb, ws1_ref[...])) * _dot(xb, ws3_ref[...])).astype(BF16), ws2_ref[...])
    for s in range(n_slabs):
        cols = slice(s * width, (s + 1) * width)
        h1, h3 = ahead
        if s + 1 < n_slabs:
            ahead = up_project(s + 1)
        hidden = _silu(h1) * h3
        gated = jnp.concatenate(
            [hidden[:, e * EXPERT_HIDDEN:(e + 1) * EXPERT_HIDDEN] * gates[:, s * group + e:s * group + e + 1]
             for e in range(group)], axis=-1)
        acc = acc + _dot(gated.astype(BF16), w2_ref[cols, :])
    o_ref[...] = _layer_norm(DN_ALPHA * x + acc, lnw_ref[...], lnb_ref[...])


def _moe(x, w, tm):
    n = x.shape[0]
    row = lambda i: (i, 0)
    hid = N_EXPERTS * EXPERT_HIDDEN
    return pl.pallas_call(
        _moe_kernel,
        grid=(n // tm,),
        in_specs=[
            pl.BlockSpec((tm, D_MODEL), row),
            _const_spec((D_MODEL, LANES)), _const_spec((1, LANES)),
            pl.BlockSpec(memory_space=pl.ANY), pl.BlockSpec(memory_space=pl.ANY), _const_spec((hid, D_MODEL)),
            _const_spec((D_MODEL, EXPERT_HIDDEN)), _const_spec((D_MODEL, EXPERT_HIDDEN)),
            _const_spec((EXPERT_HIDDEN, D_MODEL)),
            _const_spec((1, D_MODEL)), _const_spec((1, D_MODEL)),
        ],
        out_specs=pl.BlockSpec((tm, D_MODEL), row),
        out_shape=jax.ShapeDtypeStruct((n, D_MODEL), F32),
        scratch_shapes=[pltpu.VMEM((D_MODEL, hid), BF16), pltpu.VMEM((D_MODEL, hid), BF16),
                        pltpu.SemaphoreType.DMA((2,))],
        compiler_params=_params(("arbitrary",)),
        name="moe_ln",
    )(x, w["w_router"], w["router_bias"], w["w_e1"], w["w_e3"], w["w_e2"], w["w_s1"], w["w_s3"], w["w_s2"],
      w["ln2_w"], w["ln2_b"])


def _prep_weights(w_in, b_in, rwkv_mu, rwkv_w0, rwkv_w2, rwkv_a0, rwkv_a2, rwkv_g2, rwkv_k_k, rwkv_k_a,
                  rwkv_r_k, rwkv_ln_w, rwkv_ln_b, mlstm_conv_w, mlstm_conv_b, mlstm_b_i, mlstm_b_f,
                  mlstm_ln_w, mlstm_ln_b, w_mem_kv, w_pa, w_pb, w_pc, w_out, ln1_w, ln1_b,
                  w_router, router_bias, w_e1, w_e3, w_e2, w_s1, w_s3, w_s2, ln2_w, ln2_b):
    off_ml = RWKV_PROJ
    off_if = off_ml + 4 * D_MLSTM
    off_mq = off_if + 2 * MLSTM_HEADS
    pad_if = LANES - 2 * MLSTM_HEADS
    row = lambda t: t.reshape(1, -1)
    hid = N_EXPERTS * EXPERT_HIDDEN
    w_lora = jnp.zeros((LORA_ALL, 3 * D_RWKV), F32)
    w_lora = w_lora.at[0:W_LORA, 0:D_RWKV].set(rwkv_w2)
    w_lora = w_lora.at[W_LORA:W_LORA + A_LORA, D_RWKV:2 * D_RWKV].set(rwkv_a2)
    w_lora = w_lora.at[W_LORA + A_LORA:, 2 * D_RWKV:].set(rwkv_g2)
    return dict(
        w_rw=w_in[:, :off_ml].astype(BF16), b_rw=row(b_in[:off_ml]),
        w_qk=w_in[:, off_ml:off_ml + 2 * D_MLSTM].astype(BF16), b_qk=row(b_in[off_ml:off_ml + 2 * D_MLSTM]),
        w_vo=jnp.pad(w_in[:, off_ml + 2 * D_MLSTM:off_mq], ((0, 0), (0, pad_if))).astype(BF16),
        b_vo=row(jnp.pad(b_in[off_ml + 2 * D_MLSTM:off_mq], (0, pad_if))),
        w_mq=w_in[:, off_mq:off_mq + D_MEM].astype(BF16), b_mq=row(b_in[off_mq:off_mq + D_MEM]),
        w_gate=w_in[:, off_mq + D_MEM:].astype(BF16), b_gate=row(b_in[off_mq + D_MEM:]),
        w_mem_k=w_mem_kv[:, :D_MEM].astype(BF16), w_mem_v=w_mem_kv[:, D_MEM:].astype(BF16),
        mu=row(rwkv_mu), w0=row(rwkv_w0), a0=row(rwkv_a0), k_k=row(rwkv_k_k), k_a=row(rwkv_k_a),
        r_k=row(rwkv_r_k), rw_ln_w=row(rwkv_ln_w), rw_ln_b=row(rwkv_ln_b), w_lora=w_lora.astype(BF16),
        conv_w=mlstm_conv_w, conv_b=row(mlstm_conv_b),
        b_if=row(jnp.pad(jnp.concatenate([mlstm_b_i, mlstm_b_f]), (0, pad_if))),
        ml_ln_w=row(mlstm_ln_w), ml_ln_b=row(mlstm_ln_b),
        w_pa=w_pa.astype(BF16), w_pb=w_pb.astype(BF16), w_pc=w_pc.astype(BF16), w_out=w_out.astype(BF16),
        ln1_w=row(ln1_w), ln1_b=row(ln1_b),
        w_router=jnp.pad(w_router, ((0, 0), (0, LANES - N_EXPERTS))).astype(BF16),
        router_bias=row(jnp.pad(router_bias, (0, LANES - N_EXPERTS))),
        w_e1=w_e1.astype(BF16), w_e3=w_e3.astype(BF16),
        w_e2=w_e2.reshape(hid, D_MODEL).astype(BF16),
        w_s1=w_s1.astype(BF16), w_s3=w_s3.astype(BF16), w_s2=w_s2.astype(BF16),
        ln2_w=row(ln2_w), ln2_b=row(ln2_b),
    )


def _pair_state(wkv):
    bsz = wkv.shape[0]
    s = wkv.reshape(bsz, RWKV_PAIRS, 2, RWKV_HEAD_DIM, RWKV_HEAD_DIM)
    z = jnp.zeros_like(s[:, :, 0])
    top = jnp.concatenate([s[:, :, 0], z], axis=-1)
    bot = jnp.concatenate([z, s[:, :, 1]], axis=-1)
    return jnp.concatenate([top, bot], axis=-2)


def _unpair_state(h_bd):
    bsz = h_bd.shape[0]
    hd = RWKV_HEAD_DIM
    heads = jnp.stack([h_bd[:, :, :hd, :hd], h_bd[:, :, hd:, hd:]], axis=2)
    return heads.reshape(bsz, RWKV_HEADS, hd, hd)


def _trunk(x3, mem_k, mem_v, shift_prev, wkv0, conv_prev, c0, n0, m0, w, chunk, tm_proj, tail_rows, tail_batches,
           tm_moe):
    bsz, t_len, _ = x3.shape
    n = bsz * t_len
    x = x3.reshape(n, D_MODEL)
    p_rw = _matmul_bias(x, w["w_rw"], w["b_rw"], tm_proj, RWKV_PROJ)
    qk, qk_tail, p_vo = _mlstm_proj(x, w["w_qk"], w["b_qk"], w["conv_w"], w["conv_b"], conv_prev, w["w_vo"],
                                    w["b_vo"], tm_proj)
    p_mq, p_gate = _matmul_bias2(x, w["w_mq"], w["b_mq"], w["w_gate"], w["b_gate"], tm_proj, BF16, F32)

    p_rw3 = p_rw.reshape(bsz, t_len, RWKV_PROJ)
    ya, h_bd, hb, c_t, n_t, m_t = _recurrent(
        p_rw3, shift_prev, _pair_state(wkv0), qk.reshape(bsz, t_len, 2 * D_MLSTM), p_vo.reshape(bsz, t_len, VO_COLS),
        c0, n0, m0.reshape(bsz, 1, MLSTM_HEADS), w, chunk)
    x1 = _tail(x, ya.reshape(n, D_RWKV), hb.reshape(n, D_MLSTM), p_mq, p_gate, mem_k, mem_v, w, tail_rows,
               tail_batches)
    y = _moe(x1, w, tm_moe)

    new_shift = p_rw3[:, t_len - 1:, :]
    new_conv = qk_tail[:, 8 - (CONV_W - 1):, :]
    states = (new_shift, _unpair_state(h_bd), new_conv, c_t, n_t, m_t.reshape(bsz, MLSTM_HEADS))
    return y.reshape(bsz, t_len, D_MODEL), states


def kernel(x_prompt, x_sample, state_rwkv_shift, state_rwkv_wkv, state_mlstm_conv, state_mlstm_c, state_mlstm_n, state_mlstm_m, cache_mem_k, cache_mem_v, mem_prompt, w_in, b_in, rwkv_mu, rwkv_w0, rwkv_w2, rwkv_a0, rwkv_a2, rwkv_g2, rwkv_k_k, rwkv_k_a, rwkv_r_k, rwkv_ln_w, rwkv_ln_b, mlstm_conv_w, mlstm_conv_b, mlstm_b_i, mlstm_b_f, mlstm_ln_w, mlstm_ln_b, w_mem_kv, w_pa, w_pb, w_pc, w_out, ln1_w, ln1_b, w_router, router_bias, w_e1, w_e3, w_e2, w_s1, w_s3, w_s2, ln2_w, ln2_b):
    weights = (w_in, b_in, rwkv_mu, rwkv_w0, rwkv_w2, rwkv_a0, rwkv_a2, rwkv_g2, rwkv_k_k, rwkv_k_a, rwkv_r_k,
               rwkv_ln_w, rwkv_ln_b, mlstm_conv_w, mlstm_conv_b, mlstm_b_i, mlstm_b_f, mlstm_ln_w, mlstm_ln_b,
               w_mem_kv, w_pa, w_pb, w_pc, w_out, ln1_w, ln1_b, w_router, router_bias, w_e1, w_e3, w_e2,
               w_s1, w_s3, w_s2, ln2_w, ln2_b)
    w = _prep_weights(*(t[0] for t in weights))
    bp, t_p, _ = x_prompt.shape
    bs, t_s, _ = x_sample.shape
    hd = MLSTM_HEAD_DIM

    mem_flat = mem_prompt.reshape(bp * N_MEM, D_MODEL)
    zero_bias = jnp.zeros((1, D_MEM), F32)
    mk = _matmul_bias(mem_flat, w["w_mem_k"], zero_bias, bp * N_MEM, D_MEM).reshape(bp, N_MEM, D_MEM)
    mv = _matmul_bias(mem_flat, w["w_mem_v"], zero_bias, bp * N_MEM, D_MEM).reshape(bp, N_MEM, D_MEM)

    y_p, st_p = _trunk(
        x_prompt, mk, mv,
        jnp.zeros((bp, 1, RWKV_PROJ), F32), jnp.zeros((bp, RWKV_HEADS, RWKV_HEAD_DIM, RWKV_HEAD_DIM), F32),
        jnp.zeros((bp, CONV_W - 1, 2 * D_MLSTM), F32), jnp.zeros((bp, MLSTM_HEADS, hd, hd), F32),
        jnp.zeros((bp, MLSTM_HEADS, hd), F32), jnp.zeros((bp, MLSTM_HEADS), F32),
        w, chunk=min(MLSTM_CHUNK, t_p), tm_proj=1024, tail_rows=512, tail_batches=1, tm_moe=512)
    y_s, st_s = _trunk(
        x_sample, cache_mem_k[0].reshape(bs, N_MEM, D_MEM), cache_mem_v[0].reshape(bs, N_MEM, D_MEM),
        state_rwkv_shift[0], state_rwkv_wkv[0], state_mlstm_conv[0], state_mlstm_c[0], state_mlstm_n[0],
        state_mlstm_m[0],
        w, chunk=min(MLSTM_CHUNK, t_s), tm_proj=bs * t_s, tail_rows=t_s, tail_batches=4, tm_moe=bs * t_s)

    lead = lambda t: t[None]
    mem_shape = (1, bp, N_MEM, MEM_HEADS, MEM_HEAD_DIM)
    return (y_p, y_s, *(lead(t) for t in st_p), mk.reshape(mem_shape), mv.reshape(mem_shape),
            *(lead(t) for t in st_s))
```
